```python
import jax, jax.numpy as jnp
from jax import lax
import numpy as np

D_MODEL = 1024
BATCH = 8
SEQ = 4096
DEPTH = 1

D_PLE = 256
D_RNN = 1024
RNN_BLOCKS = 8
RNN_BW = D_RNN // RNN_BLOCKS
RNN_CONV = 4
LRU_C = 8.0
D_CONV = 1024
SHORT_CONV = 3
N_BRANCH = 2
D_FF = 3 * D_MODEL
FFN_CONV = 3
D_IN = 2 * D_RNN + 3 * D_CONV + N_BRANCH * D_MODEL
EPS = 1e-6

kernel_name = "hybrid_rglru_shortconv_block"


def rmsnorm(u, g):
    uf = u.astype(jnp.float32)
    y = uf * lax.rsqrt(jnp.mean(uf * uf, axis=-1, keepdims=True) + EPS)
    return (y * g.astype(jnp.float32)).astype(u.dtype)


def causal_dwconv(u, w):
    K = w.shape[0]
    S = u.shape[1]
    up = jnp.pad(u, ((0, 0), (K - 1, 0), (0, 0)))
    y = up[:, 0:S] * w[0]
    for k in range(1, K):
        y = y + up[:, k:k + S] * w[k]
    return y


def rg_lru(xc, w_a, b_a, w_x, b_x, lam):
    Bn, S, _ = xc.shape
    xb = xc.reshape(Bn, S, RNN_BLOCKS, RNN_BW)
    r = jax.nn.sigmoid((jnp.einsum('bshi,hij->bshj', xb, w_a).reshape(Bn, S, D_RNN) + b_a).astype(jnp.float32))
    i = jax.nn.sigmoid((jnp.einsum('bshi,hij->bshj', xb, w_x).reshape(Bn, S, D_RNN) + b_x).astype(jnp.float32))
    log_a = -LRU_C * r * jax.nn.softplus(-lam.astype(jnp.float32))
    a = jnp.exp(log_a)
    mult = jnp.sqrt(-jnp.expm1(2.0 * log_a))
    b = mult * (i * xc.astype(jnp.float32))

    def combine(lhs, rhs):
        a1, b1 = lhs
        a2, b2 = rhs
        return a1 * a2, a2 * b1 + b2

    _, h = lax.associative_scan(combine, (a, b), axis=1)
    return h.astype(xc.dtype)


def _fwd_setup_inputs(seed: int = 0) -> dict:
    key = jax.random.key(seed)
    ks = jax.random.split(key, 24)
    f32 = jnp.float32

    def nrm(k, shape, scale):
        return jax.random.normal(k, shape, f32) * scale

    u = jax.random.uniform(ks[10], (DEPTH, D_RNN), f32, 0.9, 0.999)
    a0 = u ** (1.0 / LRU_C)
    lru_lambda = jnp.log(a0) - jnp.log1p(-a0)
    return {
        "x": nrm(ks[0], (BATCH, SEQ, D_MODEL), 1.0),
        "p": nrm(ks[1], (DEPTH, BATCH, SEQ, D_PLE), 1.0),
        "g_mix": 1.0 + nrm(ks[2], (DEPTH, D_MODEL), 0.02),
        "w_in": nrm(ks[3], (DEPTH, D_MODEL, D_IN), D_MODEL ** -0.5),
        "rnn_conv_w": nrm(ks[4], (DEPTH, RNN_CONV, D_RNN), RNN_CONV ** -0.5),
        "rnn_conv_b": nrm(ks[5], (DEPTH, D_RNN), 0.01),
        "w_rg_a": nrm(ks[6], (DEPTH, RNN_BLOCKS, RNN_BW, RNN_BW), RNN_BW ** -0.5),
        "b_rg_a": nrm(ks[7], (DEPTH, D_RNN), 0.01),
        "w_rg_x": nrm(ks[8], (DEPTH, RNN_BLOCKS, RNN_BW, RNN_BW), RNN_BW ** -0.5),
        "b_rg_x": nrm(ks[9], (DEPTH, D_RNN), 0.01),
        "lru_lambda": lru_lambda,
        "sc_conv_w": nrm(ks[11], (DEPTH, SHORT_CONV, D_CONV), SHORT_CONV ** -0.5),
        "w_proj_a": nrm(ks[12], (DEPTH, D_RNN, D_MODEL), D_RNN ** -0.5),
        "w_proj_b": nrm(ks[13], (DEPTH, D_CONV, D_MODEL), D_CONV ** -0.5),
        "w_out": nrm(ks[14], (DEPTH, D_MODEL, D_MODEL), D_MODEL ** -0.5),
        "g_ffn": 1.0 + nrm(ks[15], (DEPTH, D_MODEL), 0.02),
        "w_up": nrm(ks[16], (DEPTH, D_MODEL, 2 * D_FF), D_MODEL ** -0.5),
        "ffn_conv_w": nrm(ks[17], (DEPTH, FFN_CONV, 2 * D_FF), FFN_CONV ** -0.5),
        "ffn_conv_b": nrm(ks[18], (DEPTH, 2 * D_FF), 0.01),
        "w_down": nrm(ks[19], (DEPTH, D_FF, D_MODEL), D_FF ** -0.5),
        "w_ple_gate": nrm(ks[20], (DEPTH, D_MODEL, D_MODEL), D_MODEL ** -0.5),
        "w_ple_proj": nrm(ks[21], (DEPTH, D_PLE, D_MODEL), D_PLE ** -0.5),
        "g_ple": 1.0 + nrm(ks[22], (DEPTH, D_MODEL), 0.02),
        "g_final": 1.0 + nrm(ks[23], (D_MODEL,), 0.02),
    }


def _fwd_reference(x, p, g_mix, w_in, rnn_conv_w, rnn_conv_b, w_rg_a, b_rg_a, w_rg_x, b_rg_x,
              lru_lambda, sc_conv_w, w_proj_a, w_proj_b, w_out, g_ffn, w_up, ffn_conv_w,
              ffn_conv_b, w_down, w_ple_gate, w_ple_proj, g_ple, g_final):
    split_idx = (D_RNN, 2 * D_RNN, 2 * D_RNN + D_CONV, 2 * D_RNN + 2 * D_CONV,
                 2 * D_RNN + 3 * D_CONV, 2 * D_RNN + 3 * D_CONV + D_MODEL)
    for l in range(DEPTH):
        h = rmsnorm(x, g_mix[l])
        z = h @ w_in[l]
        xr, gr, cb, cc, cx, ga, gb = jnp.split(z, split_idx, axis=-1)

        xr = causal_dwconv(xr, rnn_conv_w[l]) + rnn_conv_b[l]
        ya = rg_lru(xr, w_rg_a[l], b_rg_a[l], w_rg_x[l], b_rg_x[l], lru_lambda[l]) * jax.nn.gelu(gr)

        yb = cb * causal_dwconv(cc * cx, sc_conv_w[l])

        m = jax.nn.sigmoid(ga) * (ya @ w_proj_a[l]) + jax.nn.sigmoid(gb) * (yb @ w_proj_b[l])
        x = x + m @ w_out[l]

        h = rmsnorm(x, g_ffn[l])
        u = causal_dwconv(h @ w_up[l], ffn_conv_w[l]) + ffn_conv_b[l]
        ug, uv = jnp.split(u, 2, axis=-1)
        x = x + (jax.nn.gelu(ug) * uv) @ w_down[l]

        e = rmsnorm(p[l] @ w_ple_proj[l], g_ple[l])
        x = x + jax.nn.sigmoid(x @ w_ple_gate[l]) * e
    return rmsnorm(x, g_final)


import jax as _jax
import jax.numpy as _jnp

TWIN_FORMAT = 'train_step'
FWD_PARAMS = ['x', 'p', 'g_mix', 'w_in', 'rnn_conv_w', 'rnn_conv_b', 'w_rg_a', 'b_rg_a', 'w_rg_x', 'b_rg_x', 'lru_lambda', 'sc_conv_w', 'w_proj_a', 'w_proj_b', 'w_out', 'g_ffn', 'w_up', 'ffn_conv_w', 'ffn_conv_b', 'w_down', 'w_ple_gate', 'w_ple_proj', 'g_ple', 'g_final']
TWIN_WEIGHTS = ['g_mix', 'w_in', 'rnn_conv_w', 'rnn_conv_b', 'w_rg_a', 'b_rg_a', 'w_rg_x', 'b_rg_x', 'lru_lambda', 'sc_conv_w', 'w_proj_a', 'w_proj_b', 'w_out', 'g_ffn', 'w_up', 'ffn_conv_w', 'ffn_conv_b', 'w_down', 'w_ple_gate', 'w_ple_proj', 'g_ple', 'g_final']
TWIN_DIFF_INPUT = 'x'
TWIN_INPUTS = ['x', 'p', 'g_mix', 'w_in', 'rnn_conv_w', 'rnn_conv_b', 'w_rg_a', 'b_rg_a', 'w_rg_x', 'b_rg_x', 'lru_lambda', 'sc_conv_w', 'w_proj_a', 'w_proj_b', 'w_out', 'g_ffn', 'w_up', 'ffn_conv_w', 'ffn_conv_b', 'w_down', 'w_ple_gate', 'w_ple_proj', 'g_ple', 'g_final', 'loss_target', 'm_g_mix', 'm_w_in', 'm_rnn_conv_w', 'm_rnn_conv_b', 'm_w_rg_a', 'm_b_rg_a', 'm_w_rg_x', 'm_b_rg_x', 'm_lru_lambda', 'm_sc_conv_w', 'm_w_proj_a', 'm_w_proj_b', 'm_w_out', 'm_g_ffn', 'm_w_up', 'm_ffn_conv_w', 'm_ffn_conv_b', 'm_w_down', 'm_w_ple_gate', 'm_w_ple_proj', 'm_g_ple', 'm_g_final', 'v_g_mix', 'v_w_in', 'v_rnn_conv_w', 'v_rnn_conv_b', 'v_w_rg_a', 'v_b_rg_a', 'v_w_rg_x', 'v_b_rg_x', 'v_lru_lambda', 'v_sc_conv_w', 'v_w_proj_a', 'v_w_proj_b', 'v_w_out', 'v_g_ffn', 'v_w_up', 'v_ffn_conv_w', 'v_ffn_conv_b', 'v_w_down', 'v_w_ple_gate', 'v_w_ple_proj', 'v_g_ple', 'v_g_final']
TWIN_OUTPUTS = ['loss', 'grad_x', 'grad_g_mix', 'grad_w_in', 'grad_rnn_conv_w', 'grad_rnn_conv_b', 'grad_w_rg_a', 'grad_b_rg_a', 'grad_w_rg_x', 'grad_b_rg_x', 'grad_lru_lambda', 'grad_sc_conv_w', 'grad_w_proj_a', 'grad_w_proj_b', 'grad_w_out', 'grad_g_ffn', 'grad_w_up', 'grad_ffn_conv_w', 'grad_ffn_conv_b', 'grad_w_down', 'grad_w_ple_gate', 'grad_w_ple_proj', 'grad_g_ple', 'grad_g_final', 'delta_g_mix', 'delta_w_in', 'delta_rnn_conv_w', 'delta_rnn_conv_b', 'delta_w_rg_a', 'delta_b_rg_a', 'delta_w_rg_x', 'delta_b_rg_x', 'delta_lru_lambda', 'delta_sc_conv_w', 'delta_w_proj_a', 'delta_w_proj_b', 'delta_w_out', 'delta_g_ffn', 'delta_w_up', 'delta_ffn_conv_w', 'delta_ffn_conv_b', 'delta_w_down', 'delta_w_ple_gate', 'delta_w_ple_proj', 'delta_g_ple', 'delta_g_final', 'new_m_g_mix', 'new_m_w_in', 'new_m_rnn_conv_w', 'new_m_rnn_conv_b', 'new_m_w_rg_a', 'new_m_b_rg_a', 'new_m_w_rg_x', 'new_m_b_rg_x', 'new_m_lru_lambda', 'new_m_sc_conv_w', 'new_m_w_proj_a', 'new_m_w_proj_b', 'new_m_w_out', 'new_m_g_ffn', 'new_m_w_up', 'new_m_ffn_conv_w', 'new_m_ffn_conv_b', 'new_m_w_down', 'new_m_w_ple_gate', 'new_m_w_ple_proj', 'new_m_g_ple', 'new_m_g_final', 'new_v_g_mix', 'new_v_w_in', 'new_v_rnn_conv_w', 'new_v_rnn_conv_b', 'new_v_w_rg_a', 'new_v_b_rg_a', 'new_v_w_rg_x', 'new_v_b_rg_x', 'new_v_lru_lambda', 'new_v_sc_conv_w', 'new_v_w_proj_a', 'new_v_w_proj_b', 'new_v_w_out', 'new_v_g_ffn', 'new_v_w_up', 'new_v_ffn_conv_w', 'new_v_ffn_conv_b', 'new_v_w_down', 'new_v_w_ple_gate', 'new_v_w_ple_proj', 'new_v_g_ple', 'new_v_g_final']
TWIN_LEAF_KINDS = {'loss': 'loss', 'grad_x': 'grad_x', 'grad_g_mix': 'grad_w', 'grad_w_in': 'grad_w', 'grad_rnn_conv_w': 'grad_w', 'grad_rnn_conv_b': 'grad_w', 'grad_w_rg_a': 'grad_w', 'grad_b_rg_a': 'grad_w', 'grad_w_rg_x': 'grad_w', 'grad_b_rg_x': 'grad_w', 'grad_lru_lambda': 'grad_w', 'grad_sc_conv_w': 'grad_w', 'grad_w_proj_a': 'grad_w', 'grad_w_proj_b': 'grad_w', 'grad_w_out': 'grad_w', 'grad_g_ffn': 'grad_w', 'grad_w_up': 'grad_w', 'grad_ffn_conv_w': 'grad_w', 'grad_ffn_conv_b': 'grad_w', 'grad_w_down': 'grad_w', 'grad_w_ple_gate': 'grad_w', 'grad_w_ple_proj': 'grad_w', 'grad_g_ple': 'grad_w', 'grad_g_final': 'grad_w', 'delta_g_mix': 'delta_w', 'delta_w_in': 'delta_w', 'delta_rnn_conv_w': 'delta_w', 'delta_rnn_conv_b': 'delta_w', 'delta_w_rg_a': 'delta_w', 'delta_b_rg_a': 'delta_w', 'delta_w_rg_x': 'delta_w', 'delta_b_rg_x': 'delta_w', 'delta_lru_lambda': 'delta_w', 'delta_sc_conv_w': 'delta_w', 'delta_w_proj_a': 'delta_w', 'delta_w_proj_b': 'delta_w', 'delta_w_out': 'delta_w', 'delta_g_ffn': 'delta_w', 'delta_w_up': 'delta_w', 'delta_ffn_conv_w': 'delta_w', 'delta_ffn_conv_b': 'delta_w', 'delta_w_down': 'delta_w', 'delta_w_ple_gate': 'delta_w', 'delta_w_ple_proj': 'delta_w', 'delta_g_ple': 'delta_w', 'delta_g_final': 'delta_w', 'new_m_g_mix': 'new_m', 'new_m_w_in': 'new_m', 'new_m_rnn_conv_w': 'new_m', 'new_m_rnn_conv_b': 'new_m', 'new_m_w_rg_a': 'new_m', 'new_m_b_rg_a': 'new_m', 'new_m_w_rg_x': 'new_m', 'new_m_b_rg_x': 'new_m', 'new_m_lru_lambda': 'new_m', 'new_m_sc_conv_w': 'new_m', 'new_m_w_proj_a': 'new_m', 'new_m_w_proj_b': 'new_m', 'new_m_w_out': 'new_m', 'new_m_g_ffn': 'new_m', 'new_m_w_up': 'new_m', 'new_m_ffn_conv_w': 'new_m', 'new_m_ffn_conv_b': 'new_m', 'new_m_w_down': 'new_m', 'new_m_w_ple_gate': 'new_m', 'new_m_w_ple_proj': 'new_m', 'new_m_g_ple': 'new_m', 'new_m_g_final': 'new_m', 'new_v_g_mix': 'new_v', 'new_v_w_in': 'new_v', 'new_v_rnn_conv_w': 'new_v', 'new_v_rnn_conv_b': 'new_v', 'new_v_w_rg_a': 'new_v', 'new_v_b_rg_a': 'new_v', 'new_v_w_rg_x': 'new_v', 'new_v_b_rg_x': 'new_v', 'new_v_lru_lambda': 'new_v', 'new_v_sc_conv_w': 'new_v', 'new_v_w_proj_a': 'new_v', 'new_v_w_proj_b': 'new_v', 'new_v_w_out': 'new_v', 'new_v_g_ffn': 'new_v', 'new_v_w_up': 'new_v', 'new_v_ffn_conv_w': 'new_v', 'new_v_ffn_conv_b': 'new_v', 'new_v_w_down': 'new_v', 'new_v_w_ple_gate': 'new_v', 'new_v_w_ple_proj': 'new_v', 'new_v_g_ple': 'new_v', 'new_v_g_final': 'new_v'}


def _forward(args):
    return _fwd_reference(*[args[k] for k in FWD_PARAMS])


def _output_shape():
    out = _jax.eval_shape(lambda: _forward(_fwd_setup_inputs(0)))
    return out.shape, out.dtype

N_MICROBATCH = 1
ADAM_LR = 0.001
ADAM_B1 = 0.9
ADAM_B2 = 0.999
ADAM_EPS = 1e-08
ADAM_WD = 0.01
ADAM_STEP = 10
PER_EXAMPLE_BATCH_AXIS = {'x': 0, 'p': 1, 'loss_target': 0}
SHARED_INPUTS = []
_WEIGHT_DTYPES = {'g_mix': _jnp.float32, 'w_in': _jnp.float32, 'rnn_conv_w': _jnp.float32, 'rnn_conv_b': _jnp.float32, 'w_rg_a': _jnp.float32, 'b_rg_a': _jnp.float32, 'w_rg_x': _jnp.float32, 'b_rg_x': _jnp.float32, 'lru_lambda': _jnp.float32, 'sc_conv_w': _jnp.float32, 'w_proj_a': _jnp.float32, 'w_proj_b': _jnp.float32, 'w_out': _jnp.float32, 'g_ffn': _jnp.float32, 'w_up': _jnp.float32, 'ffn_conv_w': _jnp.float32, 'ffn_conv_b': _jnp.float32, 'w_down': _jnp.float32, 'w_ple_gate': _jnp.float32, 'w_ple_proj': _jnp.float32, 'g_ple': _jnp.float32, 'g_final': _jnp.float32}
MOMENT_SCALE = {'g_mix': 1.619333e-01, 'w_in': 6.164225e-02, 'rnn_conv_w': 3.750942e-02, 'rnn_conv_b': 4.898748e-01, 'w_rg_a': 1.317262e-02, 'b_rg_a': 9.984998e-03, 'w_rg_x': 2.382173e-02, 'b_rg_x': 1.268516e-02, 'lru_lambda': 2.026121e-02, 'sc_conv_w': 8.654370e-02, 'w_proj_a': 3.894367e-02, 'w_proj_b': 8.658170e-02, 'w_out': 9.461731e-02, 'g_ffn': 1.278919e-01, 'w_up': 4.783502e-02, 'ffn_conv_w': 4.852342e-02, 'ffn_conv_b': 4.772311e-02, 'w_down': 8.148443e-02, 'w_ple_gate': 3.204614e-02, 'w_ple_proj': 6.965927e-02, 'g_ple': 1.360982e-01, 'g_final': 3.192383e+01}


def _to_microbatches(a, axis):
    t = _jnp.moveaxis(a, axis, 0)
    t = t.reshape((N_MICROBATCH, t.shape[0] // N_MICROBATCH) + t.shape[1:])
    return _jnp.moveaxis(t, 1, axis + 1)


def setup_inputs(seed: int = 0) -> dict:
    inp = _fwd_setup_inputs(seed)
    key = _jax.random.fold_in(_jax.random.key(seed), 7919)
    shape, _ = _output_shape()
    out = dict(inp)
    out["loss_target"] = _jax.random.normal(_jax.random.fold_in(key, 0), shape, _jnp.float32)
    for i, name in enumerate(TWIN_WEIGHTS):
        w = inp[name].astype(_jnp.float32)
        if MOMENT_SCALE is None:
            s = _jnp.sqrt(_jnp.mean(_jnp.square(w)) + 1e-30)
        else:
            s = MOMENT_SCALE[name]
        km, kv = _jax.random.split(_jax.random.fold_in(key, i + 1))
        out[name] = w
        out["m_" + name] = s * _jax.random.normal(km, w.shape, _jnp.float32)
        out["v_" + name] = (s * s) * _jax.random.uniform(kv, w.shape, _jnp.float32, 0.5, 1.5)
    if N_MICROBATCH > 1:
        for name, axis in PER_EXAMPLE_BATCH_AXIS.items():
            out[name] = _to_microbatches(out[name], axis)
    return {'x': out['x'], 'p': out['p'], 'g_mix': out['g_mix'], 'w_in': out['w_in'], 'rnn_conv_w': out['rnn_conv_w'], 'rnn_conv_b': out['rnn_conv_b'], 'w_rg_a': out['w_rg_a'], 'b_rg_a': out['b_rg_a'], 'w_rg_x': out['w_rg_x'], 'b_rg_x': out['b_rg_x'], 'lru_lambda': out['lru_lambda'], 'sc_conv_w': out['sc_conv_w'], 'w_proj_a': out['w_proj_a'], 'w_proj_b': out['w_proj_b'], 'w_out': out['w_out'], 'g_ffn': out['g_ffn'], 'w_up': out['w_up'], 'ffn_conv_w': out['ffn_conv_w'], 'ffn_conv_b': out['ffn_conv_b'], 'w_down': out['w_down'], 'w_ple_gate': out['w_ple_gate'], 'w_ple_proj': out['w_ple_proj'], 'g_ple': out['g_ple'], 'g_final': out['g_final'], 'loss_target': out['loss_target'], 'm_g_mix': out['m_g_mix'], 'm_w_in': out['m_w_in'], 'm_rnn_conv_w': out['m_rnn_conv_w'], 'm_rnn_conv_b': out['m_rnn_conv_b'], 'm_w_rg_a': out['m_w_rg_a'], 'm_b_rg_a': out['m_b_rg_a'], 'm_w_rg_x': out['m_w_rg_x'], 'm_b_rg_x': out['m_b_rg_x'], 'm_lru_lambda': out['m_lru_lambda'], 'm_sc_conv_w': out['m_sc_conv_w'], 'm_w_proj_a': out['m_w_proj_a'], 'm_w_proj_b': out['m_w_proj_b'], 'm_w_out': out['m_w_out'], 'm_g_ffn': out['m_g_ffn'], 'm_w_up': out['m_w_up'], 'm_ffn_conv_w': out['m_ffn_conv_w'], 'm_ffn_conv_b': out['m_ffn_conv_b'], 'm_w_down': out['m_w_down'], 'm_w_ple_gate': out['m_w_ple_gate'], 'm_w_ple_proj': out['m_w_ple_proj'], 'm_g_ple': out['m_g_ple'], 'm_g_final': out['m_g_final'], 'v_g_mix': out['v_g_mix'], 'v_w_in': out['v_w_in'], 'v_rnn_conv_w': out['v_rnn_conv_w'], 'v_rnn_conv_b': out['v_rnn_conv_b'], 'v_w_rg_a': out['v_w_rg_a'], 'v_b_rg_a': out['v_b_rg_a'], 'v_w_rg_x': out['v_w_rg_x'], 'v_b_rg_x': out['v_b_rg_x'], 'v_lru_lambda': out['v_lru_lambda'], 'v_sc_conv_w': out['v_sc_conv_w'], 'v_w_proj_a': out['v_w_proj_a'], 'v_w_proj_b': out['v_w_proj_b'], 'v_w_out': out['v_w_out'], 'v_g_ffn': out['v_g_ffn'], 'v_w_up': out['v_w_up'], 'v_ffn_conv_w': out['v_ffn_conv_w'], 'v_ffn_conv_b': out['v_ffn_conv_b'], 'v_w_down': out['v_w_down'], 'v_w_ple_gate': out['v_w_ple_gate'], 'v_w_ple_proj': out['v_w_ple_proj'], 'v_g_ple': out['v_g_ple'], 'v_g_final': out['v_g_final']}


def _loss(weights, diff, rest, loss_target):
    with _jax.named_scope("forward"):
        args = {**rest, TWIN_DIFF_INPUT: diff, **{k: w.astype(_WEIGHT_DTYPES[k]) for k, w in weights.items()}}
        y = _forward(args)
    with _jax.named_scope("loss_head"):
        err = _jnp.square(y.astype(_jnp.float32) - loss_target)
        return 0.5 * _jnp.sum(_jnp.mean(err, axis=-1)) if err.ndim else 0.5 * err


def _adamw(w, g, m, v):
    m = ADAM_B1 * m + (1.0 - ADAM_B1) * g
    v = ADAM_B2 * v + (1.0 - ADAM_B2) * _jnp.square(g)
    m_hat = m / (1.0 - ADAM_B1 ** ADAM_STEP)
    v_hat = v / (1.0 - ADAM_B2 ** ADAM_STEP)
    delta = -ADAM_LR * (m_hat / (_jnp.sqrt(v_hat) + ADAM_EPS) + ADAM_WD * w)
    return delta, m, v


def reference(x, p, g_mix, w_in, rnn_conv_w, rnn_conv_b, w_rg_a, b_rg_a, w_rg_x, b_rg_x, lru_lambda, sc_conv_w, w_proj_a, w_proj_b, w_out, g_ffn, w_up, ffn_conv_w, ffn_conv_b, w_down, w_ple_gate, w_ple_proj, g_ple, g_final, loss_target, m_g_mix, m_w_in, m_rnn_conv_w, m_rnn_conv_b, m_w_rg_a, m_b_rg_a, m_w_rg_x, m_b_rg_x, m_lru_lambda, m_sc_conv_w, m_w_proj_a, m_w_proj_b, m_w_out, m_g_ffn, m_w_up, m_ffn_conv_w, m_ffn_conv_b, m_w_down, m_w_ple_gate, m_w_ple_proj, m_g_ple, m_g_final, v_g_mix, v_w_in, v_rnn_conv_w, v_rnn_conv_b, v_w_rg_a, v_b_rg_a, v_w_rg_x, v_b_rg_x, v_lru_lambda, v_sc_conv_w, v_w_proj_a, v_w_proj_b, v_w_out, v_g_ffn, v_w_up, v_ffn_conv_w, v_ffn_conv_b, v_w_down, v_w_ple_gate, v_w_ple_proj, v_g_ple, v_g_final):
    given = dict(x=x, p=p, g_mix=g_mix, w_in=w_in, rnn_conv_w=rnn_conv_w, rnn_conv_b=rnn_conv_b, w_rg_a=w_rg_a, b_rg_a=b_rg_a, w_rg_x=w_rg_x, b_rg_x=b_rg_x, lru_lambda=lru_lambda, sc_conv_w=sc_conv_w, w_proj_a=w_proj_a, w_proj_b=w_proj_b, w_out=w_out, g_ffn=g_ffn, w_up=w_up, ffn_conv_w=ffn_conv_w, ffn_conv_b=ffn_conv_b, w_down=w_down, w_ple_gate=w_ple_gate, w_ple_proj=w_ple_proj, g_ple=g_ple, g_final=g_final, loss_target=loss_target, m_g_mix=m_g_mix, m_w_in=m_w_in, m_rnn_conv_w=m_rnn_conv_w, m_rnn_conv_b=m_rnn_conv_b, m_w_rg_a=m_w_rg_a, m_b_rg_a=m_b_rg_a, m_w_rg_x=m_w_rg_x, m_b_rg_x=m_b_rg_x, m_lru_lambda=m_lru_lambda, m_sc_conv_w=m_sc_conv_w, m_w_proj_a=m_w_proj_a, m_w_proj_b=m_w_proj_b, m_w_out=m_w_out, m_g_ffn=m_g_ffn, m_w_up=m_w_up, m_ffn_conv_w=m_ffn_conv_w, m_ffn_conv_b=m_ffn_conv_b, m_w_down=m_w_down, m_w_ple_gate=m_w_ple_gate, m_w_ple_proj=m_w_ple_proj, m_g_ple=m_g_ple, m_g_final=m_g_final, v_g_mix=v_g_mix, v_w_in=v_w_in, v_rnn_conv_w=v_rnn_conv_w, v_rnn_conv_b=v_rnn_conv_b, v_w_rg_a=v_w_rg_a, v_b_rg_a=v_b_rg_a, v_w_rg_x=v_w_rg_x, v_b_rg_x=v_b_rg_x, v_lru_lambda=v_lru_lambda, v_sc_conv_w=v_sc_conv_w, v_w_proj_a=v_w_proj_a, v_w_proj_b=v_w_proj_b, v_w_out=v_w_out, v_g_ffn=v_g_ffn, v_w_up=v_w_up, v_ffn_conv_w=v_ffn_conv_w, v_ffn_conv_b=v_ffn_conv_b, v_w_down=v_w_down, v_w_ple_gate=v_w_ple_gate, v_w_ple_proj=v_w_ple_proj, v_g_ple=v_g_ple, v_g_final=v_g_final)
    weights = {n: given[n] for n in TWIN_WEIGHTS}
    shared = {n: given[n] for n in SHARED_INPUTS}
    per_example = {n: given[n] for n in ['x', 'p']}
    grad_fn = _jax.value_and_grad(_loss, argnums=(0, 1))

    def one_microbatch(ex, loss_target):
        ex = dict(ex)
        diff = ex.pop(TWIN_DIFF_INPUT)
        return grad_fn(weights, diff, {**shared, **ex}, loss_target)

    if N_MICROBATCH == 1:
        loss, (grad_w, grad_x) = one_microbatch(per_example, given["loss_target"])
    else:
        def body(carry, xs):
            loss_sum, grad_sum = carry
            l_k, (gw_k, gx_k) = one_microbatch(xs[0], xs[1])
            with _jax.named_scope("update"):
                return (loss_sum + l_k, _jax.tree.map(_jnp.add, grad_sum, gw_k)), gx_k

        init = (_jnp.zeros((), _jnp.float32), _jax.tree.map(_jnp.zeros_like, weights))
        (loss, grad_w), grad_x = _jax.lax.scan(body, init, (per_example, given["loss_target"]))
    with _jax.named_scope("update"):
        delta_w, new_m, new_v = {}, {}, {}
        for n in TWIN_WEIGHTS:
            delta_w[n], new_m[n], new_v[n] = _adamw(weights[n], grad_w[n], given["m_" + n], given["v_" + n])
    return (loss, grad_x, *[grad_w[n] for n in TWIN_WEIGHTS], *[delta_w[n] for n in TWIN_WEIGHTS],
            *[new_m[n] for n in TWIN_WEIGHTS], *[new_v[n] for n in TWIN_WEIGHTS])
```

```python
import functools

import jax
import jax.numpy as jnp
from jax import lax
from jax.experimental import pallas as pl
from jax.experimental.pallas import tpu as pltpu

F32 = jnp.float32
BF16 = jnp.bfloat16
MESH_ID = pl.DeviceIdType.MESH

N_DEV = 8
D_MODEL = 1024
D_PLE = 256
RNN_BW = 128
N_SEG = 7
D_FF = 3072
LRU_C = 8.0
EPS = 1e-6
ADAM_LR = 0.001
ADAM_B1 = 0.9
ADAM_B2 = 0.999
ADAM_EPS = 1e-08
ADAM_WD = 0.01
ADAM_STEP = 10

HALO = 16
VMEM_LIMIT = 56 * 1024 * 1024
PACK_ROWS = 320
PIECE = PACK_ROWS // N_DEV


def _dot(a, b):
    return jnp.dot(a, b, preferred_element_type=F32)


def _dot_nt(a, b):
    return lax.dot_general(a, b, (((1,), (1,)), ((), ())), preferred_element_type=F32)


def _dot_tn(a, b):
    return lax.dot_general(a, b, (((0,), (0,)), ((), ())), preferred_element_type=F32)


def _sigmoid(x):
    return jax.nn.sigmoid(x)


_GELU_C = 0.7978845608028654
_GELU_K = 0.044715


def _gelu(x):
    return 0.5 * x * (1.0 + jnp.tanh(_GELU_C * (x + _GELU_K * (x * x * x))))


def _gelu_and_grad(x):
    x2 = x * x
    t = jnp.tanh(_GELU_C * (x + _GELU_K * (x * x2)))
    g = 0.5 * x * (1.0 + t)
    dg = 0.5 * (1.0 + t) + 0.5 * x * (1.0 - t * t) * (_GELU_C * (1.0 + 3.0 * _GELU_K * x2))
    return g, dg


def _expm1(x):
    poly = x * (1.0 + x * (0.5 + x * (1.0 / 6.0 + x * (1.0 / 24.0 + x * (1.0 / 120.0)))))
    return jnp.where(jnp.abs(x) < 0.08, poly, jnp.exp(x) - 1.0)


def _softplus(x):
    return jnp.maximum(x, 0.0) + jnp.log1p(jnp.exp(-jnp.abs(x)))


def _rms(u):
    r = lax.rsqrt(jnp.mean(u * u, axis=-1, keepdims=True) + EPS)
    return u * r, r


def _rms_bwd(dn, un, r):
    return r * (dn - un * jnp.mean(dn * un, axis=-1, keepdims=True))


def _shift_dn(ext, s, n):
    if s == 0:
        return ext[HALO:HALO + n]
    return pltpu.roll(ext, s, 0)[HALO:HALO + n]


def _shift_up(ext, s, n):
    if s == 0:
        return ext[0:n]
    return pltpu.roll(ext, n + HALO - s, 0)[0:n]


def _ext_before(ref, c, ch):
    t0 = c * ch
    prev = pl.multiple_of(jnp.maximum(t0 - HALO, 0), HALO)
    halo = jnp.where(c > 0, ref[pl.ds(prev, HALO), :].astype(F32), 0.0)
    cur = ref[pl.ds(pl.multiple_of(t0, HALO), ch), :].astype(F32)
    return jnp.concatenate([halo, cur], axis=0)


def _ext_after(ref, c, ch, n_chunks):
    t0 = c * ch
    nxt = pl.multiple_of(jnp.minimum(t0 + ch, (n_chunks - 1) * ch + ch - HALO), HALO)
    halo = jnp.where(c < n_chunks - 1, ref[pl.ds(nxt, HALO), :].astype(F32), 0.0)
    cur = ref[pl.ds(pl.multiple_of(t0, HALO), ch), :].astype(F32)
    return jnp.concatenate([cur, halo], axis=0)


def _rows(ref, c, ch):
    return ref[pl.ds(pl.multiple_of(c * ch, HALO), ch), :]


def _scan_fwd(a_ref, b_ref, h_ref, n_rows, cw):
    rows = lax.broadcasted_iota(jnp.int32, (8, cw), 0)

    def body(g, carry):
        t = pl.multiple_of(g * 8, 8)
        a = a_ref[pl.ds(t, 8), :]
        b = b_ref[pl.ds(t, 8), :]
        for d in (1, 2, 4):
            m = rows >= d
            b = jnp.where(m, a * pltpu.roll(b, d, 0) + b, b)
            a = jnp.where(m, a * pltpu.roll(a, d, 0), a)
        h = a * carry + b
        h_ref[pl.ds(t, 8), :] = h
        return jnp.broadcast_to(h[7:8, :], (8, cw))

    lax.fori_loop(0, n_rows // 8, body, jnp.zeros((8, cw), F32), unroll=2)


def _scan_rev(a_ref, g_ref, n_rows, cw):
    rows = lax.broadcasted_iota(jnp.int32, (8, cw), 0)
    n_groups = n_rows // 8

    def body(i, carry):
        dh_next, a_next = carry
        t = pl.multiple_of((n_groups - 1 - i) * 8, 8)
        a = a_ref[pl.ds(t, 8), :]
        g = g_ref[pl.ds(t, 8), :]
        an = jnp.where(rows < 7, pltpu.roll(a, 7, 0), a_next)
        for d in (1, 2, 4):
            m = rows < 8 - d
            g = jnp.where(m, an * pltpu.roll(g, 8 - d, 0) + g, g)
            an = jnp.where(m, an * pltpu.roll(an, 8 - d, 0), an)
        dh = an * dh_next + g
        g_ref[pl.ds(t, 8), :] = dh
        return (jnp.broadcast_to(dh[0:1, :], (8, cw)), jnp.broadcast_to(a[0:1, :], (8, cw)))

    zero = jnp.zeros((8, cw), F32)
    lax.fori_loop(0, n_groups, body, (zero, zero), unroll=2)


def _params(*sem):
    return pltpu.CompilerParams(dimension_semantics=sem, vmem_limit_bytes=VMEM_LIMIT)


def _sds(shape, dtype):
    return jax.ShapeDtypeStruct(shape, dtype)


def _all_gather(shards):
    n = len(shards)

    def body(*refs):
        ins, outs = refs[:n], refs[n:2 * n]
        send_sems, recv_sems, local_sems = refs[2 * n:]
        x, y, c = lax.axis_index("x"), lax.axis_index("y"), lax.axis_index("c")
        me, sibling = (x, y, c), (x, y, 1 - c)
        chips = [(1 - x, y), (x, 1 - y), (1 - x, 1 - y)]

        def slot(t, dev):
            return outs[t].at[4 * dev[0] + 2 * dev[1] + dev[2]]

        def copy(t, k, block, to, src=None):
            return pltpu.make_async_remote_copy(
                src_ref=slot(t, block) if src is None else src, dst_ref=slot(t, block),
                send_sem=send_sems.at[t, k], recv_sem=recv_sems.at[t, k],
                device_id=to, device_id_type=MESH_ID)

        mine = [pltpu.make_async_copy(ins[t], slot(t, me), local_sems.at[t]) for t in range(n)]
        for cp in mine:
            cp.start()
        first = []
        for t in range(n):
            first.append(copy(t, 0, me, sibling, src=ins[t]))
            first += [copy(t, 1 + j, me, (*chip, c), src=ins[t]) for j, chip in enumerate(chips)]
        for cp in first:
            cp.start()
        passed = []
        for t in range(n):
            for j, chip in enumerate(chips):
                copy(t, 1 + j, (*chip, c), me).wait_recv()
                cp = copy(t, 4 + j, (*chip, c), sibling)
                cp.start()
                passed.append(cp)
        for t in range(n):
            copy(t, 0, sibling, me).wait_recv()
            for j, chip in enumerate(chips):
                copy(t, 4 + j, (*chip, 1 - c), me).wait_recv()
        for cp in first + passed:
            cp.wait_send()
        for cp in mine:
            cp.wait()

    any_spec = pl.BlockSpec(memory_space=pl.ANY)
    return pl.pallas_call(
        body, name="weights_all_gather",
        out_shape=[_sds((N_DEV,) + s.shape, s.dtype) for s in shards],
        in_specs=[any_spec] * n, out_specs=[any_spec] * n,
        scratch_shapes=[pltpu.SemaphoreType.DMA((n, 7)), pltpu.SemaphoreType.DMA((n, 7)),
                        pltpu.SemaphoreType.DMA((n,))],
    )(*shards)


def _peer(d):
    x, y, c = lax.axis_index("x"), lax.axis_index("y"), lax.axis_index("c")
    px = 1 - x if d & 4 else x
    py = 1 - y if d & 2 else y
    pc = 1 - c if d & 1 else c
    return (px, py, pc), 4 * px + 2 * py + pc


def _grad_exchange(parts):
    n = len(parts)

    def body(*refs):
        ins, outs = refs[:n], refs[n:2 * n]
        send_sems, recv_sems, local_sems = refs[2 * n:]
        _, me = _peer(0)

        def copy(t, d):
            peer, idx = _peer(d)
            return pltpu.make_async_remote_copy(
                src_ref=ins[t].at[idx], dst_ref=outs[t].at[d],
                send_sem=send_sems.at[t, d - 1], recv_sem=recv_sems.at[t, d - 1],
                device_id=peer, device_id_type=MESH_ID)

        mine = [pltpu.make_async_copy(ins[t].at[me], outs[t].at[0], local_sems.at[t]) for t in range(n)]
        for cp in mine:
            cp.start()
        copies = [copy(t, d) for t in range(n) for d in range(1, N_DEV)]
        for cp in copies:
            cp.start()
        for cp in copies:
            cp.wait_recv()
        for cp in copies:
            cp.wait_send()
        for cp in mine:
            cp.wait()

    any_spec = pl.BlockSpec(memory_space=pl.ANY)
    return pl.pallas_call(
        body, name="grad_exchange",
        out_shape=[_sds(p.shape, p.dtype) for p in parts],
        in_specs=[any_spec] * n, out_specs=[any_spec] * n,
        scratch_shapes=[pltpu.SemaphoreType.DMA((n, 7)), pltpu.SemaphoreType.DMA((n, 7)),
                        pltpu.SemaphoreType.DMA((n,))],
    )(*parts)


def _small_all_reduce(pack):
    def body(p_ref, o_ref, land, s1, r1, s2, r2):
        _, me = _peer(0)

        def piece(ref, idx):
            return ref.at[pl.ds(pl.multiple_of(idx * PIECE, 8), PIECE), :]

        def scatter(d):
            peer, idx = _peer(d)
            return pltpu.make_async_remote_copy(
                src_ref=piece(p_ref, idx), dst_ref=land.at[d - 1], send_sem=s1.at[d - 1], recv_sem=r1.at[d - 1],
                device_id=peer, device_id_type=MESH_ID)

        def gather(d, from_idx):
            peer, _ = _peer(d)
            return pltpu.make_async_remote_copy(
                src_ref=piece(o_ref, from_idx), dst_ref=piece(o_ref, from_idx), send_sem=s2.at[d - 1],
                recv_sem=r2.at[d - 1], device_id=peer, device_id_type=MESH_ID)

        first = [scatter(d) for d in range(1, N_DEV)]
        for cp in first:
            cp.start()
        acc = piece(p_ref, me)[...]
        for d in range(1, N_DEV):
            first[d - 1].wait_recv()
            acc = acc + land[d - 1]
        piece(o_ref, me)[...] = acc
        second = [gather(d, me) for d in range(1, N_DEV)]
        for cp in second:
            cp.start()
        for d in range(1, N_DEV):
            gather(d, _peer(d)[1]).wait_recv()
        for cp in first + second:
            cp.wait_send()

    vmem = pl.BlockSpec(memory_space=pltpu.VMEM)
    return pl.pallas_call(
        body, name="small_all_reduce",
        out_shape=_sds(pack.shape, F32), in_specs=[vmem], out_specs=vmem,
        scratch_shapes=[pltpu.VMEM((N_DEV - 1, PIECE, D_MODEL), F32)] + [pltpu.SemaphoreType.DMA((N_DEV - 1,))] * 4,
    )(pack)


def _norm_in_proj(x, g_mix, w_in, tm):
    s = x.shape[0]

    def body(x_ref, g_ref, w_ref, z_ref, h_ref):
        @pl.when(pl.program_id(1) == 0)
        def _():
            xn, _ = _rms(x_ref[...])
            h_ref[...] = (xn * g_ref[...]).astype(BF16)

        z_ref[...] = _dot(h_ref[...], w_ref[...]).astype(BF16)

    return pl.pallas_call(
        body, name="norm_in_proj", grid=(s // tm, N_SEG),
        in_specs=[pl.BlockSpec((tm, D_MODEL), lambda i, j: (i, 0)),
                  pl.BlockSpec((1, D_MODEL), lambda i, j: (0, 0)),
                  pl.BlockSpec((D_MODEL, D_MODEL), lambda i, j: (0, j))],
        out_specs=[pl.BlockSpec((tm, D_MODEL), lambda i, j: (i, j)),
                   pl.BlockSpec((tm, D_MODEL), lambda i, j: (i, 0))],
        out_shape=[_sds((s, N_SEG * D_MODEL), BF16), _sds((s, D_MODEL), BF16)],
        compiler_params=_params("parallel", "arbitrary"),
    )(x, g_mix, w_in)


def _gate_chunk(xr_ext, w4_ref, cb_ref, wa, wx, ba_ref, bx_ref, sp, ch):
    xc = cb_ref[...] + w4_ref[3:4, :] * _shift_dn(xr_ext, 0, ch)
    for k in range(3):
        xc = xc + w4_ref[k:k + 1, :] * _shift_dn(xr_ext, 3 - k, ch)
    xcb = xc.astype(BF16)
    ra = _sigmoid(_dot(xcb, wa) + ba_ref[...])
    ia = _sigmoid(_dot(xcb, wx) + bx_ref[...])
    log_a = (-LRU_C) * ra * sp
    return xc, ra, ia, log_a


def _conv3(q_ext, w3_ref, ch):
    y = w3_ref[2:3, :] * _shift_dn(q_ext, 0, ch)
    for k in range(2):
        y = y + w3_ref[k:k + 1, :] * _shift_dn(q_ext, 2 - k, ch)
    return y


def _z_block_specs(s, cw, segs):
    nb = D_MODEL // cw
    return [pl.BlockSpec((s, cw), functools.partial(lambda j, seg: (0, seg * nb + j), seg=seg)) for seg in segs]


def _mixers_fwd(z, w4, b4, w_a, b_a, w_x, b_x, lam, w3, ch):
    s = z.shape[0]
    cw = RNN_BW
    n_chunks = s // ch

    def body(xr_ref, gr_ref, cb_ref, cc_ref, cx_ref, w4_ref, b4_ref, wa_ref, ba_ref, wx_ref, bx_ref, lam_ref, w3_ref,
             ya_ref, yb_ref, h_ref, a_scr, b_scr):
        sp = _softplus(-lam_ref[...])
        wa = wa_ref[0].astype(BF16)
        wx = wx_ref[0].astype(BF16)

        def gates(c, _):
            xc, _, ia, log_a = _gate_chunk(_ext_before(xr_ref, c, ch), w4_ref, b4_ref, wa, wx, ba_ref, bx_ref, sp, ch)
            rows = pl.ds(pl.multiple_of(c * ch, HALO), ch)
            a_scr[rows, :] = jnp.exp(log_a)
            b_scr[rows, :] = jnp.sqrt(-_expm1(2.0 * log_a)) * (ia * xc)
            q_ext = _ext_before(cc_ref, c, ch) * _ext_before(cx_ref, c, ch)
            yb_ref[rows, :] = (_rows(cb_ref, c, ch).astype(F32) * _conv3(q_ext, w3_ref, ch)).astype(BF16)
            return 0

        lax.fori_loop(0, n_chunks, gates, 0)
        _scan_fwd(a_scr, b_scr, h_ref, s, cw)

        def outputs(c, _):
            rows = pl.ds(pl.multiple_of(c * ch, HALO), ch)
            ya_ref[rows, :] = (h_ref[rows, :] * _gelu(gr_ref[rows, :].astype(F32))).astype(BF16)
            return 0

        lax.fori_loop(0, n_chunks, outputs, 0)

    col = lambda j: (0, j)
    vec = pl.BlockSpec((1, cw), col)
    sq = pl.BlockSpec((1, cw, cw), lambda j: (j, 0, 0))
    act = pl.BlockSpec((s, cw), col)
    return pl.pallas_call(
        body, name="mixers_fwd", grid=(D_MODEL // cw,),
        in_specs=_z_block_specs(s, cw, range(5)) + [pl.BlockSpec((4, cw), col), vec, sq, vec, sq, vec, vec,
                                                     pl.BlockSpec((3, cw), col)],
        out_specs=[act, act, act],
        out_shape=[_sds((s, D_MODEL), BF16), _sds((s, D_MODEL), BF16), _sds((s, D_MODEL), F32)],
        scratch_shapes=[pltpu.VMEM((s, cw), F32), pltpu.VMEM((s, cw), F32)],
        compiler_params=_params("parallel"),
    )(z, z, z, z, z, w4, b4, w_a, b_a, w_x, b_x, lam, w3)


def _merge_out(ya, yb, z, x, w_pa, w_pb, w_out, g_ffn, tm):
    s = x.shape[0]

    def body(ya_ref, yb_ref, ga_ref, gb_ref, x_ref, wa_ref, wb_ref, wo_ref, g_ref, pa_ref, pb_ref, m_ref, x2_ref, h2_ref):
        pa = _dot(ya_ref[...], wa_ref[...])
        pb = _dot(yb_ref[...], wb_ref[...])
        pa_ref[...] = pa.astype(BF16)
        pb_ref[...] = pb.astype(BF16)
        m = (_sigmoid(ga_ref[...].astype(F32)) * pa + _sigmoid(gb_ref[...].astype(F32)) * pb).astype(BF16)
        m_ref[...] = m
        x2 = x_ref[...] + _dot(m, wo_ref[...])
        x2_ref[...] = x2
        xn, _ = _rms(x2)
        h2_ref[...] = (xn * g_ref[...]).astype(BF16)

    tile = pl.BlockSpec((tm, D_MODEL), lambda i: (i, 0))
    full = pl.BlockSpec((D_MODEL, D_MODEL), lambda i: (0, 0))
    return pl.pallas_call(
        body, name="merge_out", grid=(s // tm,),
        in_specs=[tile, tile, pl.BlockSpec((tm, D_MODEL), lambda i: (i, 5)), pl.BlockSpec((tm, D_MODEL), lambda i: (i, 6)),
                  tile, full, full, full, pl.BlockSpec((1, D_MODEL), lambda i: (0, 0))],
        out_specs=[tile] * 5,
        out_shape=[_sds((s, D_MODEL), BF16)] * 3 + [_sds((s, D_MODEL), F32), _sds((s, D_MODEL), BF16)],
        compiler_params=_params("parallel"),
    )(ya, yb, z, z, x, w_pa, w_pb, w_out, g_ffn)


def _up_proj(h2, w_up_g, tm):
    s = h2.shape[0]
    bw = w_up_g.shape[2]
    per_half = N_DEV // 2

    def body(h_ref, w_ref, u_ref):
        u_ref[...] = _dot(h_ref[...], w_ref[...]).astype(BF16)

    return pl.pallas_call(
        body, name="up_proj", grid=(s // tm, N_DEV),
        in_specs=[pl.BlockSpec((tm, D_MODEL), lambda i, j: (i, 0)),
                  pl.BlockSpec((None, D_MODEL, bw), lambda i, j: (j, 0, 0))],
        out_specs=pl.BlockSpec((None, tm, bw), lambda i, j: (j // per_half, i, j % per_half)),
        out_shape=_sds((2, s, D_FF), BF16),
        compiler_params=_params("parallel", "arbitrary"),
    )(h2, w_up_g)


def _ffn_conv(u_ref, w_ref, b_ref, half, c, ch):
    ext = _ext_before(u_ref.at[half], c, ch)
    w = w_ref.at[half]
    return b_ref[half] + _conv3(ext, w, ch), ext


def _ffn_act(u0, wc, bc, cw, ch):
    s = u0.shape[1]
    n_chunks = s // ch

    def body(u_ref, w_ref, b_ref, f_ref):
        def chunk(c, _):
            ug, _ = _ffn_conv(u_ref, w_ref, b_ref, 0, c, ch)
            uv, _ = _ffn_conv(u_ref, w_ref, b_ref, 1, c, ch)
            f_ref[pl.ds(pl.multiple_of(c * ch, HALO), ch), :] = (_gelu(ug) * uv).astype(BF16)
            return 0

        lax.fori_loop(0, n_chunks, chunk, 0)

    return pl.pallas_call(
        body, name="ffn_act", grid=(D_FF // cw,),
        in_specs=[pl.BlockSpec((2, s, cw), lambda j: (0, 0, j)), pl.BlockSpec((2, 3, cw), lambda j: (0, 0, j)),
                  pl.BlockSpec((2, 1, cw), lambda j: (0, 0, j))],
        out_specs=pl.BlockSpec((s, cw), lambda j: (0, j)),
        out_shape=_sds((s, D_FF), BF16),
        compiler_params=_params("parallel"),
    )(u0, wc, bc)


def _head(x2, f, p, target, w_down, w_gate, w_pp, g_ple, g_final, tm):
    s = x2.shape[0]

    def body(x2_ref, f_ref, p_ref, t_ref, wd_ref, wg_ref, wp_ref, gp_ref, gf_ref,
             x3_ref, dgt_ref, de0_ref, dx3_ref, df_ref, sums_ref):
        @pl.when(pl.program_id(0) == 0)
        def _():
            sums_ref[...] = jnp.zeros_like(sums_ref)

        x3 = x2_ref[...] + _dot(f_ref[...], wd_ref[...])
        x3b = x3.astype(BF16)
        x3_ref[...] = x3b
        e0n, re = _rms(_dot(p_ref[...].astype(BF16), wp_ref[...]))
        e = e0n * gp_ref[...]
        sg = _sigmoid(_dot(x3b, wg_ref[...]))
        x4n, r4 = _rms(x3 + sg * e)
        diff = x4n * gf_ref[...] - t_ref[...]
        sums_ref[0:1, :] += jnp.sum(diff * diff, axis=0, keepdims=True)
        dy = diff * (1.0 / D_MODEL)
        sums_ref[1:2, :] += jnp.sum(dy * x4n, axis=0, keepdims=True)
        dx4 = _rms_bwd(dy * gf_ref[...], x4n, r4)
        de = dx4 * sg
        dgt = ((dx4 * e) * (sg * (1.0 - sg))).astype(BF16)
        dgt_ref[...] = dgt
        sums_ref[2:3, :] += jnp.sum(de * e0n, axis=0, keepdims=True)
        de0_ref[...] = _rms_bwd(de * gp_ref[...], e0n, re).astype(BF16)
        dx3 = dx4 + _dot_nt(dgt, wg_ref[...])
        dx3_ref[...] = dx3
        df_ref[...] = _dot_nt(dx3.astype(BF16), wd_ref[...]).astype(BF16)

    tile = pl.BlockSpec((tm, D_MODEL), lambda i: (i, 0))
    vec = pl.BlockSpec((1, D_MODEL), lambda i: (0, 0))
    wide = pl.BlockSpec((tm, D_FF), lambda i: (i, 0))
    return pl.pallas_call(
        body, name="loss_head", grid=(s // tm,),
        in_specs=[tile, wide, pl.BlockSpec((tm, D_PLE), lambda i: (i, 0)), tile,
                  pl.BlockSpec((D_FF, D_MODEL), lambda i: (0, 0)), pl.BlockSpec((D_MODEL, D_MODEL), lambda i: (0, 0)),
                  pl.BlockSpec((D_PLE, D_MODEL), lambda i: (0, 0)), vec, vec],
        out_specs=[tile, tile, tile, tile, wide, pl.BlockSpec((8, D_MODEL), lambda i: (0, 0))],
        out_shape=[_sds((s, D_MODEL), BF16)] * 3 + [_sds((s, D_MODEL), F32), _sds((s, D_FF), BF16),
                                                    _sds((8, D_MODEL), F32)],
        compiler_params=_params("arbitrary"),
    )(x2, f, p, target, w_down, w_gate, w_pp, g_ple, g_final)


def _ffn_act_bwd(u0, df, wc, bc, cw, ch):
    s = u0.shape[1]
    n_chunks = s // ch

    def body(u_ref, df_ref, w_ref, b_ref, du0_ref, dw_ref, db_ref, du_scr):
        dw_ref[...] = jnp.zeros_like(dw_ref)
        db_ref[...] = jnp.zeros_like(db_ref)

        def first(c, _):
            ug, ext_g = _ffn_conv(u_ref, w_ref, b_ref, 0, c, ch)
            uv, ext_v = _ffn_conv(u_ref, w_ref, b_ref, 1, c, ch)
            gel, dgel = _gelu_and_grad(ug)
            d = _rows(df_ref, c, ch).astype(F32)
            rows = pl.ds(pl.multiple_of(c * ch, HALO), ch)
            for half, du, ext in ((0, d * uv * dgel, ext_g), (1, d * gel, ext_v)):
                du_scr[half, rows, :] = du
                db_ref[half] += jnp.sum(du, axis=0, keepdims=True)
                for k in range(3):
                    dw_ref[half, k:k + 1, :] += jnp.sum(du * _shift_dn(ext, 2 - k, ch), axis=0, keepdims=True)
            return 0

        lax.fori_loop(0, n_chunks, first, 0)

        def second(c, _):
            rows = pl.ds(pl.multiple_of(c * ch, HALO), ch)
            for half in range(2):
                ext = _ext_after(du_scr.at[half], c, ch, n_chunks)
                w = w_ref.at[half]
                acc = w[2:3, :] * _shift_up(ext, 0, ch)
                for k in range(2):
                    acc = acc + w[k:k + 1, :] * _shift_up(ext, 2 - k, ch)
                du0_ref[half, rows, :] = acc.astype(BF16)
            return 0

        lax.fori_loop(0, n_chunks, second, 0)

    blk = lambda r: pl.BlockSpec((2, r, cw), lambda j: (0, 0, j))
    return pl.pallas_call(
        body, name="ffn_act_bwd", grid=(D_FF // cw,),
        in_specs=[blk(s), pl.BlockSpec((s, cw), lambda j: (0, j)), blk(3), blk(1)],
        out_specs=[blk(s), blk(3), blk(1)],
        out_shape=[_sds((2, s, D_FF), BF16), _sds((2, 3, D_FF), F32), _sds((2, 1, D_FF), F32)],
        scratch_shapes=[pltpu.VMEM((2, s, cw), F32)],
        compiler_params=_params("parallel"),
    )(u0, df, wc, bc)


def _up_bwd_merge_bwd(du0, w_up_g, x2, dx3, z, pa, pb, w_out, w_pa, w_pb, g_ffn, tm):
    s = x2.shape[0]
    bw = w_up_g.shape[2]
    per_half = N_DEV // 2

    def body(du_ref, wu_ref, x2_ref, dx3_ref, ga_ref, gb_ref, pa_ref, pb_ref, wo_ref, wa_ref, wb_ref, g_ref,
             dx2_ref, dzg_ref, dpa_ref, dpb_ref, dya_ref, dyb_ref, sums_ref, acc):
        k = pl.program_id(1)

        @pl.when((pl.program_id(0) == 0) & (k == 0))
        def _():
            sums_ref[...] = jnp.zeros_like(sums_ref)

        @pl.when(k == 0)
        def _():
            acc[...] = jnp.zeros_like(acc)

        acc[...] += _dot_nt(du_ref[...], wu_ref[...])

        @pl.when(k == N_DEV - 1)
        def _():
            dh2 = acc[...]
            x2n, r2 = _rms(x2_ref[...])
            sums_ref[0:1, :] += jnp.sum(dh2 * x2n, axis=0, keepdims=True)
            dx2 = dx3_ref[...] + _rms_bwd(dh2 * g_ref[...], x2n, r2)
            dx2_ref[...] = dx2
            dm = _dot_nt(dx2.astype(BF16), wo_ref[...])
            sa = _sigmoid(ga_ref[...].astype(F32))
            sb = _sigmoid(gb_ref[...].astype(F32))
            dzg_ref[0] = (dm * pa_ref[...].astype(F32) * (sa * (1.0 - sa))).astype(BF16)
            dzg_ref[1] = (dm * pb_ref[...].astype(F32) * (sb * (1.0 - sb))).astype(BF16)
            dpa = (dm * sa).astype(BF16)
            dpb = (dm * sb).astype(BF16)
            dpa_ref[...] = dpa
            dpb_ref[...] = dpb
            dya_ref[...] = _dot_nt(dpa, wa_ref[...]).astype(BF16)
            dyb_ref[...] = _dot_nt(dpb, wb_ref[...]).astype(BF16)

    tile = pl.BlockSpec((tm, D_MODEL), lambda i, k: (i, 0))
    full = pl.BlockSpec((D_MODEL, D_MODEL), lambda i, k: (0, 0))
    return pl.pallas_call(
        body, name="up_bwd_merge_bwd", grid=(s // tm, N_DEV),
        in_specs=[pl.BlockSpec((None, tm, bw), lambda i, k: (k // per_half, i, k % per_half)),
                  pl.BlockSpec((None, D_MODEL, bw), lambda i, k: (k, 0, 0)),
                  tile, tile, pl.BlockSpec((tm, D_MODEL), lambda i, k: (i, 5)),
                  pl.BlockSpec((tm, D_MODEL), lambda i, k: (i, 6)), tile, tile, full, full, full,
                  pl.BlockSpec((1, D_MODEL), lambda i, k: (0, 0))],
        out_specs=[tile, pl.BlockSpec((2, tm, D_MODEL), lambda i, k: (0, i, 0)), tile, tile, tile, tile,
                   pl.BlockSpec((8, D_MODEL), lambda i, k: (0, 0))],
        out_shape=[_sds((s, D_MODEL), F32), _sds((2, s, D_MODEL), BF16)] + [_sds((s, D_MODEL), BF16)] * 4
        + [_sds((8, D_MODEL), F32)],
        scratch_shapes=[pltpu.VMEM((tm, D_MODEL), F32)],
        compiler_params=_params("arbitrary", "arbitrary"),
    )(du0, w_up_g, x2, dx3, z, z, pa, pb, w_out, w_pa, w_pb, g_ffn)


def _mixers_bwd(z, h, dya, dyb, w4, b4, w_a, b_a, w_x, b_x, lam, w3, ch):
    s = z.shape[0]
    cw = RNN_BW
    n_chunks = s // ch

    def body(xr_ref, gr_ref, cb_ref, cc_ref, cx_ref, h_ref, dya_ref, dyb_ref, w4_ref, b4_ref, wa_ref, ba_ref, wx_ref,
             bx_ref, lam_ref, w3_ref,
             dz_ref, dw4_ref, db4_ref, dwa_ref, dba_ref, dwx_ref, dbx_ref, dlam_ref, dw3_ref,
             a_scr, g_scr, q_scr):
        sp = _softplus(-lam_ref[...])
        wa = wa_ref[0].astype(BF16)
        wx = wx_ref[0].astype(BF16)
        for ref in (dw4_ref, db4_ref, dwa_ref, dba_ref, dwx_ref, dbx_ref, dlam_ref, dw3_ref):
            ref[...] = jnp.zeros_like(ref)

        def gates(c):
            xr_ext = _ext_before(xr_ref, c, ch)
            return (xr_ext,) + _gate_chunk(xr_ext, w4_ref, b4_ref, wa, wx, ba_ref, bx_ref, sp, ch)

        def first(c, _):
            rows = pl.ds(pl.multiple_of(c * ch, HALO), ch)
            _, _, _, _, log_a = gates(c)
            a_scr[rows, :] = jnp.exp(log_a)
            gel, dgel = _gelu_and_grad(gr_ref[rows, :].astype(F32))
            dya_c = dya_ref[rows, :].astype(F32)
            g_scr[rows, :] = dya_c * gel
            dz_ref[1, rows, :] = (dya_c * h_ref[rows, :] * dgel).astype(BF16)
            q_ext = _ext_before(cc_ref, c, ch) * _ext_before(cx_ref, c, ch)
            dyb_c = dyb_ref[rows, :].astype(F32)
            dz_ref[2, rows, :] = (dyb_c * _conv3(q_ext, w3_ref, ch)).astype(BF16)
            dyq = dyb_c * cb_ref[rows, :].astype(F32)
            q_scr[rows, :] = dyq
            for k in range(3):
                dw3_ref[k:k + 1, :] += jnp.sum(dyq * _shift_dn(q_ext, 2 - k, ch), axis=0, keepdims=True)
            return 0

        lax.fori_loop(0, n_chunks, first, 0)
        _scan_rev(a_scr, g_scr, s, cw)

        def second(c, _):
            rows = pl.ds(pl.multiple_of(c * ch, HALO), ch)
            _, xc, ra, ia, log_a = gates(c)
            a = a_scr[rows, :]
            mult = jnp.sqrt(-_expm1(2.0 * log_a))
            dh = g_scr[rows, :]
            h_prev = _shift_dn(_ext_before(h_ref, c, ch), 1, ch)
            dlog_a = dh * h_prev * a - (dh * ia * xc) * (a * a) / mult
            dlam_ref[...] += jnp.sum(dlog_a * ra, axis=0, keepdims=True)
            dpre_a = (dlog_a * ((-LRU_C) * sp)) * (ra * (1.0 - ra))
            dpre_x = (dh * mult * xc) * (ia * (1.0 - ia))
            dba_ref[...] += jnp.sum(dpre_a, axis=0, keepdims=True)
            dbx_ref[...] += jnp.sum(dpre_x, axis=0, keepdims=True)
            xcb = xc.astype(BF16)
            dpa_b = dpre_a.astype(BF16)
            dpx_b = dpre_x.astype(BF16)
            dwa_ref[0] += _dot_tn(xcb, dpa_b)
            dwx_ref[0] += _dot_tn(xcb, dpx_b)
            a_scr[rows, :] = dh * mult * ia + _dot_nt(dpa_b, wa) + _dot_nt(dpx_b, wx)
            return 0

        lax.fori_loop(0, n_chunks, second, 0)
        dlam_ref[...] = dlam_ref[...] * (LRU_C * _sigmoid(-lam_ref[...]))

        def third(c, _):
            rows = pl.ds(pl.multiple_of(c * ch, HALO), ch)
            dxc_ext = _ext_after(a_scr, c, ch, n_chunks)
            dxc = _shift_up(dxc_ext, 0, ch)
            xr_ext = _ext_before(xr_ref, c, ch)
            db4_ref[...] += jnp.sum(dxc, axis=0, keepdims=True)
            dxr = w4_ref[3:4, :] * dxc
            dw4_ref[3:4, :] += jnp.sum(dxc * _shift_dn(xr_ext, 0, ch), axis=0, keepdims=True)
            for k in range(3):
                dxr = dxr + w4_ref[k:k + 1, :] * _shift_up(dxc_ext, 3 - k, ch)
                dw4_ref[k:k + 1, :] += jnp.sum(dxc * _shift_dn(xr_ext, 3 - k, ch), axis=0, keepdims=True)
            dz_ref[0, rows, :] = dxr.astype(BF16)
            dyq_ext = _ext_after(q_scr, c, ch, n_chunks)
            dq = w3_ref[2:3, :] * _shift_up(dyq_ext, 0, ch)
            for k in range(2):
                dq = dq + w3_ref[k:k + 1, :] * _shift_up(dyq_ext, 2 - k, ch)
            dz_ref[3, rows, :] = (dq * cx_ref[rows, :].astype(F32)).astype(BF16)
            dz_ref[4, rows, :] = (dq * cc_ref[rows, :].astype(F32)).astype(BF16)
            return 0

        lax.fori_loop(0, n_chunks, third, 0)

    col = lambda j: (0, j)
    vec = pl.BlockSpec((1, cw), col)
    sq = pl.BlockSpec((1, cw, cw), lambda j: (j, 0, 0))
    act = pl.BlockSpec((s, cw), col)
    w4s, w3s = pl.BlockSpec((4, cw), col), pl.BlockSpec((3, cw), col)
    vec_shape = _sds((1, D_MODEL), F32)
    sq_shape = _sds((D_MODEL // cw, cw, cw), F32)
    return pl.pallas_call(
        body, name="mixers_bwd", grid=(D_MODEL // cw,),
        in_specs=_z_block_specs(s, cw, range(5)) + [act, act, act, w4s, vec, sq, vec, sq, vec, vec, w3s],
        out_specs=[pl.BlockSpec((5, s, cw), lambda j: (0, 0, j)), w4s, vec, sq, vec, sq, vec, vec, w3s],
        out_shape=[_sds((5, s, D_MODEL), BF16), _sds((4, D_MODEL), F32), vec_shape, sq_shape, vec_shape, sq_shape,
                   vec_shape, vec_shape, _sds((3, D_MODEL), F32)],
        scratch_shapes=[pltpu.VMEM((s, cw), F32)] * 3,
        compiler_params=_params("parallel"),
    )(z, z, z, z, z, h, dya, dyb, w4, b4, w_a, b_a, w_x, b_x, lam, w3)


def _in_proj_bwd(dz5, dzg, w_in, x, dx2, g_mix, tm):
    s = x.shape[0]

    def body(d5_ref, dg_ref, w_ref, x_ref, dx2_ref, g_ref, dx_ref, sums_ref, acc):
        k = pl.program_id(1)

        @pl.when((pl.program_id(0) == 0) & (k == 0))
        def _():
            sums_ref[...] = jnp.zeros_like(sums_ref)

        @pl.when(k == 0)
        def _():
            acc[...] = jnp.zeros_like(acc)

        @pl.when(k < 5)
        def _():
            acc[...] += _dot_nt(d5_ref[...], w_ref[...])

        @pl.when(k >= 5)
        def _():
            acc[...] += _dot_nt(dg_ref[...], w_ref[...])

        @pl.when(k == N_SEG - 1)
        def _():
            dh1 = acc[...]
            xn, r1 = _rms(x_ref[...])
            sums_ref[0:1, :] += jnp.sum(dh1 * xn, axis=0, keepdims=True)
            dx_ref[...] = dx2_ref[...] + _rms_bwd(dh1 * g_ref[...], xn, r1)

    tile = pl.BlockSpec((tm, D_MODEL), lambda i, k: (i, 0))
    return pl.pallas_call(
        body, name="in_proj_bwd", grid=(s // tm, N_SEG),
        in_specs=[pl.BlockSpec((None, tm, D_MODEL), lambda i, k: (jnp.minimum(k, 4), i, 0)),
                  pl.BlockSpec((None, tm, D_MODEL), lambda i, k: (jnp.maximum(k - 5, 0), i, 0)),
                  pl.BlockSpec((D_MODEL, D_MODEL), lambda i, k: (0, k)), tile, tile,
                  pl.BlockSpec((1, D_MODEL), lambda i, k: (0, 0))],
        out_specs=[tile, pl.BlockSpec((8, D_MODEL), lambda i, k: (0, 0))],
        out_shape=[_sds((s, D_MODEL), F32), _sds((8, D_MODEL), F32)],
        scratch_shapes=[pltpu.VMEM((tm, D_MODEL), F32)],
        compiler_params=_params("arbitrary", "arbitrary"),
    )(dz5, dzg, w_in, x, dx2, g_mix)


def _weight_grad(a, b, a_spec, b_spec, out_spec, out_shape, grid, tm, tn, name):
    nk = grid[2]

    def body(a_ref, b_ref, o_ref, acc):
        k = pl.program_id(2)

        @pl.when(k == 0)
        def _():
            acc[...] = jnp.zeros_like(acc)

        acc[...] += _dot_tn(a_ref[...].astype(BF16), b_ref[...].astype(BF16))

        @pl.when(k == nk - 1)
        def _():
            o_ref[...] = acc[...].astype(o_ref.dtype)

    return pl.pallas_call(
        body, name=name, grid=grid, in_specs=[a_spec, b_spec], out_specs=out_spec, out_shape=out_shape,
        scratch_shapes=[pltpu.VMEM((tm, tn), F32)],
        compiler_params=_params("parallel", "parallel", "arbitrary"),
    )(a, b)


def _wgrad_2d(a, b, name, tk):
    s, m = a.shape
    n = b.shape[1]
    tm, tn = min(m, 512), min(n, 1024)
    return _weight_grad(
        a, b, pl.BlockSpec((tk, tm), lambda i, j, k: (k, i)), pl.BlockSpec((tk, tn), lambda i, j, k: (k, j)),
        pl.BlockSpec((tm, tn), lambda i, j, k: (i, j)), _sds((m, n), BF16), (m // tm, n // tn, s // tk), tm, tn, name)


def _wgrad_segments(a, b3, name, tk):
    s, m = a.shape
    g = b3.shape[0]
    tm = 512
    return _weight_grad(
        a, b3, pl.BlockSpec((tk, tm), lambda i, j, k: (k, i)), pl.BlockSpec((None, tk, D_MODEL), lambda i, j, k: (j, k, 0)),
        pl.BlockSpec((tm, D_MODEL), lambda i, j, k: (i, j)), _sds((m, g * D_MODEL), BF16), (m // tm, g, s // tk), tm,
        D_MODEL, name)


def _wgrad_up(h2, du0, bw, tk):
    s = h2.shape[0]
    tm = 512
    per_half = N_DEV // 2
    return _weight_grad(
        h2, du0, pl.BlockSpec((tk, tm), lambda i, j, k: (k, i)),
        pl.BlockSpec((None, tk, bw), lambda i, j, k: (j // per_half, k, j % per_half)),
        pl.BlockSpec((None, tm, bw), lambda i, j, k: (j, i, 0)), _sds((N_DEV, D_MODEL, bw), BF16),
        (D_MODEL // tm, N_DEV, s // tk), tm, bw, "wgrad_up")


def _adam_math(w, g, m, v):
    m = ADAM_B1 * m + (1.0 - ADAM_B1) * g
    v = ADAM_B2 * v + (1.0 - ADAM_B2) * jnp.square(g)
    m_hat = m / (1.0 - ADAM_B1 ** ADAM_STEP)
    v_hat = v / (1.0 - ADAM_B2 ** ADAM_STEP)
    delta = -ADAM_LR * (m_hat / (jnp.sqrt(v_hat) + ADAM_EPS) + ADAM_WD * w)
    return delta, m, v


def _adam_shard(parts, w, m, v, name):
    r, c = w.shape
    tr = min(r, 128)

    def body(p_ref, w_ref, m_ref, v_ref, g_ref, d_ref, nm_ref, nv_ref):
        g = p_ref[0].astype(F32)
        for k in range(1, N_DEV):
            g = g + p_ref[k].astype(F32)
        g_ref[...] = g
        d_ref[...], nm_ref[...], nv_ref[...] = _adam_math(w_ref[...], g, m_ref[...], v_ref[...])

    tile = pl.BlockSpec((tr, c), lambda i: (i, 0))
    return pl.pallas_call(
        body, name=name, grid=(r // tr,),
        in_specs=[pl.BlockSpec((N_DEV, tr, c), lambda i: (0, i, 0)), tile, tile, tile],
        out_specs=[tile] * 4, out_shape=[_sds((r, c), F32)] * 4,
        compiler_params=_params("parallel"),
    )(parts, w, m, v)


def _adam_plain(g, w, m, v, name):
    def body(g_ref, w_ref, m_ref, v_ref, d_ref, nm_ref, nv_ref):
        d_ref[...], nm_ref[...], nv_ref[...] = _adam_math(w_ref[...], g_ref[...], m_ref[...], v_ref[...])

    vmem = pl.BlockSpec(memory_space=pltpu.VMEM)
    return pl.pallas_call(
        body, name=name, in_specs=[vmem] * 4, out_specs=[vmem] * 3, out_shape=[_sds(w.shape, F32)] * 3,
    )(g, w, m, v)


_REPL = ("g_mix", "rnn_conv_b", "b_rg_a", "b_rg_x", "lru_lambda", "g_ffn", "g_ple", "g_final", "ffn_conv_b", "pad2",
         "w_rg_a", "w_rg_x")
REPL_ROWS = 272


def _pack_repl(t):
    rows = [t[n].reshape(1, D_MODEL) for n in _REPL[:8]]
    rows.append(t["ffn_conv_b"].reshape(6, D_MODEL))
    rows.append(t.get("pad2", jnp.zeros((2, D_MODEL), F32)))
    rows.append(t["w_rg_a"].reshape(128, D_MODEL))
    rows.append(t["w_rg_x"].reshape(128, D_MODEL))
    return jnp.concatenate(rows, axis=0)


def _unpack_repl(pack, shapes):
    out = {n: pack[i].reshape(shapes[n]) for i, n in enumerate(_REPL[:8])}
    out["ffn_conv_b"] = pack[8:14].reshape(shapes["ffn_conv_b"])
    out["w_rg_a"] = pack[16:144].reshape(shapes["w_rg_a"])
    out["w_rg_x"] = pack[144:272].reshape(shapes["w_rg_x"])
    return out


def _pack_conv_shard(rnn, sc, ffn):
    top = jnp.concatenate([rnn[:3], sc, ffn], axis=1)
    row3 = jnp.concatenate([rnn[3:4], jnp.zeros((1, D_MODEL - RNN_BW), F32)], axis=1)
    return jnp.concatenate([top, row3, jnp.zeros((4, D_MODEL), F32)], axis=0)


def _unpack_conv_shard(pack):
    rnn = jnp.concatenate([pack[:3, :RNN_BW], pack[3:4, :RNN_BW]], axis=0)
    return rnn, pack[:3, RNN_BW:2 * RNN_BW], pack[:3, 2 * RNN_BW:]


_SHARDED_BIG = ("w_in", "w_proj_a", "w_proj_b", "w_out", "w_up", "w_down", "w_ple_gate", "w_ple_proj")
_NAMES = ("g_mix", "w_in", "rnn_conv_w", "rnn_conv_b", "w_rg_a", "b_rg_a", "w_rg_x", "b_rg_x", "lru_lambda", "sc_conv_w",
          "w_proj_a", "w_proj_b", "w_out", "g_ffn", "w_up", "ffn_conv_w", "ffn_conv_b", "w_down", "w_ple_gate",
          "w_ple_proj", "g_ple", "g_final")


def _step(x, p, target, w, m, v):
    s = x.shape[0]
    tm = min(s, 256)
    tm_wide = min(s, 512)
    ch = min(s, 256)
    my_index = 4 * lax.axis_index("x") + 2 * lax.axis_index("y") + lax.axis_index("c")

    big = {n: w[n][0] for n in _SHARDED_BIG}
    conv_shard = _pack_conv_shard(w["rnn_conv_w"][0], w["sc_conv_w"][0], w["ffn_conv_w"][0])
    gathered = _all_gather([big[n].astype(BF16) for n in _SHARDED_BIG] + [conv_shard])
    full = dict(zip(_SHARDED_BIG, gathered[:-1]))
    conv_all = gathered[-1]
    w_in = jnp.transpose(full["w_in"], (1, 0, 2)).reshape(D_MODEL, N_SEG * D_MODEL)
    w_pa = full["w_proj_a"].reshape(D_MODEL, D_MODEL)
    w_pb = full["w_proj_b"].reshape(D_MODEL, D_MODEL)
    w_out = full["w_out"].reshape(D_MODEL, D_MODEL)
    w_gate = full["w_ple_gate"].reshape(D_MODEL, D_MODEL)
    w_down = full["w_down"].reshape(D_FF, D_MODEL)
    w_up_g = full["w_up"]
    w_pp = jnp.transpose(full["w_ple_proj"], (1, 0, 2)).reshape(D_PLE, D_MODEL)
    w4 = jnp.transpose(jnp.concatenate([conv_all[:, :3, :RNN_BW], conv_all[:, 3:4, :RNN_BW]], axis=1),
                       (1, 0, 2)).reshape(4, D_MODEL)
    w3 = jnp.transpose(conv_all[:, :3, RNN_BW:2 * RNN_BW], (1, 0, 2)).reshape(3, D_MODEL)
    wc = jnp.transpose(conv_all[:, :3, 2 * RNN_BW:], (1, 0, 2)).reshape(3, 2, D_FF).transpose(1, 0, 2)
    bc = w["ffn_conv_b"].reshape(2, 1, D_FF)
    b4, b_a, b_x, lam = w["rnn_conv_b"], w["b_rg_a"], w["b_rg_x"], w["lru_lambda"]
    w_a, w_x = w["w_rg_a"][0], w["w_rg_x"][0]
    g_final = w["g_final"].reshape(1, D_MODEL)

    z, h1 = _norm_in_proj(x, w["g_mix"], w_in, tm_wide)
    ya, yb, h = _mixers_fwd(z, w4, b4, w_a, b_a, w_x, b_x, lam, w3, ch)
    pa, pb, mm, x2, h2 = _merge_out(ya, yb, z, x, w_pa, w_pb, w_out, w["g_ffn"], tm)
    u0 = _up_proj(h2, w_up_g, tm_wide)
    f = _ffn_act(u0, wc, bc, 256, ch)
    x3, dgt, de0, dx3, df, head_sums = _head(x2, f, p, target, w_down, w_gate, w_pp, w["g_ple"], g_final, tm)
    du0, dwc, dbc = _ffn_act_bwd(u0, df, wc, bc, 256, ch)
    dx2, dzg, dpa, dpb, dya, dyb, ffn_sums = _up_bwd_merge_bwd(du0, w_up_g, x2, dx3, z, pa, pb, w_out, w_pa, w_pb,
                                                                w["g_ffn"], tm)
    dz5, dw4, db4, dwa, dba, dwx, dbx, dlam, dw3 = _mixers_bwd(z, h, dya, dyb, w4, b4, w_a, b_a, w_x, b_x, lam, w3, ch)
    grad_x, mix_sums = _in_proj_bwd(dz5, dzg, w_in, x, dx2, w["g_mix"], tm)

    tk = min(s, 512)
    dw_in = jnp.concatenate([_wgrad_segments(h1, dz5, "wgrad_in_mix", tk), _wgrad_segments(h1, dzg, "wgrad_in_gate", tk)],
                            axis=1)
    bw_in = N_SEG * D_MODEL // N_DEV
    parts = {
        "w_in": jnp.transpose(dw_in.reshape(D_MODEL, N_DEV, bw_in), (1, 0, 2)),
        "w_proj_a": _wgrad_2d(ya, dpa, "wgrad_proj_a", tk).reshape(N_DEV, RNN_BW, D_MODEL),
        "w_proj_b": _wgrad_2d(yb, dpb, "wgrad_proj_b", tk).reshape(N_DEV, RNN_BW, D_MODEL),
        "w_out": _wgrad_2d(mm, dx2, "wgrad_out", tk).reshape(N_DEV, RNN_BW, D_MODEL),
        "w_up": _wgrad_up(h2, du0, w_up_g.shape[2], tk),
        "w_down": _wgrad_2d(f, dx3, "wgrad_down", tk).reshape(N_DEV, D_FF // N_DEV, D_MODEL),
        "w_ple_gate": _wgrad_2d(x3, dgt, "wgrad_ple_gate", tk).reshape(N_DEV, RNN_BW, D_MODEL),
        "w_ple_proj": jnp.transpose(_wgrad_2d(p, de0, "wgrad_ple_proj", tk).reshape(D_PLE, N_DEV, RNN_BW), (1, 0, 2)),
    }
    received = dict(zip(_SHARDED_BIG, _grad_exchange([parts[n] for n in _SHARDED_BIG])))

    small = {"g_mix": mix_sums[0], "rnn_conv_b": db4, "b_rg_a": dba, "b_rg_x": dbx, "lru_lambda": dlam,
             "g_ffn": ffn_sums[0], "g_ple": head_sums[2], "g_final": head_sums[1], "ffn_conv_b": dbc,
             "pad2": jnp.concatenate([head_sums[0:1], jnp.zeros((1, D_MODEL), F32)], axis=0),
             "w_rg_a": dwa, "w_rg_x": dwx}
    conv_rows = jnp.concatenate([dw4, dw3, jnp.zeros((1, D_MODEL), F32),
                                 jnp.transpose(dwc, (1, 0, 2)).reshape(18, D_MODEL),
                                 jnp.zeros((PACK_ROWS - REPL_ROWS - 26, D_MODEL), F32)], axis=0)
    total = _small_all_reduce(jnp.concatenate([_pack_repl(small), conv_rows], axis=0))
    loss = jnp.sum(total[14]) * (0.5 / D_MODEL)

    out = {}
    for n in _SHARDED_BIG:
        shard = big[n]
        res = _adam_shard(received[n].reshape((N_DEV,) + shard.shape), shard, m[n][0], v[n][0], "adam_" + n)
        out[n] = [r[None] for r in res]
    shapes = {n: w[n].shape for n in _REPL if n != "pad2"}
    g_repl = total[:REPL_ROWS]
    res = _adam_plain(g_repl, _pack_repl(w), _pack_repl(m), _pack_repl(v), "adam_replicated")
    unpacked = [_unpack_repl(r, shapes) for r in (g_repl,) + tuple(res)]
    for n in shapes:
        out[n] = [u[n] for u in unpacked]
    g_rnn = lax.dynamic_slice(total[272:276], (0, my_index * RNN_BW), (4, RNN_BW))
    g_sc = lax.dynamic_slice(total[276:279], (0, my_index * RNN_BW), (3, RNN_BW))
    bw_ffn = 2 * D_FF // N_DEV
    g_ffn_conv = lax.dynamic_slice(total[280:298].reshape(3, 2 * D_FF), (0, my_index * bw_ffn), (3, bw_ffn))
    g_conv = _pack_conv_shard(g_rnn, g_sc, g_ffn_conv)
    res = _adam_plain(g_conv, conv_shard,
                      _pack_conv_shard(m["rnn_conv_w"][0], m["sc_conv_w"][0], m["ffn_conv_w"][0]),
                      _pack_conv_shard(v["rnn_conv_w"][0], v["sc_conv_w"][0], v["ffn_conv_w"][0]), "adam_conv")
    unpacked = [_unpack_conv_shard(r) for r in (g_conv,) + tuple(res)]
    for i, n in enumerate(("rnn_conv_w", "sc_conv_w", "ffn_conv_w")):
        out[n] = [u[i][None] for u in unpacked]

    return (loss, grad_x[None]) + tuple(out[n][k] for k in range(4) for n in _NAMES)


def kernel(x, p, g_mix, w_in, rnn_conv_w, rnn_conv_b, w_rg_a, b_rg_a, w_rg_x, b_rg_x, lru_lambda, sc_conv_w, w_proj_a, w_proj_b, w_out, g_ffn, w_up, ffn_conv_w, ffn_conv_b, w_down, w_ple_gate, w_ple_proj, g_ple, g_final, loss_target, m_g_mix, m_w_in, m_rnn_conv_w, m_rnn_conv_b, m_w_rg_a, m_b_rg_a, m_w_rg_x, m_b_rg_x, m_lru_lambda, m_sc_conv_w, m_w_proj_a, m_w_proj_b, m_w_out, m_g_ffn, m_w_up, m_ffn_conv_w, m_ffn_conv_b, m_w_down, m_w_ple_gate, m_w_ple_proj, m_g_ple, m_g_final, v_g_mix, v_w_in, v_rnn_conv_w, v_rnn_conv_b, v_w_rg_a, v_b_rg_a, v_w_rg_x, v_b_rg_x, v_lru_lambda, v_sc_conv_w, v_w_proj_a, v_w_proj_b, v_w_out, v_g_ffn, v_w_up, v_ffn_conv_w, v_ffn_conv_b, v_w_down, v_w_ple_gate, v_w_ple_proj, v_g_ple, v_g_final):
    given = dict(locals())
    w = {n: given[n] for n in _NAMES}
    m = {n: given["m_" + n] for n in _NAMES}
    v = {n: given["v_" + n] for n in _NAMES}
    return _step(x[0], p[0, 0], loss_target[0], w, m, v)
```

```python
import functools

import jax
import jax.numpy as jnp
from jax import lax
from jax.experimental import pallas as pl
from jax.experimental.pallas import tpu as pltpu

F32 = jnp.float32
BF16 = jnp.bfloat16
MESH_ID = pl.DeviceIdType.MESH

N_DEV = 8
D_MODEL = 1024
D_PLE = 256
RNN_BW = 128
N_SEG = 7
D_FF = 3072
LRU_C = 8.0
EPS = 1e-6
ADAM_LR = 0.001
ADAM_B1 = 0.9
ADAM_B2 = 0.999
ADAM_EPS = 1e-08
ADAM_WD = 0.01
ADAM_STEP = 10

HALO = 16
VMEM_LIMIT = 56 * 1024 * 1024
PACK_ROWS = 320
PIECE = PACK_ROWS // N_DEV


def _dot(a, b):
    return jnp.dot(a, b, preferred_element_type=F32)


def _dot_nt(a, b):
    return lax.dot_general(a, b, (((1,), (1,)), ((), ())), preferred_element_type=F32)


def _dot_tn(a, b):
    return lax.dot_general(a, b, (((0,), (0,)), ((), ())), preferred_element_type=F32)


def _sigmoid(x):
    return jax.nn.sigmoid(x)


_GELU_C = 0.7978845608028654
_GELU_K = 0.044715


def _gelu(x):
    return 0.5 * x * (1.0 + jnp.tanh(_GELU_C * (x + _GELU_K * (x * x * x))))


def _gelu_and_grad(x):
    x2 = x * x
    t = jnp.tanh(_GELU_C * (x + _GELU_K * (x * x2)))
    g = 0.5 * x * (1.0 + t)
    dg = 0.5 * (1.0 + t) + 0.5 * x * (1.0 - t * t) * (_GELU_C * (1.0 + 3.0 * _GELU_K * x2))
    return g, dg


def _expm1(x):
    poly = x * (1.0 + x * (0.5 + x * (1.0 / 6.0 + x * (1.0 / 24.0 + x * (1.0 / 120.0)))))
    return jnp.where(jnp.abs(x) < 0.08, poly, jnp.exp(x) - 1.0)


def _softplus(x):
    return jnp.maximum(x, 0.0) + jnp.log1p(jnp.exp(-jnp.abs(x)))


def _rms(u):
    r = lax.rsqrt(jnp.mean(u * u, axis=-1, keepdims=True) + EPS)
    return u * r, r


def _rms_bwd(dn, un, r):
    return r * (dn - un * jnp.mean(dn * un, axis=-1, keepdims=True))


def _shift_dn(ext, s, n):
    if s == 0:
        return ext[HALO:HALO + n]
    return pltpu.roll(ext, s, 0)[HALO:HALO + n]


def _shift_up(ext, s, n):
    if s == 0:
        return ext[0:n]
    return pltpu.roll(ext, n + HALO - s, 0)[0:n]


def _ext_before(ref, c, ch):
    t0 = c * ch
    prev = pl.multiple_of(jnp.maximum(t0 - HALO, 0), HALO)
    halo = jnp.where(c > 0, ref[pl.ds(prev, HALO), :].astype(F32), 0.0)
    cur = ref[pl.ds(pl.multiple_of(t0, HALO), ch), :].astype(F32)
    return jnp.concatenate([halo, cur], axis=0)


def _ext_after(ref, c, ch, n_chunks):
    t0 = c * ch
    nxt = pl.multiple_of(jnp.minimum(t0 + ch, (n_chunks - 1) * ch + ch - HALO), HALO)
    halo = jnp.where(c < n_chunks - 1, ref[pl.ds(nxt, HALO), :].astype(F32), 0.0)
    cur = ref[pl.ds(pl.multiple_of(t0, HALO), ch), :].astype(F32)
    return jnp.concatenate([cur, halo], axis=0)


def _rows(ref, c, ch):
    return ref[pl.ds(pl.multiple_of(c * ch, HALO), ch), :]


def _scan_fwd(a_ref, b_ref, h_ref, n_rows, cw):
    rows = lax.broadcasted_iota(jnp.int32, (8, cw), 0)

    def body(g, carry):
        t = pl.multiple_of(g * 8, 8)
        a = a_ref[pl.ds(t, 8), :]
        b = b_ref[pl.ds(t, 8), :]
        for d in (1, 2, 4):
            m = rows >= d
            b = jnp.where(m, a * pltpu.roll(b, d, 0) + b, b)
            a = jnp.where(m, a * pltpu.roll(a, d, 0), a)
        h = a * carry + b
        h_ref[pl.ds(t, 8), :] = h
        return jnp.broadcast_to(h[7:8, :], (8, cw))

    lax.fori_loop(0, n_rows // 8, body, jnp.zeros((8, cw), F32), unroll=2)


def _scan_rev(a_ref, g_ref, n_rows, cw):
    rows = lax.broadcasted_iota(jnp.int32, (8, cw), 0)
    n_groups = n_rows // 8

    def body(i, carry):
        dh_next, a_next = carry
        t = pl.multiple_of((n_groups - 1 - i) * 8, 8)
        a = a_ref[pl.ds(t, 8), :]
        g = g_ref[pl.ds(t, 8), :]
        an = jnp.where(rows < 7, pltpu.roll(a, 7, 0), a_next)
        for d in (1, 2, 4):
            m = rows < 8 - d
            g = jnp.where(m, an * pltpu.roll(g, 8 - d, 0) + g, g)
            an = jnp.where(m, an * pltpu.roll(an, 8 - d, 0), an)
        dh = an * dh_next + g
        g_ref[pl.ds(t, 8), :] = dh
        return (jnp.broadcast_to(dh[0:1, :], (8, cw)), jnp.broadcast_to(a[0:1, :], (8, cw)))

    zero = jnp.zeros((8, cw), F32)
    lax.fori_loop(0, n_groups, body, (zero, zero), unroll=2)


def _params(*sem):
    return pltpu.CompilerParams(dimension_semantics=sem, vmem_limit_bytes=VMEM_LIMIT)


def _sds(shape, dtype):
    return jax.ShapeDtypeStruct(shape, dtype)


class _Comm:
    def __init__(self, inputs, out_shape, scratch, start, finish):
        self.inputs, self.out_shape, self.scratch, self.start, self.finish = inputs, out_shape, scratch, start, finish


def _sem_scratch(n):
    return [pltpu.SemaphoreType.DMA((n, 7)), pltpu.SemaphoreType.DMA((n, 7)), pltpu.SemaphoreType.DMA((n,))]


def _gather_comm(shards):
    n = len(shards)

    def plan(ins, outs, sems):
        send_sems, recv_sems, local_sems = sems
        x, y, c = lax.axis_index("x"), lax.axis_index("y"), lax.axis_index("c")
        me, sibling = (x, y, c), (x, y, 1 - c)
        chips = [(1 - x, y), (x, 1 - y), (1 - x, 1 - y)]

        def slot(t, dev):
            return outs[t].at[4 * dev[0] + 2 * dev[1] + dev[2]]

        def copy(t, k, block, to, src=None):
            return pltpu.make_async_remote_copy(
                src_ref=slot(t, block) if src is None else src, dst_ref=slot(t, block),
                send_sem=send_sems.at[t, k], recv_sem=recv_sems.at[t, k],
                device_id=to, device_id_type=MESH_ID)

        mine = [pltpu.make_async_copy(ins[t], slot(t, me), local_sems.at[t]) for t in range(n)]
        first = []
        for t in range(n):
            first.append(copy(t, 0, me, sibling, src=ins[t]))
            first += [copy(t, 1 + j, me, (*chip, c), src=ins[t]) for j, chip in enumerate(chips)]
        return me, sibling, chips, c, copy, mine, first

    def start(ins, outs, sems):
        *_, mine, first = plan(ins, outs, sems)
        for cp in mine + first:
            cp.start()

    def finish(ins, outs, sems):
        me, sibling, chips, c, copy, mine, first = plan(ins, outs, sems)
        passed = []
        for t in range(n):
            for j, chip in enumerate(chips):
                copy(t, 1 + j, (*chip, c), me).wait_recv()
                cp = copy(t, 4 + j, (*chip, c), sibling)
                cp.start()
                passed.append(cp)
        for t in range(n):
            copy(t, 0, sibling, me).wait_recv()
            for j, chip in enumerate(chips):
                copy(t, 4 + j, (*chip, 1 - c), me).wait_recv()
        for cp in first + passed:
            cp.wait_send()
        for cp in mine:
            cp.wait()

    return _Comm(list(shards), [_sds((N_DEV,) + s.shape, s.dtype) for s in shards], _sem_scratch(n), start, finish)


def _peer(d):
    x, y, c = lax.axis_index("x"), lax.axis_index("y"), lax.axis_index("c")
    px = 1 - x if d & 4 else x
    py = 1 - y if d & 2 else y
    pc = 1 - c if d & 1 else c
    return (px, py, pc), 4 * px + 2 * py + pc


def _exchange_comm(parts):
    n = len(parts)

    def plan(ins, outs, sems):
        send_sems, recv_sems, local_sems = sems
        _, me = _peer(0)

        def copy(t, d):
            peer, idx = _peer(d)
            return pltpu.make_async_remote_copy(
                src_ref=ins[t].at[idx], dst_ref=outs[t].at[d],
                send_sem=send_sems.at[t, d - 1], recv_sem=recv_sems.at[t, d - 1],
                device_id=peer, device_id_type=MESH_ID)

        mine = [pltpu.make_async_copy(ins[t].at[me], outs[t].at[0], local_sems.at[t]) for t in range(n)]
        return mine, [copy(t, d) for t in range(n) for d in range(1, N_DEV)]

    def start(ins, outs, sems):
        mine, copies = plan(ins, outs, sems)
        for cp in mine + copies:
            cp.start()

    def finish(ins, outs, sems):
        mine, copies = plan(ins, outs, sems)
        for cp in copies:
            cp.wait_recv()
        for cp in copies:
            cp.wait_send()
        for cp in mine:
            cp.wait()

    return _Comm(list(parts), [_sds(p.shape, p.dtype) for p in parts], _sem_scratch(n), start, finish)


def _comm_only(comm, name):
    n = len(comm.inputs)

    def body(*refs):
        ins, outs, sems = refs[:n], refs[n:2 * n], refs[2 * n:]
        comm.start(ins, outs, sems)
        comm.finish(ins, outs, sems)

    any_spec = pl.BlockSpec(memory_space=pl.ANY)
    return pl.pallas_call(body, name=name, out_shape=comm.out_shape, in_specs=[any_spec] * n, out_specs=[any_spec] * n,
                          scratch_shapes=comm.scratch)(*comm.inputs)


def _host_call(body, *, name, grid, in_specs, out_specs, out_shape, scratch_shapes=(), semantics, args, comm=None):
    if comm is None:
        res = pl.pallas_call(body, name=name, grid=grid, in_specs=in_specs, out_specs=out_specs, out_shape=out_shape,
                             scratch_shapes=list(scratch_shapes), compiler_params=_params(*semantics))(*args)
        return res, []
    n_in, n_out, n_scr, n_c = len(in_specs), len(out_specs), len(scratch_shapes), len(comm.inputs)

    def with_comm(*refs):
        ins, refs = refs[:n_in], refs[n_in:]
        c_ins, refs = refs[:n_c], refs[n_c:]
        outs, refs = refs[:n_out], refs[n_out:]
        c_outs, refs = refs[:n_c], refs[n_c:]
        scr, sems = refs[:n_scr], refs[n_scr:]
        ids = [pl.program_id(a) for a in range(len(grid))]
        first = functools.reduce(jnp.logical_and, [i == 0 for i in ids])
        last = functools.reduce(jnp.logical_and, [i == g - 1 for i, g in zip(ids, grid)])

        @pl.when(first)
        def _():
            comm.start(c_ins, c_outs, sems)

        body(*ins, *outs, *scr)

        @pl.when(last)
        def _():
            comm.finish(c_ins, c_outs, sems)

    any_spec = pl.BlockSpec(memory_space=pl.ANY)
    res = pl.pallas_call(
        with_comm, name=name, grid=grid, in_specs=list(in_specs) + [any_spec] * n_c,
        out_specs=list(out_specs) + [any_spec] * n_c, out_shape=list(out_shape) + comm.out_shape,
        scratch_shapes=list(scratch_shapes) + comm.scratch,
        compiler_params=_params(*(["arbitrary"] * len(grid))))(*args, *comm.inputs)
    return res[:n_out], res[n_out:]


def _small_all_reduce(pack):
    def body(p_ref, o_ref, land, s1, r1, s2, r2):
        _, me = _peer(0)

        def piece(ref, idx):
            return ref.at[pl.ds(pl.multiple_of(idx * PIECE, 8), PIECE), :]

        def scatter(d):
            peer, idx = _peer(d)
            return pltpu.make_async_remote_copy(
                src_ref=piece(p_ref, idx), dst_ref=land.at[d - 1], send_sem=s1.at[d - 1], recv_sem=r1.at[d - 1],
                device_id=peer, device_id_type=MESH_ID)

        def gather(d, from_idx):
            peer, _ = _peer(d)
            return pltpu.make_async_remote_copy(
                src_ref=piece(o_ref, from_idx), dst_ref=piece(o_ref, from_idx), send_sem=s2.at[d - 1],
                recv_sem=r2.at[d - 1], device_id=peer, device_id_type=MESH_ID)

        first = [scatter(d) for d in range(1, N_DEV)]
        for cp in first:
            cp.start()
        acc = piece(p_ref, me)[...]
        for d in range(1, N_DEV):
            first[d - 1].wait_recv()
            acc = acc + land[d - 1]
        piece(o_ref, me)[...] = acc
        second = [gather(d, me) for d in range(1, N_DEV)]
        for cp in second:
            cp.start()
        for d in range(1, N_DEV):
            gather(d, _peer(d)[1]).wait_recv()
        for cp in first + second:
            cp.wait_send()

    vmem = pl.BlockSpec(memory_space=pltpu.VMEM)
    return pl.pallas_call(
        body, name="small_all_reduce",
        out_shape=_sds(pack.shape, F32), in_specs=[vmem], out_specs=vmem,
        scratch_shapes=[pltpu.VMEM((N_DEV - 1, PIECE, D_MODEL), F32)] + [pltpu.SemaphoreType.DMA((N_DEV - 1,))] * 4,
    )(pack)


def _norm_in_proj(x, g_mix, w_in, tm, comm):
    s = x.shape[0]

    def body(x_ref, g_ref, w_ref, z_ref, h_ref):
        @pl.when(pl.program_id(1) == 0)
        def _():
            xn, _ = _rms(x_ref[...])
            h_ref[...] = (xn * g_ref[...]).astype(BF16)

        z_ref[...] = _dot(h_ref[...], w_ref[...]).astype(BF16)

    return _host_call(
        body, name="norm_in_proj", grid=(s // tm, N_SEG),
        in_specs=[pl.BlockSpec((tm, D_MODEL), lambda i, j: (i, 0)),
                  pl.BlockSpec((1, D_MODEL), lambda i, j: (0, 0)),
                  pl.BlockSpec((D_MODEL, D_MODEL), lambda i, j: (0, j))],
        out_specs=[pl.BlockSpec((tm, D_MODEL), lambda i, j: (i, j)),
                   pl.BlockSpec((tm, D_MODEL), lambda i, j: (i, 0))],
        out_shape=[_sds((s, N_SEG * D_MODEL), BF16), _sds((s, D_MODEL), BF16)],
        semantics=("parallel", "arbitrary"), args=(x, g_mix, w_in), comm=comm)


def _gate_chunk(xr_ext, w4_ref, cb_ref, wa, wx, ba_ref, bx_ref, sp, ch):
    xc = cb_ref[...] + w4_ref[3:4, :] * _shift_dn(xr_ext, 0, ch)
    for k in range(3):
        xc = xc + w4_ref[k:k + 1, :] * _shift_dn(xr_ext, 3 - k, ch)
    xcb = xc.astype(BF16)
    ra = _sigmoid(_dot(xcb, wa) + ba_ref[...])
    ia = _sigmoid(_dot(xcb, wx) + bx_ref[...])
    log_a = (-LRU_C) * ra * sp
    return xc, ra, ia, log_a


def _conv3(q_ext, w3_ref, ch):
    y = w3_ref[2:3, :] * _shift_dn(q_ext, 0, ch)
    for k in range(2):
        y = y + w3_ref[k:k + 1, :] * _shift_dn(q_ext, 2 - k, ch)
    return y


def _z_block_specs(s, cw, segs):
    nb = D_MODEL // cw
    return [pl.BlockSpec((s, cw), functools.partial(lambda j, seg: (0, seg * nb + j), seg=seg)) for seg in segs]


def _mixers_fwd(z, w4, b4, w_a, b_a, w_x, b_x, lam, w3, ch, comm):
    s = z.shape[0]
    cw = RNN_BW
    n_chunks = s // ch

    def body(xr_ref, gr_ref, cb_ref, cc_ref, cx_ref, w4_ref, b4_ref, wa_ref, ba_ref, wx_ref, bx_ref, lam_ref, w3_ref,
             ya_ref, yb_ref, h_ref, a_scr, b_scr):
        sp = _softplus(-lam_ref[...])
        wa = wa_ref[0].astype(BF16)
        wx = wx_ref[0].astype(BF16)

        def gates(c, _):
            xc, _, ia, log_a = _gate_chunk(_ext_before(xr_ref, c, ch), w4_ref, b4_ref, wa, wx, ba_ref, bx_ref, sp, ch)
            rows = pl.ds(pl.multiple_of(c * ch, HALO), ch)
            a_scr[rows, :] = jnp.exp(log_a)
            b_scr[rows, :] = jnp.sqrt(-_expm1(2.0 * log_a)) * (ia * xc)
            q_ext = _ext_before(cc_ref, c, ch) * _ext_before(cx_ref, c, ch)
            yb_ref[rows, :] = (_rows(cb_ref, c, ch).astype(F32) * _conv3(q_ext, w3_ref, ch)).astype(BF16)
            return 0

        lax.fori_loop(0, n_chunks, gates, 0)
        _scan_fwd(a_scr, b_scr, h_ref, s, cw)

        def outputs(c, _):
            rows = pl.ds(pl.multiple_of(c * ch, HALO), ch)
            ya_ref[rows, :] = (h_ref[rows, :] * _gelu(gr_ref[rows, :].astype(F32))).astype(BF16)
            return 0

        lax.fori_loop(0, n_chunks, outputs, 0)

    col = lambda j: (0, j)
    vec = pl.BlockSpec((1, cw), col)
    sq = pl.BlockSpec((1, cw, cw), lambda j: (j, 0, 0))
    act = pl.BlockSpec((s, cw), col)
    return _host_call(
        body, name="mixers_fwd", grid=(D_MODEL // cw,),
        in_specs=_z_block_specs(s, cw, range(5)) + [pl.BlockSpec((4, cw), col), vec, sq, vec, sq, vec, vec,
                                                     pl.BlockSpec((3, cw), col)],
        out_specs=[act, act, act],
        out_shape=[_sds((s, D_MODEL), BF16), _sds((s, D_MODEL), BF16), _sds((s, D_MODEL), F32)],
        scratch_shapes=[pltpu.VMEM((s, cw), F32), pltpu.VMEM((s, cw), F32)],
        semantics=("parallel",), args=(z, z, z, z, z, w4, b4, w_a, b_a, w_x, b_x, lam, w3), comm=comm)


def _merge_out(ya, yb, z, x, w_pa, w_pb, w_out, g_ffn, tm):
    s = x.shape[0]

    def body(ya_ref, yb_ref, ga_ref, gb_ref, x_ref, wa_ref, wb_ref, wo_ref, g_ref, pa_ref, pb_ref, m_ref, x2_ref, h2_ref):
        pa = _dot(ya_ref[...], wa_ref[...])
        pb = _dot(yb_ref[...], wb_ref[...])
        pa_ref[...] = pa.astype(BF16)
        pb_ref[...] = pb.astype(BF16)
        m = (_sigmoid(ga_ref[...].astype(F32)) * pa + _sigmoid(gb_ref[...].astype(F32)) * pb).astype(BF16)
        m_ref[...] = m
        x2 = x_ref[...] + _dot(m, wo_ref[...])
        x2_ref[...] = x2
        xn, _ = _rms(x2)
        h2_ref[...] = (xn * g_ref[...]).astype(BF16)

    tile = pl.BlockSpec((tm, D_MODEL), lambda i: (i, 0))
    full = pl.BlockSpec((D_MODEL, D_MODEL), lambda i: (0, 0))
    return pl.pallas_call(
        body, name="merge_out", grid=(s // tm,),
        in_specs=[tile, tile, pl.BlockSpec((tm, D_MODEL), lambda i: (i, 5)), pl.BlockSpec((tm, D_MODEL), lambda i: (i, 6)),
                  tile, full, full, full, pl.BlockSpec((1, D_MODEL), lambda i: (0, 0))],
        out_specs=[tile] * 5,
        out_shape=[_sds((s, D_MODEL), BF16)] * 3 + [_sds((s, D_MODEL), F32), _sds((s, D_MODEL), BF16)],
        compiler_params=_params("parallel"),
    )(ya, yb, z, z, x, w_pa, w_pb, w_out, g_ffn)


def _up_proj(h2, w_up_g, tm):
    s = h2.shape[0]
    bw = w_up_g.shape[2]
    per_half = N_DEV // 2

    def body(h_ref, w_ref, u_ref):
        u_ref[...] = _dot(h_ref[...], w_ref[...]).astype(BF16)

    return pl.pallas_call(
        body, name="up_proj", grid=(s // tm, N_DEV),
        in_specs=[pl.BlockSpec((tm, D_MODEL), lambda i, j: (i, 0)),
                  pl.BlockSpec((None, D_MODEL, bw), lambda i, j: (j, 0, 0))],
        out_specs=pl.BlockSpec((None, tm, bw), lambda i, j: (j // per_half, i, j % per_half)),
        out_shape=_sds((2, s, D_FF), BF16),
        compiler_params=_params("parallel", "arbitrary"),
    )(h2, w_up_g)


def _ffn_conv(u_ref, w_ref, b_ref, half, c, ch):
    ext = _ext_before(u_ref.at[half], c, ch)
    w = w_ref.at[half]
    return b_ref[half] + _conv3(ext, w, ch), ext


def _ffn_act(u0, wc, bc, cw, ch):
    s = u0.shape[1]
    n_chunks = s // ch

    def body(u_ref, w_ref, b_ref, f_ref):
        def chunk(c, _):
            ug, _ = _ffn_conv(u_ref, w_ref, b_ref, 0, c, ch)
            uv, _ = _ffn_conv(u_ref, w_ref, b_ref, 1, c, ch)
            f_ref[pl.ds(pl.multiple_of(c * ch, HALO), ch), :] = (_gelu(ug) * uv).astype(BF16)
            return 0

        lax.fori_loop(0, n_chunks, chunk, 0)

    return pl.pallas_call(
        body, name="ffn_act", grid=(D_FF // cw,),
        in_specs=[pl.BlockSpec((2, s, cw), lambda j: (0, 0, j)), pl.BlockSpec((2, 3, cw), lambda j: (0, 0, j)),
                  pl.BlockSpec((2, 1, cw), lambda j: (0, 0, j))],
        out_specs=pl.BlockSpec((s, cw), lambda j: (0, j)),
        out_shape=_sds((s, D_FF), BF16),
        compiler_params=_params("parallel"),
    )(u0, wc, bc)


def _head(x2, f, p, target, w_down, w_gate, w_pp, g_ple, g_final, tm):
    s = x2.shape[0]

    def body(x2_ref, f_ref, p_ref, t_ref, wd_ref, wg_ref, wp_ref, gp_ref, gf_ref,
             x3_ref, dgt_ref, de0_ref, dx3_ref, df_ref, sums_ref):
        @pl.when(pl.program_id(0) == 0)
        def _():
            sums_ref[...] = jnp.zeros_like(sums_ref)

        x3 = x2_ref[...] + _dot(f_ref[...], wd_ref[...])
        x3b = x3.astype(BF16)
        x3_ref[...] = x3b
        e0n, re = _rms(_dot(p_ref[...].astype(BF16), wp_ref[...]))
        e = e0n * gp_ref[...]
        sg = _sigmoid(_dot(x3b, wg_ref[...]))
        x4n, r4 = _rms(x3 + sg * e)
        diff = x4n * gf_ref[...] - t_ref[...]
        sums_ref[0:1, :] += jnp.sum(diff * diff, axis=0, keepdims=True)
        dy = diff * (1.0 / D_MODEL)
        sums_ref[1:2, :] += jnp.sum(dy * x4n, axis=0, keepdims=True)
        dx4 = _rms_bwd(dy * gf_ref[...], x4n, r4)
        de = dx4 * sg
        dgt = ((dx4 * e) * (sg * (1.0 - sg))).astype(BF16)
        dgt_ref[...] = dgt
        sums_ref[2:3, :] += jnp.sum(de * e0n, axis=0, keepdims=True)
        de0_ref[...] = _rms_bwd(de * gp_ref[...], e0n, re).astype(BF16)
        dx3 = dx4 + _dot_nt(dgt, wg_ref[...])
        dx3_ref[...] = dx3
        df_ref[...] = _dot_nt(dx3.astype(BF16), wd_ref[...]).astype(BF16)

    tile = pl.BlockSpec((tm, D_MODEL), lambda i: (i, 0))
    vec = pl.BlockSpec((1, D_MODEL), lambda i: (0, 0))
    wide = pl.BlockSpec((tm, D_FF), lambda i: (i, 0))
    return pl.pallas_call(
        body, name="loss_head", grid=(s // tm,),
        in_specs=[tile, wide, pl.BlockSpec((tm, D_PLE), lambda i: (i, 0)), tile,
                  pl.BlockSpec((D_FF, D_MODEL), lambda i: (0, 0)), pl.BlockSpec((D_MODEL, D_MODEL), lambda i: (0, 0)),
                  pl.BlockSpec((D_PLE, D_MODEL), lambda i: (0, 0)), vec, vec],
        out_specs=[tile, tile, tile, tile, wide, pl.BlockSpec((8, D_MODEL), lambda i: (0, 0))],
        out_shape=[_sds((s, D_MODEL), BF16)] * 3 + [_sds((s, D_MODEL), F32), _sds((s, D_FF), BF16),
                                                    _sds((8, D_MODEL), F32)],
        compiler_params=_params("arbitrary"),
    )(x2, f, p, target, w_down, w_gate, w_pp, g_ple, g_final)


def _ffn_act_bwd(u0, df, wc, bc, cw, ch, comm):
    s = u0.shape[1]
    n_chunks = s // ch

    def body(u_ref, df_ref, w_ref, b_ref, du0_ref, dw_ref, db_ref, du_scr):
        dw_ref[...] = jnp.zeros_like(dw_ref)
        db_ref[...] = jnp.zeros_like(db_ref)

        def first(c, _):
            ug, ext_g = _ffn_conv(u_ref, w_ref, b_ref, 0, c, ch)
            uv, ext_v = _ffn_conv(u_ref, w_ref, b_ref, 1, c, ch)
            gel, dgel = _gelu_and_grad(ug)
            d = _rows(df_ref, c, ch).astype(F32)
            rows = pl.ds(pl.multiple_of(c * ch, HALO), ch)
            for half, du, ext in ((0, d * uv * dgel, ext_g), (1, d * gel, ext_v)):
                du_scr[half, rows, :] = du
                db_ref[half] += jnp.sum(du, axis=0, keepdims=True)
                for k in range(3):
                    dw_ref[half, k:k + 1, :] += jnp.sum(du * _shift_dn(ext, 2 - k, ch), axis=0, keepdims=True)
            return 0

        lax.fori_loop(0, n_chunks, first, 0)

        def second(c, _):
            rows = pl.ds(pl.multiple_of(c * ch, HALO), ch)
            for half in range(2):
                ext = _ext_after(du_scr.at[half], c, ch, n_chunks)
                w = w_ref.at[half]
                acc = w[2:3, :] * _shift_up(ext, 0, ch)
                for k in range(2):
                    acc = acc + w[k:k + 1, :] * _shift_up(ext, 2 - k, ch)
                du0_ref[half, rows, :] = acc.astype(BF16)
            return 0

        lax.fori_loop(0, n_chunks, second, 0)

    blk = lambda r: pl.BlockSpec((2, r, cw), lambda j: (0, 0, j))
    return _host_call(
        body, name="ffn_act_bwd", grid=(D_FF // cw,),
        in_specs=[blk(s), pl.BlockSpec((s, cw), lambda j: (0, j)), blk(3), blk(1)],
        out_specs=[blk(s), blk(3), blk(1)],
        out_shape=[_sds((2, s, D_FF), BF16), _sds((2, 3, D_FF), F32), _sds((2, 1, D_FF), F32)],
        scratch_shapes=[pltpu.VMEM((2, s, cw), F32)],
        semantics=("parallel",), args=(u0, df, wc, bc), comm=comm)


def _up_bwd_merge_bwd(du0, w_up_g, x2, dx3, z, pa, pb, w_out, w_pa, w_pb, g_ffn, tm, comm):
    s = x2.shape[0]
    bw = w_up_g.shape[2]
    per_half = N_DEV // 2

    def body(du_ref, wu_ref, x2_ref, dx3_ref, ga_ref, gb_ref, pa_ref, pb_ref, wo_ref, wa_ref, wb_ref, g_ref,
             dx2_ref, dzg_ref, dpa_ref, dpb_ref, dya_ref, dyb_ref, sums_ref, acc):
        k = pl.program_id(1)

        @pl.when((pl.program_id(0) == 0) & (k == 0))
        def _():
            sums_ref[...] = jnp.zeros_like(sums_ref)

        @pl.when(k == 0)
        def _():
            acc[...] = jnp.zeros_like(acc)

        acc[...] += _dot_nt(du_ref[...], wu_ref[...])

        @pl.when(k == N_DEV - 1)
        def _():
            dh2 = acc[...]
            x2n, r2 = _rms(x2_ref[...])
            sums_ref[0:1, :] += jnp.sum(dh2 * x2n, axis=0, keepdims=True)
            dx2 = dx3_ref[...] + _rms_bwd(dh2 * g_ref[...], x2n, r2)
            dx2_ref[...] = dx2
            dm = _dot_nt(dx2.astype(BF16), wo_ref[...])
            sa = _sigmoid(ga_ref[...].astype(F32))
            sb = _sigmoid(gb_ref[...].astype(F32))
            dzg_ref[0] = (dm * pa_ref[...].astype(F32) * (sa * (1.0 - sa))).astype(BF16)
            dzg_ref[1] = (dm * pb_ref[...].astype(F32) * (sb * (1.0 - sb))).astype(BF16)
            dpa = (dm * sa).astype(BF16)
            dpb = (dm * sb).astype(BF16)
            dpa_ref[...] = dpa
            dpb_ref[...] = dpb
            dya_ref[...] = _dot_nt(dpa, wa_ref[...]).astype(BF16)
            dyb_ref[...] = _dot_nt(dpb, wb_ref[...]).astype(BF16)

    tile = pl.BlockSpec((tm, D_MODEL), lambda i, k: (i, 0))
    full = pl.BlockSpec((D_MODEL, D_MODEL), lambda i, k: (0, 0))
    return _host_call(
        body, name="up_bwd_merge_bwd", grid=(s // tm, N_DEV),
        in_specs=[pl.BlockSpec((None, tm, bw), lambda i, k: (k // per_half, i, k % per_half)),
                  pl.BlockSpec((None, D_MODEL, bw), lambda i, k: (k, 0, 0)),
                  tile, tile, pl.BlockSpec((tm, D_MODEL), lambda i, k: (i, 5)),
                  pl.BlockSpec((tm, D_MODEL), lambda i, k: (i, 6)), tile, tile, full, full, full,
                  pl.BlockSpec((1, D_MODEL), lambda i, k: (0, 0))],
        out_specs=[tile, pl.BlockSpec((2, tm, D_MODEL), lambda i, k: (0, i, 0)), tile, tile, tile, tile,
                   pl.BlockSpec((8, D_MODEL), lambda i, k: (0, 0))],
        out_shape=[_sds((s, D_MODEL), F32), _sds((2, s, D_MODEL), BF16)] + [_sds((s, D_MODEL), BF16)] * 4
        + [_sds((8, D_MODEL), F32)],
        scratch_shapes=[pltpu.VMEM((tm, D_MODEL), F32)],
        semantics=("arbitrary", "arbitrary"), args=(du0, w_up_g, x2, dx3, z, z, pa, pb, w_out, w_pa, w_pb, g_ffn),
        comm=comm)


def _mixers_bwd(z, h, dya, dyb, w4, b4, w_a, b_a, w_x, b_x, lam, w3, ch, comm):
    s = z.shape[0]
    cw = RNN_BW
    n_chunks = s // ch

    def body(xr_ref, gr_ref, cb_ref, cc_ref, cx_ref, h_ref, dya_ref, dyb_ref, w4_ref, b4_ref, wa_ref, ba_ref, wx_ref,
             bx_ref, lam_ref, w3_ref,
             dz_ref, dw4_ref, db4_ref, dwa_ref, dba_ref, dwx_ref, dbx_ref, dlam_ref, dw3_ref,
             a_scr, g_scr, q_scr):
        sp = _softplus(-lam_ref[...])
        wa = wa_ref[0].astype(BF16)
        wx = wx_ref[0].astype(BF16)
        for ref in (dw4_ref, db4_ref, dwa_ref, dba_ref, dwx_ref, dbx_ref, dlam_ref, dw3_ref):
            ref[...] = jnp.zeros_like(ref)

        def gates(c):
            xr_ext = _ext_before(xr_ref, c, ch)
            return (xr_ext,) + _gate_chunk(xr_ext, w4_ref, b4_ref, wa, wx, ba_ref, bx_ref, sp, ch)

        def first(c, _):
            rows = pl.ds(pl.multiple_of(c * ch, HALO), ch)
            _, _, _, _, log_a = gates(c)
            a_scr[rows, :] = jnp.exp(log_a)
            gel, dgel = _gelu_and_grad(gr_ref[rows, :].astype(F32))
            dya_c = dya_ref[rows, :].astype(F32)
            g_scr[rows, :] = dya_c * gel
            dz_ref[1, rows, :] = (dya_c * h_ref[rows, :] * dgel).astype(BF16)
            q_ext = _ext_before(cc_ref, c, ch) * _ext_before(cx_ref, c, ch)
            dyb_c = dyb_ref[rows, :].astype(F32)
            dz_ref[2, rows, :] = (dyb_c * _conv3(q_ext, w3_ref, ch)).astype(BF16)
            dyq = dyb_c * cb_ref[rows, :].astype(F32)
            q_scr[rows, :] = dyq
            for k in range(3):
                dw3_ref[k:k + 1, :] += jnp.sum(dyq * _shift_dn(q_ext, 2 - k, ch), axis=0, keepdims=True)
            return 0

        lax.fori_loop(0, n_chunks, first, 0)
        _scan_rev(a_scr, g_scr, s, cw)

        def second(c, _):
            rows = pl.ds(pl.multiple_of(c * ch, HALO), ch)
            _, xc, ra, ia, log_a = gates(c)
            a = a_scr[rows, :]
            mult = jnp.sqrt(-_expm1(2.0 * log_a))
            dh = g_scr[rows, :]
            h_prev = _shift_dn(_ext_before(h_ref, c, ch), 1, ch)
            dlog_a = dh * h_prev * a - (dh * ia * xc) * (a * a) / mult
            dlam_ref[...] += jnp.sum(dlog_a * ra, axis=0, keepdims=True)
            dpre_a = (dlog_a * ((-LRU_C) * sp)) * (ra * (1.0 - ra))
            dpre_x = (dh * mult * xc) * (ia * (1.0 - ia))
            dba_ref[...] += jnp.sum(dpre_a, axis=0, keepdims=True)
            dbx_ref[...] += jnp.sum(dpre_x, axis=0, keepdims=True)
            xcb = xc.astype(BF16)
            dpa_b = dpre_a.astype(BF16)
            dpx_b = dpre_x.astype(BF16)
            dwa_ref[0] += _dot_tn(xcb, dpa_b)
            dwx_ref[0] += _dot_tn(xcb, dpx_b)
            a_scr[rows, :] = dh * mult * ia + _dot_nt(dpa_b, wa) + _dot_nt(dpx_b, wx)
            return 0

        lax.fori_loop(0, n_chunks, second, 0)
        dlam_ref[...] = dlam_ref[...] * (LRU_C * _sigmoid(-lam_ref[...]))

        def third(c, _):
            rows = pl.ds(pl.multiple_of(c * ch, HALO), ch)
            dxc_ext = _ext_after(a_scr, c, ch, n_chunks)
            dxc = _shift_up(dxc_ext, 0, ch)
            xr_ext = _ext_before(xr_ref, c, ch)
            db4_ref[...] += jnp.sum(dxc, axis=0, keepdims=True)
            dxr = w4_ref[3:4, :] * dxc
            dw4_ref[3:4, :] += jnp.sum(dxc * _shift_dn(xr_ext, 0, ch), axis=0, keepdims=True)
            for k in range(3):
                dxr = dxr + w4_ref[k:k + 1, :] * _shift_up(dxc_ext, 3 - k, ch)
                dw4_ref[k:k + 1, :] += jnp.sum(dxc * _shift_dn(xr_ext, 3 - k, ch), axis=0, keepdims=True)
            dz_ref[0, rows, :] = dxr.astype(BF16)
            dyq_ext = _ext_after(q_scr, c, ch, n_chunks)
            dq = w3_ref[2:3, :] * _shift_up(dyq_ext, 0, ch)
            for k in range(2):
                dq = dq + w3_ref[k:k + 1, :] * _shift_up(dyq_ext, 2 - k, ch)
            dz_ref[3, rows, :] = (dq * cx_ref[rows, :].astype(F32)).astype(BF16)
            dz_ref[4, rows, :] = (dq * cc_ref[rows, :].astype(F32)).astype(BF16)
            return 0

        lax.fori_loop(0, n_chunks, third, 0)

    col = lambda j: (0, j)
    vec = pl.BlockSpec((1, cw), col)
    sq = pl.BlockSpec((1, cw, cw), lambda j: (j, 0, 0))
    act = pl.BlockSpec((s, cw), col)
    w4s, w3s = pl.BlockSpec((4, cw), col), pl.BlockSpec((3, cw), col)
    vec_shape = _sds((1, D_MODEL), F32)
    sq_shape = _sds((D_MODEL // cw, cw, cw), F32)
    return _host_call(
        body, name="mixers_bwd", grid=(D_MODEL // cw,),
        in_specs=_z_block_specs(s, cw, range(5)) + [act, act, act, w4s, vec, sq, vec, sq, vec, vec, w3s],
        out_specs=[pl.BlockSpec((5, s, cw), lambda j: (0, 0, j)), w4s, vec, sq, vec, sq, vec, vec, w3s],
        out_shape=[_sds((5, s, D_MODEL), BF16), _sds((4, D_MODEL), F32), vec_shape, sq_shape, vec_shape, sq_shape,
                   vec_shape, vec_shape, _sds((3, D_MODEL), F32)],
        scratch_shapes=[pltpu.VMEM((s, cw), F32)] * 3,
        semantics=("parallel",), args=(z, z, z, z, z, h, dya, dyb, w4, b4, w_a, b_a, w_x, b_x, lam, w3), comm=comm)


def _in_proj_bwd(dz5, dzg, w_in, x, dx2, g_mix, tm, comm):
    s = x.shape[0]

    def body(d5_ref, dg_ref, w_ref, x_ref, dx2_ref, g_ref, dx_ref, sums_ref, acc):
        k = pl.program_id(1)

        @pl.when((pl.program_id(0) == 0) & (k == 0))
        def _():
            sums_ref[...] = jnp.zeros_like(sums_ref)

        @pl.when(k == 0)
        def _():
            acc[...] = jnp.zeros_like(acc)

        @pl.when(k < 5)
        def _():
            acc[...] += _dot_nt(d5_ref[...], w_ref[...])

        @pl.when(k >= 5)
        def _():
            acc[...] += _dot_nt(dg_ref[...], w_ref[...])

        @pl.when(k == N_SEG - 1)
        def _():
            dh1 = acc[...]
            xn, r1 = _rms(x_ref[...])
            sums_ref[0:1, :] += jnp.sum(dh1 * xn, axis=0, keepdims=True)
            dx_ref[...] = dx2_ref[...] + _rms_bwd(dh1 * g_ref[...], xn, r1)

    tile = pl.BlockSpec((tm, D_MODEL), lambda i, k: (i, 0))
    return _host_call(
        body, name="in_proj_bwd", grid=(s // tm, N_SEG),
        in_specs=[pl.BlockSpec((None, tm, D_MODEL), lambda i, k: (jnp.minimum(k, 4), i, 0)),
                  pl.BlockSpec((None, tm, D_MODEL), lambda i, k: (jnp.maximum(k - 5, 0), i, 0)),
                  pl.BlockSpec((D_MODEL, D_MODEL), lambda i, k: (0, k)), tile, tile,
                  pl.BlockSpec((1, D_MODEL), lambda i, k: (0, 0))],
        out_specs=[tile, pl.BlockSpec((8, D_MODEL), lambda i, k: (0, 0))],
        out_shape=[_sds((s, D_MODEL), F32), _sds((8, D_MODEL), F32)],
        scratch_shapes=[pltpu.VMEM((tm, D_MODEL), F32)],
        semantics=("arbitrary", "arbitrary"), args=(dz5, dzg, w_in, x, dx2, g_mix), comm=comm)


def _weight_grad(a, b, a_spec, b_spec, out_spec, out_shape, grid, tm, tn, name):
    nk = grid[2]

    def body(a_ref, b_ref, o_ref, acc):
        k = pl.program_id(2)

        @pl.when(k == 0)
        def _():
            acc[...] = jnp.zeros_like(acc)

        acc[...] += _dot_tn(a_ref[...].astype(BF16), b_ref[...].astype(BF16))

        @pl.when(k == nk - 1)
        def _():
            o_ref[...] = acc[...].astype(o_ref.dtype)

    return pl.pallas_call(
        body, name=name, grid=grid, in_specs=[a_spec, b_spec], out_specs=out_spec, out_shape=out_shape,
        scratch_shapes=[pltpu.VMEM((tm, tn), F32)],
        compiler_params=_params("parallel", "parallel", "arbitrary"),
    )(a, b)


def _wgrad_2d(a, b, name, tk):
    s, m = a.shape
    n = b.shape[1]
    tm, tn = min(m, 512), min(n, 1024)
    return _weight_grad(
        a, b, pl.BlockSpec((tk, tm), lambda i, j, k: (k, i)), pl.BlockSpec((tk, tn), lambda i, j, k: (k, j)),
        pl.BlockSpec((tm, tn), lambda i, j, k: (i, j)), _sds((m, n), BF16), (m // tm, n // tn, s // tk), tm, tn, name)


def _wgrad_segments(a, b3, name, tk):
    s, m = a.shape
    g = b3.shape[0]
    tm = 512
    return _weight_grad(
        a, b3, pl.BlockSpec((tk, tm), lambda i, j, k: (k, i)), pl.BlockSpec((None, tk, D_MODEL), lambda i, j, k: (j, k, 0)),
        pl.BlockSpec((tm, D_MODEL), lambda i, j, k: (i, j)), _sds((m, g * D_MODEL), BF16), (m // tm, g, s // tk), tm,
        D_MODEL, name)


def _wgrad_up(h2, du0, bw, tk):
    s = h2.shape[0]
    tm = 512
    per_half = N_DEV // 2
    return _weight_grad(
        h2, du0, pl.BlockSpec((tk, tm), lambda i, j, k: (k, i)),
        pl.BlockSpec((None, tk, bw), lambda i, j, k: (j // per_half, k, j % per_half)),
        pl.BlockSpec((None, tm, bw), lambda i, j, k: (j, i, 0)), _sds((N_DEV, D_MODEL, bw), BF16),
        (D_MODEL // tm, N_DEV, s // tk), tm, bw, "wgrad_up")


def _adam_math(w, g, m, v):
    m = ADAM_B1 * m + (1.0 - ADAM_B1) * g
    v = ADAM_B2 * v + (1.0 - ADAM_B2) * jnp.square(g)
    m_hat = m / (1.0 - ADAM_B1 ** ADAM_STEP)
    v_hat = v / (1.0 - ADAM_B2 ** ADAM_STEP)
    delta = -ADAM_LR * (m_hat / (jnp.sqrt(v_hat) + ADAM_EPS) + ADAM_WD * w)
    return delta, m, v


def _adam_shard(parts, w, m, v, name):
    r, c = w.shape
    tr = min(r, 128)

    def body(p_ref, w_ref, m_ref, v_ref, g_ref, d_ref, nm_ref, nv_ref):
        g = p_ref[0].astype(F32)
        for k in range(1, N_DEV):
            g = g + p_ref[k].astype(F32)
        g_ref[...] = g
        d_ref[...], nm_ref[...], nv_ref[...] = _adam_math(w_ref[...], g, m_ref[...], v_ref[...])

    tile = pl.BlockSpec((tr, c), lambda i: (i, 0))
    return pl.pallas_call(
        body, name=name, grid=(r // tr,),
        in_specs=[pl.BlockSpec((N_DEV, tr, c), lambda i: (0, i, 0)), tile, tile, tile],
        out_specs=[tile] * 4, out_shape=[_sds((r, c), F32)] * 4,
        compiler_params=_params("parallel"),
    )(parts, w, m, v)


def _adam_plain(g, w, m, v, name):
    def body(g_ref, w_ref, m_ref, v_ref, d_ref, nm_ref, nv_ref):
        d_ref[...], nm_ref[...], nv_ref[...] = _adam_math(w_ref[...], g_ref[...], m_ref[...], v_ref[...])

    vmem = pl.BlockSpec(memory_space=pltpu.VMEM)
    return pl.pallas_call(
        body, name=name, in_specs=[vmem] * 4, out_specs=[vmem] * 3, out_shape=[_sds(w.shape, F32)] * 3,
    )(g, w, m, v)


_REPL = ("g_mix", "rnn_conv_b", "b_rg_a", "b_rg_x", "lru_lambda", "g_ffn", "g_ple", "g_final", "ffn_conv_b", "pad2",
         "w_rg_a", "w_rg_x")
REPL_ROWS = 272


def _pack_repl(t):
    rows = [t[n].reshape(1, D_MODEL) for n in _REPL[:8]]
    rows.append(t["ffn_conv_b"].reshape(6, D_MODEL))
    rows.append(t.get("pad2", jnp.zeros((2, D_MODEL), F32)))
    rows.append(t["w_rg_a"].reshape(128, D_MODEL))
    rows.append(t["w_rg_x"].reshape(128, D_MODEL))
    return jnp.concatenate(rows, axis=0)


def _unpack_repl(pack, shapes):
    out = {n: pack[i].reshape(shapes[n]) for i, n in enumerate(_REPL[:8])}
    out["ffn_conv_b"] = pack[8:14].reshape(shapes["ffn_conv_b"])
    out["w_rg_a"] = pack[16:144].reshape(shapes["w_rg_a"])
    out["w_rg_x"] = pack[144:272].reshape(shapes["w_rg_x"])
    return out


def _pack_conv_shard(rnn, sc, ffn):
    top = jnp.concatenate([rnn[:3], sc, ffn], axis=1)
    row3 = jnp.concatenate([rnn[3:4], jnp.zeros((1, D_MODEL - RNN_BW), F32)], axis=1)
    return jnp.concatenate([top, row3, jnp.zeros((4, D_MODEL), F32)], axis=0)


def _unpack_conv_shard(pack):
    rnn = jnp.concatenate([pack[:3, :RNN_BW], pack[3:4, :RNN_BW]], axis=0)
    return rnn, pack[:3, RNN_BW:2 * RNN_BW], pack[:3, 2 * RNN_BW:]


_SHARDED_BIG = ("w_in", "w_proj_a", "w_proj_b", "w_out", "w_up", "w_down", "w_ple_gate", "w_ple_proj")
_NAMES = ("g_mix", "w_in", "rnn_conv_w", "rnn_conv_b", "w_rg_a", "b_rg_a", "w_rg_x", "b_rg_x", "lru_lambda", "sc_conv_w",
          "w_proj_a", "w_proj_b", "w_out", "g_ffn", "w_up", "ffn_conv_w", "ffn_conv_b", "w_down", "w_ple_gate",
          "w_ple_proj", "g_ple", "g_final")


def _step(x, p, target, w, m, v):
    s = x.shape[0]
    tm = min(s, 256)
    tm_wide = min(s, 512)
    ch = min(s, 256)
    my_index = 4 * lax.axis_index("x") + 2 * lax.axis_index("y") + lax.axis_index("c")

    big = {n: w[n][0] for n in _SHARDED_BIG}
    conv_shard = _pack_conv_shard(w["rnn_conv_w"][0], w["sc_conv_w"][0], w["ffn_conv_w"][0])
    shards = {n: big[n].astype(BF16) for n in _SHARDED_BIG}
    w_in_g, conv_all = _comm_only(_gather_comm([shards["w_in"], conv_shard]), "in_proj_weight_gather")
    w_in = jnp.transpose(w_in_g, (1, 0, 2)).reshape(D_MODEL, N_SEG * D_MODEL)
    w4 = jnp.transpose(jnp.concatenate([conv_all[:, :3, :RNN_BW], conv_all[:, 3:4, :RNN_BW]], axis=1),
                       (1, 0, 2)).reshape(4, D_MODEL)
    w3 = jnp.transpose(conv_all[:, :3, RNN_BW:2 * RNN_BW], (1, 0, 2)).reshape(3, D_MODEL)
    wc = jnp.transpose(conv_all[:, :3, 2 * RNN_BW:], (1, 0, 2)).reshape(3, 2, D_FF).transpose(1, 0, 2)
    bc = w["ffn_conv_b"].reshape(2, 1, D_FF)
    b4, b_a, b_x, lam = w["rnn_conv_b"], w["b_rg_a"], w["b_rg_x"], w["lru_lambda"]
    w_a, w_x = w["w_rg_a"][0], w["w_rg_x"][0]
    g_final = w["g_final"].reshape(1, D_MODEL)

    tk = min(s, 512)
    received = {}

    comm = _gather_comm([shards[n] for n in ("w_proj_a", "w_proj_b", "w_out", "w_up")])
    (z, h1), (w_pa, w_pb, w_out, w_up_g) = _norm_in_proj(x, w["g_mix"], w_in, tm_wide, comm)
    w_pa, w_pb, w_out = (a.reshape(D_MODEL, D_MODEL) for a in (w_pa, w_pb, w_out))
    comm = _gather_comm([shards[n] for n in ("w_down", "w_ple_gate", "w_ple_proj")])
    (ya, yb, h), (w_down, w_gate, w_pp) = _mixers_fwd(z, w4, b4, w_a, b_a, w_x, b_x, lam, w3, ch, comm)
    w_down = w_down.reshape(D_FF, D_MODEL)
    w_gate = w_gate.reshape(D_MODEL, D_MODEL)
    w_pp = jnp.transpose(w_pp, (1, 0, 2)).reshape(D_PLE, D_MODEL)
    pa, pb, mm, x2, h2 = _merge_out(ya, yb, z, x, w_pa, w_pb, w_out, w["g_ffn"], tm)
    u0 = _up_proj(h2, w_up_g, tm_wide)
    f = _ffn_act(u0, wc, bc, 256, ch)
    x3, dgt, de0, dx3, df, head_sums = _head(x2, f, p, target, w_down, w_gate, w_pp, w["g_ple"], g_final, tm)
    parts = [_wgrad_2d(x3, dgt, "wgrad_ple_gate", tk).reshape(N_DEV, RNN_BW, D_MODEL),
             jnp.transpose(_wgrad_2d(p, de0, "wgrad_ple_proj", tk).reshape(D_PLE, N_DEV, RNN_BW), (1, 0, 2)),
             _wgrad_2d(f, dx3, "wgrad_down", tk).reshape(N_DEV, D_FF // N_DEV, D_MODEL)]
    (du0, dwc, dbc), got = _ffn_act_bwd(u0, df, wc, bc, 256, ch, _exchange_comm(parts))
    received.update(zip(("w_ple_gate", "w_ple_proj", "w_down"), got))
    parts = [_wgrad_up(h2, du0, w_up_g.shape[2], tk)]
    (dx2, dzg, dpa, dpb, dya, dyb, ffn_sums), got = _up_bwd_merge_bwd(du0, w_up_g, x2, dx3, z, pa, pb, w_out, w_pa, w_pb,
                                                                       w["g_ffn"], tm, _exchange_comm(parts))
    received["w_up"] = got[0]
    parts = [_wgrad_2d(mm, dx2, "wgrad_out", tk).reshape(N_DEV, RNN_BW, D_MODEL),
             _wgrad_2d(ya, dpa, "wgrad_proj_a", tk).reshape(N_DEV, RNN_BW, D_MODEL),
             _wgrad_2d(yb, dpb, "wgrad_proj_b", tk).reshape(N_DEV, RNN_BW, D_MODEL)]
    (dz5, dw4, db4, dwa, dba, dwx, dbx, dlam, dw3), got = _mixers_bwd(z, h, dya, dyb, w4, b4, w_a, b_a, w_x, b_x, lam, w3,
                                                                       ch, _exchange_comm(parts))
    received.update(zip(("w_out", "w_proj_a", "w_proj_b"), got))
    dw_in = jnp.concatenate([_wgrad_segments(h1, dz5, "wgrad_in_mix", tk), _wgrad_segments(h1, dzg, "wgrad_in_gate", tk)],
                            axis=1)
    bw_in = N_SEG * D_MODEL // N_DEV
    parts = [jnp.transpose(dw_in.reshape(D_MODEL, N_DEV, bw_in), (1, 0, 2))]
    (grad_x, mix_sums), got = _in_proj_bwd(dz5, dzg, w_in, x, dx2, w["g_mix"], tm, _exchange_comm(parts))
    received["w_in"] = got[0]

    small = {"g_mix": mix_sums[0], "rnn_conv_b": db4, "b_rg_a": dba, "b_rg_x": dbx, "lru_lambda": dlam,
             "g_ffn": ffn_sums[0], "g_ple": head_sums[2], "g_final": head_sums[1], "ffn_conv_b": dbc,
             "pad2": jnp.concatenate([head_sums[0:1], jnp.zeros((1, D_MODEL), F32)], axis=0),
             "w_rg_a": dwa, "w_rg_x": dwx}
    conv_rows = jnp.concatenate([dw4, dw3, jnp.zeros((1, D_MODEL), F32),
                                 jnp.transpose(dwc, (1, 0, 2)).reshape(18, D_MODEL),
                                 jnp.zeros((PACK_ROWS - REPL_ROWS - 26, D_MODEL), F32)], axis=0)
    total = _small_all_reduce(jnp.concatenate([_pack_repl(small), conv_rows], axis=0))
    loss = jnp.sum(total[14]) * (0.5 / D_MODEL)

    out = {}
    for n in _SHARDED_BIG:
        shard = big[n]
        res = _adam_shard(received[n].reshape((N_DEV,) + shard.shape), shard, m[n][0], v[n][0], "adam_" + n)
        out[n] = [r[None] for r in res]
    shapes = {n: w[n].shape for n in _REPL if n != "pad2"}
    g_repl = total[:REPL_ROWS]
    res = _adam_plain(g_repl, _pack_repl(w), _pack_repl(m), _pack_repl(v), "adam_replicated")
    unpacked = [_unpack_repl(r, shapes) for r in (g_repl,) + tuple(res)]
    for n in shapes:
        out[n] = [u[n] for u in unpacked]
    g_rnn = lax.dynamic_slice(total[272:276], (0, my_index * RNN_BW), (4, RNN_BW))
    g_sc = lax.dynamic_slice(total[276:279], (0, my_index * RNN_BW), (3, RNN_BW))
    bw_ffn = 2 * D_FF // N_DEV
    g_ffn_conv = lax.dynamic_slice(total[280:298].reshape(3, 2 * D_FF), (0, my_index * bw_ffn), (3, bw_ffn))
    g_conv = _pack_conv_shard(g_rnn, g_sc, g_ffn_conv)
    res = _adam_plain(g_conv, conv_shard,
                      _pack_conv_shard(m["rnn_conv_w"][0], m["sc_conv_w"][0], m["ffn_conv_w"][0]),
                      _pack_conv_shard(v["rnn_conv_w"][0], v["sc_conv_w"][0], v["ffn_conv_w"][0]), "adam_conv")
    unpacked = [_unpack_conv_shard(r) for r in (g_conv,) + tuple(res)]
    for i, n in enumerate(("rnn_conv_w", "sc_conv_w", "ffn_conv_w")):
        out[n] = [u[i][None] for u in unpacked]

    return (loss, grad_x[None]) + tuple(out[n][k] for k in range(4) for n in _NAMES)


def kernel(x, p, g_mix, w_in, rnn_conv_w, rnn_conv_b, w_rg_a, b_rg_a, w_rg_x, b_rg_x, lru_lambda, sc_conv_w, w_proj_a, w_proj_b, w_out, g_ffn, w_up, ffn_conv_w, ffn_conv_b, w_down, w_ple_gate, w_ple_proj, g_ple, g_final, loss_target, m_g_mix, m_w_in, m_rnn_conv_w, m_rnn_conv_b, m_w_rg_a, m_b_rg_a, m_w_rg_x, m_b_rg_x, m_lru_lambda, m_sc_conv_w, m_w_proj_a, m_w_proj_b, m_w_out, m_g_ffn, m_w_up, m_ffn_conv_w, m_ffn_conv_b, m_w_down, m_w_ple_gate, m_w_ple_proj, m_g_ple, m_g_final, v_g_mix, v_w_in, v_rnn_conv_w, v_rnn_conv_b, v_w_rg_a, v_b_rg_a, v_w_rg_x, v_b_rg_x, v_lru_lambda, v_sc_conv_w, v_w_proj_a, v_w_proj_b, v_w_out, v_g_ffn, v_w_up, v_ffn_conv_w, v_ffn_conv_b, v_w_down, v_w_ple_gate, v_w_ple_proj, v_g_ple, v_g_final):
    given = dict(locals())
    w = {n: given[n] for n in _NAMES}
    m = {n: given["m_" + n] for n in _NAMES}
    v = {n: given["v_" + n] for n in _NAMES}
    return _step(x[0], p[0, 0], loss_target[0], w, m, v)
```

```python
import functools

import jax
import jax.numpy as jnp
from jax import lax
from jax.experimental import pallas as pl
from jax.experimental.pallas import tpu as pltpu

F32 = jnp.float32
BF16 = jnp.bfloat16
MESH_ID = pl.DeviceIdType.MESH

N_DEV = 8
D_MODEL = 1024
D_PLE = 256
RNN_BW = 128
N_SEG = 7
D_FF = 3072
LRU_C = 8.0
EPS = 1e-6
ADAM_LR = 0.001
ADAM_B1 = 0.9
ADAM_B2 = 0.999
ADAM_EPS = 1e-08
ADAM_WD = 0.01
ADAM_STEP = 10

HALO = 16
VMEM_LIMIT = 56 * 1024 * 1024
PACK_ROWS = 320
PIECE = PACK_ROWS // N_DEV


def _dot(a, b):
    return jnp.dot(a, b, preferred_element_type=F32)


def _dot_nt(a, b):
    return lax.dot_general(a, b, (((1,), (1,)), ((), ())), preferred_element_type=F32)


def _dot_tn(a, b):
    return lax.dot_general(a, b, (((0,), (0,)), ((), ())), preferred_element_type=F32)


def _sigmoid(x):
    return jax.nn.sigmoid(x)


_GELU_C = 0.7978845608028654
_GELU_K = 0.044715


def _gelu(x):
    return 0.5 * x * (1.0 + jnp.tanh(_GELU_C * (x + _GELU_K * (x * x * x))))


def _gelu_and_grad(x):
    x2 = x * x
    t = jnp.tanh(_GELU_C * (x + _GELU_K * (x * x2)))
    g = 0.5 * x * (1.0 + t)
    dg = 0.5 * (1.0 + t) + 0.5 * x * (1.0 - t * t) * (_GELU_C * (1.0 + 3.0 * _GELU_K * x2))
    return g, dg


def _expm1(x):
    poly = x * (1.0 + x * (0.5 + x * (1.0 / 6.0 + x * (1.0 / 24.0 + x * (1.0 / 120.0)))))
    return jnp.where(jnp.abs(x) < 0.08, poly, jnp.exp(x) - 1.0)


def _softplus(x):
    return jnp.maximum(x, 0.0) + jnp.log1p(jnp.exp(-jnp.abs(x)))


def _rms(u):
    r = lax.rsqrt(jnp.mean(u * u, axis=-1, keepdims=True) + EPS)
    return u * r, r


def _rms_bwd(dn, un, r):
    return r * (dn - un * jnp.mean(dn * un, axis=-1, keepdims=True))


def _shift_dn(ext, s, n):
    if s == 0:
        return ext[HALO:HALO + n]
    return pltpu.roll(ext, s, 0)[HALO:HALO + n]


def _shift_up(ext, s, n):
    if s == 0:
        return ext[0:n]
    return pltpu.roll(ext, n + HALO - s, 0)[0:n]


def _ext_before(ref, c, ch):
    t0 = c * ch
    prev = pl.multiple_of(jnp.maximum(t0 - HALO, 0), HALO)
    halo = jnp.where(c > 0, ref[pl.ds(prev, HALO), :].astype(F32), 0.0)
    cur = ref[pl.ds(pl.multiple_of(t0, HALO), ch), :].astype(F32)
    return jnp.concatenate([halo, cur], axis=0)


def _ext_after(ref, c, ch, n_chunks):
    t0 = c * ch
    nxt = pl.multiple_of(jnp.minimum(t0 + ch, (n_chunks - 1) * ch + ch - HALO), HALO)
    halo = jnp.where(c < n_chunks - 1, ref[pl.ds(nxt, HALO), :].astype(F32), 0.0)
    cur = ref[pl.ds(pl.multiple_of(t0, HALO), ch), :].astype(F32)
    return jnp.concatenate([cur, halo], axis=0)


def _rows(ref, c, ch):
    return ref[pl.ds(pl.multiple_of(c * ch, HALO), ch), :]


def _scan_fwd(a_ref, b_ref, h_ref, n_rows, cw):
    rows = lax.broadcasted_iota(jnp.int32, (8, cw), 0)

    def body(g, carry):
        t = pl.multiple_of(g * 8, 8)
        a = a_ref[pl.ds(t, 8), :]
        b = b_ref[pl.ds(t, 8), :]
        for d in (1, 2, 4):
            m = rows >= d
            b = jnp.where(m, a * pltpu.roll(b, d, 0) + b, b)
            a = jnp.where(m, a * pltpu.roll(a, d, 0), a)
        h = a * carry + b
        h_ref[pl.ds(t, 8), :] = h
        return jnp.broadcast_to(h[7:8, :], (8, cw))

    lax.fori_loop(0, n_rows // 8, body, jnp.zeros((8, cw), F32), unroll=2)


def _scan_rev(a_ref, g_ref, n_rows, cw):
    rows = lax.broadcasted_iota(jnp.int32, (8, cw), 0)
    n_groups = n_rows // 8

    def body(i, carry):
        dh_next, a_next = carry
        t = pl.multiple_of((n_groups - 1 - i) * 8, 8)
        a = a_ref[pl.ds(t, 8), :]
        g = g_ref[pl.ds(t, 8), :]
        an = jnp.where(rows < 7, pltpu.roll(a, 7, 0), a_next)
        for d in (1, 2, 4):
            m = rows < 8 - d
            g = jnp.where(m, an * pltpu.roll(g, 8 - d, 0) + g, g)
            an = jnp.where(m, an * pltpu.roll(an, 8 - d, 0), an)
        dh = an * dh_next + g
        g_ref[pl.ds(t, 8), :] = dh
        return (jnp.broadcast_to(dh[0:1, :], (8, cw)), jnp.broadcast_to(a[0:1, :], (8, cw)))

    zero = jnp.zeros((8, cw), F32)
    lax.fori_loop(0, n_groups, body, (zero, zero), unroll=2)


def _params(*sem):
    return pltpu.CompilerParams(dimension_semantics=sem, vmem_limit_bytes=VMEM_LIMIT)


def _sds(shape, dtype):
    return jax.ShapeDtypeStruct(shape, dtype)


def _resident(shape):
    zeros = (0,) * len(shape)
    return pl.BlockSpec(shape, lambda *_: zeros, pipeline_mode=pl.Buffered(1))


class _Comm:
    def __init__(self, inputs, out_shape, scratch, start, finish):
        self.inputs, self.out_shape, self.scratch, self.start, self.finish = inputs, out_shape, scratch, start, finish


def _sem_scratch(n):
    return [pltpu.SemaphoreType.DMA((n, 7)), pltpu.SemaphoreType.DMA((n, 7)), pltpu.SemaphoreType.DMA((n,))]


def _gather_comm(shards):
    n = len(shards)

    def plan(ins, outs, sems):
        send_sems, recv_sems, local_sems = sems
        x, y, c = lax.axis_index("x"), lax.axis_index("y"), lax.axis_index("c")
        me, sibling = (x, y, c), (x, y, 1 - c)
        chips = [(1 - x, y), (x, 1 - y), (1 - x, 1 - y)]

        def slot(t, dev):
            return outs[t].at[4 * dev[0] + 2 * dev[1] + dev[2]]

        def copy(t, k, block, to, src=None):
            return pltpu.make_async_remote_copy(
                src_ref=slot(t, block) if src is None else src, dst_ref=slot(t, block),
                send_sem=send_sems.at[t, k], recv_sem=recv_sems.at[t, k],
                device_id=to, device_id_type=MESH_ID)

        mine = [pltpu.make_async_copy(ins[t], slot(t, me), local_sems.at[t]) for t in range(n)]
        first = []
        for t in range(n):
            first.append(copy(t, 0, me, sibling, src=ins[t]))
            first += [copy(t, 1 + j, me, (*chip, c), src=ins[t]) for j, chip in enumerate(chips)]
        return me, sibling, chips, c, copy, mine, first

    def start(ins, outs, sems):
        *_, mine, first = plan(ins, outs, sems)
        for cp in mine + first:
            cp.start()

    def finish(ins, outs, sems):
        me, sibling, chips, c, copy, mine, first = plan(ins, outs, sems)
        passed = []
        for t in range(n):
            for j, chip in enumerate(chips):
                copy(t, 1 + j, (*chip, c), me).wait_recv()
                cp = copy(t, 4 + j, (*chip, c), sibling)
                cp.start()
                passed.append(cp)
        for t in range(n):
            copy(t, 0, sibling, me).wait_recv()
            for j, chip in enumerate(chips):
                copy(t, 4 + j, (*chip, 1 - c), me).wait_recv()
        for cp in first + passed:
            cp.wait_send()
        for cp in mine:
            cp.wait()

    return _Comm(list(shards), [_sds((N_DEV,) + s.shape, s.dtype) for s in shards], _sem_scratch(n), start, finish)


def _peer(d):
    x, y, c = lax.axis_index("x"), lax.axis_index("y"), lax.axis_index("c")
    px = 1 - x if d & 4 else x
    py = 1 - y if d & 2 else y
    pc = 1 - c if d & 1 else c
    return (px, py, pc), 4 * px + 2 * py + pc


def _exchange_comm(parts):
    n = len(parts)

    def plan(ins, outs, sems):
        send_sems, recv_sems, local_sems = sems
        _, me = _peer(0)

        def copy(t, d):
            peer, idx = _peer(d)
            return pltpu.make_async_remote_copy(
                src_ref=ins[t].at[idx], dst_ref=outs[t].at[d],
                send_sem=send_sems.at[t, d - 1], recv_sem=recv_sems.at[t, d - 1],
                device_id=peer, device_id_type=MESH_ID)

        mine = [pltpu.make_async_copy(ins[t].at[me], outs[t].at[0], local_sems.at[t]) for t in range(n)]
        return mine, [copy(t, d) for t in range(n) for d in range(1, N_DEV)]

    def start(ins, outs, sems):
        mine, copies = plan(ins, outs, sems)
        for cp in mine + copies:
            cp.start()

    def finish(ins, outs, sems):
        mine, copies = plan(ins, outs, sems)
        for cp in copies:
            cp.wait_recv()
        for cp in copies:
            cp.wait_send()
        for cp in mine:
            cp.wait()

    return _Comm(list(parts), [_sds(p.shape, p.dtype) for p in parts], _sem_scratch(n), start, finish)


def _comm_only(comm, name):
    n = len(comm.inputs)

    def body(*refs):
        ins, outs, sems = refs[:n], refs[n:2 * n], refs[2 * n:]
        comm.start(ins, outs, sems)
        comm.finish(ins, outs, sems)

    any_spec = pl.BlockSpec(memory_space=pl.ANY)
    return pl.pallas_call(body, name=name, out_shape=comm.out_shape, in_specs=[any_spec] * n, out_specs=[any_spec] * n,
                          scratch_shapes=comm.scratch)(*comm.inputs)


def _host_call(body, *, name, grid, in_specs, out_specs, out_shape, scratch_shapes=(), semantics, args, comm=None):
    if comm is None:
        res = pl.pallas_call(body, name=name, grid=grid, in_specs=in_specs, out_specs=out_specs, out_shape=out_shape,
                             scratch_shapes=list(scratch_shapes), compiler_params=_params(*semantics))(*args)
        return res, []
    n_in, n_out, n_scr, n_c = len(in_specs), len(out_specs), len(scratch_shapes), len(comm.inputs)

    def with_comm(*refs):
        ins, refs = refs[:n_in], refs[n_in:]
        c_ins, refs = refs[:n_c], refs[n_c:]
        outs, refs = refs[:n_out], refs[n_out:]
        c_outs, refs = refs[:n_c], refs[n_c:]
        scr, sems = refs[:n_scr], refs[n_scr:]
        ids = [pl.program_id(a) for a in range(len(grid))]
        first = functools.reduce(jnp.logical_and, [i == 0 for i in ids])
        last = functools.reduce(jnp.logical_and, [i == g - 1 for i, g in zip(ids, grid)])

        @pl.when(first)
        def _():
            comm.start(c_ins, c_outs, sems)

        body(*ins, *outs, *scr)

        @pl.when(last)
        def _():
            comm.finish(c_ins, c_outs, sems)

    any_spec = pl.BlockSpec(memory_space=pl.ANY)
    res = pl.pallas_call(
        with_comm, name=name, grid=grid, in_specs=list(in_specs) + [any_spec] * n_c,
        out_specs=list(out_specs) + [any_spec] * n_c, out_shape=list(out_shape) + comm.out_shape,
        scratch_shapes=list(scratch_shapes) + comm.scratch,
        compiler_params=_params(*(["arbitrary"] * len(grid))))(*args, *comm.inputs)
    return res[:n_out], res[n_out:]


def _small_all_reduce(pack):
    def body(p_ref, o_ref, land, s1, r1, s2, r2):
        _, me = _peer(0)

        def piece(ref, idx):
            return ref.at[pl.ds(pl.multiple_of(idx * PIECE, 8), PIECE), :]

        def scatter(d):
            peer, idx = _peer(d)
            return pltpu.make_async_remote_copy(
                src_ref=piece(p_ref, idx), dst_ref=land.at[d - 1], send_sem=s1.at[d - 1], recv_sem=r1.at[d - 1],
                device_id=peer, device_id_type=MESH_ID)

        def gather(d, from_idx):
            peer, _ = _peer(d)
            return pltpu.make_async_remote_copy(
                src_ref=piece(o_ref, from_idx), dst_ref=piece(o_ref, from_idx), send_sem=s2.at[d - 1],
                recv_sem=r2.at[d - 1], device_id=peer, device_id_type=MESH_ID)

        first = [scatter(d) for d in range(1, N_DEV)]
        for cp in first:
            cp.start()
        acc = piece(p_ref, me)[...]
        for d in range(1, N_DEV):
            first[d - 1].wait_recv()
            acc = acc + land[d - 1]
        piece(o_ref, me)[...] = acc
        second = [gather(d, me) for d in range(1, N_DEV)]
        for cp in second:
            cp.start()
        for d in range(1, N_DEV):
            gather(d, _peer(d)[1]).wait_recv()
        for cp in first + second:
            cp.wait_send()

    vmem = pl.BlockSpec(memory_space=pltpu.VMEM)
    return pl.pallas_call(
        body, name="small_all_reduce",
        out_shape=_sds(pack.shape, F32), in_specs=[vmem], out_specs=vmem,
        scratch_shapes=[pltpu.VMEM((N_DEV - 1, PIECE, D_MODEL), F32)] + [pltpu.SemaphoreType.DMA((N_DEV - 1,))] * 4,
    )(pack)


def _norm_in_proj(x, g_mix, w_in, tm, comm):
    s = x.shape[0]

    def body(x_ref, g_ref, w_ref, z_ref, h_ref):
        xn, _ = _rms(x_ref[...])
        h = (xn * g_ref[...]).astype(BF16)
        h_ref[...] = h
        for j in range(N_SEG):
            cols = slice(j * D_MODEL, (j + 1) * D_MODEL)
            z_ref[:, cols] = _dot(h, w_ref[:, cols]).astype(BF16)

    return _host_call(
        body, name="norm_in_proj", grid=(s // tm,),
        in_specs=[pl.BlockSpec((tm, D_MODEL), lambda i: (i, 0)), pl.BlockSpec((1, D_MODEL), lambda i: (0, 0)),
                  _resident(w_in.shape)],
        out_specs=[pl.BlockSpec((tm, N_SEG * D_MODEL), lambda i: (i, 0)), pl.BlockSpec((tm, D_MODEL), lambda i: (i, 0))],
        out_shape=[_sds((s, N_SEG * D_MODEL), BF16), _sds((s, D_MODEL), BF16)],
        semantics=("parallel",), args=(x, g_mix, w_in), comm=comm)


def _gate_chunk(xr_ext, w4_ref, cb_ref, wa, wx, ba_ref, bx_ref, sp, ch):
    xc = cb_ref[...] + w4_ref[3:4, :] * _shift_dn(xr_ext, 0, ch)
    for k in range(3):
        xc = xc + w4_ref[k:k + 1, :] * _shift_dn(xr_ext, 3 - k, ch)
    xcb = xc.astype(BF16)
    ra = _sigmoid(_dot(xcb, wa) + ba_ref[...])
    ia = _sigmoid(_dot(xcb, wx) + bx_ref[...])
    log_a = (-LRU_C) * ra * sp
    return xc, ra, ia, log_a


def _conv3(q_ext, w3_ref, ch):
    y = w3_ref[2:3, :] * _shift_dn(q_ext, 0, ch)
    for k in range(2):
        y = y + w3_ref[k:k + 1, :] * _shift_dn(q_ext, 2 - k, ch)
    return y


def _z_block_specs(s, cw, segs):
    nb = D_MODEL // cw
    return [pl.BlockSpec((s, cw), functools.partial(lambda j, seg: (0, seg * nb + j), seg=seg)) for seg in segs]


def _mixers_fwd(z, w4, b4, w_a, b_a, w_x, b_x, lam, w3, ch, comm):
    s = z.shape[0]
    cw = RNN_BW
    n_chunks = s // ch

    def body(xr_ref, gr_ref, cb_ref, cc_ref, cx_ref, w4_ref, b4_ref, wa_ref, ba_ref, wx_ref, bx_ref, lam_ref, w3_ref,
             ya_ref, yb_ref, h_ref, a_scr, b_scr):
        sp = _softplus(-lam_ref[...])
        wa = wa_ref[0].astype(BF16)
        wx = wx_ref[0].astype(BF16)

        def gates(c, _):
            xc, _, ia, log_a = _gate_chunk(_ext_before(xr_ref, c, ch), w4_ref, b4_ref, wa, wx, ba_ref, bx_ref, sp, ch)
            rows = pl.ds(pl.multiple_of(c * ch, HALO), ch)
            a_scr[rows, :] = jnp.exp(log_a)
            b_scr[rows, :] = jnp.sqrt(-_expm1(2.0 * log_a)) * (ia * xc)
            q_ext = _ext_before(cc_ref, c, ch) * _ext_before(cx_ref, c, ch)
            yb_ref[rows, :] = (_rows(cb_ref, c, ch).astype(F32) * _conv3(q_ext, w3_ref, ch)).astype(BF16)
            return 0

        lax.fori_loop(0, n_chunks, gates, 0)
        _scan_fwd(a_scr, b_scr, h_ref, s, cw)

        def outputs(c, _):
            rows = pl.ds(pl.multiple_of(c * ch, HALO), ch)
            ya_ref[rows, :] = (h_ref[rows, :] * _gelu(gr_ref[rows, :].astype(F32))).astype(BF16)
            return 0

        lax.fori_loop(0, n_chunks, outputs, 0)

    col = lambda j: (0, j)
    vec = pl.BlockSpec((1, cw), col)
    sq = pl.BlockSpec((1, cw, cw), lambda j: (j, 0, 0))
    act = pl.BlockSpec((s, cw), col)
    return _host_call(
        body, name="mixers_fwd", grid=(D_MODEL // cw,),
        in_specs=_z_block_specs(s, cw, range(5)) + [pl.BlockSpec((4, cw), col), vec, sq, vec, sq, vec, vec,
                                                     pl.BlockSpec((3, cw), col)],
        out_specs=[act, act, act],
        out_shape=[_sds((s, D_MODEL), BF16), _sds((s, D_MODEL), BF16), _sds((s, D_MODEL), F32)],
        scratch_shapes=[pltpu.VMEM((s, cw), F32), pltpu.VMEM((s, cw), F32)],
        semantics=("parallel",), args=(z, z, z, z, z, w4, b4, w_a, b_a, w_x, b_x, lam, w3), comm=comm)


def _merge_out(ya, yb, z, x, w_pa, w_pb, w_out, g_ffn, tm):
    s = x.shape[0]

    def body(ya_ref, yb_ref, ga_ref, gb_ref, x_ref, wa_ref, wb_ref, wo_ref, g_ref, pa_ref, pb_ref, m_ref, x2_ref, h2_ref):
        pa = _dot(ya_ref[...], wa_ref[...])
        pb = _dot(yb_ref[...], wb_ref[...])
        pa_ref[...] = pa.astype(BF16)
        pb_ref[...] = pb.astype(BF16)
        m = (_sigmoid(ga_ref[...].astype(F32)) * pa + _sigmoid(gb_ref[...].astype(F32)) * pb).astype(BF16)
        m_ref[...] = m
        x2 = x_ref[...] + _dot(m, wo_ref[...])
        x2_ref[...] = x2
        xn, _ = _rms(x2)
        h2_ref[...] = (xn * g_ref[...]).astype(BF16)

    tile = pl.BlockSpec((tm, D_MODEL), lambda i: (i, 0))
    full = _resident((D_MODEL, D_MODEL))
    return pl.pallas_call(
        body, name="merge_out", grid=(s // tm,),
        in_specs=[tile, tile, pl.BlockSpec((tm, D_MODEL), lambda i: (i, 5)), pl.BlockSpec((tm, D_MODEL), lambda i: (i, 6)),
                  tile, full, full, full, pl.BlockSpec((1, D_MODEL), lambda i: (0, 0))],
        out_specs=[tile] * 5,
        out_shape=[_sds((s, D_MODEL), BF16)] * 3 + [_sds((s, D_MODEL), F32), _sds((s, D_MODEL), BF16)],
        compiler_params=_params("parallel"),
    )(ya, yb, z, z, x, w_pa, w_pb, w_out, g_ffn)


def _up_proj(h2, w_up_g, tm):
    s = h2.shape[0]
    bw = w_up_g.shape[2]
    per_half = N_DEV // 2

    def body(h_ref, w_ref, u_ref):
        h = h_ref[...]
        for j in range(N_DEV):
            cols = slice((j % per_half) * bw, (j % per_half + 1) * bw)
            u_ref[j // per_half, :, cols] = _dot(h, w_ref[j]).astype(BF16)

    return pl.pallas_call(
        body, name="up_proj", grid=(s // tm,),
        in_specs=[pl.BlockSpec((tm, D_MODEL), lambda i: (i, 0)), _resident(w_up_g.shape)],
        out_specs=pl.BlockSpec((2, tm, D_FF), lambda i: (0, i, 0)),
        out_shape=_sds((2, s, D_FF), BF16),
        compiler_params=_params("parallel"),
    )(h2, w_up_g)


def _ffn_conv(u_ref, w_ref, b_ref, half, c, ch):
    ext = _ext_before(u_ref.at[half], c, ch)
    w = w_ref.at[half]
    return b_ref[half] + _conv3(ext, w, ch), ext


def _ffn_act(u0, wc, bc, cw, ch):
    s = u0.shape[1]
    n_chunks = s // ch

    def body(u_ref, w_ref, b_ref, f_ref):
        def chunk(c, _):
            ug, _ = _ffn_conv(u_ref, w_ref, b_ref, 0, c, ch)
            uv, _ = _ffn_conv(u_ref, w_ref, b_ref, 1, c, ch)
            f_ref[pl.ds(pl.multiple_of(c * ch, HALO), ch), :] = (_gelu(ug) * uv).astype(BF16)
            return 0

        lax.fori_loop(0, n_chunks, chunk, 0)

    return pl.pallas_call(
        body, name="ffn_act", grid=(D_FF // cw,),
        in_specs=[pl.BlockSpec((2, s, cw), lambda j: (0, 0, j)), pl.BlockSpec((2, 3, cw), lambda j: (0, 0, j)),
                  pl.BlockSpec((2, 1, cw), lambda j: (0, 0, j))],
        out_specs=pl.BlockSpec((s, cw), lambda j: (0, j)),
        out_shape=_sds((s, D_FF), BF16),
        compiler_params=_params("parallel"),
    )(u0, wc, bc)


def _head(x2, f, p, target, w_down, w_gate, w_pp, g_ple, g_final, tm):
    s = x2.shape[0]

    def body(x2_ref, f_ref, p_ref, t_ref, wd_ref, wg_ref, wp_ref, gp_ref, gf_ref,
             x3_ref, dgt_ref, de0_ref, dx3_ref, df_ref, sums_ref):
        @pl.when(pl.program_id(0) == 0)
        def _():
            sums_ref[...] = jnp.zeros_like(sums_ref)

        x3 = x2_ref[...] + _dot(f_ref[...], wd_ref[...])
        x3b = x3.astype(BF16)
        x3_ref[...] = x3b
        e0n, re = _rms(_dot(p_ref[...].astype(BF16), wp_ref[...]))
        e = e0n * gp_ref[...]
        sg = _sigmoid(_dot(x3b, wg_ref[...]))
        x4n, r4 = _rms(x3 + sg * e)
        diff = x4n * gf_ref[...] - t_ref[...]
        sums_ref[0:1, :] += jnp.sum(diff * diff, axis=0, keepdims=True)
        dy = diff * (1.0 / D_MODEL)
        sums_ref[1:2, :] += jnp.sum(dy * x4n, axis=0, keepdims=True)
        dx4 = _rms_bwd(dy * gf_ref[...], x4n, r4)
        de = dx4 * sg
        dgt = ((dx4 * e) * (sg * (1.0 - sg))).astype(BF16)
        dgt_ref[...] = dgt
        sums_ref[2:3, :] += jnp.sum(de * e0n, axis=0, keepdims=True)
        de0_ref[...] = _rms_bwd(de * gp_ref[...], e0n, re).astype(BF16)
        dx3 = dx4 + _dot_nt(dgt, wg_ref[...])
        dx3_ref[...] = dx3
        df_ref[...] = _dot_nt(dx3.astype(BF16), wd_ref[...]).astype(BF16)

    tile = pl.BlockSpec((tm, D_MODEL), lambda i: (i, 0))
    vec = pl.BlockSpec((1, D_MODEL), lambda i: (0, 0))
    wide = pl.BlockSpec((tm, D_FF), lambda i: (i, 0))
    return pl.pallas_call(
        body, name="loss_head", grid=(s // tm,),
        in_specs=[tile, wide, pl.BlockSpec((tm, D_PLE), lambda i: (i, 0)), tile,
                  _resident((D_FF, D_MODEL)), _resident((D_MODEL, D_MODEL)), _resident((D_PLE, D_MODEL)), vec, vec],
        out_specs=[tile, tile, tile, tile, wide, pl.BlockSpec((8, D_MODEL), lambda i: (0, 0))],
        out_shape=[_sds((s, D_MODEL), BF16)] * 3 + [_sds((s, D_MODEL), F32), _sds((s, D_FF), BF16),
                                                    _sds((8, D_MODEL), F32)],
        compiler_params=_params("arbitrary"),
    )(x2, f, p, target, w_down, w_gate, w_pp, g_ple, g_final)


def _ffn_act_bwd(u0, df, wc, bc, cw, ch, comm):
    s = u0.shape[1]
    n_chunks = s // ch

    def body(u_ref, df_ref, w_ref, b_ref, du0_ref, dw_ref, db_ref, du_scr):
        dw_ref[...] = jnp.zeros_like(dw_ref)
        db_ref[...] = jnp.zeros_like(db_ref)

        def first(c, _):
            ug, ext_g = _ffn_conv(u_ref, w_ref, b_ref, 0, c, ch)
            uv, ext_v = _ffn_conv(u_ref, w_ref, b_ref, 1, c, ch)
            gel, dgel = _gelu_and_grad(ug)
            d = _rows(df_ref, c, ch).astype(F32)
            rows = pl.ds(pl.multiple_of(c * ch, HALO), ch)
            for half, du, ext in ((0, d * uv * dgel, ext_g), (1, d * gel, ext_v)):
                du_scr[half, rows, :] = du
                db_ref[half] += jnp.sum(du, axis=0, keepdims=True)
                for k in range(3):
                    dw_ref[half, k:k + 1, :] += jnp.sum(du * _shift_dn(ext, 2 - k, ch), axis=0, keepdims=True)
            return 0

        lax.fori_loop(0, n_chunks, first, 0)

        def second(c, _):
            rows = pl.ds(pl.multiple_of(c * ch, HALO), ch)
            for half in range(2):
                ext = _ext_after(du_scr.at[half], c, ch, n_chunks)
                w = w_ref.at[half]
                acc = w[2:3, :] * _shift_up(ext, 0, ch)
                for k in range(2):
                    acc = acc + w[k:k + 1, :] * _shift_up(ext, 2 - k, ch)
                du0_ref[half, rows, :] = acc.astype(BF16)
            return 0

        lax.fori_loop(0, n_chunks, second, 0)

    blk = lambda r: pl.BlockSpec((2, r, cw), lambda j: (0, 0, j))
    return _host_call(
        body, name="ffn_act_bwd", grid=(D_FF // cw,),
        in_specs=[blk(s), pl.BlockSpec((s, cw), lambda j: (0, j)), blk(3), blk(1)],
        out_specs=[blk(s), blk(3), blk(1)],
        out_shape=[_sds((2, s, D_FF), BF16), _sds((2, 3, D_FF), F32), _sds((2, 1, D_FF), F32)],
        scratch_shapes=[pltpu.VMEM((2, s, cw), F32)],
        semantics=("parallel",), args=(u0, df, wc, bc), comm=comm)


def _up_bwd_merge_bwd(du0, w_up_g, x2, dx3, z, pa, pb, w_out, w_pa, w_pb, g_ffn, tm, comm):
    s = x2.shape[0]
    bw = w_up_g.shape[2]
    per_half = N_DEV // 2

    def body(du_ref, wu_ref, x2_ref, dx3_ref, ga_ref, gb_ref, pa_ref, pb_ref, wo_ref, wa_ref, wb_ref, g_ref,
             dx2_ref, dzg_ref, dpa_ref, dpb_ref, dya_ref, dyb_ref, sums_ref):
        @pl.when(pl.program_id(0) == 0)
        def _():
            sums_ref[...] = jnp.zeros_like(sums_ref)

        dh2 = None
        for j in range(N_DEV):
            cols = slice((j % per_half) * bw, (j % per_half + 1) * bw)
            term = _dot_nt(du_ref[j // per_half, :, cols], wu_ref[j])
            dh2 = term if dh2 is None else dh2 + term
        x2n, r2 = _rms(x2_ref[...])
        sums_ref[0:1, :] += jnp.sum(dh2 * x2n, axis=0, keepdims=True)
        dx2 = dx3_ref[...] + _rms_bwd(dh2 * g_ref[...], x2n, r2)
        dx2_ref[...] = dx2
        dm = _dot_nt(dx2.astype(BF16), wo_ref[...])
        sa = _sigmoid(ga_ref[...].astype(F32))
        sb = _sigmoid(gb_ref[...].astype(F32))
        dzg_ref[0] = (dm * pa_ref[...].astype(F32) * (sa * (1.0 - sa))).astype(BF16)
        dzg_ref[1] = (dm * pb_ref[...].astype(F32) * (sb * (1.0 - sb))).astype(BF16)
        dpa = (dm * sa).astype(BF16)
        dpb = (dm * sb).astype(BF16)
        dpa_ref[...] = dpa
        dpb_ref[...] = dpb
        dya_ref[...] = _dot_nt(dpa, wa_ref[...]).astype(BF16)
        dyb_ref[...] = _dot_nt(dpb, wb_ref[...]).astype(BF16)

    tile = pl.BlockSpec((tm, D_MODEL), lambda i: (i, 0))
    full = _resident((D_MODEL, D_MODEL))
    return _host_call(
        body, name="up_bwd_merge_bwd", grid=(s // tm,),
        in_specs=[pl.BlockSpec((2, tm, D_FF), lambda i: (0, i, 0)), _resident(w_up_g.shape),
                  tile, tile, pl.BlockSpec((tm, D_MODEL), lambda i: (i, 5)),
                  pl.BlockSpec((tm, D_MODEL), lambda i: (i, 6)), tile, tile, full, full, full,
                  pl.BlockSpec((1, D_MODEL), lambda i: (0, 0))],
        out_specs=[tile, pl.BlockSpec((2, tm, D_MODEL), lambda i: (0, i, 0)), tile, tile, tile, tile,
                   pl.BlockSpec((8, D_MODEL), lambda i: (0, 0))],
        out_shape=[_sds((s, D_MODEL), F32), _sds((2, s, D_MODEL), BF16)] + [_sds((s, D_MODEL), BF16)] * 4
        + [_sds((8, D_MODEL), F32)],
        semantics=("arbitrary",), args=(du0, w_up_g, x2, dx3, z, z, pa, pb, w_out, w_pa, w_pb, g_ffn),
        comm=comm)


def _mixers_bwd(z, h, dya, dyb, w4, b4, w_a, b_a, w_x, b_x, lam, w3, ch, comm):
    s = z.shape[0]
    cw = RNN_BW
    n_chunks = s // ch

    def body(xr_ref, gr_ref, cb_ref, cc_ref, cx_ref, h_ref, dya_ref, dyb_ref, w4_ref, b4_ref, wa_ref, ba_ref, wx_ref,
             bx_ref, lam_ref, w3_ref,
             dz_ref, dw4_ref, db4_ref, dwa_ref, dba_ref, dwx_ref, dbx_ref, dlam_ref, dw3_ref,
             a_scr, g_scr, q_scr):
        sp = _softplus(-lam_ref[...])
        wa = wa_ref[0].astype(BF16)
        wx = wx_ref[0].astype(BF16)
        for ref in (dw4_ref, db4_ref, dwa_ref, dba_ref, dwx_ref, dbx_ref, dlam_ref, dw3_ref):
            ref[...] = jnp.zeros_like(ref)

        def gates(c):
            xr_ext = _ext_before(xr_ref, c, ch)
            return (xr_ext,) + _gate_chunk(xr_ext, w4_ref, b4_ref, wa, wx, ba_ref, bx_ref, sp, ch)

        def first(c, _):
            rows = pl.ds(pl.multiple_of(c * ch, HALO), ch)
            _, _, _, _, log_a = gates(c)
            a_scr[rows, :] = jnp.exp(log_a)
            gel, dgel = _gelu_and_grad(gr_ref[rows, :].astype(F32))
            dya_c = dya_ref[rows, :].astype(F32)
            g_scr[rows, :] = dya_c * gel
            dz_ref[1, rows, :] = (dya_c * h_ref[rows, :] * dgel).astype(BF16)
            q_ext = _ext_before(cc_ref, c, ch) * _ext_before(cx_ref, c, ch)
            dyb_c = dyb_ref[rows, :].astype(F32)
            dz_ref[2, rows, :] = (dyb_c * _conv3(q_ext, w3_ref, ch)).astype(BF16)
            dyq = dyb_c * cb_ref[rows, :].astype(F32)
            q_scr[rows, :] = dyq
            for k in range(3):
                dw3_ref[k:k + 1, :] += jnp.sum(dyq * _shift_dn(q_ext, 2 - k, ch), axis=0, keepdims=True)
            return 0

        lax.fori_loop(0, n_chunks, first, 0)
        _scan_rev(a_scr, g_scr, s, cw)

        def second(c, _):
            rows = pl.ds(pl.multiple_of(c * ch, HALO), ch)
            _, xc, ra, ia, log_a = gates(c)
            a = a_scr[rows, :]
            mult = jnp.sqrt(-_expm1(2.0 * log_a))
            dh = g_scr[rows, :]
            h_prev = _shift_dn(_ext_before(h_ref, c, ch), 1, ch)
            dlog_a = dh * h_prev * a - (dh * ia * xc) * (a * a) / mult
            dlam_ref[...] += jnp.sum(dlog_a * ra, axis=0, keepdims=True)
            dpre_a = (dlog_a * ((-LRU_C) * sp)) * (ra * (1.0 - ra))
            dpre_x = (dh * mult * xc) * (ia * (1.0 - ia))
            dba_ref[...] += jnp.sum(dpre_a, axis=0, keepdims=True)
            dbx_ref[...] += jnp.sum(dpre_x, axis=0, keepdims=True)
            xcb = xc.astype(BF16)
            dpa_b = dpre_a.astype(BF16)
            dpx_b = dpre_x.astype(BF16)
            dwa_ref[0] += _dot_tn(xcb, dpa_b)
            dwx_ref[0] += _dot_tn(xcb, dpx_b)
            a_scr[rows, :] = dh * mult * ia + _dot_nt(dpa_b, wa) + _dot_nt(dpx_b, wx)
            return 0

        lax.fori_loop(0, n_chunks, second, 0)
        dlam_ref[...] = dlam_ref[...] * (LRU_C * _sigmoid(-lam_ref[...]))

        def third(c, _):
            rows = pl.ds(pl.multiple_of(c * ch, HALO), ch)
            dxc_ext = _ext_after(a_scr, c, ch, n_chunks)
            dxc = _shift_up(dxc_ext, 0, ch)
            xr_ext = _ext_before(xr_ref, c, ch)
            db4_ref[...] += jnp.sum(dxc, axis=0, keepdims=True)
            dxr = w4_ref[3:4, :] * dxc
            dw4_ref[3:4, :] += jnp.sum(dxc * _shift_dn(xr_ext, 0, ch), axis=0, keepdims=True)
            for k in range(3):
                dxr = dxr + w4_ref[k:k + 1, :] * _shift_up(dxc_ext, 3 - k, ch)
                dw4_ref[k:k + 1, :] += jnp.sum(dxc * _shift_dn(xr_ext, 3 - k, ch), axis=0, keepdims=True)
            dz_ref[0, rows, :] = dxr.astype(BF16)
            dyq_ext = _ext_after(q_scr, c, ch, n_chunks)
            dq = w3_ref[2:3, :] * _shift_up(dyq_ext, 0, ch)
            for k in range(2):
                dq = dq + w3_ref[k:k + 1, :] * _shift_up(dyq_ext, 2 - k, ch)
            dz_ref[3, rows, :] = (dq * cx_ref[rows, :].astype(F32)).astype(BF16)
            dz_ref[4, rows, :] = (dq * cc_ref[rows, :].astype(F32)).astype(BF16)
            return 0

        lax.fori_loop(0, n_chunks, third, 0)

    col = lambda j: (0, j)
    vec = pl.BlockSpec((1, cw), col)
    sq = pl.BlockSpec((1, cw, cw), lambda j: (j, 0, 0))
    act = pl.BlockSpec((s, cw), col)
    w4s, w3s = pl.BlockSpec((4, cw), col), pl.BlockSpec((3, cw), col)
    vec_shape = _sds((1, D_MODEL), F32)
    sq_shape = _sds((D_MODEL // cw, cw, cw), F32)
    return _host_call(
        body, name="mixers_bwd", grid=(D_MODEL // cw,),
        in_specs=_z_block_specs(s, cw, range(5)) + [act, act, act, w4s, vec, sq, vec, sq, vec, vec, w3s],
        out_specs=[pl.BlockSpec((5, s, cw), lambda j: (0, 0, j)), w4s, vec, sq, vec, sq, vec, vec, w3s],
        out_shape=[_sds((5, s, D_MODEL), BF16), _sds((4, D_MODEL), F32), vec_shape, sq_shape, vec_shape, sq_shape,
                   vec_shape, vec_shape, _sds((3, D_MODEL), F32)],
        scratch_shapes=[pltpu.VMEM((s, cw), F32)] * 3,
        semantics=("parallel",), args=(z, z, z, z, z, h, dya, dyb, w4, b4, w_a, b_a, w_x, b_x, lam, w3), comm=comm)


def _in_proj_bwd(dz5, dzg, w_in, x, dx2, g_mix, tm, comm):
    s = x.shape[0]

    def body(d5_ref, dg_ref, w_ref, x_ref, dx2_ref, g_ref, dx_ref, sums_ref):
        @pl.when(pl.program_id(0) == 0)
        def _():
            sums_ref[...] = jnp.zeros_like(sums_ref)

        dh1 = None
        for k in range(N_SEG):
            d = d5_ref[k] if k < 5 else dg_ref[k - 5]
            term = _dot_nt(d, w_ref[:, k * D_MODEL:(k + 1) * D_MODEL])
            dh1 = term if dh1 is None else dh1 + term
        xn, r1 = _rms(x_ref[...])
        sums_ref[0:1, :] += jnp.sum(dh1 * xn, axis=0, keepdims=True)
        dx_ref[...] = dx2_ref[...] + _rms_bwd(dh1 * g_ref[...], xn, r1)

    tile = pl.BlockSpec((tm, D_MODEL), lambda i: (i, 0))
    return _host_call(
        body, name="in_proj_bwd", grid=(s // tm,),
        in_specs=[pl.BlockSpec((5, tm, D_MODEL), lambda i: (0, i, 0)), pl.BlockSpec((2, tm, D_MODEL), lambda i: (0, i, 0)),
                  _resident(w_in.shape), tile, tile, pl.BlockSpec((1, D_MODEL), lambda i: (0, 0))],
        out_specs=[tile, pl.BlockSpec((8, D_MODEL), lambda i: (0, 0))],
        out_shape=[_sds((s, D_MODEL), F32), _sds((8, D_MODEL), F32)],
        semantics=("arbitrary",), args=(dz5, dzg, w_in, x, dx2, g_mix), comm=comm)


def _weight_grad(a, b, b_spec, out_spec, out_shape, *, n_blocks, chunks, width, tm, tk, name):
    s, m = a.shape
    nk = s // tk
    b_chunked = len([d for d in b_spec.block_shape if d is not None]) == 3
    blocks_out = len([d for d in out_spec.block_shape if d is not None]) == 3

    def body(a_ref, b_ref, o_ref, acc):
        k = pl.program_id(2)

        @pl.when(k == 0)
        def _():
            acc[...] = jnp.zeros_like(acc)

        at = a_ref[...].astype(BF16).T
        for j in range(chunks):
            cols = slice(j * width, (j + 1) * width)
            acc[:, cols] += _dot(at, (b_ref[j] if b_chunked else b_ref[:, cols]).astype(BF16))

        @pl.when(k == nk - 1)
        def _():
            if blocks_out:
                for j in range(chunks):
                    o_ref[j] = acc[:, j * width:(j + 1) * width].astype(o_ref.dtype)
            else:
                o_ref[...] = acc[...].astype(o_ref.dtype)

    return pl.pallas_call(
        body, name=name, grid=(m // tm, n_blocks, nk),
        in_specs=[pl.BlockSpec((tk, tm), lambda i, j, k: (k, i)), b_spec], out_specs=out_spec, out_shape=out_shape,
        scratch_shapes=[pltpu.VMEM((tm, chunks * width), F32)],
        compiler_params=_params("parallel", "parallel", "arbitrary"),
    )(a, b)


def _wgrad_2d(a, b, name, tk):
    m, n = a.shape[1], b.shape[1]
    tm = min(m, 1024)
    return _weight_grad(a, b, pl.BlockSpec((tk, n), lambda i, j, k: (k, 0)), pl.BlockSpec((tm, n), lambda i, j, k: (i, 0)),
                        _sds((m, n), BF16), n_blocks=1, chunks=1, width=n, tm=tm, tk=tk, name=name)


def _wgrad_segments(a, b3, name, tm, tk):
    m = a.shape[1]
    g = b3.shape[0]
    return _weight_grad(a, b3, pl.BlockSpec((g, tk, D_MODEL), lambda i, j, k: (0, k, 0)),
                        pl.BlockSpec((tm, g * D_MODEL), lambda i, j, k: (i, 0)), _sds((m, g * D_MODEL), BF16),
                        n_blocks=1, chunks=g, width=D_MODEL, tm=tm, tk=tk, name=name)


def _wgrad_up(h2, du0, bw, tk):
    per_half = N_DEV // 2
    return _weight_grad(h2, du0, pl.BlockSpec((None, tk, D_FF), lambda i, j, k: (j, k, 0)),
                        pl.BlockSpec((per_half, D_MODEL, bw), lambda i, j, k: (j, 0, 0)), _sds((N_DEV, D_MODEL, bw), BF16),
                        n_blocks=2, chunks=per_half, width=bw, tm=D_MODEL, tk=tk, name="wgrad_up")


def _adam_math(w, g, m, v):
    m = ADAM_B1 * m + (1.0 - ADAM_B1) * g
    v = ADAM_B2 * v + (1.0 - ADAM_B2) * jnp.square(g)
    m_hat = m / (1.0 - ADAM_B1 ** ADAM_STEP)
    v_hat = v / (1.0 - ADAM_B2 ** ADAM_STEP)
    delta = -ADAM_LR * (m_hat / (jnp.sqrt(v_hat) + ADAM_EPS) + ADAM_WD * w)
    return delta, m, v


def _adam_shard(parts, w, m, v, name):
    r, c = w.shape
    tr = min(r, 128)

    def body(p_ref, w_ref, m_ref, v_ref, g_ref, d_ref, nm_ref, nv_ref):
        g = p_ref[0].astype(F32)
        for k in range(1, N_DEV):
            g = g + p_ref[k].astype(F32)
        g_ref[...] = g
        d_ref[...], nm_ref[...], nv_ref[...] = _adam_math(w_ref[...], g, m_ref[...], v_ref[...])

    tile = pl.BlockSpec((tr, c), lambda i: (i, 0))
    return pl.pallas_call(
        body, name=name, grid=(r // tr,),
        in_specs=[pl.BlockSpec((N_DEV, tr, c), lambda i: (0, i, 0)), tile, tile, tile],
        out_specs=[tile] * 4, out_shape=[_sds((r, c), F32)] * 4,
        compiler_params=_params("parallel"),
    )(parts, w, m, v)


def _adam_plain(g, w, m, v, name):
    def body(g_ref, w_ref, m_ref, v_ref, d_ref, nm_ref, nv_ref):
        d_ref[...], nm_ref[...], nv_ref[...] = _adam_math(w_ref[...], g_ref[...], m_ref[...], v_ref[...])

    vmem = pl.BlockSpec(memory_space=pltpu.VMEM)
    return pl.pallas_call(
        body, name=name, in_specs=[vmem] * 4, out_specs=[vmem] * 3, out_shape=[_sds(w.shape, F32)] * 3,
    )(g, w, m, v)


_REPL = ("g_mix", "rnn_conv_b", "b_rg_a", "b_rg_x", "lru_lambda", "g_ffn", "g_ple", "g_final", "ffn_conv_b", "pad2",
         "w_rg_a", "w_rg_x")
REPL_ROWS = 272


def _pack_repl(t):
    rows = [t[n].reshape(1, D_MODEL) for n in _REPL[:8]]
    rows.append(t["ffn_conv_b"].reshape(6, D_MODEL))
    rows.append(t.get("pad2", jnp.zeros((2, D_MODEL), F32)))
    rows.append(t["w_rg_a"].reshape(128, D_MODEL))
    rows.append(t["w_rg_x"].reshape(128, D_MODEL))
    return jnp.concatenate(rows, axis=0)


def _unpack_repl(pack, shapes):
    out = {n: pack[i].reshape(shapes[n]) for i, n in enumerate(_REPL[:8])}
    out["ffn_conv_b"] = pack[8:14].reshape(shapes["ffn_conv_b"])
    out["w_rg_a"] = pack[16:144].reshape(shapes["w_rg_a"])
    out["w_rg_x"] = pack[144:272].reshape(shapes["w_rg_x"])
    return out


def _pack_conv_shard(rnn, sc, ffn):
    top = jnp.concatenate([rnn[:3], sc, ffn], axis=1)
    row3 = jnp.concatenate([rnn[3:4], jnp.zeros((1, D_MODEL - RNN_BW), F32)], axis=1)
    return jnp.concatenate([top, row3, jnp.zeros((4, D_MODEL), F32)], axis=0)


def _unpack_conv_shard(pack):
    rnn = jnp.concatenate([pack[:3, :RNN_BW], pack[3:4, :RNN_BW]], axis=0)
    return rnn, pack[:3, RNN_BW:2 * RNN_BW], pack[:3, 2 * RNN_BW:]


_SHARDED_BIG = ("w_in", "w_proj_a", "w_proj_b", "w_out", "w_up", "w_down", "w_ple_gate", "w_ple_proj")
_NAMES = ("g_mix", "w_in", "rnn_conv_w", "rnn_conv_b", "w_rg_a", "b_rg_a", "w_rg_x", "b_rg_x", "lru_lambda", "sc_conv_w",
          "w_proj_a", "w_proj_b", "w_out", "g_ffn", "w_up", "ffn_conv_w", "ffn_conv_b", "w_down", "w_ple_gate",
          "w_ple_proj", "g_ple", "g_final")


def _step(x, p, target, w, m, v):
    s = x.shape[0]
    tm = min(s, 256)
    tm_wide = min(s, 512)
    ch = min(s, 256)
    my_index = 4 * lax.axis_index("x") + 2 * lax.axis_index("y") + lax.axis_index("c")

    big = {n: w[n][0] for n in _SHARDED_BIG}
    conv_shard = _pack_conv_shard(w["rnn_conv_w"][0], w["sc_conv_w"][0], w["ffn_conv_w"][0])
    shards = {n: big[n].astype(BF16) for n in _SHARDED_BIG}
    w_in_g, conv_all = _comm_only(_gather_comm([shards["w_in"], conv_shard]), "in_proj_weight_gather")
    w_in = jnp.transpose(w_in_g, (1, 0, 2)).reshape(D_MODEL, N_SEG * D_MODEL)
    w4 = jnp.transpose(jnp.concatenate([conv_all[:, :3, :RNN_BW], conv_all[:, 3:4, :RNN_BW]], axis=1),
                       (1, 0, 2)).reshape(4, D_MODEL)
    w3 = jnp.transpose(conv_all[:, :3, RNN_BW:2 * RNN_BW], (1, 0, 2)).reshape(3, D_MODEL)
    wc = jnp.transpose(conv_all[:, :3, 2 * RNN_BW:], (1, 0, 2)).reshape(3, 2, D_FF).transpose(1, 0, 2)
    bc = w["ffn_conv_b"].reshape(2, 1, D_FF)
    b4, b_a, b_x, lam = w["rnn_conv_b"], w["b_rg_a"], w["b_rg_x"], w["lru_lambda"]
    w_a, w_x = w["w_rg_a"][0], w["w_rg_x"][0]
    g_final = w["g_final"].reshape(1, D_MODEL)

    tk = min(s, 512)
    received = {}

    comm = _gather_comm([shards[n] for n in ("w_proj_a", "w_proj_b", "w_out", "w_up")])
    (z, h1), (w_pa, w_pb, w_out, w_up_g) = _norm_in_proj(x, w["g_mix"], w_in, tm_wide, comm)
    w_pa, w_pb, w_out = (a.reshape(D_MODEL, D_MODEL) for a in (w_pa, w_pb, w_out))
    comm = _gather_comm([shards[n] for n in ("w_down", "w_ple_gate", "w_ple_proj")])
    (ya, yb, h), (w_down, w_gate, w_pp) = _mixers_fwd(z, w4, b4, w_a, b_a, w_x, b_x, lam, w3, ch, comm)
    w_down = w_down.reshape(D_FF, D_MODEL)
    w_gate = w_gate.reshape(D_MODEL, D_MODEL)
    w_pp = jnp.transpose(w_pp, (1, 0, 2)).reshape(D_PLE, D_MODEL)
    pa, pb, mm, x2, h2 = _merge_out(ya, yb, z, x, w_pa, w_pb, w_out, w["g_ffn"], tm)
    u0 = _up_proj(h2, w_up_g, tm_wide)
    f = _ffn_act(u0, wc, bc, 256, ch)
    x3, dgt, de0, dx3, df, head_sums = _head(x2, f, p, target, w_down, w_gate, w_pp, w["g_ple"], g_final, tm)
    parts = [_wgrad_2d(x3, dgt, "wgrad_ple_gate", tk).reshape(N_DEV, RNN_BW, D_MODEL),
             jnp.transpose(_wgrad_2d(p, de0, "wgrad_ple_proj", tk).reshape(D_PLE, N_DEV, RNN_BW), (1, 0, 2)),
             _wgrad_2d(f, dx3, "wgrad_down", tk).reshape(N_DEV, D_FF // N_DEV, D_MODEL)]
    (du0, dwc, dbc), got = _ffn_act_bwd(u0, df, wc, bc, 256, ch, _exchange_comm(parts))
    received.update(zip(("w_ple_gate", "w_ple_proj", "w_down"), got))
    parts = [_wgrad_up(h2, du0, w_up_g.shape[2], tk)]
    (dx2, dzg, dpa, dpb, dya, dyb, ffn_sums), got = _up_bwd_merge_bwd(du0, w_up_g, x2, dx3, z, pa, pb, w_out, w_pa, w_pb,
                                                                       w["g_ffn"], tm, _exchange_comm(parts))
    received["w_up"] = got[0]
    parts = [_wgrad_2d(mm, dx2, "wgrad_out", tk).reshape(N_DEV, RNN_BW, D_MODEL),
             _wgrad_2d(ya, dpa, "wgrad_proj_a", tk).reshape(N_DEV, RNN_BW, D_MODEL),
             _wgrad_2d(yb, dpb, "wgrad_proj_b", tk).reshape(N_DEV, RNN_BW, D_MODEL)]
    (dz5, dw4, db4, dwa, dba, dwx, dbx, dlam, dw3), got = _mixers_bwd(z, h, dya, dyb, w4, b4, w_a, b_a, w_x, b_x, lam, w3,
                                                                       ch, _exchange_comm(parts))
    received.update(zip(("w_out", "w_proj_a", "w_proj_b"), got))
    dw_in = jnp.concatenate([_wgrad_segments(h1, dz5, "wgrad_in_mix", 512, tk),
                             _wgrad_segments(h1, dzg, "wgrad_in_gate", D_MODEL, tk)], axis=1)
    bw_in = N_SEG * D_MODEL // N_DEV
    parts = [jnp.transpose(dw_in.reshape(D_MODEL, N_DEV, bw_in), (1, 0, 2))]
    (grad_x, mix_sums), got = _in_proj_bwd(dz5, dzg, w_in, x, dx2, w["g_mix"], tm, _exchange_comm(parts))
    received["w_in"] = got[0]

    small = {"g_mix": mix_sums[0], "rnn_conv_b": db4, "b_rg_a": dba, "b_rg_x": dbx, "lru_lambda": dlam,
             "g_ffn": ffn_sums[0], "g_ple": head_sums[2], "g_final": head_sums[1], "ffn_conv_b": dbc,
             "pad2": jnp.concatenate([head_sums[0:1], jnp.zeros((1, D_MODEL), F32)], axis=0),
             "w_rg_a": dwa, "w_rg_x": dwx}
    conv_rows = jnp.concatenate([dw4, dw3, jnp.zeros((1, D_MODEL), F32),
                                 jnp.transpose(dwc, (1, 0, 2)).reshape(18, D_MODEL),
                                 jnp.zeros((PACK_ROWS - REPL_ROWS - 26, D_MODEL), F32)], axis=0)
    total = _small_all_reduce(jnp.concatenate([_pack_repl(small), conv_rows], axis=0))
    loss = jnp.sum(total[14]) * (0.5 / D_MODEL)

    out = {}
    for n in _SHARDED_BIG:
        shard = big[n]
        res = _adam_shard(received[n].reshape((N_DEV,) + shard.shape), shard, m[n][0], v[n][0], "adam_" + n)
        out[n] = [r[None] for r in res]
    shapes = {n: w[n].shape for n in _REPL if n != "pad2"}
    g_repl = total[:REPL_ROWS]
    res = _adam_plain(g_repl, _pack_repl(w), _pack_repl(m), _pack_repl(v), "adam_replicated")
    unpacked = [_unpack_repl(r, shapes) for r in (g_repl,) + tuple(res)]
    for n in shapes:
        out[n] = [u[n] for u in unpacked]
    g_rnn = lax.dynamic_slice(total[272:276], (0, my_index * RNN_BW), (4, RNN_BW))
    g_sc = lax.dynamic_slice(total[276:279], (0, my_index * RNN_BW), (3, RNN_BW))
    bw_ffn = 2 * D_FF // N_DEV
    g_ffn_conv = lax.dynamic_slice(total[280:298].reshape(3, 2 * D_FF), (0, my_index * bw_ffn), (3, bw_ffn))
    g_conv = _pack_conv_shard(g_rnn, g_sc, g_ffn_conv)
    res = _adam_plain(g_conv, conv_shard,
                      _pack_conv_shard(m["rnn_conv_w"][0], m["sc_conv_w"][0], m["ffn_conv_w"][0]),
                      _pack_conv_shard(v["rnn_conv_w"][0], v["sc_conv_w"][0], v["ffn_conv_w"][0]), "adam_conv")
    unpacked = [_unpack_conv_shard(r) for r in (g_conv,) + tuple(res)]
    for i, n in enumerate(("rnn_conv_w", "sc_conv_w", "ffn_conv_w")):
        out[n] = [u[i][None] for u in unpacked]

    return (loss, grad_x[None]) + tuple(out[n][k] for k in range(4) for n in _NAMES)


def kernel(x, p, g_mix, w_in, rnn_conv_w, rnn_conv_b, w_rg_a, b_rg_a, w_rg_x, b_rg_x, lru_lambda, sc_conv_w, w_proj_a, w_proj_b, w_out, g_ffn, w_up, ffn_conv_w, ffn_conv_b, w_down, w_ple_gate, w_ple_proj, g_ple, g_final, loss_target, m_g_mix, m_w_in, m_rnn_conv_w, m_rnn_conv_b, m_w_rg_a, m_b_rg_a, m_w_rg_x, m_b_rg_x, m_lru_lambda, m_sc_conv_w, m_w_proj_a, m_w_proj_b, m_w_out, m_g_ffn, m_w_up, m_ffn_conv_w, m_ffn_conv_b, m_w_down, m_w_ple_gate, m_w_ple_proj, m_g_ple, m_g_final, v_g_mix, v_w_in, v_rnn_conv_w, v_rnn_conv_b, v_w_rg_a, v_b_rg_a, v_w_rg_x, v_b_rg_x, v_lru_lambda, v_sc_conv_w, v_w_proj_a, v_w_proj_b, v_w_out, v_g_ffn, v_w_up, v_ffn_conv_w, v_ffn_conv_b, v_w_down, v_w_ple_gate, v_w_ple_proj, v_g_ple, v_g_final):
    given = dict(locals())
    w = {n: given[n] for n in _NAMES}
    m = {n: given["m_" + n] for n in _NAMES}
    v = {n: given["v_" + n] for n in _NAMES}
    return _step(x[0], p[0, 0], loss_target[0], w, m, v)
```

```python
import functools

import jax
import jax.numpy as jnp
from jax import lax
from jax.experimental import pallas as pl
from jax.experimental.pallas import tpu as pltpu

F32 = jnp.float32
BF16 = jnp.bfloat16
MESH_ID = pl.DeviceIdType.MESH

N_DEV = 8
D_MODEL = 1024
D_PLE = 256
RNN_BW = 128
N_SEG = 7
D_FF = 3072
LRU_C = 8.0
EPS = 1e-6
ADAM_LR = 0.001
ADAM_B1 = 0.9
ADAM_B2 = 0.999
ADAM_EPS = 1e-08
ADAM_WD = 0.01
ADAM_STEP = 10

HALO = 16
SCAN_UNROLL = 4
VMEM_LIMIT = 56 * 1024 * 1024
PACK_ROWS = 320
PIECE = PACK_ROWS // N_DEV


def _dot(a, b):
    return jnp.dot(a, b, preferred_element_type=F32)


def _dot_nt(a, b):
    return lax.dot_general(a, b, (((1,), (1,)), ((), ())), preferred_element_type=F32)


def _dot_tn(a, b):
    return lax.dot_general(a, b, (((0,), (0,)), ((), ())), preferred_element_type=F32)


def _sigmoid(x):
    return jax.nn.sigmoid(x)


_GELU_C = 0.7978845608028654
_GELU_K = 0.044715


def _gelu(x):
    return 0.5 * x * (1.0 + jnp.tanh(_GELU_C * (x + _GELU_K * (x * x * x))))


def _gelu_and_grad(x):
    x2 = x * x
    t = jnp.tanh(_GELU_C * (x + _GELU_K * (x * x2)))
    g = 0.5 * x * (1.0 + t)
    dg = 0.5 * (1.0 + t) + 0.5 * x * (1.0 - t * t) * (_GELU_C * (1.0 + 3.0 * _GELU_K * x2))
    return g, dg


def _expm1(x):
    poly = x * (1.0 + x * (0.5 + x * (1.0 / 6.0 + x * (1.0 / 24.0 + x * (1.0 / 120.0)))))
    return jnp.where(jnp.abs(x) < 0.08, poly, jnp.exp(x) - 1.0)


def _softplus(x):
    return jnp.maximum(x, 0.0) + jnp.log1p(jnp.exp(-jnp.abs(x)))


def _rms(u):
    r = lax.rsqrt(jnp.mean(u * u, axis=-1, keepdims=True) + EPS)
    return u * r, r


def _rms_bwd(dn, un, r):
    return r * (dn - un * jnp.mean(dn * un, axis=-1, keepdims=True))


def _shift_dn(ext, s, n):
    if s == 0:
        return ext[HALO:HALO + n]
    return pltpu.roll(ext, s, 0)[HALO:HALO + n]


def _shift_up(ext, s, n):
    if s == 0:
        return ext[0:n]
    return pltpu.roll(ext, n + HALO - s, 0)[0:n]


def _ext_before(ref, c, ch):
    t0 = c * ch
    prev = pl.multiple_of(jnp.maximum(t0 - HALO, 0), HALO)
    halo = jnp.where(c > 0, ref[pl.ds(prev, HALO), :].astype(F32), 0.0)
    cur = ref[pl.ds(pl.multiple_of(t0, HALO), ch), :].astype(F32)
    return jnp.concatenate([halo, cur], axis=0)


def _ext_after(ref, c, ch, n_chunks):
    t0 = c * ch
    nxt = pl.multiple_of(jnp.minimum(t0 + ch, (n_chunks - 1) * ch + ch - HALO), HALO)
    halo = jnp.where(c < n_chunks - 1, ref[pl.ds(nxt, HALO), :].astype(F32), 0.0)
    cur = ref[pl.ds(pl.multiple_of(t0, HALO), ch), :].astype(F32)
    return jnp.concatenate([cur, halo], axis=0)


def _rows(ref, c, ch):
    return ref[pl.ds(pl.multiple_of(c * ch, HALO), ch), :]


def _scan_fwd(a_ref, b_ref, h_ref, n_rows, cw):
    rows = lax.broadcasted_iota(jnp.int32, (8, cw), 0)

    def body(i, carry):
        for u in range(SCAN_UNROLL):
            t = pl.multiple_of((i * SCAN_UNROLL + u) * 8, 8)
            a = a_ref[pl.ds(t, 8), :]
            b = b_ref[pl.ds(t, 8), :]
            for d in (1, 2, 4):
                m = rows >= d
                b = jnp.where(m, a * pltpu.roll(b, d, 0) + b, b)
                a = jnp.where(m, a * pltpu.roll(a, d, 0), a)
            h_ref[pl.ds(t, 8), :] = a * carry + b
            carry = jnp.broadcast_to(a[7:8, :], (8, cw)) * carry + jnp.broadcast_to(b[7:8, :], (8, cw))
        return carry

    lax.fori_loop(0, n_rows // (8 * SCAN_UNROLL), body, jnp.zeros((8, cw), F32))


def _scan_rev(a_ref, g_ref, n_rows, cw):
    rows = lax.broadcasted_iota(jnp.int32, (8, cw), 0)
    n_groups = n_rows // 8

    def body(i, carry):
        dh_next, a_next = carry
        for u in range(SCAN_UNROLL):
            t = pl.multiple_of((n_groups - 1 - (i * SCAN_UNROLL + u)) * 8, 8)
            a = a_ref[pl.ds(t, 8), :]
            g = g_ref[pl.ds(t, 8), :]
            an = jnp.where(rows < 7, pltpu.roll(a, 7, 0), a_next)
            a_next = jnp.broadcast_to(a[0:1, :], (8, cw))
            for d in (1, 2, 4):
                m = rows < 8 - d
                g = jnp.where(m, an * pltpu.roll(g, 8 - d, 0) + g, g)
                an = jnp.where(m, an * pltpu.roll(an, 8 - d, 0), an)
            g_ref[pl.ds(t, 8), :] = an * dh_next + g
            dh_next = jnp.broadcast_to(an[0:1, :], (8, cw)) * dh_next + jnp.broadcast_to(g[0:1, :], (8, cw))
        return dh_next, a_next

    zero = jnp.zeros((8, cw), F32)
    lax.fori_loop(0, n_groups // SCAN_UNROLL, body, (zero, zero))


def _params(*sem):
    return pltpu.CompilerParams(dimension_semantics=sem, vmem_limit_bytes=VMEM_LIMIT)


def _sds(shape, dtype):
    return jax.ShapeDtypeStruct(shape, dtype)


def _resident(shape):
    zeros = (0,) * len(shape)
    return pl.BlockSpec(shape, lambda *_: zeros, pipeline_mode=pl.Buffered(1))


class _Comm:
    def __init__(self, inputs, out_shape, scratch, start, finish):
        self.inputs, self.out_shape, self.scratch, self.start, self.finish = inputs, out_shape, scratch, start, finish


def _sem_scratch(n, copies=7):
    return [pltpu.SemaphoreType.DMA((n, copies)), pltpu.SemaphoreType.DMA((n, copies)), pltpu.SemaphoreType.DMA((n,))]


def _gather_comm(shards):
    n = len(shards)

    def plan(ins, outs, sems):
        send_sems, recv_sems, local_sems = sems
        x, y, c = lax.axis_index("x"), lax.axis_index("y"), lax.axis_index("c")
        me, sibling = (x, y, c), (x, y, 1 - c)
        chips = [(1 - x, y), (x, 1 - y), (1 - x, 1 - y)]

        def slot(t, dev):
            return outs[t].at[4 * dev[0] + 2 * dev[1] + dev[2]]

        def copy(t, k, block, to, src=None):
            return pltpu.make_async_remote_copy(
                src_ref=slot(t, block) if src is None else src, dst_ref=slot(t, block),
                send_sem=send_sems.at[t, k], recv_sem=recv_sems.at[t, k],
                device_id=to, device_id_type=MESH_ID)

        mine = [pltpu.make_async_copy(ins[t], slot(t, me), local_sems.at[t]) for t in range(n)]
        first = []
        for t in range(n):
            first.append(copy(t, 0, me, sibling, src=ins[t]))
            first += [copy(t, 1 + j, me, (*chip, c), src=ins[t]) for j, chip in enumerate(chips)]
        return me, sibling, chips, c, copy, mine, first

    def start(ins, outs, sems):
        *_, mine, first = plan(ins, outs, sems)
        for cp in mine + first:
            cp.start()

    def finish(ins, outs, sems):
        me, sibling, chips, c, copy, mine, first = plan(ins, outs, sems)
        passed = []
        for t in range(n):
            for j, chip in enumerate(chips):
                copy(t, 1 + j, (*chip, c), me).wait_recv()
                cp = copy(t, 4 + j, (*chip, c), sibling)
                cp.start()
                passed.append(cp)
        for t in range(n):
            copy(t, 0, sibling, me).wait_recv()
            for j, chip in enumerate(chips):
                copy(t, 4 + j, (*chip, 1 - c), me).wait_recv()
        for cp in first + passed:
            cp.wait_send()
        for cp in mine:
            cp.wait()

    return _Comm(list(shards), [_sds((N_DEV,) + s.shape, s.dtype) for s in shards], _sem_scratch(n), start, finish)


def _peer(d):
    x, y, c = lax.axis_index("x"), lax.axis_index("y"), lax.axis_index("c")
    px = 1 - x if d & 4 else x
    py = 1 - y if d & 2 else y
    pc = 1 - c if d & 1 else c
    return (px, py, pc), 4 * px + 2 * py + pc


def _exchange_comm(parts):
    n = len(parts)

    def plan(ins, outs, sems):
        send_sems, recv_sems, local_sems = sems
        _, me = _peer(0)

        def copy(t, d):
            peer, idx = _peer(d)
            return pltpu.make_async_remote_copy(
                src_ref=ins[t].at[idx], dst_ref=outs[t].at[d],
                send_sem=send_sems.at[t, d - 1], recv_sem=recv_sems.at[t, d - 1],
                device_id=peer, device_id_type=MESH_ID)

        mine = [pltpu.make_async_copy(ins[t].at[me], outs[t].at[0], local_sems.at[t]) for t in range(n)]
        return mine, [copy(t, d) for t in range(n) for d in range(1, N_DEV)]

    def start(ins, outs, sems):
        mine, copies = plan(ins, outs, sems)
        for cp in mine + copies:
            cp.start()

    def finish(ins, outs, sems):
        mine, copies = plan(ins, outs, sems)
        for cp in copies:
            cp.wait_recv()
        for cp in copies:
            cp.wait_send()
        for cp in mine:
            cp.wait()

    return _Comm(list(parts), [_sds(p.shape, p.dtype) for p in parts], _sem_scratch(n), start, finish)


def _chip_exchange_comm(parts):
    n = len(parts)

    def plan(ins, outs, sems):
        send_sems, recv_sems, local_sems = sems
        x, y, c = lax.axis_index("x"), lax.axis_index("y"), lax.axis_index("c")

        def copy(t, dq):
            px = 1 - x if dq & 2 else x
            py = 1 - y if dq & 1 else y
            return pltpu.make_async_remote_copy(
                src_ref=ins[t].at[2 * px + py], dst_ref=outs[t].at[dq],
                send_sem=send_sems.at[t, dq - 1], recv_sem=recv_sems.at[t, dq - 1],
                device_id=(px, py, c), device_id_type=MESH_ID)

        mine = [pltpu.make_async_copy(ins[t].at[2 * x + y], outs[t].at[0], local_sems.at[t]) for t in range(n)]
        return mine, [copy(t, dq) for t in range(n) for dq in range(1, 4)]

    def start(ins, outs, sems):
        mine, copies = plan(ins, outs, sems)
        for cp in mine + copies:
            cp.start()

    def finish(ins, outs, sems):
        mine, copies = plan(ins, outs, sems)
        for cp in copies:
            cp.wait_recv()
        for cp in copies:
            cp.wait_send()
        for cp in mine:
            cp.wait()

    return _Comm(list(parts), [_sds(p.shape, p.dtype) for p in parts], _sem_scratch(n, 3), start, finish)


def _pair_reduce(part, name):
    _, r, c = part.shape
    n_chips = N_DEV // 2
    tr = 128

    def body(p_ref, o_ref, own, land, send_sems, recv_sems, local_sems):
        x, y, core = lax.axis_index("x"), lax.axis_index("y"), lax.axis_index("c")
        sends = [pltpu.make_async_remote_copy(
            src_ref=p_ref.at[2 * q + (1 - core)], dst_ref=land.at[q], send_sem=send_sems.at[q], recv_sem=recv_sems.at[q],
            device_id=(x, y, 1 - core), device_id_type=MESH_ID) for q in range(n_chips)]
        loads = [pltpu.make_async_copy(p_ref.at[2 * q + core], own.at[q], local_sems.at[q]) for q in range(n_chips)]
        for cp in sends + loads:
            cp.start()
        for q in range(n_chips):
            loads[q].wait()
            sends[q].wait_recv()

            def add_rows(i, _):
                rows = pl.ds(pl.multiple_of(i * tr, tr), tr)
                o_ref[q, rows, :] = (own[q, rows, :].astype(F32) + land[q, rows, :].astype(F32)).astype(BF16)
                return 0

            lax.fori_loop(0, r // tr, add_rows, 0)
        for cp in sends:
            cp.wait_send()

    return pl.pallas_call(
        body, name=name, out_shape=_sds((n_chips, r, c), BF16),
        in_specs=[pl.BlockSpec(memory_space=pl.ANY)], out_specs=pl.BlockSpec(memory_space=pltpu.VMEM),
        scratch_shapes=[pltpu.VMEM((n_chips, r, c), BF16), pltpu.VMEM((n_chips, r, c), BF16)]
        + [pltpu.SemaphoreType.DMA((n_chips,))] * 3,
        compiler_params=pltpu.CompilerParams(vmem_limit_bytes=VMEM_LIMIT),
    )(part)


def _comm_only(comm, name):
    n = len(comm.inputs)

    def body(*refs):
        ins, outs, sems = refs[:n], refs[n:2 * n], refs[2 * n:]
        comm.start(ins, outs, sems)
        comm.finish(ins, outs, sems)

    any_spec = pl.BlockSpec(memory_space=pl.ANY)
    return pl.pallas_call(body, name=name, out_shape=comm.out_shape, in_specs=[any_spec] * n, out_specs=[any_spec] * n,
                          scratch_shapes=comm.scratch)(*comm.inputs)


def _host_call(body, *, name, grid, in_specs, out_specs, out_shape, scratch_shapes=(), semantics, args, comm=None):
    if comm is None:
        res = pl.pallas_call(body, name=name, grid=grid, in_specs=in_specs, out_specs=out_specs, out_shape=out_shape,
                             scratch_shapes=list(scratch_shapes), compiler_params=_params(*semantics))(*args)
        return res, []
    n_in, n_out, n_scr, n_c = len(in_specs), len(out_specs), len(scratch_shapes), len(comm.inputs)

    def with_comm(*refs):
        ins, refs = refs[:n_in], refs[n_in:]
        c_ins, refs = refs[:n_c], refs[n_c:]
        outs, refs = refs[:n_out], refs[n_out:]
        c_outs, refs = refs[:n_c], refs[n_c:]
        scr, sems = refs[:n_scr], refs[n_scr:]
        ids = [pl.program_id(a) for a in range(len(grid))]
        first = functools.reduce(jnp.logical_and, [i == 0 for i in ids])
        last = functools.reduce(jnp.logical_and, [i == g - 1 for i, g in zip(ids, grid)])

        @pl.when(first)
        def _():
            comm.start(c_ins, c_outs, sems)

        body(*ins, *outs, *scr)

        @pl.when(last)
        def _():
            comm.finish(c_ins, c_outs, sems)

    any_spec = pl.BlockSpec(memory_space=pl.ANY)
    res = pl.pallas_call(
        with_comm, name=name, grid=grid, in_specs=list(in_specs) + [any_spec] * n_c,
        out_specs=list(out_specs) + [any_spec] * n_c, out_shape=list(out_shape) + comm.out_shape,
        scratch_shapes=list(scratch_shapes) + comm.scratch,
        compiler_params=_params(*(["arbitrary"] * len(grid))))(*args, *comm.inputs)
    return res[:n_out], res[n_out:]


def _small_all_reduce(pack):
    def body(p_ref, o_ref, land, s1, r1, s2, r2):
        _, me = _peer(0)

        def piece(ref, idx):
            return ref.at[pl.ds(pl.multiple_of(idx * PIECE, 8), PIECE), :]

        def scatter(d):
            peer, idx = _peer(d)
            return pltpu.make_async_remote_copy(
                src_ref=piece(p_ref, idx), dst_ref=land.at[d - 1], send_sem=s1.at[d - 1], recv_sem=r1.at[d - 1],
                device_id=peer, device_id_type=MESH_ID)

        def gather(d, from_idx):
            peer, _ = _peer(d)
            return pltpu.make_async_remote_copy(
                src_ref=piece(o_ref, from_idx), dst_ref=piece(o_ref, from_idx), send_sem=s2.at[d - 1],
                recv_sem=r2.at[d - 1], device_id=peer, device_id_type=MESH_ID)

        first = [scatter(d) for d in range(1, N_DEV)]
        for cp in first:
            cp.start()
        acc = piece(p_ref, me)[...]
        for d in range(1, N_DEV):
            first[d - 1].wait_recv()
            acc = acc + land[d - 1]
        piece(o_ref, me)[...] = acc
        second = [gather(d, me) for d in range(1, N_DEV)]
        for cp in second:
            cp.start()
        for d in range(1, N_DEV):
            gather(d, _peer(d)[1]).wait_recv()
        for cp in first + second:
            cp.wait_send()

    vmem = pl.BlockSpec(memory_space=pltpu.VMEM)
    return pl.pallas_call(
        body, name="small_all_reduce",
        out_shape=_sds(pack.shape, F32), in_specs=[vmem], out_specs=vmem,
        scratch_shapes=[pltpu.VMEM((N_DEV - 1, PIECE, D_MODEL), F32)] + [pltpu.SemaphoreType.DMA((N_DEV - 1,))] * 4,
    )(pack)


def _norm_in_proj(x, g_mix, w_in, tm, comm):
    s = x.shape[0]

    def body(x_ref, g_ref, w_ref, z_ref, h_ref):
        xn, _ = _rms(x_ref[...])
        h = (xn * g_ref[...]).astype(BF16)
        h_ref[...] = h
        for j in range(N_SEG):
            cols = slice(j * D_MODEL, (j + 1) * D_MODEL)
            z_ref[:, cols] = _dot(h, w_ref[:, cols]).astype(BF16)

    return _host_call(
        body, name="norm_in_proj", grid=(s // tm,),
        in_specs=[pl.BlockSpec((tm, D_MODEL), lambda i: (i, 0)), pl.BlockSpec((1, D_MODEL), lambda i: (0, 0)),
                  _resident(w_in.shape)],
        out_specs=[pl.BlockSpec((tm, N_SEG * D_MODEL), lambda i: (i, 0)), pl.BlockSpec((tm, D_MODEL), lambda i: (i, 0))],
        out_shape=[_sds((s, N_SEG * D_MODEL), BF16), _sds((s, D_MODEL), BF16)],
        semantics=("parallel",), args=(x, g_mix, w_in), comm=comm)


def _gate_chunk(xr_ext, w4_ref, cb_ref, wa, wx, ba_ref, bx_ref, sp, ch):
    xc = cb_ref[...] + w4_ref[3:4, :] * _shift_dn(xr_ext, 0, ch)
    for k in range(3):
        xc = xc + w4_ref[k:k + 1, :] * _shift_dn(xr_ext, 3 - k, ch)
    xcb = xc.astype(BF16)
    ra = _sigmoid(_dot(xcb, wa) + ba_ref[...])
    ia = _sigmoid(_dot(xcb, wx) + bx_ref[...])
    log_a = (-LRU_C) * ra * sp
    return xc, ra, ia, log_a


def _conv3(q_ext, w3_ref, ch):
    y = w3_ref[2:3, :] * _shift_dn(q_ext, 0, ch)
    for k in range(2):
        y = y + w3_ref[k:k + 1, :] * _shift_dn(q_ext, 2 - k, ch)
    return y


def _z_block_specs(s, cw, segs):
    nb = D_MODEL // cw
    return [pl.BlockSpec((s, cw), functools.partial(lambda j, seg: (0, seg * nb + j), seg=seg)) for seg in segs]


def _mixers_fwd(z, w4, b4, w_a, b_a, w_x, b_x, lam, w3, ch, comm):
    s = z.shape[0]
    cw = RNN_BW
    n_chunks = s // ch

    def body(xr_ref, gr_ref, cb_ref, cc_ref, cx_ref, w4_ref, b4_ref, wa_ref, ba_ref, wx_ref, bx_ref, lam_ref, w3_ref,
             ya_ref, yb_ref, h_ref, a_scr, b_scr):
        sp = _softplus(-lam_ref[...])
        wa = wa_ref[0].astype(BF16)
        wx = wx_ref[0].astype(BF16)

        def gates(c, _):
            xc, _, ia, log_a = _gate_chunk(_ext_before(xr_ref, c, ch), w4_ref, b4_ref, wa, wx, ba_ref, bx_ref, sp, ch)
            rows = pl.ds(pl.multiple_of(c * ch, HALO), ch)
            a_scr[rows, :] = jnp.exp(log_a)
            b_scr[rows, :] = jnp.sqrt(-_expm1(2.0 * log_a)) * (ia * xc)
            q_ext = _ext_before(cc_ref, c, ch) * _ext_before(cx_ref, c, ch)
            yb_ref[rows, :] = (_rows(cb_ref, c, ch).astype(F32) * _conv3(q_ext, w3_ref, ch)).astype(BF16)
            return 0

        lax.fori_loop(0, n_chunks, gates, 0)
        _scan_fwd(a_scr, b_scr, h_ref, s, cw)

        def outputs(c, _):
            rows = pl.ds(pl.multiple_of(c * ch, HALO), ch)
            ya_ref[rows, :] = (h_ref[rows, :] * _gelu(gr_ref[rows, :].astype(F32))).astype(BF16)
            return 0

        lax.fori_loop(0, n_chunks, outputs, 0)

    col = lambda j: (0, j)
    vec = pl.BlockSpec((1, cw), col)
    sq = pl.BlockSpec((1, cw, cw), lambda j: (j, 0, 0))
    act = pl.BlockSpec((s, cw), col)
    return _host_call(
        body, name="mixers_fwd", grid=(D_MODEL // cw,),
        in_specs=_z_block_specs(s, cw, range(5)) + [pl.BlockSpec((4, cw), col), vec, sq, vec, sq, vec, vec,
                                                     pl.BlockSpec((3, cw), col)],
        out_specs=[act, act, act],
        out_shape=[_sds((s, D_MODEL), BF16), _sds((s, D_MODEL), BF16), _sds((s, D_MODEL), F32)],
        scratch_shapes=[pltpu.VMEM((s, cw), F32), pltpu.VMEM((s, cw), F32)],
        semantics=("parallel",), args=(z, z, z, z, z, w4, b4, w_a, b_a, w_x, b_x, lam, w3), comm=comm)


def _merge_out(ya, yb, z, x, w_pa, w_pb, w_out, g_ffn, tm):
    s = x.shape[0]

    def body(ya_ref, yb_ref, ga_ref, gb_ref, x_ref, wa_ref, wb_ref, wo_ref, g_ref, pa_ref, pb_ref, m_ref, x2_ref, h2_ref):
        pa = _dot(ya_ref[...], wa_ref[...])
        pb = _dot(yb_ref[...], wb_ref[...])
        pa_ref[...] = pa.astype(BF16)
        pb_ref[...] = pb.astype(BF16)
        m = (_sigmoid(ga_ref[...].astype(F32)) * pa + _sigmoid(gb_ref[...].astype(F32)) * pb).astype(BF16)
        m_ref[...] = m
        x2 = x_ref[...] + _dot(m, wo_ref[...])
        x2_ref[...] = x2
        xn, _ = _rms(x2)
        h2_ref[...] = (xn * g_ref[...]).astype(BF16)

    tile = pl.BlockSpec((tm, D_MODEL), lambda i: (i, 0))
    full = _resident((D_MODEL, D_MODEL))
    return pl.pallas_call(
        body, name="merge_out", grid=(s // tm,),
        in_specs=[tile, tile, pl.BlockSpec((tm, D_MODEL), lambda i: (i, 5)), pl.BlockSpec((tm, D_MODEL), lambda i: (i, 6)),
                  tile, full, full, full, pl.BlockSpec((1, D_MODEL), lambda i: (0, 0))],
        out_specs=[tile] * 5,
        out_shape=[_sds((s, D_MODEL), BF16)] * 3 + [_sds((s, D_MODEL), F32), _sds((s, D_MODEL), BF16)],
        compiler_params=_params("parallel"),
    )(ya, yb, z, z, x, w_pa, w_pb, w_out, g_ffn)


def _up_proj(h2, w_up_g, tm):
    s = h2.shape[0]
    bw = w_up_g.shape[2]
    per_half = N_DEV // 2

    def body(h_ref, w_ref, u_ref):
        h = h_ref[...]
        for j in range(N_DEV):
            cols = slice((j % per_half) * bw, (j % per_half + 1) * bw)
            u_ref[j // per_half, :, cols] = _dot(h, w_ref[j]).astype(BF16)

    return pl.pallas_call(
        body, name="up_proj", grid=(s // tm,),
        in_specs=[pl.BlockSpec((tm, D_MODEL), lambda i: (i, 0)), _resident(w_up_g.shape)],
        out_specs=pl.BlockSpec((2, tm, D_FF), lambda i: (0, i, 0)),
        out_shape=_sds((2, s, D_FF), BF16),
        compiler_params=_params("parallel"),
    )(h2, w_up_g)


def _ffn_conv(u_ref, w_ref, b_ref, half, c, ch):
    ext = _ext_before(u_ref.at[half], c, ch)
    w = w_ref.at[half]
    return b_ref[half] + _conv3(ext, w, ch), ext


def _ffn_act(u0, wc, bc, cw, ch):
    s = u0.shape[1]
    n_chunks = s // ch

    def body(u_ref, w_ref, b_ref, f_ref):
        def chunk(c, _):
            ug, _ = _ffn_conv(u_ref, w_ref, b_ref, 0, c, ch)
            uv, _ = _ffn_conv(u_ref, w_ref, b_ref, 1, c, ch)
            f_ref[pl.ds(pl.multiple_of(c * ch, HALO), ch), :] = (_gelu(ug) * uv).astype(BF16)
            return 0

        lax.fori_loop(0, n_chunks, chunk, 0)

    return pl.pallas_call(
        body, name="ffn_act", grid=(D_FF // cw,),
        in_specs=[pl.BlockSpec((2, s, cw), lambda j: (0, 0, j)), pl.BlockSpec((2, 3, cw), lambda j: (0, 0, j)),
                  pl.BlockSpec((2, 1, cw), lambda j: (0, 0, j))],
        out_specs=pl.BlockSpec((s, cw), lambda j: (0, j)),
        out_shape=_sds((s, D_FF), BF16),
        compiler_params=_params("parallel"),
    )(u0, wc, bc)


def _head(x2, f, p, target, w_down, w_gate, w_pp, g_ple, g_final, tm):
    s = x2.shape[0]

    def body(x2_ref, f_ref, p_ref, t_ref, wd_ref, wg_ref, wp_ref, gp_ref, gf_ref,
             x3_ref, dgt_ref, de0_ref, dx3_ref, df_ref, sums_ref):
        @pl.when(pl.program_id(0) == 0)
        def _():
            sums_ref[...] = jnp.zeros_like(sums_ref)

        x3 = x2_ref[...] + _dot(f_ref[...], wd_ref[...])
        x3b = x3.astype(BF16)
        x3_ref[...] = x3b
        e0n, re = _rms(_dot(p_ref[...].astype(BF16), wp_ref[...]))
        e = e0n * gp_ref[...]
        sg = _sigmoid(_dot(x3b, wg_ref[...]))
        x4n, r4 = _rms(x3 + sg * e)
        diff = x4n * gf_ref[...] - t_ref[...]
        sums_ref[0:1, :] += jnp.sum(diff * diff, axis=0, keepdims=True)
        dy = diff * (1.0 / D_MODEL)
        sums_ref[1:2, :] += jnp.sum(dy * x4n, axis=0, keepdims=True)
        dx4 = _rms_bwd(dy * gf_ref[...], x4n, r4)
        de = dx4 * sg
        dgt = ((dx4 * e) * (sg * (1.0 - sg))).astype(BF16)
        dgt_ref[...] = dgt
        sums_ref[2:3, :] += jnp.sum(de * e0n, axis=0, keepdims=True)
        de0_ref[...] = _rms_bwd(de * gp_ref[...], e0n, re).astype(BF16)
        dx3 = dx4 + _dot_nt(dgt, wg_ref[...])
        dx3_ref[...] = dx3
        df_ref[...] = _dot_nt(dx3.astype(BF16), wd_ref[...]).astype(BF16)

    tile = pl.BlockSpec((tm, D_MODEL), lambda i: (i, 0))
    vec = pl.BlockSpec((1, D_MODEL), lambda i: (0, 0))
    wide = pl.BlockSpec((tm, D_FF), lambda i: (i, 0))
    return pl.pallas_call(
        body, name="loss_head", grid=(s // tm,),
        in_specs=[tile, wide, pl.BlockSpec((tm, D_PLE), lambda i: (i, 0)), tile,
                  _resident((D_FF, D_MODEL)), _resident((D_MODEL, D_MODEL)), _resident((D_PLE, D_MODEL)), vec, vec],
        out_specs=[tile, tile, tile, tile, wide, pl.BlockSpec((8, D_MODEL), lambda i: (0, 0))],
        out_shape=[_sds((s, D_MODEL), BF16)] * 3 + [_sds((s, D_MODEL), F32), _sds((s, D_FF), BF16),
                                                    _sds((8, D_MODEL), F32)],
        compiler_params=_params("arbitrary"),
    )(x2, f, p, target, w_down, w_gate, w_pp, g_ple, g_final)


def _ffn_act_bwd(u0, df, wc, bc, cw, ch, comm):
    s = u0.shape[1]
    n_chunks = s // ch

    def body(u_ref, df_ref, w_ref, b_ref, du0_ref, dw_ref, db_ref, du_scr):
        dw_ref[...] = jnp.zeros_like(dw_ref)
        db_ref[...] = jnp.zeros_like(db_ref)

        def first(c, _):
            ug, ext_g = _ffn_conv(u_ref, w_ref, b_ref, 0, c, ch)
            uv, ext_v = _ffn_conv(u_ref, w_ref, b_ref, 1, c, ch)
            gel, dgel = _gelu_and_grad(ug)
            d = _rows(df_ref, c, ch).astype(F32)
            rows = pl.ds(pl.multiple_of(c * ch, HALO), ch)
            for half, du, ext in ((0, d * uv * dgel, ext_g), (1, d * gel, ext_v)):
                du_scr[half, rows, :] = du
                db_ref[half] += jnp.sum(du, axis=0, keepdims=True)
                for k in range(3):
                    dw_ref[half, k:k + 1, :] += jnp.sum(du * _shift_dn(ext, 2 - k, ch), axis=0, keepdims=True)
            return 0

        lax.fori_loop(0, n_chunks, first, 0)

        def second(c, _):
            rows = pl.ds(pl.multiple_of(c * ch, HALO), ch)
            for half in range(2):
                ext = _ext_after(du_scr.at[half], c, ch, n_chunks)
                w = w_ref.at[half]
                acc = w[2:3, :] * _shift_up(ext, 0, ch)
                for k in range(2):
                    acc = acc + w[k:k + 1, :] * _shift_up(ext, 2 - k, ch)
                du0_ref[half, rows, :] = acc.astype(BF16)
            return 0

        lax.fori_loop(0, n_chunks, second, 0)

    blk = lambda r: pl.BlockSpec((2, r, cw), lambda j: (0, 0, j))
    return _host_call(
        body, name="ffn_act_bwd", grid=(D_FF // cw,),
        in_specs=[blk(s), pl.BlockSpec((s, cw), lambda j: (0, j)), blk(3), blk(1)],
        out_specs=[blk(s), blk(3), blk(1)],
        out_shape=[_sds((2, s, D_FF), BF16), _sds((2, 3, D_FF), F32), _sds((2, 1, D_FF), F32)],
        scratch_shapes=[pltpu.VMEM((2, s, cw), F32)],
        semantics=("parallel",), args=(u0, df, wc, bc), comm=comm)


def _up_bwd_merge_bwd(du0, w_up_g, x2, dx3, z, pa, pb, w_out, w_pa, w_pb, g_ffn, tm, comm):
    s = x2.shape[0]
    bw = w_up_g.shape[2]
    per_half = N_DEV // 2

    def body(du_ref, wu_ref, x2_ref, dx3_ref, ga_ref, gb_ref, pa_ref, pb_ref, wo_ref, wa_ref, wb_ref, g_ref,
             dx2_ref, dzg_ref, dpa_ref, dpb_ref, dya_ref, dyb_ref, sums_ref):
        @pl.when(pl.program_id(0) == 0)
        def _():
            sums_ref[...] = jnp.zeros_like(sums_ref)

        dh2 = None
        for j in range(N_DEV):
            cols = slice((j % per_half) * bw, (j % per_half + 1) * bw)
            term = _dot_nt(du_ref[j // per_half, :, cols], wu_ref[j])
            dh2 = term if dh2 is None else dh2 + term
        x2n, r2 = _rms(x2_ref[...])
        sums_ref[0:1, :] += jnp.sum(dh2 * x2n, axis=0, keepdims=True)
        dx2 = dx3_ref[...] + _rms_bwd(dh2 * g_ref[...], x2n, r2)
        dx2_ref[...] = dx2
        dm = _dot_nt(dx2.astype(BF16), wo_ref[...])
        sa = _sigmoid(ga_ref[...].astype(F32))
        sb = _sigmoid(gb_ref[...].astype(F32))
        dzg_ref[0] = (dm * pa_ref[...].astype(F32) * (sa * (1.0 - sa))).astype(BF16)
        dzg_ref[1] = (dm * pb_ref[...].astype(F32) * (sb * (1.0 - sb))).astype(BF16)
        dpa = (dm * sa).astype(BF16)
        dpb = (dm * sb).astype(BF16)
        dpa_ref[...] = dpa
        dpb_ref[...] = dpb
        dya_ref[...] = _dot_nt(dpa, wa_ref[...]).astype(BF16)
        dyb_ref[...] = _dot_nt(dpb, wb_ref[...]).astype(BF16)

    tile = pl.BlockSpec((tm, D_MODEL), lambda i: (i, 0))
    full = _resident((D_MODEL, D_MODEL))
    return _host_call(
        body, name="up_bwd_merge_bwd", grid=(s // tm,),
        in_specs=[pl.BlockSpec((2, tm, D_FF), lambda i: (0, i, 0)), _resident(w_up_g.shape),
                  tile, tile, pl.BlockSpec((tm, D_MODEL), lambda i: (i, 5)),
                  pl.BlockSpec((tm, D_MODEL), lambda i: (i, 6)), tile, tile, full, full, full,
                  pl.BlockSpec((1, D_MODEL), lambda i: (0, 0))],
        out_specs=[tile, pl.BlockSpec((2, tm, D_MODEL), lambda i: (0, i, 0)), tile, tile, tile, tile,
                   pl.BlockSpec((8, D_MODEL), lambda i: (0, 0))],
        out_shape=[_sds((s, D_MODEL), F32), _sds((2, s, D_MODEL), BF16)] + [_sds((s, D_MODEL), BF16)] * 4
        + [_sds((8, D_MODEL), F32)],
        semantics=("arbitrary",), args=(du0, w_up_g, x2, dx3, z, z, pa, pb, w_out, w_pa, w_pb, g_ffn),
        comm=comm)


def _mixers_bwd(z, h, dya, dyb, w4, b4, w_a, b_a, w_x, b_x, lam, w3, ch, comm):
    s = z.shape[0]
    cw = RNN_BW
    n_chunks = s // ch

    def body(xr_ref, gr_ref, cb_ref, cc_ref, cx_ref, h_ref, dya_ref, dyb_ref, w4_ref, b4_ref, wa_ref, ba_ref, wx_ref,
             bx_ref, lam_ref, w3_ref,
             dz_ref, dw4_ref, db4_ref, dwa_ref, dba_ref, dwx_ref, dbx_ref, dlam_ref, dw3_ref,
             a_scr, g_scr, q_scr, xc_scr, ra_scr, ia_scr):
        sp = _softplus(-lam_ref[...])
        wa = wa_ref[0].astype(BF16)
        wx = wx_ref[0].astype(BF16)
        for ref in (dw4_ref, db4_ref, dwa_ref, dba_ref, dwx_ref, dbx_ref, dlam_ref, dw3_ref):
            ref[...] = jnp.zeros_like(ref)

        def first(c, _):
            rows = pl.ds(pl.multiple_of(c * ch, HALO), ch)
            xc, ra, ia, log_a = _gate_chunk(_ext_before(xr_ref, c, ch), w4_ref, b4_ref, wa, wx, ba_ref, bx_ref, sp, ch)
            xc_scr[rows, :] = xc
            ra_scr[rows, :] = ra
            ia_scr[rows, :] = ia
            a_scr[rows, :] = jnp.exp(log_a)
            gel, dgel = _gelu_and_grad(gr_ref[rows, :].astype(F32))
            dya_c = dya_ref[rows, :].astype(F32)
            g_scr[rows, :] = dya_c * gel
            dz_ref[1, rows, :] = (dya_c * h_ref[rows, :] * dgel).astype(BF16)
            q_ext = _ext_before(cc_ref, c, ch) * _ext_before(cx_ref, c, ch)
            dyb_c = dyb_ref[rows, :].astype(F32)
            dz_ref[2, rows, :] = (dyb_c * _conv3(q_ext, w3_ref, ch)).astype(BF16)
            dyq = dyb_c * cb_ref[rows, :].astype(F32)
            q_scr[rows, :] = dyq
            for k in range(3):
                dw3_ref[k:k + 1, :] += jnp.sum(dyq * _shift_dn(q_ext, 2 - k, ch), axis=0, keepdims=True)
            return 0

        lax.fori_loop(0, n_chunks, first, 0)
        _scan_rev(a_scr, g_scr, s, cw)

        def second(c, _):
            rows = pl.ds(pl.multiple_of(c * ch, HALO), ch)
            xc, ra, ia, a = xc_scr[rows, :], ra_scr[rows, :], ia_scr[rows, :], a_scr[rows, :]
            mult = jnp.sqrt(-_expm1((-2.0 * LRU_C) * ra * sp))
            dh = g_scr[rows, :]
            h_prev = _shift_dn(_ext_before(h_ref, c, ch), 1, ch)
            dlog_a = dh * h_prev * a - (dh * ia * xc) * (a * a) / mult
            dlam_ref[...] += jnp.sum(dlog_a * ra, axis=0, keepdims=True)
            dpre_a = (dlog_a * ((-LRU_C) * sp)) * (ra * (1.0 - ra))
            dpre_x = (dh * mult * xc) * (ia * (1.0 - ia))
            dba_ref[...] += jnp.sum(dpre_a, axis=0, keepdims=True)
            dbx_ref[...] += jnp.sum(dpre_x, axis=0, keepdims=True)
            xcb = xc.astype(BF16)
            dpa_b = dpre_a.astype(BF16)
            dpx_b = dpre_x.astype(BF16)
            dwa_ref[0] += _dot_tn(xcb, dpa_b)
            dwx_ref[0] += _dot_tn(xcb, dpx_b)
            a_scr[rows, :] = dh * mult * ia + _dot_nt(dpa_b, wa) + _dot_nt(dpx_b, wx)
            return 0

        lax.fori_loop(0, n_chunks, second, 0)
        dlam_ref[...] = dlam_ref[...] * (LRU_C * _sigmoid(-lam_ref[...]))

        def third(c, _):
            rows = pl.ds(pl.multiple_of(c * ch, HALO), ch)
            dxc_ext = _ext_after(a_scr, c, ch, n_chunks)
            dxc = _shift_up(dxc_ext, 0, ch)
            xr_ext = _ext_before(xr_ref, c, ch)
            db4_ref[...] += jnp.sum(dxc, axis=0, keepdims=True)
            dxr = w4_ref[3:4, :] * dxc
            dw4_ref[3:4, :] += jnp.sum(dxc * _shift_dn(xr_ext, 0, ch), axis=0, keepdims=True)
            for k in range(3):
                dxr = dxr + w4_ref[k:k + 1, :] * _shift_up(dxc_ext, 3 - k, ch)
                dw4_ref[k:k + 1, :] += jnp.sum(dxc * _shift_dn(xr_ext, 3 - k, ch), axis=0, keepdims=True)
            dz_ref[0, rows, :] = dxr.astype(BF16)
            dyq_ext = _ext_after(q_scr, c, ch, n_chunks)
            dq = w3_ref[2:3, :] * _shift_up(dyq_ext, 0, ch)
            for k in range(2):
                dq = dq + w3_ref[k:k + 1, :] * _shift_up(dyq_ext, 2 - k, ch)
            dz_ref[3, rows, :] = (dq * cx_ref[rows, :].astype(F32)).astype(BF16)
            dz_ref[4, rows, :] = (dq * cc_ref[rows, :].astype(F32)).astype(BF16)
            return 0

        lax.fori_loop(0, n_chunks, third, 0)

    col = lambda j: (0, j)
    vec = pl.BlockSpec((1, cw), col)
    sq = pl.BlockSpec((1, cw, cw), lambda j: (j, 0, 0))
    act = pl.BlockSpec((s, cw), col)
    w4s, w3s = pl.BlockSpec((4, cw), col), pl.BlockSpec((3, cw), col)
    vec_shape = _sds((1, D_MODEL), F32)
    sq_shape = _sds((D_MODEL // cw, cw, cw), F32)
    return _host_call(
        body, name="mixers_bwd", grid=(D_MODEL // cw,),
        in_specs=_z_block_specs(s, cw, range(5)) + [act, act, act, w4s, vec, sq, vec, sq, vec, vec, w3s],
        out_specs=[pl.BlockSpec((5, s, cw), lambda j: (0, 0, j)), w4s, vec, sq, vec, sq, vec, vec, w3s],
        out_shape=[_sds((5, s, D_MODEL), BF16), _sds((4, D_MODEL), F32), vec_shape, sq_shape, vec_shape, sq_shape,
                   vec_shape, vec_shape, _sds((3, D_MODEL), F32)],
        scratch_shapes=[pltpu.VMEM((s, cw), F32)] * 6,
        semantics=("parallel",), args=(z, z, z, z, z, h, dya, dyb, w4, b4, w_a, b_a, w_x, b_x, lam, w3), comm=comm)


def _in_proj_bwd(dz5, dzg, w_in, x, dx2, g_mix, tm, comm):
    s = x.shape[0]

    def body(d5_ref, dg_ref, w_ref, x_ref, dx2_ref, g_ref, dx_ref, sums_ref):
        @pl.when(pl.program_id(0) == 0)
        def _():
            sums_ref[...] = jnp.zeros_like(sums_ref)

        dh1 = None
        for k in range(N_SEG):
            d = d5_ref[k] if k < 5 else dg_ref[k - 5]
            term = _dot_nt(d, w_ref[:, k * D_MODEL:(k + 1) * D_MODEL])
            dh1 = term if dh1 is None else dh1 + term
        xn, r1 = _rms(x_ref[...])
        sums_ref[0:1, :] += jnp.sum(dh1 * xn, axis=0, keepdims=True)
        dx_ref[...] = dx2_ref[...] + _rms_bwd(dh1 * g_ref[...], xn, r1)

    tile = pl.BlockSpec((tm, D_MODEL), lambda i: (i, 0))
    return _host_call(
        body, name="in_proj_bwd", grid=(s // tm,),
        in_specs=[pl.BlockSpec((5, tm, D_MODEL), lambda i: (0, i, 0)), pl.BlockSpec((2, tm, D_MODEL), lambda i: (0, i, 0)),
                  _resident(w_in.shape), tile, tile, pl.BlockSpec((1, D_MODEL), lambda i: (0, 0))],
        out_specs=[tile, pl.BlockSpec((8, D_MODEL), lambda i: (0, 0))],
        out_shape=[_sds((s, D_MODEL), F32), _sds((8, D_MODEL), F32)],
        semantics=("arbitrary",), args=(dz5, dzg, w_in, x, dx2, g_mix), comm=comm)


def _weight_grad(a, b, b_spec, out_spec, out_shape, *, n_blocks, chunks, width, tm, tk, name):
    s, m = a.shape
    nk = s // tk
    b_chunked = len([d for d in b_spec.block_shape if d is not None]) == 3
    blocks_out = len([d for d in out_spec.block_shape if d is not None]) == 3

    def body(a_ref, b_ref, o_ref, acc):
        k = pl.program_id(2)

        @pl.when(k == 0)
        def _():
            acc[...] = jnp.zeros_like(acc)

        at = a_ref[...].astype(BF16).T
        for j in range(chunks):
            cols = slice(j * width, (j + 1) * width)
            acc[:, cols] += _dot(at, (b_ref[j] if b_chunked else b_ref[:, cols]).astype(BF16))

        @pl.when(k == nk - 1)
        def _():
            if blocks_out:
                for j in range(chunks):
                    o_ref[j] = acc[:, j * width:(j + 1) * width].astype(o_ref.dtype)
            else:
                o_ref[...] = acc[...].astype(o_ref.dtype)

    return pl.pallas_call(
        body, name=name, grid=(m // tm, n_blocks, nk),
        in_specs=[pl.BlockSpec((tk, tm), lambda i, j, k: (k, i)), b_spec], out_specs=out_spec, out_shape=out_shape,
        scratch_shapes=[pltpu.VMEM((tm, chunks * width), F32)],
        compiler_params=_params("parallel", "parallel", "arbitrary"),
    )(a, b)


def _wgrad_2d(a, b, name, tk):
    m, n = a.shape[1], b.shape[1]
    tm = min(m, 1024)
    return _weight_grad(a, b, pl.BlockSpec((tk, n), lambda i, j, k: (k, 0)), pl.BlockSpec((tm, n), lambda i, j, k: (i, 0)),
                        _sds((m, n), BF16), n_blocks=1, chunks=1, width=n, tm=tm, tk=tk, name=name)


def _wgrad_segments(a, b3, name, tm, tk):
    m = a.shape[1]
    g = b3.shape[0]
    return _weight_grad(a, b3, pl.BlockSpec((g, tk, D_MODEL), lambda i, j, k: (0, k, 0)),
                        pl.BlockSpec((tm, g * D_MODEL), lambda i, j, k: (i, 0)), _sds((m, g * D_MODEL), BF16),
                        n_blocks=1, chunks=g, width=D_MODEL, tm=tm, tk=tk, name=name)


def _wgrad_up(h2, du0, bw, tk):
    per_half = N_DEV // 2
    return _weight_grad(h2, du0, pl.BlockSpec((None, tk, D_FF), lambda i, j, k: (j, k, 0)),
                        pl.BlockSpec((per_half, D_MODEL, bw), lambda i, j, k: (j, 0, 0)), _sds((N_DEV, D_MODEL, bw), BF16),
                        n_blocks=2, chunks=per_half, width=bw, tm=D_MODEL, tk=tk, name="wgrad_up")


def _adam_math(w, g, m, v):
    m = ADAM_B1 * m + (1.0 - ADAM_B1) * g
    v = ADAM_B2 * v + (1.0 - ADAM_B2) * jnp.square(g)
    m_hat = m / (1.0 - ADAM_B1 ** ADAM_STEP)
    v_hat = v / (1.0 - ADAM_B2 ** ADAM_STEP)
    delta = -ADAM_LR * (m_hat / (jnp.sqrt(v_hat) + ADAM_EPS) + ADAM_WD * w)
    return delta, m, v


def _adam_shard(parts, w, m, v, name):
    r, c = w.shape
    tr = min(r, 128)
    n_parts = parts.shape[0]

    def body(p_ref, w_ref, m_ref, v_ref, g_ref, d_ref, nm_ref, nv_ref):
        g = p_ref[0].astype(F32)
        for k in range(1, n_parts):
            g = g + p_ref[k].astype(F32)
        g_ref[...] = g
        d_ref[...], nm_ref[...], nv_ref[...] = _adam_math(w_ref[...], g, m_ref[...], v_ref[...])

    tile = pl.BlockSpec((tr, c), lambda i: (i, 0))
    return pl.pallas_call(
        body, name=name, grid=(r // tr,),
        in_specs=[pl.BlockSpec((n_parts, tr, c), lambda i: (0, i, 0)), tile, tile, tile],
        out_specs=[tile] * 4, out_shape=[_sds((r, c), F32)] * 4,
        compiler_params=_params("parallel"),
    )(parts, w, m, v)


def _adam_plain(g, w, m, v, name):
    def body(g_ref, w_ref, m_ref, v_ref, d_ref, nm_ref, nv_ref):
        d_ref[...], nm_ref[...], nv_ref[...] = _adam_math(w_ref[...], g_ref[...], m_ref[...], v_ref[...])

    vmem = pl.BlockSpec(memory_space=pltpu.VMEM)
    return pl.pallas_call(
        body, name=name, in_specs=[vmem] * 4, out_specs=[vmem] * 3, out_shape=[_sds(w.shape, F32)] * 3,
    )(g, w, m, v)


_REPL = ("g_mix", "rnn_conv_b", "b_rg_a", "b_rg_x", "lru_lambda", "g_ffn", "g_ple", "g_final", "ffn_conv_b", "pad2",
         "w_rg_a", "w_rg_x")
REPL_ROWS = 272


def _pack_repl(t):
    rows = [t[n].reshape(1, D_MODEL) for n in _REPL[:8]]
    rows.append(t["ffn_conv_b"].reshape(6, D_MODEL))
    rows.append(t.get("pad2", jnp.zeros((2, D_MODEL), F32)))
    rows.append(t["w_rg_a"].reshape(128, D_MODEL))
    rows.append(t["w_rg_x"].reshape(128, D_MODEL))
    return jnp.concatenate(rows, axis=0)


def _unpack_repl(pack, shapes):
    out = {n: pack[i].reshape(shapes[n]) for i, n in enumerate(_REPL[:8])}
    out["ffn_conv_b"] = pack[8:14].reshape(shapes["ffn_conv_b"])
    out["w_rg_a"] = pack[16:144].reshape(shapes["w_rg_a"])
    out["w_rg_x"] = pack[144:272].reshape(shapes["w_rg_x"])
    return out


def _pack_conv_shard(rnn, sc, ffn):
    top = jnp.concatenate([rnn[:3], sc, ffn], axis=1)
    row3 = jnp.concatenate([rnn[3:4], jnp.zeros((1, D_MODEL - RNN_BW), F32)], axis=1)
    return jnp.concatenate([top, row3, jnp.zeros((4, D_MODEL), F32)], axis=0)


def _unpack_conv_shard(pack):
    rnn = jnp.concatenate([pack[:3, :RNN_BW], pack[3:4, :RNN_BW]], axis=0)
    return rnn, pack[:3, RNN_BW:2 * RNN_BW], pack[:3, 2 * RNN_BW:]


_SHARDED_BIG = ("w_in", "w_proj_a", "w_proj_b", "w_out", "w_up", "w_down", "w_ple_gate", "w_ple_proj")
_NAMES = ("g_mix", "w_in", "rnn_conv_w", "rnn_conv_b", "w_rg_a", "b_rg_a", "w_rg_x", "b_rg_x", "lru_lambda", "sc_conv_w",
          "w_proj_a", "w_proj_b", "w_out", "g_ffn", "w_up", "ffn_conv_w", "ffn_conv_b", "w_down", "w_ple_gate",
          "w_ple_proj", "g_ple", "g_final")


def _step(x, p, target, w, m, v):
    s = x.shape[0]
    tm = min(s, 256)
    tm_wide = min(s, 512)
    ch = min(s, 256)
    my_index = 4 * lax.axis_index("x") + 2 * lax.axis_index("y") + lax.axis_index("c")

    big = {n: w[n][0] for n in _SHARDED_BIG}
    conv_shard = _pack_conv_shard(w["rnn_conv_w"][0], w["sc_conv_w"][0], w["ffn_conv_w"][0])
    shards = {n: big[n].astype(BF16) for n in _SHARDED_BIG}
    w_in_g, conv_all = _comm_only(_gather_comm([shards["w_in"], conv_shard]), "in_proj_weight_gather")
    w_in = jnp.transpose(w_in_g, (1, 0, 2)).reshape(D_MODEL, N_SEG * D_MODEL)
    w4 = jnp.transpose(jnp.concatenate([conv_all[:, :3, :RNN_BW], conv_all[:, 3:4, :RNN_BW]], axis=1),
                       (1, 0, 2)).reshape(4, D_MODEL)
    w3 = jnp.transpose(conv_all[:, :3, RNN_BW:2 * RNN_BW], (1, 0, 2)).reshape(3, D_MODEL)
    wc = jnp.transpose(conv_all[:, :3, 2 * RNN_BW:], (1, 0, 2)).reshape(3, 2, D_FF).transpose(1, 0, 2)
    bc = w["ffn_conv_b"].reshape(2, 1, D_FF)
    b4, b_a, b_x, lam = w["rnn_conv_b"], w["b_rg_a"], w["b_rg_x"], w["lru_lambda"]
    w_a, w_x = w["w_rg_a"][0], w["w_rg_x"][0]
    g_final = w["g_final"].reshape(1, D_MODEL)

    tk = min(s, 512)
    received = {}

    comm = _gather_comm([shards[n] for n in ("w_proj_a", "w_proj_b", "w_out")])
    (z, h1), (w_pa, w_pb, w_out) = _norm_in_proj(x, w["g_mix"], w_in, tm_wide, comm)
    w_pa, w_pb, w_out = (a.reshape(D_MODEL, D_MODEL) for a in (w_pa, w_pb, w_out))
    comm = _gather_comm([shards[n] for n in ("w_up", "w_down", "w_ple_gate", "w_ple_proj")])
    (ya, yb, h), (w_up_g, w_down, w_gate, w_pp) = _mixers_fwd(z, w4, b4, w_a, b_a, w_x, b_x, lam, w3, ch, comm)
    w_down = w_down.reshape(D_FF, D_MODEL)
    w_gate = w_gate.reshape(D_MODEL, D_MODEL)
    w_pp = jnp.transpose(w_pp, (1, 0, 2)).reshape(D_PLE, D_MODEL)
    pa, pb, mm, x2, h2 = _merge_out(ya, yb, z, x, w_pa, w_pb, w_out, w["g_ffn"], tm)
    u0 = _up_proj(h2, w_up_g, tm_wide)
    f = _ffn_act(u0, wc, bc, 256, ch)
    x3, dgt, de0, dx3, df, head_sums = _head(x2, f, p, target, w_down, w_gate, w_pp, w["g_ple"], g_final, tm)
    parts = [_wgrad_2d(x3, dgt, "wgrad_ple_gate", tk).reshape(N_DEV, RNN_BW, D_MODEL),
             jnp.transpose(_wgrad_2d(p, de0, "wgrad_ple_proj", tk).reshape(D_PLE, N_DEV, RNN_BW), (1, 0, 2)),
             _wgrad_2d(f, dx3, "wgrad_down", tk).reshape(N_DEV, D_FF // N_DEV, D_MODEL)]
    (du0, dwc, dbc), got = _ffn_act_bwd(u0, df, wc, bc, 256, ch, _exchange_comm(parts))
    received.update(zip(("w_ple_gate", "w_ple_proj", "w_down"), got))
    dw_up = _wgrad_up(h2, du0, w_up_g.shape[2], tk)
    (dx2, dzg, dpa, dpb, dya, dyb, ffn_sums), _ = _up_bwd_merge_bwd(du0, w_up_g, x2, dx3, z, pa, pb, w_out, w_pa, w_pb,
                                                                     w["g_ffn"], tm, None)
    parts = [dw_up,
             _wgrad_2d(mm, dx2, "wgrad_out", tk).reshape(N_DEV, RNN_BW, D_MODEL),
             _wgrad_2d(ya, dpa, "wgrad_proj_a", tk).reshape(N_DEV, RNN_BW, D_MODEL),
             _wgrad_2d(yb, dpb, "wgrad_proj_b", tk).reshape(N_DEV, RNN_BW, D_MODEL)]
    (dz5, dw4, db4, dwa, dba, dwx, dbx, dlam, dw3), got = _mixers_bwd(z, h, dya, dyb, w4, b4, w_a, b_a, w_x, b_x, lam, w3,
                                                                       ch, _exchange_comm(parts))
    received.update(zip(("w_up", "w_out", "w_proj_a", "w_proj_b"), got))
    dw_in = jnp.concatenate([_wgrad_segments(h1, dz5, "wgrad_in_mix", 512, tk),
                             _wgrad_segments(h1, dzg, "wgrad_in_gate", D_MODEL, tk)], axis=1)
    bw_in = N_SEG * D_MODEL // N_DEV
    chip_sum = _pair_reduce(jnp.transpose(dw_in.reshape(D_MODEL, N_DEV, bw_in), (1, 0, 2)), "in_proj_grad_pair_reduce")
    (grad_x, mix_sums), got = _in_proj_bwd(dz5, dzg, w_in, x, dx2, w["g_mix"], tm, _chip_exchange_comm([chip_sum]))
    received["w_in"] = got[0]

    small = {"g_mix": mix_sums[0], "rnn_conv_b": db4, "b_rg_a": dba, "b_rg_x": dbx, "lru_lambda": dlam,
             "g_ffn": ffn_sums[0], "g_ple": head_sums[2], "g_final": head_sums[1], "ffn_conv_b": dbc,
             "pad2": jnp.concatenate([head_sums[0:1], jnp.zeros((1, D_MODEL), F32)], axis=0),
             "w_rg_a": dwa, "w_rg_x": dwx}
    conv_rows = jnp.concatenate([dw4, dw3, jnp.zeros((1, D_MODEL), F32),
                                 jnp.transpose(dwc, (1, 0, 2)).reshape(18, D_MODEL),
                                 jnp.zeros((PACK_ROWS - REPL_ROWS - 26, D_MODEL), F32)], axis=0)
    total = _small_all_reduce(jnp.concatenate([_pack_repl(small), conv_rows], axis=0))
    loss = jnp.sum(total[14]) * (0.5 / D_MODEL)

    out = {}
    for n in _SHARDED_BIG:
        shard = big[n]
        res = _adam_shard(received[n], shard, m[n][0], v[n][0], "adam_" + n)
        out[n] = [r[None] for r in res]
    shapes = {n: w[n].shape for n in _REPL if n != "pad2"}
    g_repl = total[:REPL_ROWS]
    res = _adam_plain(g_repl, _pack_repl(w), _pack_repl(m), _pack_repl(v), "adam_replicated")
    unpacked = [_unpack_repl(r, shapes) for r in (g_repl,) + tuple(res)]
    for n in shapes:
        out[n] = [u[n] for u in unpacked]
    g_rnn = lax.dynamic_slice(total[272:276], (0, my_index * RNN_BW), (4, RNN_BW))
    g_sc = lax.dynamic_slice(total[276:279], (0, my_index * RNN_BW), (3, RNN_BW))
    bw_ffn = 2 * D_FF // N_DEV
    g_ffn_conv = lax.dynamic_slice(total[280:298].reshape(3, 2 * D_FF), (0, my_index * bw_ffn), (3, bw_ffn))
    g_conv = _pack_conv_shard(g_rnn, g_sc, g_ffn_conv)
    res = _adam_plain(g_conv, conv_shard,
                      _pack_conv_shard(m["rnn_conv_w"][0], m["sc_conv_w"][0], m["ffn_conv_w"][0]),
                      _pack_conv_shard(v["rnn_conv_w"][0], v["sc_conv_w"][0], v["ffn_conv_w"][0]), "adam_conv")
    unpacked = [_unpack_conv_shard(r) for r in (g_conv,) + tuple(res)]
    for i, n in enumerate(("rnn_conv_w", "sc_conv_w", "ffn_conv_w")):
        out[n] = [u[i][None] for u in unpacked]

    return (loss, grad_x[None]) + tuple(out[n][k] for k in range(4) for n in _NAMES)


def kernel(x, p, g_mix, w_in, rnn_conv_w, rnn_conv_b, w_rg_a, b_rg_a, w_rg_x, b_rg_x, lru_lambda, sc_conv_w, w_proj_a, w_proj_b, w_out, g_ffn, w_up, ffn_conv_w, ffn_conv_b, w_down, w_ple_gate, w_ple_proj, g_ple, g_final, loss_target, m_g_mix, m_w_in, m_rnn_conv_w, m_rnn_conv_b, m_w_rg_a, m_b_rg_a, m_w_rg_x, m_b_rg_x, m_lru_lambda, m_sc_conv_w, m_w_proj_a, m_w_proj_b, m_w_out, m_g_ffn, m_w_up, m_ffn_conv_w, m_ffn_conv_b, m_w_down, m_w_ple_gate, m_w_ple_proj, m_g_ple, m_g_final, v_g_mix, v_w_in, v_rnn_conv_w, v_rnn_conv_b, v_w_rg_a, v_b_rg_a, v_w_rg_x, v_b_rg_x, v_lru_lambda, v_sc_conv_w, v_w_proj_a, v_w_proj_b, v_w_out, v_g_ffn, v_w_up, v_ffn_conv_w, v_ffn_conv_b, v_w_down, v_w_ple_gate, v_w_ple_proj, v_g_ple, v_g_final):
    given = dict(locals())
    w = {n: given[n] for n in _NAMES}
    m = {n: given["m_" + n] for n in _NAMES}
    v = {n: given["v_" + n] for n in _NAMES}
    return _step(x[0], p[0, 0], loss_target[0], w, m, v)
```

```python
import functools

import jax
import jax.numpy as jnp
from jax import lax
from jax.experimental import pallas as pl
from jax.experimental.pallas import tpu as pltpu

F32 = jnp.float32
BF16 = jnp.bfloat16
MESH_ID = pl.DeviceIdType.MESH

N_DEV = 8
D_MODEL = 1024
D_PLE = 256
RNN_BW = 128
N_SEG = 7
D_FF = 3072
LRU_C = 8.0
EPS = 1e-6
ADAM_LR = 0.001
ADAM_B1 = 0.9
ADAM_B2 = 0.999
ADAM_EPS = 1e-08
ADAM_WD = 0.01
ADAM_STEP = 10

HALO = 16
SCAN_UNROLL = 4
VMEM_LIMIT = 56 * 1024 * 1024


def _dot(a, b):
    return jnp.dot(a, b, preferred_element_type=F32)


def _dot_nt(a, b):
    return lax.dot_general(a, b, (((1,), (1,)), ((), ())), preferred_element_type=F32)


def _dot_tn(a, b):
    return lax.dot_general(a, b, (((0,), (0,)), ((), ())), preferred_element_type=F32)


def _sigmoid(x):
    return jax.nn.sigmoid(x)


_GELU_C = 0.7978845608028654
_GELU_K = 0.044715


def _gelu(x):
    return 0.5 * x * (1.0 + jnp.tanh(_GELU_C * (x + _GELU_K * (x * x * x))))


def _gelu_and_grad(x):
    x2 = x * x
    t = jnp.tanh(_GELU_C * (x + _GELU_K * (x * x2)))
    g = 0.5 * x * (1.0 + t)
    dg = 0.5 * (1.0 + t) + 0.5 * x * (1.0 - t * t) * (_GELU_C * (1.0 + 3.0 * _GELU_K * x2))
    return g, dg


def _expm1(x):
    poly = x * (1.0 + x * (0.5 + x * (1.0 / 6.0 + x * (1.0 / 24.0 + x * (1.0 / 120.0)))))
    return jnp.where(jnp.abs(x) < 0.08, poly, jnp.exp(x) - 1.0)


def _softplus(x):
    return jnp.maximum(x, 0.0) + jnp.log1p(jnp.exp(-jnp.abs(x)))


def _rms(u):
    r = lax.rsqrt(jnp.mean(u * u, axis=-1, keepdims=True) + EPS)
    return u * r, r


def _rms_bwd(dn, un, r):
    return r * (dn - un * jnp.mean(dn * un, axis=-1, keepdims=True))


def _shift_dn(ext, s, n):
    if s == 0:
        return ext[HALO:HALO + n]
    return pltpu.roll(ext, s, 0)[HALO:HALO + n]


def _shift_up(ext, s, n):
    if s == 0:
        return ext[0:n]
    return pltpu.roll(ext, n + HALO - s, 0)[0:n]


def _ext_before(ref, c, ch):
    t0 = c * ch
    prev = pl.multiple_of(jnp.maximum(t0 - HALO, 0), HALO)
    halo = jnp.where(c > 0, ref[pl.ds(prev, HALO), :].astype(F32), 0.0)
    cur = ref[pl.ds(pl.multiple_of(t0, HALO), ch), :].astype(F32)
    return jnp.concatenate([halo, cur], axis=0)


def _ext_after(ref, c, ch, n_chunks):
    t0 = c * ch
    nxt = pl.multiple_of(jnp.minimum(t0 + ch, (n_chunks - 1) * ch + ch - HALO), HALO)
    halo = jnp.where(c < n_chunks - 1, ref[pl.ds(nxt, HALO), :].astype(F32), 0.0)
    cur = ref[pl.ds(pl.multiple_of(t0, HALO), ch), :].astype(F32)
    return jnp.concatenate([cur, halo], axis=0)


def _rows(ref, c, ch):
    return ref[pl.ds(pl.multiple_of(c * ch, HALO), ch), :]


def _scan_fwd(a_ref, b_ref, h_ref, n_rows, cw):
    rows = lax.broadcasted_iota(jnp.int32, (8, cw), 0)

    def body(i, carry):
        for u in range(SCAN_UNROLL):
            t = pl.multiple_of((i * SCAN_UNROLL + u) * 8, 8)
            a = a_ref[pl.ds(t, 8), :]
            b = b_ref[pl.ds(t, 8), :]
            for d in (1, 2, 4):
                m = rows >= d
                b = jnp.where(m, a * pltpu.roll(b, d, 0) + b, b)
                a = jnp.where(m, a * pltpu.roll(a, d, 0), a)
            h_ref[pl.ds(t, 8), :] = a * carry + b
            carry = jnp.broadcast_to(a[7:8, :], (8, cw)) * carry + jnp.broadcast_to(b[7:8, :], (8, cw))
        return carry

    lax.fori_loop(0, n_rows // (8 * SCAN_UNROLL), body, jnp.zeros((8, cw), F32))


def _scan_rev(a_ref, g_ref, n_rows, cw):
    rows = lax.broadcasted_iota(jnp.int32, (8, cw), 0)
    n_groups = n_rows // 8

    def body(i, carry):
        dh_next, a_next = carry
        for u in range(SCAN_UNROLL):
            t = pl.multiple_of((n_groups - 1 - (i * SCAN_UNROLL + u)) * 8, 8)
            a = a_ref[pl.ds(t, 8), :]
            g = g_ref[pl.ds(t, 8), :]
            an = jnp.where(rows < 7, pltpu.roll(a, 7, 0), a_next)
            a_next = jnp.broadcast_to(a[0:1, :], (8, cw))
            for d in (1, 2, 4):
                m = rows < 8 - d
                g = jnp.where(m, an * pltpu.roll(g, 8 - d, 0) + g, g)
                an = jnp.where(m, an * pltpu.roll(an, 8 - d, 0), an)
            g_ref[pl.ds(t, 8), :] = an * dh_next + g
            dh_next = jnp.broadcast_to(an[0:1, :], (8, cw)) * dh_next + jnp.broadcast_to(g[0:1, :], (8, cw))
        return dh_next, a_next

    zero = jnp.zeros((8, cw), F32)
    lax.fori_loop(0, n_groups // SCAN_UNROLL, body, (zero, zero))


def _params(*sem):
    return pltpu.CompilerParams(dimension_semantics=sem, vmem_limit_bytes=VMEM_LIMIT)


def _sds(shape, dtype):
    return jax.ShapeDtypeStruct(shape, dtype)


def _resident(shape):
    zeros = (0,) * len(shape)
    return pl.BlockSpec(shape, lambda *_: zeros, pipeline_mode=pl.Buffered(1))


class _Comm:
    def __init__(self, inputs, out_shape, scratch, start, finish):
        self.inputs, self.out_shape, self.scratch, self.start, self.finish = inputs, out_shape, scratch, start, finish


def _join_comms(a, b):
    n, k = len(a.inputs), len(a.scratch)

    def start(ins, outs, sems):
        a.start(ins[:n], outs[:n], sems[:k])
        b.start(ins[n:], outs[n:], sems[k:])

    def finish(ins, outs, sems):
        a.finish(ins[:n], outs[:n], sems[:k])
        b.finish(ins[n:], outs[n:], sems[k:])

    return _Comm(a.inputs + b.inputs, a.out_shape + b.out_shape, a.scratch + b.scratch, start, finish)


def _sem_scratch(n, copies=7):
    return [pltpu.SemaphoreType.DMA((n, copies)), pltpu.SemaphoreType.DMA((n, copies)), pltpu.SemaphoreType.DMA((n,))]


def _gather_comm(shards):
    n = len(shards)

    def plan(ins, outs, sems):
        send_sems, recv_sems, local_sems = sems
        x, y, c = lax.axis_index("x"), lax.axis_index("y"), lax.axis_index("c")
        me, sibling = (x, y, c), (x, y, 1 - c)
        chips = [(1 - x, y), (x, 1 - y), (1 - x, 1 - y)]

        def slot(t, dev):
            return outs[t].at[4 * dev[0] + 2 * dev[1] + dev[2]]

        def copy(t, k, block, to, src=None):
            return pltpu.make_async_remote_copy(
                src_ref=slot(t, block) if src is None else src, dst_ref=slot(t, block),
                send_sem=send_sems.at[t, k], recv_sem=recv_sems.at[t, k],
                device_id=to, device_id_type=MESH_ID)

        mine = [pltpu.make_async_copy(ins[t], slot(t, me), local_sems.at[t]) for t in range(n)]
        first = []
        for t in range(n):
            first.append(copy(t, 0, me, sibling, src=ins[t]))
            first += [copy(t, 1 + j, me, (*chip, c), src=ins[t]) for j, chip in enumerate(chips)]
        return me, sibling, chips, c, copy, mine, first

    def start(ins, outs, sems):
        *_, mine, first = plan(ins, outs, sems)
        for cp in mine + first:
            cp.start()

    def finish(ins, outs, sems):
        me, sibling, chips, c, copy, mine, first = plan(ins, outs, sems)
        passed = []
        for t in range(n):
            for j, chip in enumerate(chips):
                copy(t, 1 + j, (*chip, c), me).wait_recv()
                cp = copy(t, 4 + j, (*chip, c), sibling)
                cp.start()
                passed.append(cp)
        for t in range(n):
            copy(t, 0, sibling, me).wait_recv()
            for j, chip in enumerate(chips):
                copy(t, 4 + j, (*chip, 1 - c), me).wait_recv()
        for cp in first + passed:
            cp.wait_send()
        for cp in mine:
            cp.wait()

    return _Comm(list(shards), [_sds((N_DEV,) + s.shape, s.dtype) for s in shards], _sem_scratch(n), start, finish)


def _peer(d):
    x, y, c = lax.axis_index("x"), lax.axis_index("y"), lax.axis_index("c")
    px = 1 - x if d & 4 else x
    py = 1 - y if d & 2 else y
    pc = 1 - c if d & 1 else c
    return (px, py, pc), 4 * px + 2 * py + pc


def _exchange_comm(parts, deltas=tuple(range(1, N_DEV))):
    n = len(parts)

    def plan(ins, outs, sems):
        send_sems, recv_sems, local_sems = sems
        _, me = _peer(0)

        def copy(t, i):
            peer, idx = _peer(deltas[i])
            return pltpu.make_async_remote_copy(
                src_ref=ins[t].at[idx], dst_ref=outs[t].at[1 + i],
                send_sem=send_sems.at[t, i], recv_sem=recv_sems.at[t, i],
                device_id=peer, device_id_type=MESH_ID)

        mine = [pltpu.make_async_copy(ins[t].at[me], outs[t].at[0], local_sems.at[t]) for t in range(n)]
        return mine, [copy(t, i) for t in range(n) for i in range(len(deltas))]

    def start(ins, outs, sems):
        mine, copies = plan(ins, outs, sems)
        for cp in mine + copies:
            cp.start()

    def finish(ins, outs, sems):
        mine, copies = plan(ins, outs, sems)
        for cp in copies:
            cp.wait_recv()
        for cp in copies:
            cp.wait_send()
        for cp in mine:
            cp.wait()

    return _Comm(list(parts), [_sds((1 + len(deltas),) + p.shape[1:], p.dtype) for p in parts],
                 _sem_scratch(n, len(deltas)), start, finish)


def _chip_exchange_comm(parts):
    n = len(parts)

    def plan(ins, outs, sems):
        send_sems, recv_sems, local_sems = sems
        x, y, c = lax.axis_index("x"), lax.axis_index("y"), lax.axis_index("c")

        def copy(t, dq):
            px = 1 - x if dq & 2 else x
            py = 1 - y if dq & 1 else y
            return pltpu.make_async_remote_copy(
                src_ref=ins[t].at[2 * px + py], dst_ref=outs[t].at[dq],
                send_sem=send_sems.at[t, dq - 1], recv_sem=recv_sems.at[t, dq - 1],
                device_id=(px, py, c), device_id_type=MESH_ID)

        mine = [pltpu.make_async_copy(ins[t].at[2 * x + y], outs[t].at[0], local_sems.at[t]) for t in range(n)]
        return mine, [copy(t, dq) for t in range(n) for dq in range(1, 4)]

    def start(ins, outs, sems):
        mine, copies = plan(ins, outs, sems)
        for cp in mine + copies:
            cp.start()

    def finish(ins, outs, sems):
        mine, copies = plan(ins, outs, sems)
        for cp in copies:
            cp.wait_recv()
        for cp in copies:
            cp.wait_send()
        for cp in mine:
            cp.wait()

    return _Comm(list(parts), [_sds(p.shape, p.dtype) for p in parts], _sem_scratch(n, 3), start, finish)


def _pair_reduce(part, name):
    _, r, c = part.shape
    n_chips = N_DEV // 2
    tr = 128

    def body(p_ref, o_ref, own, land, send_sems, recv_sems, local_sems):
        x, y, core = lax.axis_index("x"), lax.axis_index("y"), lax.axis_index("c")
        sends = [pltpu.make_async_remote_copy(
            src_ref=p_ref.at[2 * q + (1 - core)], dst_ref=land.at[q], send_sem=send_sems.at[q], recv_sem=recv_sems.at[q],
            device_id=(x, y, 1 - core), device_id_type=MESH_ID) for q in range(n_chips)]
        loads = [pltpu.make_async_copy(p_ref.at[2 * q + core], own.at[q], local_sems.at[q]) for q in range(n_chips)]
        for cp in sends + loads:
            cp.start()
        for q in range(n_chips):
            loads[q].wait()
            sends[q].wait_recv()

            def add_rows(i, _):
                rows = pl.ds(pl.multiple_of(i * tr, tr), tr)
                o_ref[q, rows, :] = (own[q, rows, :].astype(F32) + land[q, rows, :].astype(F32)).astype(BF16)
                return 0

            lax.fori_loop(0, r // tr, add_rows, 0)
        for cp in sends:
            cp.wait_send()

    return pl.pallas_call(
        body, name=name, out_shape=_sds((n_chips, r, c), BF16),
        in_specs=[pl.BlockSpec(memory_space=pl.ANY)], out_specs=pl.BlockSpec(memory_space=pltpu.VMEM),
        scratch_shapes=[pltpu.VMEM((n_chips, r, c), BF16), pltpu.VMEM((n_chips, r, c), BF16)]
        + [pltpu.SemaphoreType.DMA((n_chips,))] * 3,
        compiler_params=pltpu.CompilerParams(vmem_limit_bytes=VMEM_LIMIT),
    )(part)


def _comm_only(comm, name):
    n = len(comm.inputs)

    def body(*refs):
        ins, outs, sems = refs[:n], refs[n:2 * n], refs[2 * n:]
        comm.start(ins, outs, sems)
        comm.finish(ins, outs, sems)

    any_spec = pl.BlockSpec(memory_space=pl.ANY)
    return pl.pallas_call(body, name=name, out_shape=comm.out_shape, in_specs=[any_spec] * n, out_specs=[any_spec] * n,
                          scratch_shapes=comm.scratch)(*comm.inputs)


def _host_call(body, *, name, grid, in_specs, out_specs, out_shape, scratch_shapes=(), semantics, args, comm=None):
    if comm is None:
        res = pl.pallas_call(body, name=name, grid=grid, in_specs=in_specs, out_specs=out_specs, out_shape=out_shape,
                             scratch_shapes=list(scratch_shapes), compiler_params=_params(*semantics))(*args)
        return res, []
    n_in, n_out, n_scr, n_c = len(in_specs), len(out_specs), len(scratch_shapes), len(comm.inputs)

    def with_comm(*refs):
        ins, refs = refs[:n_in], refs[n_in:]
        c_ins, refs = refs[:n_c], refs[n_c:]
        outs, refs = refs[:n_out], refs[n_out:]
        c_outs, refs = refs[:n_c], refs[n_c:]
        scr, sems = refs[:n_scr], refs[n_scr:]
        ids = [pl.program_id(a) for a in range(len(grid))]
        first = functools.reduce(jnp.logical_and, [i == 0 for i in ids])
        last = functools.reduce(jnp.logical_and, [i == g - 1 for i, g in zip(ids, grid)])

        @pl.when(first)
        def _():
            comm.start(c_ins, c_outs, sems)

        body(*ins, *outs, *scr)

        @pl.when(last)
        def _():
            comm.finish(c_ins, c_outs, sems)

    any_spec = pl.BlockSpec(memory_space=pl.ANY)
    res = pl.pallas_call(
        with_comm, name=name, grid=grid, in_specs=list(in_specs) + [any_spec] * n_c,
        out_specs=list(out_specs) + [any_spec] * n_c, out_shape=list(out_shape) + comm.out_shape,
        scratch_shapes=list(scratch_shapes) + comm.scratch,
        compiler_params=_params(*(["arbitrary"] * len(grid))))(*args, *comm.inputs)
    return res[:n_out], res[n_out:]


def _small_all_reduce(packs):
    n = len(packs)
    pieces = [p.shape[0] // N_DEV for p in packs]

    def body(*refs):
        p_refs, o_refs, lands = refs[:n], refs[n:2 * n], refs[2 * n:3 * n]
        s1, r1, s2, r2 = refs[3 * n:]
        _, me = _peer(0)

        def piece(t, ref, idx):
            return ref.at[pl.ds(pl.multiple_of(idx * pieces[t], 8), pieces[t]), :]

        def scatter(t, d):
            peer, idx = _peer(d)
            return pltpu.make_async_remote_copy(
                src_ref=piece(t, p_refs[t], idx), dst_ref=lands[t].at[d - 1], send_sem=s1.at[t, d - 1],
                recv_sem=r1.at[t, d - 1], device_id=peer, device_id_type=MESH_ID)

        def gather(t, d, from_idx):
            peer, _ = _peer(d)
            return pltpu.make_async_remote_copy(
                src_ref=piece(t, o_refs[t], from_idx), dst_ref=piece(t, o_refs[t], from_idx), send_sem=s2.at[t, d - 1],
                recv_sem=r2.at[t, d - 1], device_id=peer, device_id_type=MESH_ID)

        first = [[scatter(t, d) for d in range(1, N_DEV)] for t in range(n)]
        for cp in sum(first, []):
            cp.start()
        second = []
        for t in range(n):
            acc = piece(t, p_refs[t], me)[...]
            for d in range(1, N_DEV):
                first[t][d - 1].wait_recv()
                acc = acc + lands[t][d - 1]
            piece(t, o_refs[t], me)[...] = acc
            second += [gather(t, d, me) for d in range(1, N_DEV)]
            for cp in second[-(N_DEV - 1):]:
                cp.start()
        for t in range(n):
            for d in range(1, N_DEV):
                gather(t, d, _peer(d)[1]).wait_recv()
        for cp in sum(first, []) + second:
            cp.wait_send()

    vmem = pl.BlockSpec(memory_space=pltpu.VMEM)
    return pl.pallas_call(
        body, name="small_all_reduce",
        out_shape=[_sds(p.shape, F32) for p in packs], in_specs=[vmem] * n, out_specs=[vmem] * n,
        scratch_shapes=[pltpu.VMEM((N_DEV - 1, pc, p.shape[1]), F32) for pc, p in zip(pieces, packs)]
        + [pltpu.SemaphoreType.DMA((n, N_DEV - 1))] * 4,
    )(*packs)


def _norm_in_proj(x, g_mix, w_in, tm, comm):
    s = x.shape[0]

    def body(x_ref, g_ref, w_ref, z_ref, h_ref):
        xn, _ = _rms(x_ref[...])
        h = (xn * g_ref[...]).astype(BF16)
        h_ref[...] = h
        for j in range(N_SEG):
            cols = slice(j * D_MODEL, (j + 1) * D_MODEL)
            z_ref[:, cols] = _dot(h, w_ref[:, cols]).astype(BF16)

    return _host_call(
        body, name="norm_in_proj", grid=(s // tm,),
        in_specs=[pl.BlockSpec((tm, D_MODEL), lambda i: (i, 0)), pl.BlockSpec((1, D_MODEL), lambda i: (0, 0)),
                  _resident(w_in.shape)],
        out_specs=[pl.BlockSpec((tm, N_SEG * D_MODEL), lambda i: (i, 0)), pl.BlockSpec((tm, D_MODEL), lambda i: (i, 0))],
        out_shape=[_sds((s, N_SEG * D_MODEL), BF16), _sds((s, D_MODEL), BF16)],
        semantics=("parallel",), args=(x, g_mix, w_in), comm=comm)


def _gate_chunk(xr_ext, w4_ref, cb_ref, wa, wx, ba_ref, bx_ref, sp, ch):
    xc = cb_ref[...] + w4_ref[3:4, :] * _shift_dn(xr_ext, 0, ch)
    for k in range(3):
        xc = xc + w4_ref[k:k + 1, :] * _shift_dn(xr_ext, 3 - k, ch)
    xcb = xc.astype(BF16)
    ra = _sigmoid(_dot(xcb, wa) + ba_ref[...])
    ia = _sigmoid(_dot(xcb, wx) + bx_ref[...])
    log_a = (-LRU_C) * ra * sp
    return xc, ra, ia, log_a


def _conv3(q_ext, w3_ref, ch):
    y = w3_ref[2:3, :] * _shift_dn(q_ext, 0, ch)
    for k in range(2):
        y = y + w3_ref[k:k + 1, :] * _shift_dn(q_ext, 2 - k, ch)
    return y


def _z_block_specs(s, cw, segs):
    nb = D_MODEL // cw
    return [pl.BlockSpec((s, cw), functools.partial(lambda j, seg: (0, seg * nb + j), seg=seg)) for seg in segs]


def _mixers_fwd(z, w4, b4, w_a, b_a, w_x, b_x, lam, w3, ch, comm):
    s = z.shape[0]
    cw = RNN_BW
    n_chunks = s // ch

    def body(xr_ref, gr_ref, cb_ref, cc_ref, cx_ref, w4_ref, b4_ref, wa_ref, ba_ref, wx_ref, bx_ref, lam_ref, w3_ref,
             ya_ref, yb_ref, h_ref, a_scr, b_scr):
        sp = _softplus(-lam_ref[...])
        wa = wa_ref[0].astype(BF16)
        wx = wx_ref[0].astype(BF16)

        def gates(c, _):
            xc, _, ia, log_a = _gate_chunk(_ext_before(xr_ref, c, ch), w4_ref, b4_ref, wa, wx, ba_ref, bx_ref, sp, ch)
            rows = pl.ds(pl.multiple_of(c * ch, HALO), ch)
            a_scr[rows, :] = jnp.exp(log_a)
            b_scr[rows, :] = jnp.sqrt(-_expm1(2.0 * log_a)) * (ia * xc)
            q_ext = _ext_before(cc_ref, c, ch) * _ext_before(cx_ref, c, ch)
            yb_ref[rows, :] = (_rows(cb_ref, c, ch).astype(F32) * _conv3(q_ext, w3_ref, ch)).astype(BF16)
            return 0

        lax.fori_loop(0, n_chunks, gates, 0)
        _scan_fwd(a_scr, b_scr, h_ref, s, cw)

        def outputs(c, _):
            rows = pl.ds(pl.multiple_of(c * ch, HALO), ch)
            ya_ref[rows, :] = (h_ref[rows, :] * _gelu(gr_ref[rows, :].astype(F32))).astype(BF16)
            return 0

        lax.fori_loop(0, n_chunks, outputs, 0)

    col = lambda j: (0, j)
    vec = pl.BlockSpec((1, cw), col)
    sq = pl.BlockSpec((1, cw, cw), lambda j: (j, 0, 0))
    act = pl.BlockSpec((s, cw), col)
    return _host_call(
        body, name="mixers_fwd", grid=(D_MODEL // cw,),
        in_specs=_z_block_specs(s, cw, range(5)) + [pl.BlockSpec((4, cw), col), vec, sq, vec, sq, vec, vec,
                                                     pl.BlockSpec((3, cw), col)],
        out_specs=[act, act, act],
        out_shape=[_sds((s, D_MODEL), BF16), _sds((s, D_MODEL), BF16), _sds((s, D_MODEL), F32)],
        scratch_shapes=[pltpu.VMEM((s, cw), F32), pltpu.VMEM((s, cw), F32)],
        semantics=("parallel",), args=(z, z, z, z, z, w4, b4, w_a, b_a, w_x, b_x, lam, w3), comm=comm)


def _merge_out(ya, yb, z, x, w_pa, w_pb, w_out, g_ffn, tm):
    s = x.shape[0]

    def body(ya_ref, yb_ref, ga_ref, gb_ref, x_ref, wa_ref, wb_ref, wo_ref, g_ref, pa_ref, pb_ref, m_ref, x2_ref, h2_ref):
        pa = _dot(ya_ref[...], wa_ref[...])
        pb = _dot(yb_ref[...], wb_ref[...])
        pa_ref[...] = pa.astype(BF16)
        pb_ref[...] = pb.astype(BF16)
        m = (_sigmoid(ga_ref[...].astype(F32)) * pa + _sigmoid(gb_ref[...].astype(F32)) * pb).astype(BF16)
        m_ref[...] = m
        x2 = x_ref[...] + _dot(m, wo_ref[...])
        x2_ref[...] = x2
        xn, _ = _rms(x2)
        h2_ref[...] = (xn * g_ref[...]).astype(BF16)

    tile = pl.BlockSpec((tm, D_MODEL), lambda i: (i, 0))
    full = _resident((D_MODEL, D_MODEL))
    return pl.pallas_call(
        body, name="merge_out", grid=(s // tm,),
        in_specs=[tile, tile, pl.BlockSpec((tm, D_MODEL), lambda i: (i, 5)), pl.BlockSpec((tm, D_MODEL), lambda i: (i, 6)),
                  tile, full, full, full, pl.BlockSpec((1, D_MODEL), lambda i: (0, 0))],
        out_specs=[tile] * 5,
        out_shape=[_sds((s, D_MODEL), BF16)] * 3 + [_sds((s, D_MODEL), F32), _sds((s, D_MODEL), BF16)],
        compiler_params=_params("parallel"),
    )(ya, yb, z, z, x, w_pa, w_pb, w_out, g_ffn)


def _up_proj(h2, w_up_g, tm):
    s = h2.shape[0]
    bw = w_up_g.shape[2]
    per_half = N_DEV // 2

    def body(h_ref, w_ref, u_ref):
        h = h_ref[...]
        for j in range(N_DEV):
            cols = slice((j % per_half) * bw, (j % per_half + 1) * bw)
            u_ref[j // per_half, :, cols] = _dot(h, w_ref[j]).astype(BF16)

    return pl.pallas_call(
        body, name="up_proj", grid=(s // tm,),
        in_specs=[pl.BlockSpec((tm, D_MODEL), lambda i: (i, 0)), _resident(w_up_g.shape)],
        out_specs=pl.BlockSpec((2, tm, D_FF), lambda i: (0, i, 0)),
        out_shape=_sds((2, s, D_FF), BF16),
        compiler_params=_params("parallel"),
    )(h2, w_up_g)


def _ffn_conv(u_ref, w_ref, b_ref, half, c, ch):
    ext = _ext_before(u_ref.at[half], c, ch)
    w = w_ref.at[half]
    return b_ref[half] + _conv3(ext, w, ch), ext


def _ffn_act(u0, wc, bc, cw, ch):
    s = u0.shape[1]
    n_chunks = s // ch

    def body(u_ref, w_ref, b_ref, f_ref):
        def chunk(c, _):
            ug, _ = _ffn_conv(u_ref, w_ref, b_ref, 0, c, ch)
            uv, _ = _ffn_conv(u_ref, w_ref, b_ref, 1, c, ch)
            f_ref[pl.ds(pl.multiple_of(c * ch, HALO), ch), :] = (_gelu(ug) * uv).astype(BF16)
            return 0

        lax.fori_loop(0, n_chunks, chunk, 0)

    return pl.pallas_call(
        body, name="ffn_act", grid=(D_FF // cw,),
        in_specs=[pl.BlockSpec((2, s, cw), lambda j: (0, 0, j)), pl.BlockSpec((2, 3, cw), lambda j: (0, 0, j)),
                  pl.BlockSpec((2, 1, cw), lambda j: (0, 0, j))],
        out_specs=pl.BlockSpec((s, cw), lambda j: (0, j)),
        out_shape=_sds((s, D_FF), BF16),
        compiler_params=_params("parallel"),
    )(u0, wc, bc)


def _head(x2, f, p, target, w_down, w_gate, w_pp, g_ple, g_final, tm):
    s = x2.shape[0]

    def body(x2_ref, f_ref, p_ref, t_ref, wd_ref, wg_ref, wp_ref, gp_ref, gf_ref,
             x3_ref, dgt_ref, de0_ref, dx3_ref, df_ref, sums_ref):
        @pl.when(pl.program_id(0) == 0)
        def _():
            sums_ref[...] = jnp.zeros_like(sums_ref)

        x3 = x2_ref[...] + _dot(f_ref[...], wd_ref[...])
        x3b = x3.astype(BF16)
        x3_ref[...] = x3b
        e0n, re = _rms(_dot(p_ref[...].astype(BF16), wp_ref[...]))
        e = e0n * gp_ref[...]
        sg = _sigmoid(_dot(x3b, wg_ref[...]))
        x4n, r4 = _rms(x3 + sg * e)
        diff = x4n * gf_ref[...] - t_ref[...]
        sums_ref[0:1, :] += jnp.sum(diff * diff, axis=0, keepdims=True)
        dy = diff * (1.0 / D_MODEL)
        sums_ref[1:2, :] += jnp.sum(dy * x4n, axis=0, keepdims=True)
        dx4 = _rms_bwd(dy * gf_ref[...], x4n, r4)
        de = dx4 * sg
        dgt = ((dx4 * e) * (sg * (1.0 - sg))).astype(BF16)
        dgt_ref[...] = dgt
        sums_ref[2:3, :] += jnp.sum(de * e0n, axis=0, keepdims=True)
        de0_ref[...] = _rms_bwd(de * gp_ref[...], e0n, re).astype(BF16)
        dx3 = dx4 + _dot_nt(dgt, wg_ref[...])
        dx3_ref[...] = dx3
        df_ref[...] = _dot_nt(dx3.astype(BF16), wd_ref[...]).astype(BF16)

    tile = pl.BlockSpec((tm, D_MODEL), lambda i: (i, 0))
    vec = pl.BlockSpec((1, D_MODEL), lambda i: (0, 0))
    wide = pl.BlockSpec((tm, D_FF), lambda i: (i, 0))
    return pl.pallas_call(
        body, name="loss_head", grid=(s // tm,),
        in_specs=[tile, wide, pl.BlockSpec((tm, D_PLE), lambda i: (i, 0)), tile,
                  _resident((D_FF, D_MODEL)), _resident((D_MODEL, D_MODEL)), _resident((D_PLE, D_MODEL)), vec, vec],
        out_specs=[tile, tile, tile, tile, wide, pl.BlockSpec((8, D_MODEL), lambda i: (0, 0))],
        out_shape=[_sds((s, D_MODEL), BF16)] * 3 + [_sds((s, D_MODEL), F32), _sds((s, D_FF), BF16),
                                                    _sds((8, D_MODEL), F32)],
        compiler_params=_params("arbitrary"),
    )(x2, f, p, target, w_down, w_gate, w_pp, g_ple, g_final)


def _ffn_act_bwd(u0, df, wc, bc, cw, ch, comm):
    s = u0.shape[1]
    n_chunks = s // ch

    def body(u_ref, df_ref, w_ref, b_ref, du0_ref, dw_ref, db_ref, du_scr):
        dw_ref[...] = jnp.zeros_like(dw_ref)
        db_ref[...] = jnp.zeros_like(db_ref)

        def first(c, _):
            ug, ext_g = _ffn_conv(u_ref, w_ref, b_ref, 0, c, ch)
            uv, ext_v = _ffn_conv(u_ref, w_ref, b_ref, 1, c, ch)
            gel, dgel = _gelu_and_grad(ug)
            d = _rows(df_ref, c, ch).astype(F32)
            rows = pl.ds(pl.multiple_of(c * ch, HALO), ch)
            for half, du, ext in ((0, d * uv * dgel, ext_g), (1, d * gel, ext_v)):
                du_scr[half, rows, :] = du
                db_ref[half] += jnp.sum(du, axis=0, keepdims=True)
                for k in range(3):
                    dw_ref[half, k:k + 1, :] += jnp.sum(du * _shift_dn(ext, 2 - k, ch), axis=0, keepdims=True)
            return 0

        lax.fori_loop(0, n_chunks, first, 0)

        def second(c, _):
            rows = pl.ds(pl.multiple_of(c * ch, HALO), ch)
            for half in range(2):
                ext = _ext_after(du_scr.at[half], c, ch, n_chunks)
                w = w_ref.at[half]
                acc = w[2:3, :] * _shift_up(ext, 0, ch)
                for k in range(2):
                    acc = acc + w[k:k + 1, :] * _shift_up(ext, 2 - k, ch)
                du0_ref[half, rows, :] = acc.astype(BF16)
            return 0

        lax.fori_loop(0, n_chunks, second, 0)

    blk = lambda r: pl.BlockSpec((2, r, cw), lambda j: (0, 0, j))
    return _host_call(
        body, name="ffn_act_bwd", grid=(D_FF // cw,),
        in_specs=[blk(s), pl.BlockSpec((s, cw), lambda j: (0, j)), blk(3), blk(1)],
        out_specs=[blk(s), blk(3), blk(1)],
        out_shape=[_sds((2, s, D_FF), BF16), _sds((2, 3, D_FF), F32), _sds((2, 1, D_FF), F32)],
        scratch_shapes=[pltpu.VMEM((2, s, cw), F32)],
        semantics=("parallel",), args=(u0, df, wc, bc), comm=comm)


def _up_bwd_merge_bwd(du0, w_up_g, x2, dx3, z, pa, pb, w_out, w_pa, w_pb, g_ffn, tm, comm):
    s = x2.shape[0]
    bw = w_up_g.shape[2]
    per_half = N_DEV // 2

    def body(du_ref, wu_ref, x2_ref, dx3_ref, ga_ref, gb_ref, pa_ref, pb_ref, wo_ref, wa_ref, wb_ref, g_ref,
             dx2_ref, dzg_ref, dpa_ref, dpb_ref, dya_ref, dyb_ref, sums_ref):
        @pl.when(pl.program_id(0) == 0)
        def _():
            sums_ref[...] = jnp.zeros_like(sums_ref)

        dh2 = None
        for j in range(N_DEV):
            cols = slice((j % per_half) * bw, (j % per_half + 1) * bw)
            term = _dot_nt(du_ref[j // per_half, :, cols], wu_ref[j])
            dh2 = term if dh2 is None else dh2 + term
        x2n, r2 = _rms(x2_ref[...])
        sums_ref[0:1, :] += jnp.sum(dh2 * x2n, axis=0, keepdims=True)
        dx2 = dx3_ref[...] + _rms_bwd(dh2 * g_ref[...], x2n, r2)
        dx2_ref[...] = dx2
        dm = _dot_nt(dx2.astype(BF16), wo_ref[...])
        sa = _sigmoid(ga_ref[...].astype(F32))
        sb = _sigmoid(gb_ref[...].astype(F32))
        dzg_ref[0] = (dm * pa_ref[...].astype(F32) * (sa * (1.0 - sa))).astype(BF16)
        dzg_ref[1] = (dm * pb_ref[...].astype(F32) * (sb * (1.0 - sb))).astype(BF16)
        dpa = (dm * sa).astype(BF16)
        dpb = (dm * sb).astype(BF16)
        dpa_ref[...] = dpa
        dpb_ref[...] = dpb
        dya_ref[...] = _dot_nt(dpa, wa_ref[...]).astype(BF16)
        dyb_ref[...] = _dot_nt(dpb, wb_ref[...]).astype(BF16)

    tile = pl.BlockSpec((tm, D_MODEL), lambda i: (i, 0))
    full = _resident((D_MODEL, D_MODEL))
    return _host_call(
        body, name="up_bwd_merge_bwd", grid=(s // tm,),
        in_specs=[pl.BlockSpec((2, tm, D_FF), lambda i: (0, i, 0)), _resident(w_up_g.shape),
                  tile, tile, pl.BlockSpec((tm, D_MODEL), lambda i: (i, 5)),
                  pl.BlockSpec((tm, D_MODEL), lambda i: (i, 6)), tile, tile, full, full, full,
                  pl.BlockSpec((1, D_MODEL), lambda i: (0, 0))],
        out_specs=[tile, pl.BlockSpec((2, tm, D_MODEL), lambda i: (0, i, 0)), tile, tile, tile, tile,
                   pl.BlockSpec((8, D_MODEL), lambda i: (0, 0))],
        out_shape=[_sds((s, D_MODEL), F32), _sds((2, s, D_MODEL), BF16)] + [_sds((s, D_MODEL), BF16)] * 4
        + [_sds((8, D_MODEL), F32)],
        semantics=("arbitrary",), args=(du0, w_up_g, x2, dx3, z, z, pa, pb, w_out, w_pa, w_pb, g_ffn),
        comm=comm)


def _mixers_bwd(z, h, dya, dyb, w4, b4, w_a, b_a, w_x, b_x, lam, w3, ch, comm):
    s = z.shape[0]
    cw = RNN_BW
    n_chunks = s // ch

    def body(xr_ref, gr_ref, cb_ref, cc_ref, cx_ref, h_ref, dya_ref, dyb_ref, w4_ref, b4_ref, wa_ref, ba_ref, wx_ref,
             bx_ref, lam_ref, w3_ref,
             dz_ref, dw4_ref, db4_ref, dwa_ref, dba_ref, dwx_ref, dbx_ref, dlam_ref, dw3_ref,
             a_scr, g_scr, q_scr, xc_scr, ra_scr, ia_scr):
        sp = _softplus(-lam_ref[...])
        wa = wa_ref[0].astype(BF16)
        wx = wx_ref[0].astype(BF16)
        for ref in (dw4_ref, db4_ref, dwa_ref, dba_ref, dwx_ref, dbx_ref, dlam_ref, dw3_ref):
            ref[...] = jnp.zeros_like(ref)

        def first(c, _):
            rows = pl.ds(pl.multiple_of(c * ch, HALO), ch)
            xc, ra, ia, log_a = _gate_chunk(_ext_before(xr_ref, c, ch), w4_ref, b4_ref, wa, wx, ba_ref, bx_ref, sp, ch)
            xc_scr[rows, :] = xc
            ra_scr[rows, :] = ra
            ia_scr[rows, :] = ia
            a_scr[rows, :] = jnp.exp(log_a)
            gel, dgel = _gelu_and_grad(gr_ref[rows, :].astype(F32))
            dya_c = dya_ref[rows, :].astype(F32)
            g_scr[rows, :] = dya_c * gel
            dz_ref[1, rows, :] = (dya_c * h_ref[rows, :] * dgel).astype(BF16)
            q_ext = _ext_before(cc_ref, c, ch) * _ext_before(cx_ref, c, ch)
            dyb_c = dyb_ref[rows, :].astype(F32)
            dz_ref[2, rows, :] = (dyb_c * _conv3(q_ext, w3_ref, ch)).astype(BF16)
            dyq = dyb_c * cb_ref[rows, :].astype(F32)
            q_scr[rows, :] = dyq
            for k in range(3):
                dw3_ref[k:k + 1, :] += jnp.sum(dyq * _shift_dn(q_ext, 2 - k, ch), axis=0, keepdims=True)
            return 0

        lax.fori_loop(0, n_chunks, first, 0)
        _scan_rev(a_scr, g_scr, s, cw)

        def second(c, _):
            rows = pl.ds(pl.multiple_of(c * ch, HALO), ch)
            xc, ra, ia, a = xc_scr[rows, :], ra_scr[rows, :], ia_scr[rows, :], a_scr[rows, :]
            mult = jnp.sqrt(-_expm1((-2.0 * LRU_C) * ra * sp))
            dh = g_scr[rows, :]
            h_prev = _shift_dn(_ext_before(h_ref, c, ch), 1, ch)
            dlog_a = dh * h_prev * a - (dh * ia * xc) * (a * a) / mult
            dlam_ref[...] += jnp.sum(dlog_a * ra, axis=0, keepdims=True)
            dpre_a = (dlog_a * ((-LRU_C) * sp)) * (ra * (1.0 - ra))
            dpre_x = (dh * mult * xc) * (ia * (1.0 - ia))
            dba_ref[...] += jnp.sum(dpre_a, axis=0, keepdims=True)
            dbx_ref[...] += jnp.sum(dpre_x, axis=0, keepdims=True)
            xcb = xc.astype(BF16)
            dpa_b = dpre_a.astype(BF16)
            dpx_b = dpre_x.astype(BF16)
            dwa_ref[0] += _dot_tn(xcb, dpa_b)
            dwx_ref[0] += _dot_tn(xcb, dpx_b)
            a_scr[rows, :] = dh * mult * ia + _dot_nt(dpa_b, wa) + _dot_nt(dpx_b, wx)
            return 0

        lax.fori_loop(0, n_chunks, second, 0)
        dlam_ref[...] = dlam_ref[...] * (LRU_C * _sigmoid(-lam_ref[...]))

        def third(c, _):
            rows = pl.ds(pl.multiple_of(c * ch, HALO), ch)
            dxc_ext = _ext_after(a_scr, c, ch, n_chunks)
            dxc = _shift_up(dxc_ext, 0, ch)
            xr_ext = _ext_before(xr_ref, c, ch)
            db4_ref[...] += jnp.sum(dxc, axis=0, keepdims=True)
            dxr = w4_ref[3:4, :] * dxc
            dw4_ref[3:4, :] += jnp.sum(dxc * _shift_dn(xr_ext, 0, ch), axis=0, keepdims=True)
            for k in range(3):
                dxr = dxr + w4_ref[k:k + 1, :] * _shift_up(dxc_ext, 3 - k, ch)
                dw4_ref[k:k + 1, :] += jnp.sum(dxc * _shift_dn(xr_ext, 3 - k, ch), axis=0, keepdims=True)
            dz_ref[0, rows, :] = dxr.astype(BF16)
            dyq_ext = _ext_after(q_scr, c, ch, n_chunks)
            dq = w3_ref[2:3, :] * _shift_up(dyq_ext, 0, ch)
            for k in range(2):
                dq = dq + w3_ref[k:k + 1, :] * _shift_up(dyq_ext, 2 - k, ch)
            dz_ref[3, rows, :] = (dq * cx_ref[rows, :].astype(F32)).astype(BF16)
            dz_ref[4, rows, :] = (dq * cc_ref[rows, :].astype(F32)).astype(BF16)
            return 0

        lax.fori_loop(0, n_chunks, third, 0)

    col = lambda j: (0, j)
    vec = pl.BlockSpec((1, cw), col)
    sq = pl.BlockSpec((1, cw, cw), lambda j: (j, 0, 0))
    act = pl.BlockSpec((s, cw), col)
    w4s, w3s = pl.BlockSpec((4, cw), col), pl.BlockSpec((3, cw), col)
    vec_shape = _sds((1, D_MODEL), F32)
    sq_shape = _sds((D_MODEL // cw, cw, cw), F32)
    return _host_call(
        body, name="mixers_bwd", grid=(D_MODEL // cw,),
        in_specs=_z_block_specs(s, cw, range(5)) + [act, act, act, w4s, vec, sq, vec, sq, vec, vec, w3s],
        out_specs=[pl.BlockSpec((5, s, cw), lambda j: (0, 0, j)), w4s, vec, sq, vec, sq, vec, vec, w3s],
        out_shape=[_sds((5, s, D_MODEL), BF16), _sds((4, D_MODEL), F32), vec_shape, sq_shape, vec_shape, sq_shape,
                   vec_shape, vec_shape, _sds((3, D_MODEL), F32)],
        scratch_shapes=[pltpu.VMEM((s, cw), F32)] * 6,
        semantics=("parallel",), args=(z, z, z, z, z, h, dya, dyb, w4, b4, w_a, b_a, w_x, b_x, lam, w3), comm=comm)


def _in_proj_bwd(dz5, dzg, w_in, x, dx2, g_mix, tm, comm):
    s = x.shape[0]

    def body(d5_ref, dg_ref, w_ref, x_ref, dx2_ref, g_ref, dx_ref, sums_ref):
        @pl.when(pl.program_id(0) == 0)
        def _():
            sums_ref[...] = jnp.zeros_like(sums_ref)

        dh1 = None
        for k in range(N_SEG):
            d = d5_ref[k] if k < 5 else dg_ref[k - 5]
            term = _dot_nt(d, w_ref[:, k * D_MODEL:(k + 1) * D_MODEL])
            dh1 = term if dh1 is None else dh1 + term
        xn, r1 = _rms(x_ref[...])
        sums_ref[0:1, :] += jnp.sum(dh1 * xn, axis=0, keepdims=True)
        dx_ref[...] = dx2_ref[...] + _rms_bwd(dh1 * g_ref[...], xn, r1)

    tile = pl.BlockSpec((tm, D_MODEL), lambda i: (i, 0))
    return _host_call(
        body, name="in_proj_bwd", grid=(s // tm,),
        in_specs=[pl.BlockSpec((5, tm, D_MODEL), lambda i: (0, i, 0)), pl.BlockSpec((2, tm, D_MODEL), lambda i: (0, i, 0)),
                  _resident(w_in.shape), tile, tile, pl.BlockSpec((1, D_MODEL), lambda i: (0, 0))],
        out_specs=[tile, pl.BlockSpec((8, D_MODEL), lambda i: (0, 0))],
        out_shape=[_sds((s, D_MODEL), F32), _sds((8, D_MODEL), F32)],
        semantics=("arbitrary",), args=(dz5, dzg, w_in, x, dx2, g_mix), comm=comm)


def _weight_grad(a, b, b_spec, out_spec, out_shape, *, n_blocks, chunks, width, tm, tk, name):
    s, m = a.shape
    nk = s // tk
    b_chunked = len([d for d in b_spec.block_shape if d is not None]) == 3
    blocks_out = len([d for d in out_spec.block_shape if d is not None]) == 3

    def body(a_ref, b_ref, o_ref, acc):
        k = pl.program_id(2)

        @pl.when(k == 0)
        def _():
            acc[...] = jnp.zeros_like(acc)

        at = a_ref[...].astype(BF16).T
        for j in range(chunks):
            cols = slice(j * width, (j + 1) * width)
            acc[:, cols] += _dot(at, (b_ref[j] if b_chunked else b_ref[:, cols]).astype(BF16))

        @pl.when(k == nk - 1)
        def _():
            if blocks_out:
                for j in range(chunks):
                    o_ref[j] = acc[:, j * width:(j + 1) * width].astype(o_ref.dtype)
            else:
                o_ref[...] = acc[...].astype(o_ref.dtype)

    return pl.pallas_call(
        body, name=name, grid=(m // tm, n_blocks, nk),
        in_specs=[pl.BlockSpec((tk, tm), lambda i, j, k: (k, i)), b_spec], out_specs=out_spec, out_shape=out_shape,
        scratch_shapes=[pltpu.VMEM((tm, chunks * width), F32)],
        compiler_params=_params("parallel", "parallel", "arbitrary"),
    )(a, b)


def _wgrad_2d(a, b, name, tk):
    m, n = a.shape[1], b.shape[1]
    tm = m
    return _weight_grad(a, b, pl.BlockSpec((tk, n), lambda i, j, k: (k, 0)), pl.BlockSpec((tm, n), lambda i, j, k: (i, 0)),
                        _sds((m, n), BF16), n_blocks=1, chunks=1, width=n, tm=tm, tk=tk, name=name)


def _wgrad_segments(a, b3, name, tm, tk):
    m = a.shape[1]
    g = b3.shape[0]
    return _weight_grad(a, b3, pl.BlockSpec((g, tk, D_MODEL), lambda i, j, k: (0, k, 0)),
                        pl.BlockSpec((tm, g * D_MODEL), lambda i, j, k: (i, 0)), _sds((m, g * D_MODEL), BF16),
                        n_blocks=1, chunks=g, width=D_MODEL, tm=tm, tk=tk, name=name)


def _wgrad_up(h2, du0, bw, tk):
    per_half = N_DEV // 2
    return _weight_grad(h2, du0, pl.BlockSpec((None, tk, D_FF), lambda i, j, k: (j, k, 0)),
                        pl.BlockSpec((per_half, D_MODEL, bw), lambda i, j, k: (j, 0, 0)), _sds((N_DEV, D_MODEL, bw), BF16),
                        n_blocks=2, chunks=per_half, width=bw, tm=D_MODEL, tk=tk, name="wgrad_up")


def _adam_math(w, g, m, v):
    m = ADAM_B1 * m + (1.0 - ADAM_B1) * g
    v = ADAM_B2 * v + (1.0 - ADAM_B2) * jnp.square(g)
    m_hat = m / (1.0 - ADAM_B1 ** ADAM_STEP)
    v_hat = v / (1.0 - ADAM_B2 ** ADAM_STEP)
    delta = -ADAM_LR * (m_hat / (jnp.sqrt(v_hat) + ADAM_EPS) + ADAM_WD * w)
    return delta, m, v


def _adam_shard(parts, w, m, v, name):
    r, c = w.shape
    tr = min(r, 128)
    n_arrays = len(parts)

    def body(*refs):
        p_refs = refs[:n_arrays]
        w_ref, m_ref, v_ref, g_ref, d_ref, nm_ref, nv_ref = refs[n_arrays:]
        g = None
        for p_ref, (arr, first) in zip(p_refs, parts):
            for k in range(first, arr.shape[0]):
                term = p_ref[k].astype(F32)
                g = term if g is None else g + term
        g_ref[...] = g
        d_ref[...], nm_ref[...], nv_ref[...] = _adam_math(w_ref[...], g, m_ref[...], v_ref[...])

    tile = pl.BlockSpec((tr, c), lambda i: (i, 0))
    return pl.pallas_call(
        body, name=name, grid=(r // tr,),
        in_specs=[pl.BlockSpec((arr.shape[0], tr, c), lambda i: (0, i, 0)) for arr, _ in parts] + [tile, tile, tile],
        out_specs=[tile] * 4, out_shape=[_sds((r, c), F32)] * 4,
        compiler_params=_params("parallel"),
    )(*[arr for arr, _ in parts], w, m, v)


def _adam_small(grads, ws, ms, vs, name):
    n = len(ws)
    g_arrays, g_slot = [], []
    for g in grads:
        arr = g[0] if isinstance(g, tuple) else g
        if not any(arr is a for a in g_arrays):
            g_arrays.append(arr)
        g_slot.append([arr is a for a in g_arrays].index(True))
    n_g = len(g_arrays)

    def body(*refs):
        g_refs, refs = refs[:n_g], refs[n_g:]
        w_refs, m_refs, v_refs, outs = refs[:n], refs[n:2 * n], refs[2 * n:3 * n], refs[3 * n:]
        for i in range(n):
            og, od, om, ov = outs[4 * i:4 * i + 4]
            g_ref = g_refs[g_slot[i]]
            if isinstance(grads[i], tuple):
                row = grads[i][1]
                for j in range(ws[i].shape[1] // D_MODEL):
                    cols = slice(j * D_MODEL, (j + 1) * D_MODEL)
                    g = g_ref[row + j:row + j + 1, :]
                    og[:, cols] = g
                    od[:, cols], om[:, cols], ov[:, cols] = _adam_math(w_refs[i][:, cols], g, m_refs[i][:, cols],
                                                                       v_refs[i][:, cols])
            else:
                g = g_ref[...]
                og[...] = g
                od[...], om[...], ov[...] = _adam_math(w_refs[i][...], g, m_refs[i][...], v_refs[i][...])

    vmem = pl.BlockSpec(memory_space=pltpu.VMEM)
    res = pl.pallas_call(
        body, name=name, in_specs=[vmem] * (n_g + 3 * n), out_specs=[vmem] * (4 * n),
        out_shape=[_sds(a.shape, F32) for a in ws for _ in range(4)],
    )(*g_arrays, *ws, *ms, *vs)
    return [res[4 * i:4 * i + 4] for i in range(n)]


def _pack_conv_shard(rnn, sc, ffn):
    top = jnp.concatenate([rnn[:3], sc, ffn], axis=1)
    row3 = jnp.concatenate([rnn[3:4], jnp.zeros((1, D_MODEL - RNN_BW), F32)], axis=1)
    return jnp.concatenate([top, row3, jnp.zeros((4, D_MODEL), F32)], axis=0)


_VECTORS = ("g_mix", "rnn_conv_b", "b_rg_a", "b_rg_x", "lru_lambda", "g_ffn", "g_ple", "g_final", "ffn_conv_b")
VEC_ROWS = 64
ROW_LOSS, ROW_RNN_CONV, ROW_SC_CONV, ROW_FFN_CONV = 14, 16, 20, 24


_SHARDED_BIG = ("w_in", "w_proj_a", "w_proj_b", "w_out", "w_up", "w_down", "w_ple_gate", "w_ple_proj")
_NAMES = ("g_mix", "w_in", "rnn_conv_w", "rnn_conv_b", "w_rg_a", "b_rg_a", "w_rg_x", "b_rg_x", "lru_lambda", "sc_conv_w",
          "w_proj_a", "w_proj_b", "w_out", "g_ffn", "w_up", "ffn_conv_w", "ffn_conv_b", "w_down", "w_ple_gate",
          "w_ple_proj", "g_ple", "g_final")


def _step(x, p, target, w, m, v):
    s = x.shape[0]
    tm = min(s, 256)
    tm_wide = min(s, 512)
    ch = min(s, 256)
    my_index = 4 * lax.axis_index("x") + 2 * lax.axis_index("y") + lax.axis_index("c")

    big = {n: w[n][0] for n in _SHARDED_BIG}
    conv_shard = _pack_conv_shard(w["rnn_conv_w"][0], w["sc_conv_w"][0], w["ffn_conv_w"][0])
    shards = {n: big[n].astype(BF16) for n in _SHARDED_BIG}
    w_in_g, conv_all = _comm_only(_gather_comm([shards["w_in"], conv_shard]), "in_proj_weight_gather")
    w_in = jnp.transpose(w_in_g, (1, 0, 2)).reshape(D_MODEL, N_SEG * D_MODEL)
    w4 = jnp.transpose(jnp.concatenate([conv_all[:, :3, :RNN_BW], conv_all[:, 3:4, :RNN_BW]], axis=1),
                       (1, 0, 2)).reshape(4, D_MODEL)
    w3 = jnp.transpose(conv_all[:, :3, RNN_BW:2 * RNN_BW], (1, 0, 2)).reshape(3, D_MODEL)
    wc = jnp.transpose(conv_all[:, :3, 2 * RNN_BW:], (1, 0, 2)).reshape(3, 2, D_FF).transpose(1, 0, 2)
    bc = w["ffn_conv_b"].reshape(2, 1, D_FF)
    b4, b_a, b_x, lam = w["rnn_conv_b"], w["b_rg_a"], w["b_rg_x"], w["lru_lambda"]
    w_a, w_x = w["w_rg_a"][0], w["w_rg_x"][0]
    g_final = w["g_final"].reshape(1, D_MODEL)

    tk = min(s, 512)
    received = {}

    comm = _gather_comm([shards[n] for n in ("w_proj_a", "w_proj_b", "w_out")])
    (z, h1), (w_pa, w_pb, w_out) = _norm_in_proj(x, w["g_mix"], w_in, tm_wide, comm)
    w_pa, w_pb, w_out = (a.reshape(D_MODEL, D_MODEL) for a in (w_pa, w_pb, w_out))
    comm = _gather_comm([shards[n] for n in ("w_up", "w_down", "w_ple_gate", "w_ple_proj")])
    (ya, yb, h), (w_up_g, w_down, w_gate, w_pp) = _mixers_fwd(z, w4, b4, w_a, b_a, w_x, b_x, lam, w3, ch, comm)
    w_down = w_down.reshape(D_FF, D_MODEL)
    w_gate = w_gate.reshape(D_MODEL, D_MODEL)
    w_pp = jnp.transpose(w_pp, (1, 0, 2)).reshape(D_PLE, D_MODEL)
    pa, pb, mm, x2, h2 = _merge_out(ya, yb, z, x, w_pa, w_pb, w_out, w["g_ffn"], tm)
    u0 = _up_proj(h2, w_up_g, tm_wide)
    f = _ffn_act(u0, wc, bc, 256, ch)
    x3, dgt, de0, dx3, df, head_sums = _head(x2, f, p, target, w_down, w_gate, w_pp, w["g_ple"], g_final, tm)
    parts = [_wgrad_2d(x3, dgt, "wgrad_ple_gate", tk).reshape(N_DEV, RNN_BW, D_MODEL),
             jnp.transpose(_wgrad_2d(p, de0, "wgrad_ple_proj", tk).reshape(D_PLE, N_DEV, RNN_BW), (1, 0, 2)),
             _wgrad_2d(f, dx3, "wgrad_down", tk).reshape(N_DEV, D_FF // N_DEV, D_MODEL)]
    (du0, dwc, dbc), got = _ffn_act_bwd(u0, df, wc, bc, 256, ch, _exchange_comm(parts))
    received.update({n: [(g, 0)] for n, g in zip(("w_ple_gate", "w_ple_proj", "w_down"), got)})
    dw_up = _wgrad_up(h2, du0, w_up_g.shape[2], tk)
    (dx2, dzg, dpa, dpb, dya, dyb, ffn_sums), got_up = _up_bwd_merge_bwd(
        du0, w_up_g, x2, dx3, z, pa, pb, w_out, w_pa, w_pb, w["g_ffn"], tm, _exchange_comm([dw_up], (1, 2, 3)))
    parts = [_wgrad_2d(mm, dx2, "wgrad_out", tk).reshape(N_DEV, RNN_BW, D_MODEL),
             _wgrad_2d(ya, dpa, "wgrad_proj_a", tk).reshape(N_DEV, RNN_BW, D_MODEL),
             _wgrad_2d(yb, dpb, "wgrad_proj_b", tk).reshape(N_DEV, RNN_BW, D_MODEL)]
    comm = _exchange_comm(parts)
    rest = _exchange_comm([dw_up], (4, 5, 6, 7))
    (dz5, dw4, db4, dwa, dba, dwx, dbx, dlam, dw3), got = _mixers_bwd(z, h, dya, dyb, w4, b4, w_a, b_a, w_x, b_x, lam, w3,
                                                                       ch, _join_comms(comm, rest))
    received.update({n: [(g, 0)] for n, g in zip(("w_out", "w_proj_a", "w_proj_b"), got[:3])})
    received["w_up"] = [(got_up[0], 0), (got[3], 1)]
    dw_in = jnp.concatenate([_wgrad_segments(h1, dz5, "wgrad_in_mix", 512, tk),
                             _wgrad_segments(h1, dzg, "wgrad_in_gate", D_MODEL, tk)], axis=1)
    bw_in = N_SEG * D_MODEL // N_DEV
    chip_sum = _pair_reduce(jnp.transpose(dw_in.reshape(D_MODEL, N_DEV, bw_in), (1, 0, 2)), "in_proj_grad_pair_reduce")
    (grad_x, mix_sums), got = _in_proj_bwd(dz5, dzg, w_in, x, dx2, w["g_mix"], tm, _chip_exchange_comm([chip_sum]))
    received["w_in"] = [(got[0], 0)]

    zero_row = jnp.zeros((1, D_MODEL), F32)
    vec_pack = jnp.concatenate(
        [mix_sums[0:1], db4, dba, dbx, dlam, ffn_sums[0:1], head_sums[2:3], head_sums[1:2], dbc.reshape(6, D_MODEL),
         head_sums[0:1], zero_row, dw4, dw3, zero_row, jnp.transpose(dwc, (1, 0, 2)).reshape(18, D_MODEL),
         jnp.zeros((VEC_ROWS - ROW_FFN_CONV - 18, D_MODEL), F32)], axis=0)
    gate_pack = jnp.concatenate([dwa.reshape(D_MODEL, RNN_BW), dwx.reshape(D_MODEL, RNN_BW)], axis=0)
    vec_total, gate_total = _small_all_reduce([vec_pack, gate_pack])
    loss = jnp.sum(vec_total[ROW_LOSS]) * (0.5 / D_MODEL)

    out = {}
    for n in _SHARDED_BIG:
        res = _adam_shard(received[n], big[n], m[n][0], v[n][0], "adam_" + n)
        out[n] = [r[None] for r in res]

    def flat(t, n):
        return t[n].reshape(1, -1)

    res = _adam_small([(vec_total, i) for i in range(len(_VECTORS))],
                      [flat(w, n) for n in _VECTORS], [flat(m, n) for n in _VECTORS], [flat(v, n) for n in _VECTORS],
                      "adam_vectors")
    for n, r in zip(_VECTORS, res):
        out[n] = [a.reshape(w[n].shape) for a in r]
    gates = ("w_rg_a", "w_rg_x")
    res = _adam_small([gate_total[:D_MODEL], gate_total[D_MODEL:]], *[[t[n].reshape(D_MODEL, RNN_BW) for n in gates]
                                                                      for t in (w, m, v)], "adam_gate_maps")
    for n, r in zip(gates, res):
        out[n] = [a.reshape(w[n].shape) for a in r]
    convs = ("rnn_conv_w", "sc_conv_w", "ffn_conv_w")
    bw_ffn = 2 * D_FF // N_DEV
    g_conv = [lax.dynamic_slice(vec_total[ROW_RNN_CONV:ROW_RNN_CONV + 4], (0, my_index * RNN_BW), (4, RNN_BW)),
              lax.dynamic_slice(vec_total[ROW_SC_CONV:ROW_SC_CONV + 3], (0, my_index * RNN_BW), (3, RNN_BW)),
              lax.dynamic_slice(vec_total[ROW_FFN_CONV:ROW_FFN_CONV + 18].reshape(3, 2 * D_FF), (0, my_index * bw_ffn),
                                (3, bw_ffn))]
    res = _adam_small(g_conv, *[[t[n][0] for n in convs] for t in (w, m, v)], "adam_conv")
    for n, r in zip(convs, res):
        out[n] = [a[None] for a in r]

    return (loss, grad_x[None]) + tuple(out[n][k] for k in range(4) for n in _NAMES)


def kernel(x, p, g_mix, w_in, rnn_conv_w, rnn_conv_b, w_rg_a, b_rg_a, w_rg_x, b_rg_x, lru_lambda, sc_conv_w, w_proj_a, w_proj_b, w_out, g_ffn, w_up, ffn_conv_w, ffn_conv_b, w_down, w_ple_gate, w_ple_proj, g_ple, g_final, loss_target, m_g_mix, m_w_in, m_rnn_conv_w, m_rnn_conv_b, m_w_rg_a, m_b_rg_a, m_w_rg_x, m_b_rg_x, m_lru_lambda, m_sc_conv_w, m_w_proj_a, m_w_proj_b, m_w_out, m_g_ffn, m_w_up, m_ffn_conv_w, m_ffn_conv_b, m_w_down, m_w_ple_gate, m_w_ple_proj, m_g_ple, m_g_final, v_g_mix, v_w_in, v_rnn_conv_w, v_rnn_conv_b, v_w_rg_a, v_b_rg_a, v_w_rg_x, v_b_rg_x, v_lru_lambda, v_sc_conv_w, v_w_proj_a, v_w_proj_b, v_w_out, v_g_ffn, v_w_up, v_ffn_conv_w, v_ffn_conv_b, v_w_down, v_w_ple_gate, v_w_ple_proj, v_g_ple, v_g_final):
    given = dict(locals())
    w = {n: given[n] for n in _NAMES}
    m = {n: given["m_" + n] for n in _NAMES}
    v = {n: given["v_" + n] for n in _NAMES}
    return _step(x[0], p[0, 0], loss_target[0], w, m, v)
```

```python
import functools

import jax
import jax.numpy as jnp
from jax import lax
from jax.experimental import pallas as pl
from jax.experimental.pallas import tpu as pltpu

F32 = jnp.float32
BF16 = jnp.bfloat16
MESH_ID = pl.DeviceIdType.MESH

N_DEV = 8
D_MODEL = 1024
D_PLE = 256
RNN_BW = 128
N_SEG = 7
D_FF = 3072
LRU_C = 8.0
EPS = 1e-6
ADAM_LR = 0.001
ADAM_B1 = 0.9
ADAM_B2 = 0.999
ADAM_EPS = 1e-08
ADAM_WD = 0.01
ADAM_STEP = 10

HALO = 16
SCAN_UNROLL = 4
VMEM_LIMIT = 56 * 1024 * 1024


def _dot(a, b):
    return jnp.dot(a, b, preferred_element_type=F32)


def _dot_nt(a, b):
    return lax.dot_general(a, b, (((1,), (1,)), ((), ())), preferred_element_type=F32)


def _dot_tn(a, b):
    return lax.dot_general(a, b, (((0,), (0,)), ((), ())), preferred_element_type=F32)


def _sigmoid(x):
    return jax.nn.sigmoid(x)


_GELU_C = 0.7978845608028654
_GELU_K = 0.044715


def _gelu(x):
    return 0.5 * x * (1.0 + jnp.tanh(_GELU_C * (x + _GELU_K * (x * x * x))))


def _gelu_and_grad(x):
    x2 = x * x
    t = jnp.tanh(_GELU_C * (x + _GELU_K * (x * x2)))
    g = 0.5 * x * (1.0 + t)
    dg = 0.5 * (1.0 + t) + 0.5 * x * (1.0 - t * t) * (_GELU_C * (1.0 + 3.0 * _GELU_K * x2))
    return g, dg


def _expm1(x):
    poly = x * (1.0 + x * (0.5 + x * (1.0 / 6.0 + x * (1.0 / 24.0 + x * (1.0 / 120.0)))))
    return jnp.where(jnp.abs(x) < 0.08, poly, jnp.exp(x) - 1.0)


def _softplus(x):
    return jnp.maximum(x, 0.0) + jnp.log1p(jnp.exp(-jnp.abs(x)))


def _rms(u):
    r = lax.rsqrt(jnp.mean(u * u, axis=-1, keepdims=True) + EPS)
    return u * r, r


def _rms_bwd(dn, un, r):
    return r * (dn - un * jnp.mean(dn * un, axis=-1, keepdims=True))


def _shift_dn(ext, s, n):
    if s == 0:
        return ext[HALO:HALO + n]
    return pltpu.roll(ext, s, 0)[HALO:HALO + n]


def _shift_up(ext, s, n):
    if s == 0:
        return ext[0:n]
    return pltpu.roll(ext, n + HALO - s, 0)[0:n]


def _ext_before(ref, c, ch):
    t0 = c * ch
    prev = pl.multiple_of(jnp.maximum(t0 - HALO, 0), HALO)
    halo = jnp.where(c > 0, ref[pl.ds(prev, HALO), :].astype(F32), 0.0)
    cur = ref[pl.ds(pl.multiple_of(t0, HALO), ch), :].astype(F32)
    return jnp.concatenate([halo, cur], axis=0)


def _ext_after(ref, c, ch, n_chunks):
    t0 = c * ch
    nxt = pl.multiple_of(jnp.minimum(t0 + ch, (n_chunks - 1) * ch + ch - HALO), HALO)
    halo = jnp.where(c < n_chunks - 1, ref[pl.ds(nxt, HALO), :].astype(F32), 0.0)
    cur = ref[pl.ds(pl.multiple_of(t0, HALO), ch), :].astype(F32)
    return jnp.concatenate([cur, halo], axis=0)


def _rows(ref, c, ch):
    return ref[pl.ds(pl.multiple_of(c * ch, HALO), ch), :]


def _scan_fwd(a_ref, b_ref, h_ref, n_rows, cw):
    rows = lax.broadcasted_iota(jnp.int32, (8, cw), 0)

    def body(i, carry):
        for u in range(SCAN_UNROLL):
            t = pl.multiple_of((i * SCAN_UNROLL + u) * 8, 8)
            a = a_ref[pl.ds(t, 8), :]
            b = b_ref[pl.ds(t, 8), :]
            for d in (1, 2, 4):
                m = rows >= d
                b = jnp.where(m, a * pltpu.roll(b, d, 0) + b, b)
                a = jnp.where(m, a * pltpu.roll(a, d, 0), a)
            h_ref[pl.ds(t, 8), :] = a * carry + b
            carry = jnp.broadcast_to(a[7:8, :], (8, cw)) * carry + jnp.broadcast_to(b[7:8, :], (8, cw))
        return carry

    lax.fori_loop(0, n_rows // (8 * SCAN_UNROLL), body, jnp.zeros((8, cw), F32))


def _scan_rev(a_ref, g_ref, o_ref, n_rows, cw):
    rows = lax.broadcasted_iota(jnp.int32, (8, cw), 0)
    n_groups = n_rows // 8

    def body(i, carry):
        dh_next, a_next = carry
        for u in range(SCAN_UNROLL):
            t = pl.multiple_of((n_groups - 1 - (i * SCAN_UNROLL + u)) * 8, 8)
            a = a_ref[pl.ds(t, 8), :]
            g = g_ref[pl.ds(t, 8), :]
            an = jnp.where(rows < 7, pltpu.roll(a, 7, 0), a_next)
            a_next = jnp.broadcast_to(a[0:1, :], (8, cw))
            for d in (1, 2, 4):
                m = rows < 8 - d
                g = jnp.where(m, an * pltpu.roll(g, 8 - d, 0) + g, g)
                an = jnp.where(m, an * pltpu.roll(an, 8 - d, 0), an)
            o_ref[pl.ds(t, 8), :] = an * dh_next + g
            dh_next = jnp.broadcast_to(an[0:1, :], (8, cw)) * dh_next + jnp.broadcast_to(g[0:1, :], (8, cw))
        return dh_next, a_next

    zero = jnp.zeros((8, cw), F32)
    lax.fori_loop(0, n_groups // SCAN_UNROLL, body, (zero, zero))


def _params(*sem):
    return pltpu.CompilerParams(dimension_semantics=sem, vmem_limit_bytes=VMEM_LIMIT)


def _sds(shape, dtype):
    return jax.ShapeDtypeStruct(shape, dtype)


def _resident(shape):
    zeros = (0,) * len(shape)
    return pl.BlockSpec(shape, lambda *_: zeros, pipeline_mode=pl.Buffered(1))


class _Comm:
    def __init__(self, inputs, out_shape, scratch, start, finish):
        self.inputs, self.out_shape, self.scratch, self.start, self.finish = inputs, out_shape, scratch, start, finish


def _join_comms(a, b):
    n, k = len(a.inputs), len(a.scratch)

    def start(ins, outs, sems):
        a.start(ins[:n], outs[:n], sems[:k])
        b.start(ins[n:], outs[n:], sems[k:])

    def finish(ins, outs, sems):
        a.finish(ins[:n], outs[:n], sems[:k])
        b.finish(ins[n:], outs[n:], sems[k:])

    return _Comm(a.inputs + b.inputs, a.out_shape + b.out_shape, a.scratch + b.scratch, start, finish)


def _sem_scratch(n, copies=7):
    return [pltpu.SemaphoreType.DMA((n, copies)), pltpu.SemaphoreType.DMA((n, copies)), pltpu.SemaphoreType.DMA((n,))]


def _gather_comm(shards):
    n = len(shards)

    def plan(ins, outs, sems):
        send_sems, recv_sems, local_sems = sems
        x, y, c = lax.axis_index("x"), lax.axis_index("y"), lax.axis_index("c")
        me, sibling = (x, y, c), (x, y, 1 - c)
        chips = [(1 - x, y), (x, 1 - y), (1 - x, 1 - y)]

        def slot(t, dev):
            return outs[t].at[4 * dev[0] + 2 * dev[1] + dev[2]]

        def copy(t, k, block, to, src=None):
            return pltpu.make_async_remote_copy(
                src_ref=slot(t, block) if src is None else src, dst_ref=slot(t, block),
                send_sem=send_sems.at[t, k], recv_sem=recv_sems.at[t, k],
                device_id=to, device_id_type=MESH_ID)

        mine = [pltpu.make_async_copy(ins[t], slot(t, me), local_sems.at[t]) for t in range(n)]
        first = []
        for t in range(n):
            first.append(copy(t, 0, me, sibling, src=ins[t]))
            first += [copy(t, 1 + j, me, (*chip, c), src=ins[t]) for j, chip in enumerate(chips)]
        return me, sibling, chips, c, copy, mine, first

    def start(ins, outs, sems):
        *_, mine, first = plan(ins, outs, sems)
        for cp in mine + first:
            cp.start()

    def finish(ins, outs, sems):
        me, sibling, chips, c, copy, mine, first = plan(ins, outs, sems)
        passed = []
        for t in range(n):
            for j, chip in enumerate(chips):
                copy(t, 1 + j, (*chip, c), me).wait_recv()
                cp = copy(t, 4 + j, (*chip, c), sibling)
                cp.start()
                passed.append(cp)
        for t in range(n):
            copy(t, 0, sibling, me).wait_recv()
            for j, chip in enumerate(chips):
                copy(t, 4 + j, (*chip, 1 - c), me).wait_recv()
        for cp in first + passed:
            cp.wait_send()
        for cp in mine:
            cp.wait()

    return _Comm(list(shards), [_sds((N_DEV,) + s.shape, s.dtype) for s in shards], _sem_scratch(n), start, finish)


def _peer(d):
    x, y, c = lax.axis_index("x"), lax.axis_index("y"), lax.axis_index("c")
    px = 1 - x if d & 4 else x
    py = 1 - y if d & 2 else y
    pc = 1 - c if d & 1 else c
    return (px, py, pc), 4 * px + 2 * py + pc


def _exchange_comm(parts, deltas=tuple(range(1, N_DEV))):
    n = len(parts)

    def plan(ins, outs, sems):
        send_sems, recv_sems, local_sems = sems
        _, me = _peer(0)

        def copy(t, i):
            peer, idx = _peer(deltas[i])
            return pltpu.make_async_remote_copy(
                src_ref=ins[t].at[idx], dst_ref=outs[t].at[1 + i],
                send_sem=send_sems.at[t, i], recv_sem=recv_sems.at[t, i],
                device_id=peer, device_id_type=MESH_ID)

        mine = [pltpu.make_async_copy(ins[t].at[me], outs[t].at[0], local_sems.at[t]) for t in range(n)]
        return mine, [copy(t, i) for t in range(n) for i in range(len(deltas))]

    def start(ins, outs, sems):
        mine, copies = plan(ins, outs, sems)
        for cp in mine + copies:
            cp.start()

    def finish(ins, outs, sems):
        mine, copies = plan(ins, outs, sems)
        for cp in copies:
            cp.wait_recv()
        for cp in copies:
            cp.wait_send()
        for cp in mine:
            cp.wait()

    return _Comm(list(parts), [_sds((1 + len(deltas),) + p.shape[1:], p.dtype) for p in parts],
                 _sem_scratch(n, len(deltas)), start, finish)


def _chip_exchange_comm(parts):
    n = len(parts)

    def plan(ins, outs, sems):
        send_sems, recv_sems, local_sems = sems
        x, y, c = lax.axis_index("x"), lax.axis_index("y"), lax.axis_index("c")

        def copy(t, dq):
            px = 1 - x if dq & 2 else x
            py = 1 - y if dq & 1 else y
            return pltpu.make_async_remote_copy(
                src_ref=ins[t].at[2 * px + py], dst_ref=outs[t].at[dq],
                send_sem=send_sems.at[t, dq - 1], recv_sem=recv_sems.at[t, dq - 1],
                device_id=(px, py, c), device_id_type=MESH_ID)

        mine = [pltpu.make_async_copy(ins[t].at[2 * x + y], outs[t].at[0], local_sems.at[t]) for t in range(n)]
        return mine, [copy(t, dq) for t in range(n) for dq in range(1, 4)]

    def start(ins, outs, sems):
        mine, copies = plan(ins, outs, sems)
        for cp in mine + copies:
            cp.start()

    def finish(ins, outs, sems):
        mine, copies = plan(ins, outs, sems)
        for cp in copies:
            cp.wait_recv()
        for cp in copies:
            cp.wait_send()
        for cp in mine:
            cp.wait()

    return _Comm(list(parts), [_sds(p.shape, p.dtype) for p in parts], _sem_scratch(n, 3), start, finish)


def _pair_reduce(part, name):
    _, r, c = part.shape
    n_chips = N_DEV // 2
    tr = 128

    def body(p_ref, o_ref, own, land, send_sems, recv_sems, local_sems):
        x, y, core = lax.axis_index("x"), lax.axis_index("y"), lax.axis_index("c")
        sends = [pltpu.make_async_remote_copy(
            src_ref=p_ref.at[2 * q + (1 - core)], dst_ref=land.at[q], send_sem=send_sems.at[q], recv_sem=recv_sems.at[q],
            device_id=(x, y, 1 - core), device_id_type=MESH_ID) for q in range(n_chips)]
        loads = [pltpu.make_async_copy(p_ref.at[2 * q + core], own.at[q], local_sems.at[q]) for q in range(n_chips)]
        for cp in sends + loads:
            cp.start()
        for q in range(n_chips):
            loads[q].wait()
            sends[q].wait_recv()

            def add_rows(i, _):
                rows = pl.ds(pl.multiple_of(i * tr, tr), tr)
                o_ref[q, rows, :] = (own[q, rows, :].astype(F32) + land[q, rows, :].astype(F32)).astype(BF16)
                return 0

            lax.fori_loop(0, r // tr, add_rows, 0)
        for cp in sends:
            cp.wait_send()

    return pl.pallas_call(
        body, name=name, out_shape=_sds((n_chips, r, c), BF16),
        in_specs=[pl.BlockSpec(memory_space=pl.ANY)], out_specs=pl.BlockSpec(memory_space=pltpu.VMEM),
        scratch_shapes=[pltpu.VMEM((n_chips, r, c), BF16), pltpu.VMEM((n_chips, r, c), BF16)]
        + [pltpu.SemaphoreType.DMA((n_chips,))] * 3,
        compiler_params=pltpu.CompilerParams(vmem_limit_bytes=VMEM_LIMIT),
    )(part)


def _comm_only(comm, name):
    n = len(comm.inputs)

    def body(*refs):
        ins, outs, sems = refs[:n], refs[n:2 * n], refs[2 * n:]
        comm.start(ins, outs, sems)
        comm.finish(ins, outs, sems)

    any_spec = pl.BlockSpec(memory_space=pl.ANY)
    return pl.pallas_call(body, name=name, out_shape=comm.out_shape, in_specs=[any_spec] * n, out_specs=[any_spec] * n,
                          scratch_shapes=comm.scratch)(*comm.inputs)


def _host_call(body, *, name, grid, in_specs, out_specs, out_shape, scratch_shapes=(), semantics, args, comm=None):
    if comm is None:
        res = pl.pallas_call(body, name=name, grid=grid, in_specs=in_specs, out_specs=out_specs, out_shape=out_shape,
                             scratch_shapes=list(scratch_shapes), compiler_params=_params(*semantics))(*args)
        return res, []
    n_in, n_out, n_scr, n_c = len(in_specs), len(out_specs), len(scratch_shapes), len(comm.inputs)

    def with_comm(*refs):
        ins, refs = refs[:n_in], refs[n_in:]
        c_ins, refs = refs[:n_c], refs[n_c:]
        outs, refs = refs[:n_out], refs[n_out:]
        c_outs, refs = refs[:n_c], refs[n_c:]
        scr, sems = refs[:n_scr], refs[n_scr:]
        ids = [pl.program_id(a) for a in range(len(grid))]
        first = functools.reduce(jnp.logical_and, [i == 0 for i in ids])
        last = functools.reduce(jnp.logical_and, [i == g - 1 for i, g in zip(ids, grid)])

        @pl.when(first)
        def _():
            comm.start(c_ins, c_outs, sems)

        body(*ins, *outs, *scr)

        @pl.when(last)
        def _():
            comm.finish(c_ins, c_outs, sems)

    any_spec = pl.BlockSpec(memory_space=pl.ANY)
    res = pl.pallas_call(
        with_comm, name=name, grid=grid, in_specs=list(in_specs) + [any_spec] * n_c,
        out_specs=list(out_specs) + [any_spec] * n_c, out_shape=list(out_shape) + comm.out_shape,
        scratch_shapes=list(scratch_shapes) + comm.scratch,
        compiler_params=_params(*(["arbitrary"] * len(grid))))(*args, *comm.inputs)
    return res[:n_out], res[n_out:]


def _small_all_reduce(packs):
    n = len(packs)
    pieces = [p.shape[0] // N_DEV for p in packs]

    def body(*refs):
        p_refs, o_refs, lands = refs[:n], refs[n:2 * n], refs[2 * n:3 * n]
        s1, r1, s2, r2 = refs[3 * n:]
        _, me = _peer(0)

        def piece(t, ref, idx):
            return ref.at[pl.ds(pl.multiple_of(idx * pieces[t], 8), pieces[t]), :]

        def scatter(t, d):
            peer, idx = _peer(d)
            return pltpu.make_async_remote_copy(
                src_ref=piece(t, p_refs[t], idx), dst_ref=lands[t].at[d - 1], send_sem=s1.at[t, d - 1],
                recv_sem=r1.at[t, d - 1], device_id=peer, device_id_type=MESH_ID)

        def gather(t, d, from_idx):
            peer, _ = _peer(d)
            return pltpu.make_async_remote_copy(
                src_ref=piece(t, o_refs[t], from_idx), dst_ref=piece(t, o_refs[t], from_idx), send_sem=s2.at[t, d - 1],
                recv_sem=r2.at[t, d - 1], device_id=peer, device_id_type=MESH_ID)

        first = [[scatter(t, d) for d in range(1, N_DEV)] for t in range(n)]
        for cp in sum(first, []):
            cp.start()
        second = []
        for t in range(n):
            acc = piece(t, p_refs[t], me)[...]
            for d in range(1, N_DEV):
                first[t][d - 1].wait_recv()
                acc = acc + lands[t][d - 1]
            piece(t, o_refs[t], me)[...] = acc
            second += [gather(t, d, me) for d in range(1, N_DEV)]
            for cp in second[-(N_DEV - 1):]:
                cp.start()
        for t in range(n):
            for d in range(1, N_DEV):
                gather(t, d, _peer(d)[1]).wait_recv()
        for cp in sum(first, []) + second:
            cp.wait_send()

    vmem = pl.BlockSpec(memory_space=pltpu.VMEM)
    return pl.pallas_call(
        body, name="small_all_reduce",
        out_shape=[_sds(p.shape, F32) for p in packs], in_specs=[vmem] * n, out_specs=[vmem] * n,
        scratch_shapes=[pltpu.VMEM((N_DEV - 1, pc, p.shape[1]), F32) for pc, p in zip(pieces, packs)]
        + [pltpu.SemaphoreType.DMA((n, N_DEV - 1))] * 4,
    )(*packs)


def _norm_in_proj(x, g_mix, w_in, tm, comm):
    s = x.shape[0]

    def body(x_ref, g_ref, w_ref, z_ref, h_ref):
        xn, _ = _rms(x_ref[...])
        h = (xn * g_ref[...]).astype(BF16)
        h_ref[...] = h
        for j in range(N_SEG):
            cols = slice(j * D_MODEL, (j + 1) * D_MODEL)
            z_ref[:, cols] = _dot(h, w_ref[:, cols]).astype(BF16)

    return _host_call(
        body, name="norm_in_proj", grid=(s // tm,),
        in_specs=[pl.BlockSpec((tm, D_MODEL), lambda i: (i, 0)), pl.BlockSpec((1, D_MODEL), lambda i: (0, 0)),
                  _resident(w_in.shape)],
        out_specs=[pl.BlockSpec((tm, N_SEG * D_MODEL), lambda i: (i, 0)), pl.BlockSpec((tm, D_MODEL), lambda i: (i, 0))],
        out_shape=[_sds((s, N_SEG * D_MODEL), BF16), _sds((s, D_MODEL), BF16)],
        semantics=("parallel",), args=(x, g_mix, w_in), comm=comm)


def _gate_chunk(xr_ext, w4_ref, cb_ref, wa, wx, ba_ref, bx_ref, sp, ch):
    xc = cb_ref[...] + w4_ref[3:4, :] * _shift_dn(xr_ext, 0, ch)
    for k in range(3):
        xc = xc + w4_ref[k:k + 1, :] * _shift_dn(xr_ext, 3 - k, ch)
    xcb = xc.astype(BF16)
    ra = _sigmoid(_dot(xcb, wa) + ba_ref[...])
    ia = _sigmoid(_dot(xcb, wx) + bx_ref[...])
    log_a = (-LRU_C) * ra * sp
    return xc, ra, ia, log_a


def _conv3(q_ext, w3_ref, ch):
    y = w3_ref[2:3, :] * _shift_dn(q_ext, 0, ch)
    for k in range(2):
        y = y + w3_ref[k:k + 1, :] * _shift_dn(q_ext, 2 - k, ch)
    return y


def _z_block_specs(s, cw, segs):
    nb = D_MODEL // cw
    return [pl.BlockSpec((s, cw), functools.partial(lambda j, seg: (0, seg * nb + j), seg=seg)) for seg in segs]


def _mixers_fwd(z, w4, b4, w_a, b_a, w_x, b_x, lam, w3, ch, comm):
    s = z.shape[0]
    cw = RNN_BW
    n_chunks = s // ch

    def body(xr_ref, gr_ref, cb_ref, cc_ref, cx_ref, w4_ref, b4_ref, wa_ref, ba_ref, wx_ref, bx_ref, lam_ref, w3_ref,
             ya_ref, yb_ref, h_ref, a_scr, b_scr):
        sp = _softplus(-lam_ref[...])
        wa = wa_ref[0].astype(BF16)
        wx = wx_ref[0].astype(BF16)

        def gates(c, _):
            xc, _, ia, log_a = _gate_chunk(_ext_before(xr_ref, c, ch), w4_ref, b4_ref, wa, wx, ba_ref, bx_ref, sp, ch)
            rows = pl.ds(pl.multiple_of(c * ch, HALO), ch)
            a_scr[rows, :] = jnp.exp(log_a)
            b_scr[rows, :] = jnp.sqrt(-_expm1(2.0 * log_a)) * (ia * xc)
            q_ext = _ext_before(cc_ref, c, ch) * _ext_before(cx_ref, c, ch)
            yb_ref[rows, :] = (_rows(cb_ref, c, ch).astype(F32) * _conv3(q_ext, w3_ref, ch)).astype(BF16)
            return 0

        lax.fori_loop(0, n_chunks, gates, 0)
        _scan_fwd(a_scr, b_scr, h_ref, s, cw)

        def outputs(c, _):
            rows = pl.ds(pl.multiple_of(c * ch, HALO), ch)
            ya_ref[rows, :] = (h_ref[rows, :] * _gelu(gr_ref[rows, :].astype(F32))).astype(BF16)
            return 0

        lax.fori_loop(0, n_chunks, outputs, 0)

    col = lambda j: (0, j)
    vec = pl.BlockSpec((1, cw), col)
    sq = pl.BlockSpec((1, cw, cw), lambda j: (j, 0, 0))
    act = pl.BlockSpec((s, cw), col)
    return _host_call(
        body, name="mixers_fwd", grid=(D_MODEL // cw,),
        in_specs=_z_block_specs(s, cw, range(5)) + [pl.BlockSpec((4, cw), col), vec, sq, vec, sq, vec, vec,
                                                     pl.BlockSpec((3, cw), col)],
        out_specs=[act, act, act],
        out_shape=[_sds((s, D_MODEL), BF16), _sds((s, D_MODEL), BF16), _sds((s, D_MODEL), F32)],
        scratch_shapes=[pltpu.VMEM((s, cw), F32), pltpu.VMEM((s, cw), F32)],
        semantics=("parallel",), args=(z, z, z, z, z, w4, b4, w_a, b_a, w_x, b_x, lam, w3), comm=comm)


def _merge_out(ya, yb, z, x, w_pa, w_pb, w_out, g_ffn, tm):
    s = x.shape[0]

    def body(ya_ref, yb_ref, ga_ref, gb_ref, x_ref, wa_ref, wb_ref, wo_ref, g_ref, pa_ref, pb_ref, m_ref, x2_ref, h2_ref):
        pa = _dot(ya_ref[...], wa_ref[...])
        pb = _dot(yb_ref[...], wb_ref[...])
        pa_ref[...] = pa.astype(BF16)
        pb_ref[...] = pb.astype(BF16)
        m = (_sigmoid(ga_ref[...].astype(F32)) * pa + _sigmoid(gb_ref[...].astype(F32)) * pb).astype(BF16)
        m_ref[...] = m
        x2 = x_ref[...] + _dot(m, wo_ref[...])
        x2_ref[...] = x2
        xn, _ = _rms(x2)
        h2_ref[...] = (xn * g_ref[...]).astype(BF16)

    tile = pl.BlockSpec((tm, D_MODEL), lambda i: (i, 0))
    full = _resident((D_MODEL, D_MODEL))
    return pl.pallas_call(
        body, name="merge_out", grid=(s // tm,),
        in_specs=[tile, tile, pl.BlockSpec((tm, D_MODEL), lambda i: (i, 5)), pl.BlockSpec((tm, D_MODEL), lambda i: (i, 6)),
                  tile, full, full, full, pl.BlockSpec((1, D_MODEL), lambda i: (0, 0))],
        out_specs=[tile] * 5,
        out_shape=[_sds((s, D_MODEL), BF16)] * 3 + [_sds((s, D_MODEL), F32), _sds((s, D_MODEL), BF16)],
        compiler_params=_params("parallel"),
    )(ya, yb, z, z, x, w_pa, w_pb, w_out, g_ffn)


def _up_proj_act(h2, w_up_g, wc, bc, tm):
    s = h2.shape[0]
    bw = w_up_g.shape[2]
    per_half = N_DEV // 2

    def body(h_ref, w_ref, wc_ref, bc_ref, u_ref, f_ref, halo):
        @pl.when(pl.program_id(0) == 0)
        def _():
            halo[...] = jnp.zeros_like(halo)

        h = h_ref[...]
        for j in range(per_half):
            cols = slice(j * bw, (j + 1) * bw)
            conv = []
            for half in range(2):
                u = _dot(h, w_ref[half * per_half + j]).astype(BF16)
                u_ref[half, :, cols] = u
                u = u.astype(F32)
                ext = jnp.concatenate([halo[half, :, cols], u], axis=0)
                halo[half, :, cols] = u[tm - HALO:tm]
                w = wc_ref.at[half]
                acc = bc_ref[half, :, cols] + w[2:3, cols] * u
                for k in range(2):
                    acc = acc + w[k:k + 1, cols] * _shift_dn(ext, 2 - k, tm)
                conv.append(acc)
            f_ref[:, cols] = (_gelu(conv[0]) * conv[1]).astype(BF16)

    return pl.pallas_call(
        body, name="up_proj_act", grid=(s // tm,),
        in_specs=[pl.BlockSpec((tm, D_MODEL), lambda i: (i, 0)), _resident(w_up_g.shape), _resident(wc.shape),
                  _resident(bc.shape)],
        out_specs=[pl.BlockSpec((2, tm, D_FF), lambda i: (0, i, 0)), pl.BlockSpec((tm, D_FF), lambda i: (i, 0))],
        out_shape=[_sds((2, s, D_FF), BF16), _sds((s, D_FF), BF16)],
        scratch_shapes=[pltpu.VMEM((2, HALO, D_FF), F32)],
        compiler_params=_params("arbitrary"),
    )(h2, w_up_g, wc, bc)


def _ffn_conv(u_ref, w_ref, b_ref, half, c, ch):
    ext = _ext_before(u_ref.at[half], c, ch)
    w = w_ref.at[half]
    return b_ref[half] + _conv3(ext, w, ch), ext


def _head(x2, f, p, target, w_down, w_gate, w_pp, g_ple, g_final, tm):
    s = x2.shape[0]

    def body(x2_ref, f_ref, p_ref, t_ref, wd_ref, wg_ref, wp_ref, gp_ref, gf_ref,
             x3_ref, dgt_ref, de0_ref, dx3_ref, df_ref, sums_ref):
        @pl.when(pl.program_id(0) == 0)
        def _():
            sums_ref[...] = jnp.zeros_like(sums_ref)

        x3 = x2_ref[...] + _dot(f_ref[...], wd_ref[...])
        x3b = x3.astype(BF16)
        x3_ref[...] = x3b
        e0n, re = _rms(_dot(p_ref[...].astype(BF16), wp_ref[...]))
        e = e0n * gp_ref[...]
        sg = _sigmoid(_dot(x3b, wg_ref[...]))
        x4n, r4 = _rms(x3 + sg * e)
        diff = x4n * gf_ref[...] - t_ref[...]
        sums_ref[0:1, :] += jnp.sum(diff * diff, axis=0, keepdims=True)
        dy = diff * (1.0 / D_MODEL)
        sums_ref[1:2, :] += jnp.sum(dy * x4n, axis=0, keepdims=True)
        dx4 = _rms_bwd(dy * gf_ref[...], x4n, r4)
        de = dx4 * sg
        dgt = ((dx4 * e) * (sg * (1.0 - sg))).astype(BF16)
        dgt_ref[...] = dgt
        sums_ref[2:3, :] += jnp.sum(de * e0n, axis=0, keepdims=True)
        de0_ref[...] = _rms_bwd(de * gp_ref[...], e0n, re).astype(BF16)
        dx3 = dx4 + _dot_nt(dgt, wg_ref[...])
        dx3_ref[...] = dx3
        df_ref[...] = _dot_nt(dx3.astype(BF16), wd_ref[...]).astype(BF16)

    tile = pl.BlockSpec((tm, D_MODEL), lambda i: (i, 0))
    vec = pl.BlockSpec((1, D_MODEL), lambda i: (0, 0))
    wide = pl.BlockSpec((tm, D_FF), lambda i: (i, 0))
    return pl.pallas_call(
        body, name="loss_head", grid=(s // tm,),
        in_specs=[tile, wide, pl.BlockSpec((tm, D_PLE), lambda i: (i, 0)), tile,
                  _resident((D_FF, D_MODEL)), _resident((D_MODEL, D_MODEL)), _resident((D_PLE, D_MODEL)), vec, vec],
        out_specs=[tile, tile, tile, tile, wide, pl.BlockSpec((8, D_MODEL), lambda i: (0, 0))],
        out_shape=[_sds((s, D_MODEL), BF16)] * 3 + [_sds((s, D_MODEL), F32), _sds((s, D_FF), BF16),
                                                    _sds((8, D_MODEL), F32)],
        compiler_params=_params("arbitrary"),
    )(x2, f, p, target, w_down, w_gate, w_pp, g_ple, g_final)


def _ffn_act_bwd(u0, df, wc, bc, cw, ch, comm):
    s = u0.shape[1]
    n_chunks = s // ch

    def body(u_ref, df_ref, w_ref, b_ref, du0_ref, dw_ref, db_ref, du_scr):
        dw_ref[...] = jnp.zeros_like(dw_ref)
        db_ref[...] = jnp.zeros_like(db_ref)

        def first(c, _):
            ug, ext_g = _ffn_conv(u_ref, w_ref, b_ref, 0, c, ch)
            uv, ext_v = _ffn_conv(u_ref, w_ref, b_ref, 1, c, ch)
            gel, dgel = _gelu_and_grad(ug)
            d = _rows(df_ref, c, ch).astype(F32)
            rows = pl.ds(pl.multiple_of(c * ch, HALO), ch)
            for half, du, ext in ((0, d * uv * dgel, ext_g), (1, d * gel, ext_v)):
                du_scr[half, rows, :] = du
                db_ref[half] += jnp.sum(du, axis=0, keepdims=True)
                for k in range(3):
                    dw_ref[half, k:k + 1, :] += jnp.sum(du * _shift_dn(ext, 2 - k, ch), axis=0, keepdims=True)
            return 0

        lax.fori_loop(0, n_chunks, first, 0)

        def second(c, _):
            rows = pl.ds(pl.multiple_of(c * ch, HALO), ch)
            for half in range(2):
                ext = _ext_after(du_scr.at[half], c, ch, n_chunks)
                w = w_ref.at[half]
                acc = w[2:3, :] * _shift_up(ext, 0, ch)
                for k in range(2):
                    acc = acc + w[k:k + 1, :] * _shift_up(ext, 2 - k, ch)
                du0_ref[half, rows, :] = acc.astype(BF16)
            return 0

        lax.fori_loop(0, n_chunks, second, 0)

    blk = lambda r: pl.BlockSpec((2, r, cw), lambda j: (0, 0, j))
    return _host_call(
        body, name="ffn_act_bwd", grid=(D_FF // cw,),
        in_specs=[blk(s), pl.BlockSpec((s, cw), lambda j: (0, j)), blk(3), blk(1)],
        out_specs=[blk(s), blk(3), blk(1)],
        out_shape=[_sds((2, s, D_FF), BF16), _sds((2, 3, D_FF), F32), _sds((2, 1, D_FF), F32)],
        scratch_shapes=[pltpu.VMEM((2, s, cw), F32)],
        semantics=("parallel",), args=(u0, df, wc, bc), comm=comm)


def _up_bwd_merge_bwd(du0, w_up_g, x2, dx3, z, pa, pb, w_out, w_pa, w_pb, g_ffn, tm, comm):
    s = x2.shape[0]
    bw = w_up_g.shape[2]
    per_half = N_DEV // 2

    def body(du_ref, wu_ref, x2_ref, dx3_ref, ga_ref, gb_ref, pa_ref, pb_ref, wo_ref, wa_ref, wb_ref, g_ref,
             dx2_ref, dzg_ref, dpa_ref, dpb_ref, dya_ref, dyb_ref, sums_ref):
        @pl.when(pl.program_id(0) == 0)
        def _():
            sums_ref[...] = jnp.zeros_like(sums_ref)

        dh2 = None
        for j in range(N_DEV):
            cols = slice((j % per_half) * bw, (j % per_half + 1) * bw)
            term = _dot_nt(du_ref[j // per_half, :, cols], wu_ref[j])
            dh2 = term if dh2 is None else dh2 + term
        x2n, r2 = _rms(x2_ref[...])
        sums_ref[0:1, :] += jnp.sum(dh2 * x2n, axis=0, keepdims=True)
        dx2 = dx3_ref[...] + _rms_bwd(dh2 * g_ref[...], x2n, r2)
        dx2_ref[...] = dx2
        dm = _dot_nt(dx2.astype(BF16), wo_ref[...])
        sa = _sigmoid(ga_ref[...].astype(F32))
        sb = _sigmoid(gb_ref[...].astype(F32))
        dzg_ref[0] = (dm * pa_ref[...].astype(F32) * (sa * (1.0 - sa))).astype(BF16)
        dzg_ref[1] = (dm * pb_ref[...].astype(F32) * (sb * (1.0 - sb))).astype(BF16)
        dpa = (dm * sa).astype(BF16)
        dpb = (dm * sb).astype(BF16)
        dpa_ref[...] = dpa
        dpb_ref[...] = dpb
        dya_ref[...] = _dot_nt(dpa, wa_ref[...]).astype(BF16)
        dyb_ref[...] = _dot_nt(dpb, wb_ref[...]).astype(BF16)

    tile = pl.BlockSpec((tm, D_MODEL), lambda i: (i, 0))
    full = _resident((D_MODEL, D_MODEL))
    return _host_call(
        body, name="up_bwd_merge_bwd", grid=(s // tm,),
        in_specs=[pl.BlockSpec((2, tm, D_FF), lambda i: (0, i, 0)), _resident(w_up_g.shape),
                  tile, tile, pl.BlockSpec((tm, D_MODEL), lambda i: (i, 5)),
                  pl.BlockSpec((tm, D_MODEL), lambda i: (i, 6)), tile, tile, full, full, full,
                  pl.BlockSpec((1, D_MODEL), lambda i: (0, 0))],
        out_specs=[tile, pl.BlockSpec((2, tm, D_MODEL), lambda i: (0, i, 0)), tile, tile, tile, tile,
                   pl.BlockSpec((8, D_MODEL), lambda i: (0, 0))],
        out_shape=[_sds((s, D_MODEL), F32), _sds((2, s, D_MODEL), BF16)] + [_sds((s, D_MODEL), BF16)] * 4
        + [_sds((8, D_MODEL), F32)],
        semantics=("arbitrary",), args=(du0, w_up_g, x2, dx3, z, z, pa, pb, w_out, w_pa, w_pb, g_ffn),
        comm=comm)


def _mixers_bwd(z, h, dya, dyb, w4, b4, w_a, b_a, w_x, b_x, lam, w3, ch, comm):
    s = z.shape[0]
    cw = RNN_BW
    n_chunks = s // ch

    def body(xr_ref, gr_ref, cb_ref, cc_ref, cx_ref, h_ref, dya_ref, dyb_ref, w4_ref, b4_ref, wa_ref, ba_ref, wx_ref,
             bx_ref, lam_ref, w3_ref,
             dz_ref, dw4_ref, db4_ref, dwa_ref, dba_ref, dwx_ref, dbx_ref, dlam_ref, dw3_ref,
             a_scr, g_scr, q_scr, xc_scr, ra_scr, ia_scr, dh_scr):
        sp = _softplus(-lam_ref[...])
        wa = wa_ref[0].astype(BF16)
        wx = wx_ref[0].astype(BF16)
        for ref in (dw4_ref, db4_ref, dwa_ref, dba_ref, dwx_ref, dbx_ref, dlam_ref, dw3_ref):
            ref[...] = jnp.zeros_like(ref)

        def first(c, _):
            rows = pl.ds(pl.multiple_of(c * ch, HALO), ch)
            xc, ra, ia, log_a = _gate_chunk(_ext_before(xr_ref, c, ch), w4_ref, b4_ref, wa, wx, ba_ref, bx_ref, sp, ch)
            xc_scr[rows, :] = xc
            ra_scr[rows, :] = ra
            ia_scr[rows, :] = ia
            a_scr[rows, :] = jnp.exp(log_a)
            gel, dgel = _gelu_and_grad(gr_ref[rows, :].astype(F32))
            dya_c = dya_ref[rows, :].astype(F32)
            g_scr[rows, :] = dya_c * gel
            dz_ref[1, rows, :] = (dya_c * h_ref[rows, :] * dgel).astype(BF16)
            q_ext = _ext_before(cc_ref, c, ch) * _ext_before(cx_ref, c, ch)
            dyb_c = dyb_ref[rows, :].astype(F32)
            dz_ref[2, rows, :] = (dyb_c * _conv3(q_ext, w3_ref, ch)).astype(BF16)
            dyq = dyb_c * cb_ref[rows, :].astype(F32)
            q_scr[rows, :] = dyq
            for k in range(3):
                dw3_ref[k:k + 1, :] += jnp.sum(dyq * _shift_dn(q_ext, 2 - k, ch), axis=0, keepdims=True)
            return 0

        lax.fori_loop(0, n_chunks, first, 0)
        _scan_rev(a_scr, g_scr, dh_scr, s, cw)

        def second(c, _):
            rows = pl.ds(pl.multiple_of(c * ch, HALO), ch)
            xc, ra, ia, a = xc_scr[rows, :], ra_scr[rows, :], ia_scr[rows, :], a_scr[rows, :]
            mult = jnp.sqrt(-_expm1((-2.0 * LRU_C) * ra * sp))
            dh = dh_scr[rows, :]
            h_prev = _shift_dn(_ext_before(h_ref, c, ch), 1, ch)
            dlog_a = dh * h_prev * a - (dh * ia * xc) * (a * a) / mult
            dlam_ref[...] += jnp.sum(dlog_a * ra, axis=0, keepdims=True)
            dpre_a = (dlog_a * ((-LRU_C) * sp)) * (ra * (1.0 - ra))
            dpre_x = (dh * mult * xc) * (ia * (1.0 - ia))
            dba_ref[...] += jnp.sum(dpre_a, axis=0, keepdims=True)
            dbx_ref[...] += jnp.sum(dpre_x, axis=0, keepdims=True)
            xcb = xc.astype(BF16)
            dpa_b = dpre_a.astype(BF16)
            dpx_b = dpre_x.astype(BF16)
            dwa_ref[0] += _dot_tn(xcb, dpa_b)
            dwx_ref[0] += _dot_tn(xcb, dpx_b)
            a_scr[rows, :] = dh * mult * ia + _dot_nt(dpa_b, wa) + _dot_nt(dpx_b, wx)
            return 0

        lax.fori_loop(0, n_chunks, second, 0)
        dlam_ref[...] = dlam_ref[...] * (LRU_C * _sigmoid(-lam_ref[...]))

        def third(c, _):
            rows = pl.ds(pl.multiple_of(c * ch, HALO), ch)
            dxc_ext = _ext_after(a_scr, c, ch, n_chunks)
            dxc = _shift_up(dxc_ext, 0, ch)
            xr_ext = _ext_before(xr_ref, c, ch)
            db4_ref[...] += jnp.sum(dxc, axis=0, keepdims=True)
            dxr = w4_ref[3:4, :] * dxc
            dw4_ref[3:4, :] += jnp.sum(dxc * _shift_dn(xr_ext, 0, ch), axis=0, keepdims=True)
            for k in range(3):
                dxr = dxr + w4_ref[k:k + 1, :] * _shift_up(dxc_ext, 3 - k, ch)
                dw4_ref[k:k + 1, :] += jnp.sum(dxc * _shift_dn(xr_ext, 3 - k, ch), axis=0, keepdims=True)
            dz_ref[0, rows, :] = dxr.astype(BF16)
            dyq_ext = _ext_after(q_scr, c, ch, n_chunks)
            dq = w3_ref[2:3, :] * _shift_up(dyq_ext, 0, ch)
            for k in range(2):
                dq = dq + w3_ref[k:k + 1, :] * _shift_up(dyq_ext, 2 - k, ch)
            dz_ref[3, rows, :] = (dq * cx_ref[rows, :].astype(F32)).astype(BF16)
            dz_ref[4, rows, :] = (dq * cc_ref[rows, :].astype(F32)).astype(BF16)
            return 0

        lax.fori_loop(0, n_chunks, third, 0)

    col = lambda j: (0, j)
    vec = pl.BlockSpec((1, cw), col)
    sq = pl.BlockSpec((1, cw, cw), lambda j: (j, 0, 0))
    act = pl.BlockSpec((s, cw), col)
    w4s, w3s = pl.BlockSpec((4, cw), col), pl.BlockSpec((3, cw), col)
    vec_shape = _sds((1, D_MODEL), F32)
    sq_shape = _sds((D_MODEL // cw, cw, cw), F32)
    return _host_call(
        body, name="mixers_bwd", grid=(D_MODEL // cw,),
        in_specs=_z_block_specs(s, cw, range(5)) + [act, act, act, w4s, vec, sq, vec, sq, vec, vec, w3s],
        out_specs=[pl.BlockSpec((5, s, cw), lambda j: (0, 0, j)), w4s, vec, sq, vec, sq, vec, vec, w3s],
        out_shape=[_sds((5, s, D_MODEL), BF16), _sds((4, D_MODEL), F32), vec_shape, sq_shape, vec_shape, sq_shape,
                   vec_shape, vec_shape, _sds((3, D_MODEL), F32)],
        scratch_shapes=[pltpu.VMEM((s, cw), F32)] * 7,
        semantics=("parallel",), args=(z, z, z, z, z, h, dya, dyb, w4, b4, w_a, b_a, w_x, b_x, lam, w3), comm=comm)


def _in_proj_bwd(dz5, dzg, w_in, x, dx2, g_mix, tm, comm):
    s = x.shape[0]

    def body(d5_ref, dg_ref, w_ref, x_ref, dx2_ref, g_ref, dx_ref, sums_ref):
        @pl.when(pl.program_id(0) == 0)
        def _():
            sums_ref[...] = jnp.zeros_like(sums_ref)

        dh1 = None
        for k in range(N_SEG):
            d = d5_ref[k] if k < 5 else dg_ref[k - 5]
            term = _dot_nt(d, w_ref[:, k * D_MODEL:(k + 1) * D_MODEL])
            dh1 = term if dh1 is None else dh1 + term
        xn, r1 = _rms(x_ref[...])
        sums_ref[0:1, :] += jnp.sum(dh1 * xn, axis=0, keepdims=True)
        dx_ref[...] = dx2_ref[...] + _rms_bwd(dh1 * g_ref[...], xn, r1)

    tile = pl.BlockSpec((tm, D_MODEL), lambda i: (i, 0))
    return _host_call(
        body, name="in_proj_bwd", grid=(s // tm,),
        in_specs=[pl.BlockSpec((5, tm, D_MODEL), lambda i: (0, i, 0)), pl.BlockSpec((2, tm, D_MODEL), lambda i: (0, i, 0)),
                  _resident(w_in.shape), tile, tile, pl.BlockSpec((1, D_MODEL), lambda i: (0, 0))],
        out_specs=[tile, pl.BlockSpec((8, D_MODEL), lambda i: (0, 0))],
        out_shape=[_sds((s, D_MODEL), F32), _sds((8, D_MODEL), F32)],
        semantics=("arbitrary",), args=(dz5, dzg, w_in, x, dx2, g_mix), comm=comm)


def _weight_grad(a, b, b_spec, out_spec, out_shape, *, n_blocks, chunks, width, tm, tk, name):
    s, m = a.shape
    nk = s // tk
    b_chunked = len([d for d in b_spec.block_shape if d is not None]) == 3
    blocks_out = len([d for d in out_spec.block_shape if d is not None]) == 3

    def body(a_ref, b_ref, o_ref, acc):
        k = pl.program_id(2)

        @pl.when(k == 0)
        def _():
            acc[...] = jnp.zeros_like(acc)

        at = a_ref[...].astype(BF16).T
        for j in range(chunks):
            cols = slice(j * width, (j + 1) * width)
            acc[:, cols] += _dot(at, (b_ref[j] if b_chunked else b_ref[:, cols]).astype(BF16))

        @pl.when(k == nk - 1)
        def _():
            if blocks_out:
                for j in range(chunks):
                    o_ref[j] = acc[:, j * width:(j + 1) * width].astype(o_ref.dtype)
            else:
                o_ref[...] = acc[...].astype(o_ref.dtype)

    return pl.pallas_call(
        body, name=name, grid=(m // tm, n_blocks, nk),
        in_specs=[pl.BlockSpec((tk, tm), lambda i, j, k: (k, i)), b_spec], out_specs=out_spec, out_shape=out_shape,
        scratch_shapes=[pltpu.VMEM((tm, chunks * width), F32)],
        compiler_params=_params("parallel", "parallel", "arbitrary"),
    )(a, b)


def _wgrad_2d(a, b, name, tk):
    m, n = a.shape[1], b.shape[1]
    tm = m
    return _weight_grad(a, b, pl.BlockSpec((tk, n), lambda i, j, k: (k, 0)), pl.BlockSpec((tm, n), lambda i, j, k: (i, 0)),
                        _sds((m, n), BF16), n_blocks=1, chunks=1, width=n, tm=tm, tk=tk, name=name)


def _wgrad_segments(a, b3, name, tm, tk):
    m = a.shape[1]
    g = b3.shape[0]
    return _weight_grad(a, b3, pl.BlockSpec((g, tk, D_MODEL), lambda i, j, k: (0, k, 0)),
                        pl.BlockSpec((tm, g * D_MODEL), lambda i, j, k: (i, 0)), _sds((m, g * D_MODEL), BF16),
                        n_blocks=1, chunks=g, width=D_MODEL, tm=tm, tk=tk, name=name)


def _wgrad_up(h2, du0, bw, tk):
    per_half = N_DEV // 2
    return _weight_grad(h2, du0, pl.BlockSpec((None, tk, D_FF), lambda i, j, k: (j, k, 0)),
                        pl.BlockSpec((per_half, D_MODEL, bw), lambda i, j, k: (j, 0, 0)), _sds((N_DEV, D_MODEL, bw), BF16),
                        n_blocks=2, chunks=per_half, width=bw, tm=D_MODEL, tk=tk, name="wgrad_up")


def _adam_math(w, g, m, v):
    m = ADAM_B1 * m + (1.0 - ADAM_B1) * g
    v = ADAM_B2 * v + (1.0 - ADAM_B2) * jnp.square(g)
    m_hat = m / (1.0 - ADAM_B1 ** ADAM_STEP)
    v_hat = v / (1.0 - ADAM_B2 ** ADAM_STEP)
    delta = -ADAM_LR * (m_hat / (jnp.sqrt(v_hat) + ADAM_EPS) + ADAM_WD * w)
    return delta, m, v


def _adam_shard(parts, w, m, v, name):
    r, c = w.shape
    tr = min(r, 128)
    n_arrays = len(parts)

    def body(*refs):
        p_refs = refs[:n_arrays]
        w_ref, m_ref, v_ref, g_ref, d_ref, nm_ref, nv_ref = refs[n_arrays:]
        g = None
        for p_ref, (arr, first) in zip(p_refs, parts):
            for k in range(first, arr.shape[0]):
                term = p_ref[k].astype(F32)
                g = term if g is None else g + term
        g_ref[...] = g
        d_ref[...], nm_ref[...], nv_ref[...] = _adam_math(w_ref[...], g, m_ref[...], v_ref[...])

    tile = pl.BlockSpec((tr, c), lambda i: (i, 0))
    return pl.pallas_call(
        body, name=name, grid=(r // tr,),
        in_specs=[pl.BlockSpec((arr.shape[0], tr, c), lambda i: (0, i, 0)) for arr, _ in parts] + [tile, tile, tile],
        out_specs=[tile] * 4, out_shape=[_sds((r, c), F32)] * 4,
        compiler_params=_params("parallel"),
    )(*[arr for arr, _ in parts], w, m, v)


def _adam_small(grads, ws, ms, vs, name):
    n = len(ws)
    g_arrays, g_slot = [], []
    for g in grads:
        arr = g[0] if isinstance(g, tuple) else g
        if not any(arr is a for a in g_arrays):
            g_arrays.append(arr)
        g_slot.append([arr is a for a in g_arrays].index(True))
    n_g = len(g_arrays)

    def body(*refs):
        g_refs, refs = refs[:n_g], refs[n_g:]
        w_refs, m_refs, v_refs, outs = refs[:n], refs[n:2 * n], refs[2 * n:3 * n], refs[3 * n:]
        for i in range(n):
            og, od, om, ov = outs[4 * i:4 * i + 4]
            g_ref = g_refs[g_slot[i]]
            if isinstance(grads[i], tuple):
                row = grads[i][1]
                for j in range(ws[i].shape[1] // D_MODEL):
                    cols = slice(j * D_MODEL, (j + 1) * D_MODEL)
                    g = g_ref[row + j:row + j + 1, :]
                    og[:, cols] = g
                    od[:, cols], om[:, cols], ov[:, cols] = _adam_math(w_refs[i][:, cols], g, m_refs[i][:, cols],
                                                                       v_refs[i][:, cols])
            else:
                g = g_ref[...]
                og[...] = g
                od[...], om[...], ov[...] = _adam_math(w_refs[i][...], g, m_refs[i][...], v_refs[i][...])

    vmem = pl.BlockSpec(memory_space=pltpu.VMEM)
    res = pl.pallas_call(
        body, name=name, in_specs=[vmem] * (n_g + 3 * n), out_specs=[vmem] * (4 * n),
        out_shape=[_sds(a.shape, F32) for a in ws for _ in range(4)],
    )(*g_arrays, *ws, *ms, *vs)
    return [res[4 * i:4 * i + 4] for i in range(n)]


def _pack_conv_shard(rnn, sc, ffn):
    top = jnp.concatenate([rnn[:3], sc, ffn], axis=1)
    row3 = jnp.concatenate([rnn[3:4], jnp.zeros((1, D_MODEL - RNN_BW), F32)], axis=1)
    return jnp.concatenate([top, row3, jnp.zeros((4, D_MODEL), F32)], axis=0)


_VECTORS = ("g_mix", "rnn_conv_b", "b_rg_a", "b_rg_x", "lru_lambda", "g_ffn", "g_ple", "g_final", "ffn_conv_b")
VEC_ROWS = 64
ROW_LOSS, ROW_RNN_CONV, ROW_SC_CONV, ROW_FFN_CONV = 14, 16, 20, 24


_SHARDED_BIG = ("w_in", "w_proj_a", "w_proj_b", "w_out", "w_up", "w_down", "w_ple_gate", "w_ple_proj")
_NAMES = ("g_mix", "w_in", "rnn_conv_w", "rnn_conv_b", "w_rg_a", "b_rg_a", "w_rg_x", "b_rg_x", "lru_lambda", "sc_conv_w",
          "w_proj_a", "w_proj_b", "w_out", "g_ffn", "w_up", "ffn_conv_w", "ffn_conv_b", "w_down", "w_ple_gate",
          "w_ple_proj", "g_ple", "g_final")


def _step(x, p, target, w, m, v):
    s = x.shape[0]
    tm = min(s, 256)
    tm_wide = min(s, 512)
    ch = min(s, 256)
    my_index = 4 * lax.axis_index("x") + 2 * lax.axis_index("y") + lax.axis_index("c")

    big = {n: w[n][0] for n in _SHARDED_BIG}
    conv_shard = _pack_conv_shard(w["rnn_conv_w"][0], w["sc_conv_w"][0], w["ffn_conv_w"][0])
    shards = {n: big[n].astype(BF16) for n in _SHARDED_BIG}
    w_in_g, conv_all = _comm_only(_gather_comm([shards["w_in"], conv_shard]), "in_proj_weight_gather")
    w_in = jnp.transpose(w_in_g, (1, 0, 2)).reshape(D_MODEL, N_SEG * D_MODEL)
    w4 = jnp.transpose(jnp.concatenate([conv_all[:, :3, :RNN_BW], conv_all[:, 3:4, :RNN_BW]], axis=1),
                       (1, 0, 2)).reshape(4, D_MODEL)
    w3 = jnp.transpose(conv_all[:, :3, RNN_BW:2 * RNN_BW], (1, 0, 2)).reshape(3, D_MODEL)
    wc = jnp.transpose(conv_all[:, :3, 2 * RNN_BW:], (1, 0, 2)).reshape(3, 2, D_FF).transpose(1, 0, 2)
    bc = w["ffn_conv_b"].reshape(2, 1, D_FF)
    b4, b_a, b_x, lam = w["rnn_conv_b"], w["b_rg_a"], w["b_rg_x"], w["lru_lambda"]
    w_a, w_x = w["w_rg_a"][0], w["w_rg_x"][0]
    g_final = w["g_final"].reshape(1, D_MODEL)

    tk = min(s, 512)
    received = {}

    comm = _gather_comm([shards[n] for n in ("w_proj_a", "w_proj_b", "w_out", "w_down", "w_ple_gate", "w_ple_proj")])
    (z, h1), (w_pa, w_pb, w_out, w_down, w_gate, w_pp) = _norm_in_proj(x, w["g_mix"], w_in, tm_wide, comm)
    w_pa, w_pb, w_out, w_gate = (a.reshape(D_MODEL, D_MODEL) for a in (w_pa, w_pb, w_out, w_gate))
    w_down = w_down.reshape(D_FF, D_MODEL)
    w_pp = jnp.transpose(w_pp, (1, 0, 2)).reshape(D_PLE, D_MODEL)
    (ya, yb, h), (w_up_g,) = _mixers_fwd(z, w4, b4, w_a, b_a, w_x, b_x, lam, w3, ch, _gather_comm([shards["w_up"]]))
    pa, pb, mm, x2, h2 = _merge_out(ya, yb, z, x, w_pa, w_pb, w_out, w["g_ffn"], tm)
    u0, f = _up_proj_act(h2, w_up_g, wc, bc, tm)
    x3, dgt, de0, dx3, df, head_sums = _head(x2, f, p, target, w_down, w_gate, w_pp, w["g_ple"], g_final, tm)
    parts = [_wgrad_2d(x3, dgt, "wgrad_ple_gate", tk).reshape(N_DEV, RNN_BW, D_MODEL),
             jnp.transpose(_wgrad_2d(p, de0, "wgrad_ple_proj", tk).reshape(D_PLE, N_DEV, RNN_BW), (1, 0, 2)),
             _wgrad_2d(f, dx3, "wgrad_down", tk).reshape(N_DEV, D_FF // N_DEV, D_MODEL)]
    (du0, dwc, dbc), got = _ffn_act_bwd(u0, df, wc, bc, 256, ch, _exchange_comm(parts))
    received.update({n: [(g, 0)] for n, g in zip(("w_ple_gate", "w_ple_proj", "w_down"), got)})
    dw_up = _wgrad_up(h2, du0, w_up_g.shape[2], tk)
    (dx2, dzg, dpa, dpb, dya, dyb, ffn_sums), got_up = _up_bwd_merge_bwd(
        du0, w_up_g, x2, dx3, z, pa, pb, w_out, w_pa, w_pb, w["g_ffn"], tm, _exchange_comm([dw_up], (1, 2, 3)))
    parts = [_wgrad_2d(mm, dx2, "wgrad_out", tk).reshape(N_DEV, RNN_BW, D_MODEL),
             _wgrad_2d(ya, dpa, "wgrad_proj_a", tk).reshape(N_DEV, RNN_BW, D_MODEL),
             _wgrad_2d(yb, dpb, "wgrad_proj_b", tk).reshape(N_DEV, RNN_BW, D_MODEL)]
    comm = _exchange_comm(parts)
    rest = _exchange_comm([dw_up], (4, 5, 6, 7))
    (dz5, dw4, db4, dwa, dba, dwx, dbx, dlam, dw3), got = _mixers_bwd(z, h, dya, dyb, w4, b4, w_a, b_a, w_x, b_x, lam, w3,
                                                                       ch, _join_comms(comm, rest))
    received.update({n: [(g, 0)] for n, g in zip(("w_out", "w_proj_a", "w_proj_b"), got[:3])})
    received["w_up"] = [(got_up[0], 0), (got[3], 1)]
    dw_in = jnp.concatenate([_wgrad_segments(h1, dz5, "wgrad_in_mix", 512, tk),
                             _wgrad_segments(h1, dzg, "wgrad_in_gate", D_MODEL, tk)], axis=1)
    bw_in = N_SEG * D_MODEL // N_DEV
    chip_sum = _pair_reduce(jnp.transpose(dw_in.reshape(D_MODEL, N_DEV, bw_in), (1, 0, 2)), "in_proj_grad_pair_reduce")
    (grad_x, mix_sums), got = _in_proj_bwd(dz5, dzg, w_in, x, dx2, w["g_mix"], tm, _chip_exchange_comm([chip_sum]))
    received["w_in"] = [(got[0], 0)]

    zero_row = jnp.zeros((1, D_MODEL), F32)
    vec_pack = jnp.concatenate(
        [mix_sums[0:1], db4, dba, dbx, dlam, ffn_sums[0:1], head_sums[2:3], head_sums[1:2], dbc.reshape(6, D_MODEL),
         head_sums[0:1], zero_row, dw4, dw3, zero_row, jnp.transpose(dwc, (1, 0, 2)).reshape(18, D_MODEL),
         jnp.zeros((VEC_ROWS - ROW_FFN_CONV - 18, D_MODEL), F32)], axis=0)
    gate_pack = jnp.concatenate([dwa.reshape(D_MODEL, RNN_BW), dwx.reshape(D_MODEL, RNN_BW)], axis=0)
    vec_total, gate_total = _small_all_reduce([vec_pack, gate_pack])
    loss = jnp.sum(vec_total[ROW_LOSS]) * (0.5 / D_MODEL)

    out = {}
    for n in _SHARDED_BIG:
        res = _adam_shard(received[n], big[n], m[n][0], v[n][0], "adam_" + n)
        out[n] = [r[None] for r in res]

    def flat(t, n):
        return t[n].reshape(1, -1)

    res = _adam_small([(vec_total, i) for i in range(len(_VECTORS))],
                      [flat(w, n) for n in _VECTORS], [flat(m, n) for n in _VECTORS], [flat(v, n) for n in _VECTORS],
                      "adam_vectors")
    for n, r in zip(_VECTORS, res):
        out[n] = [a.reshape(w[n].shape) for a in r]
    gates = ("w_rg_a", "w_rg_x")
    res = _adam_small([gate_total[:D_MODEL], gate_total[D_MODEL:]], *[[t[n].reshape(D_MODEL, RNN_BW) for n in gates]
                                                                      for t in (w, m, v)], "adam_gate_maps")
    for n, r in zip(gates, res):
        out[n] = [a.reshape(w[n].shape) for a in r]
    convs = ("rnn_conv_w", "sc_conv_w", "ffn_conv_w")
    bw_ffn = 2 * D_FF // N_DEV
    g_conv = [lax.dynamic_slice(vec_total[ROW_RNN_CONV:ROW_RNN_CONV + 4], (0, my_index * RNN_BW), (4, RNN_BW)),
              lax.dynamic_slice(vec_total[ROW_SC_CONV:ROW_SC_CONV + 3], (0, my_index * RNN_BW), (3, RNN_BW)),
              lax.dynamic_slice(vec_total[ROW_FFN_CONV:ROW_FFN_CONV + 18].reshape(3, 2 * D_FF), (0, my_index * bw_ffn),
                                (3, bw_ffn))]
    res = _adam_small(g_conv, *[[t[n][0] for n in convs] for t in (w, m, v)], "adam_conv")
    for n, r in zip(convs, res):
        out[n] = [a[None] for a in r]

    return (loss, grad_x[None]) + tuple(out[n][k] for k in range(4) for n in _NAMES)


def kernel(x, p, g_mix, w_in, rnn_conv_w, rnn_conv_b, w_rg_a, b_rg_a, w_rg_x, b_rg_x, lru_lambda, sc_conv_w, w_proj_a, w_proj_b, w_out, g_ffn, w_up, ffn_conv_w, ffn_conv_b, w_down, w_ple_gate, w_ple_proj, g_ple, g_final, loss_target, m_g_mix, m_w_in, m_rnn_conv_w, m_rnn_conv_b, m_w_rg_a, m_b_rg_a, m_w_rg_x, m_b_rg_x, m_lru_lambda, m_sc_conv_w, m_w_proj_a, m_w_proj_b, m_w_out, m_g_ffn, m_w_up, m_ffn_conv_w, m_ffn_conv_b, m_w_down, m_w_ple_gate, m_w_ple_proj, m_g_ple, m_g_final, v_g_mix, v_w_in, v_rnn_conv_w, v_rnn_conv_b, v_w_rg_a, v_b_rg_a, v_w_rg_x, v_b_rg_x, v_lru_lambda, v_sc_conv_w, v_w_proj_a, v_w_proj_b, v_w_out, v_g_ffn, v_w_up, v_ffn_conv_w, v_ffn_conv_b, v_w_down, v_w_ple_gate, v_w_ple_proj, v_g_ple, v_g_final):
    given = dict(locals())
    w = {n: given[n] for n in _NAMES}
    m = {n: given["m_" + n] for n in _NAMES}
    v = {n: given["v_" + n] for n in _NAMES}
    return _step(x[0], p[0, 0], loss_target[0], w, m, v)
```

```python
import functools

import jax
import jax.numpy as jnp
from jax import lax
from jax.experimental import pallas as pl
from jax.experimental.pallas import tpu as pltpu

F32 = jnp.float32
BF16 = jnp.bfloat16
MESH_ID = pl.DeviceIdType.MESH

N_DEV = 8
D_MODEL = 1024
D_PLE = 256
RNN_BW = 128
N_SEG = 7
D_FF = 3072
LRU_C = 8.0
EPS = 1e-6
ADAM_LR = 0.001
ADAM_B1 = 0.9
ADAM_B2 = 0.999
ADAM_EPS = 1e-08
ADAM_WD = 0.01
ADAM_STEP = 10

HALO = 16
SCAN_UNROLL = 4
VMEM_LIMIT = 56 * 1024 * 1024


def _dot(a, b):
    return jnp.dot(a, b, preferred_element_type=F32)


def _dot_nt(a, b):
    return lax.dot_general(a, b, (((1,), (1,)), ((), ())), preferred_element_type=F32)


def _dot_tn(a, b):
    return lax.dot_general(a, b, (((0,), (0,)), ((), ())), preferred_element_type=F32)


def _sigmoid(x):
    return 0.5 * jnp.tanh(0.5 * x) + 0.5


_GELU_C = 0.7978845608028654
_GELU_K = 0.044715


def _gelu(x):
    return 0.5 * x * (1.0 + jnp.tanh(_GELU_C * (x + _GELU_K * (x * x * x))))


def _gelu_and_grad(x):
    x2 = x * x
    t = jnp.tanh(_GELU_C * (x + _GELU_K * (x * x2)))
    g = 0.5 * x * (1.0 + t)
    dg = 0.5 * (1.0 + t) + 0.5 * x * (1.0 - t * t) * (_GELU_C * (1.0 + 3.0 * _GELU_K * x2))
    return g, dg


def _one_minus_sq(log_a, a):
    series = (-2.0 * log_a) * (1.0 + log_a * (1.0 + log_a * (2.0 / 3.0)))
    return jnp.where(log_a > -0.005, series, 1.0 - a * a)


def _softplus(x):
    return jnp.maximum(x, 0.0) + jnp.log1p(jnp.exp(-jnp.abs(x)))


def _rms(u):
    r = lax.rsqrt(jnp.mean(u * u, axis=-1, keepdims=True) + EPS)
    return u * r, r


def _rms_bwd(dn, un, r):
    return r * (dn - un * jnp.mean(dn * un, axis=-1, keepdims=True))


def _shift_dn(ext, s, n):
    if s == 0:
        return ext[HALO:HALO + n]
    return pltpu.roll(ext, s, 0)[HALO:HALO + n]


def _shift_up(ext, s, n):
    if s == 0:
        return ext[0:n]
    return pltpu.roll(ext, n + HALO - s, 0)[0:n]


def _ext_before(ref, c, ch):
    t0 = c * ch
    prev = pl.multiple_of(jnp.maximum(t0 - HALO, 0), HALO)
    halo = jnp.where(c > 0, ref[pl.ds(prev, HALO), :].astype(F32), 0.0)
    cur = ref[pl.ds(pl.multiple_of(t0, HALO), ch), :].astype(F32)
    return jnp.concatenate([halo, cur], axis=0)


def _ext_after(ref, c, ch, n_chunks):
    t0 = c * ch
    nxt = pl.multiple_of(jnp.minimum(t0 + ch, (n_chunks - 1) * ch + ch - HALO), HALO)
    halo = jnp.where(c < n_chunks - 1, ref[pl.ds(nxt, HALO), :].astype(F32), 0.0)
    cur = ref[pl.ds(pl.multiple_of(t0, HALO), ch), :].astype(F32)
    return jnp.concatenate([cur, halo], axis=0)


def _rows(ref, c, ch):
    return ref[pl.ds(pl.multiple_of(c * ch, HALO), ch), :]


def _scan_fwd(a_ref, b_ref, h_ref, n_rows, cw):
    rows = lax.broadcasted_iota(jnp.int32, (8, cw), 0)

    def body(i, carry):
        for u in range(SCAN_UNROLL):
            t = pl.multiple_of((i * SCAN_UNROLL + u) * 8, 8)
            a = a_ref[pl.ds(t, 8), :]
            b = b_ref[pl.ds(t, 8), :]
            for d in (1, 2, 4):
                m = rows >= d
                b = jnp.where(m, a * pltpu.roll(b, d, 0) + b, b)
                a = jnp.where(m, a * pltpu.roll(a, d, 0), a)
            h_ref[pl.ds(t, 8), :] = a * carry + b
            carry = jnp.broadcast_to(a[7:8, :], (8, cw)) * carry + jnp.broadcast_to(b[7:8, :], (8, cw))
        return carry

    lax.fori_loop(0, n_rows // (8 * SCAN_UNROLL), body, jnp.zeros((8, cw), F32))


def _scan_rev(a_ref, g_ref, o_ref, n_rows, cw):
    rows = lax.broadcasted_iota(jnp.int32, (8, cw), 0)
    n_groups = n_rows // 8

    def body(i, carry):
        dh_next, a_next = carry
        for u in range(SCAN_UNROLL):
            t = pl.multiple_of((n_groups - 1 - (i * SCAN_UNROLL + u)) * 8, 8)
            a = a_ref[pl.ds(t, 8), :]
            g = g_ref[pl.ds(t, 8), :]
            an = jnp.where(rows < 7, pltpu.roll(a, 7, 0), a_next)
            a_next = jnp.broadcast_to(a[0:1, :], (8, cw))
            for d in (1, 2, 4):
                m = rows < 8 - d
                g = jnp.where(m, an * pltpu.roll(g, 8 - d, 0) + g, g)
                an = jnp.where(m, an * pltpu.roll(an, 8 - d, 0), an)
            o_ref[pl.ds(t, 8), :] = an * dh_next + g
            dh_next = jnp.broadcast_to(an[0:1, :], (8, cw)) * dh_next + jnp.broadcast_to(g[0:1, :], (8, cw))
        return dh_next, a_next

    zero = jnp.zeros((8, cw), F32)
    lax.fori_loop(0, n_groups // SCAN_UNROLL, body, (zero, zero))


def _params(*sem):
    return pltpu.CompilerParams(dimension_semantics=sem, vmem_limit_bytes=VMEM_LIMIT)


def _sds(shape, dtype):
    return jax.ShapeDtypeStruct(shape, dtype)


def _resident(shape):
    zeros = (0,) * len(shape)
    return pl.BlockSpec(shape, lambda *_: zeros, pipeline_mode=pl.Buffered(1))


class _Comm:
    def __init__(self, inputs, out_shape, scratch, start, finish):
        self.inputs, self.out_shape, self.scratch, self.start, self.finish = inputs, out_shape, scratch, start, finish


def _join_comms(a, b):
    n, k = len(a.inputs), len(a.scratch)

    def start(ins, outs, sems):
        a.start(ins[:n], outs[:n], sems[:k])
        b.start(ins[n:], outs[n:], sems[k:])

    def finish(ins, outs, sems):
        a.finish(ins[:n], outs[:n], sems[:k])
        b.finish(ins[n:], outs[n:], sems[k:])

    return _Comm(a.inputs + b.inputs, a.out_shape + b.out_shape, a.scratch + b.scratch, start, finish)


def _sem_scratch(n, copies=7):
    return [pltpu.SemaphoreType.DMA((n, copies)), pltpu.SemaphoreType.DMA((n, copies)), pltpu.SemaphoreType.DMA((n,))]


def _gather_comm(shards, by_columns=()):
    n = len(shards)

    def plan(ins, outs, sems):
        send_sems, recv_sems, local_sems = sems
        x, y, c = lax.axis_index("x"), lax.axis_index("y"), lax.axis_index("c")
        me, sibling = (x, y, c), (x, y, 1 - c)
        chips = [(1 - x, y), (x, 1 - y), (1 - x, 1 - y)]

        def slot(t, dev):
            idx = 4 * dev[0] + 2 * dev[1] + dev[2]
            if t in by_columns:
                width = shards[t].shape[1]
                return outs[t].at[:, pl.ds(pl.multiple_of(idx * width, 128), width)]
            return outs[t].at[idx]

        def copy(t, k, block, to, src=None):
            return pltpu.make_async_remote_copy(
                src_ref=slot(t, block) if src is None else src, dst_ref=slot(t, block),
                send_sem=send_sems.at[t, k], recv_sem=recv_sems.at[t, k],
                device_id=to, device_id_type=MESH_ID)

        mine = [pltpu.make_async_copy(ins[t], slot(t, me), local_sems.at[t]) for t in range(n)]
        first = []
        for t in range(n):
            first.append(copy(t, 0, me, sibling, src=ins[t]))
            first += [copy(t, 1 + j, me, (*chip, c), src=ins[t]) for j, chip in enumerate(chips)]
        return me, sibling, chips, c, copy, mine, first

    def start(ins, outs, sems):
        *_, mine, first = plan(ins, outs, sems)
        for cp in mine + first:
            cp.start()

    def finish(ins, outs, sems):
        me, sibling, chips, c, copy, mine, first = plan(ins, outs, sems)
        passed = []
        for t in range(n):
            for j, chip in enumerate(chips):
                copy(t, 1 + j, (*chip, c), me).wait_recv()
                cp = copy(t, 4 + j, (*chip, c), sibling)
                cp.start()
                passed.append(cp)
        for t in range(n):
            copy(t, 0, sibling, me).wait_recv()
            for j, chip in enumerate(chips):
                copy(t, 4 + j, (*chip, 1 - c), me).wait_recv()
        for cp in first + passed:
            cp.wait_send()
        for cp in mine:
            cp.wait()

    out_shape = [_sds((s.shape[0], N_DEV * s.shape[1]) if t in by_columns else (N_DEV,) + s.shape, s.dtype)
                 for t, s in enumerate(shards)]
    return _Comm(list(shards), out_shape, _sem_scratch(n), start, finish)


def _peer(d):
    x, y, c = lax.axis_index("x"), lax.axis_index("y"), lax.axis_index("c")
    px = 1 - x if d & 4 else x
    py = 1 - y if d & 2 else y
    pc = 1 - c if d & 1 else c
    return (px, py, pc), 4 * px + 2 * py + pc


def _exchange_comm(parts, deltas=tuple(range(1, N_DEV))):
    n = len(parts)

    def plan(ins, outs, sems):
        send_sems, recv_sems, local_sems = sems
        _, me = _peer(0)

        def copy(t, i):
            peer, idx = _peer(deltas[i])
            return pltpu.make_async_remote_copy(
                src_ref=ins[t].at[idx], dst_ref=outs[t].at[1 + i],
                send_sem=send_sems.at[t, i], recv_sem=recv_sems.at[t, i],
                device_id=peer, device_id_type=MESH_ID)

        mine = [pltpu.make_async_copy(ins[t].at[me], outs[t].at[0], local_sems.at[t]) for t in range(n)]
        return mine, [copy(t, i) for t in range(n) for i in range(len(deltas))]

    def start(ins, outs, sems):
        mine, copies = plan(ins, outs, sems)
        for cp in mine + copies:
            cp.start()

    def finish(ins, outs, sems):
        mine, copies = plan(ins, outs, sems)
        for cp in copies:
            cp.wait_recv()
        for cp in copies:
            cp.wait_send()
        for cp in mine:
            cp.wait()

    return _Comm(list(parts), [_sds((1 + len(deltas),) + p.shape[1:], p.dtype) for p in parts],
                 _sem_scratch(n, len(deltas)), start, finish)


def _chip_exchange_comm(parts):
    n = len(parts)

    def plan(ins, outs, sems):
        send_sems, recv_sems, local_sems = sems
        x, y, c = lax.axis_index("x"), lax.axis_index("y"), lax.axis_index("c")

        def copy(t, dq):
            px = 1 - x if dq & 2 else x
            py = 1 - y if dq & 1 else y
            return pltpu.make_async_remote_copy(
                src_ref=ins[t].at[2 * px + py], dst_ref=outs[t].at[dq],
                send_sem=send_sems.at[t, dq - 1], recv_sem=recv_sems.at[t, dq - 1],
                device_id=(px, py, c), device_id_type=MESH_ID)

        mine = [pltpu.make_async_copy(ins[t].at[2 * x + y], outs[t].at[0], local_sems.at[t]) for t in range(n)]
        return mine, [copy(t, dq) for t in range(n) for dq in range(1, 4)]

    def start(ins, outs, sems):
        mine, copies = plan(ins, outs, sems)
        for cp in mine + copies:
            cp.start()

    def finish(ins, outs, sems):
        mine, copies = plan(ins, outs, sems)
        for cp in copies:
            cp.wait_recv()
        for cp in copies:
            cp.wait_send()
        for cp in mine:
            cp.wait()

    return _Comm(list(parts), [_sds(p.shape, p.dtype) for p in parts], _sem_scratch(n, 3), start, finish)


def _pair_reduce(part, name):
    r, c = part.shape[0], part.shape[1] // N_DEV
    n_chips = N_DEV // 2
    tr = 128

    def body(p_ref, o_ref, own, land, send_sems, recv_sems, local_sems):
        x, y, core = lax.axis_index("x"), lax.axis_index("y"), lax.axis_index("c")

        def block(k):
            return p_ref.at[:, pl.ds(pl.multiple_of(k * c, 128), c)]

        sends = [pltpu.make_async_remote_copy(
            src_ref=block(2 * q + (1 - core)), dst_ref=land.at[q], send_sem=send_sems.at[q], recv_sem=recv_sems.at[q],
            device_id=(x, y, 1 - core), device_id_type=MESH_ID) for q in range(n_chips)]
        loads = [pltpu.make_async_copy(block(2 * q + core), own.at[q], local_sems.at[q]) for q in range(n_chips)]
        for cp in sends + loads:
            cp.start()
        for q in range(n_chips):
            loads[q].wait()
            sends[q].wait_recv()

            def add_rows(i, _):
                rows = pl.ds(pl.multiple_of(i * tr, tr), tr)
                o_ref[q, rows, :] = (own[q, rows, :].astype(F32) + land[q, rows, :].astype(F32)).astype(BF16)
                return 0

            lax.fori_loop(0, r // tr, add_rows, 0)
        for cp in sends:
            cp.wait_send()

    return pl.pallas_call(
        body, name=name, out_shape=_sds((n_chips, r, c), BF16),
        in_specs=[pl.BlockSpec(memory_space=pl.ANY)], out_specs=pl.BlockSpec(memory_space=pltpu.VMEM),
        scratch_shapes=[pltpu.VMEM((n_chips, r, c), BF16), pltpu.VMEM((n_chips, r, c), BF16)]
        + [pltpu.SemaphoreType.DMA((n_chips,))] * 3,
        compiler_params=pltpu.CompilerParams(vmem_limit_bytes=VMEM_LIMIT),
    )(part)


def _comm_only(comm, name):
    n = len(comm.inputs)

    def body(*refs):
        ins, outs, sems = refs[:n], refs[n:2 * n], refs[2 * n:]
        comm.start(ins, outs, sems)
        comm.finish(ins, outs, sems)

    any_spec = pl.BlockSpec(memory_space=pl.ANY)
    return pl.pallas_call(body, name=name, out_shape=comm.out_shape, in_specs=[any_spec] * n, out_specs=[any_spec] * n,
                          scratch_shapes=comm.scratch)(*comm.inputs)


def _host_call(body, *, name, grid, in_specs, out_specs, out_shape, scratch_shapes=(), semantics, args, comm=None):
    if comm is None:
        res = pl.pallas_call(body, name=name, grid=grid, in_specs=in_specs, out_specs=out_specs, out_shape=out_shape,
                             scratch_shapes=list(scratch_shapes), compiler_params=_params(*semantics))(*args)
        return res, []
    n_in, n_out, n_scr, n_c = len(in_specs), len(out_specs), len(scratch_shapes), len(comm.inputs)

    def with_comm(*refs):
        ins, refs = refs[:n_in], refs[n_in:]
        c_ins, refs = refs[:n_c], refs[n_c:]
        outs, refs = refs[:n_out], refs[n_out:]
        c_outs, refs = refs[:n_c], refs[n_c:]
        scr, sems = refs[:n_scr], refs[n_scr:]
        ids = [pl.program_id(a) for a in range(len(grid))]
        first = functools.reduce(jnp.logical_and, [i == 0 for i in ids])
        last = functools.reduce(jnp.logical_and, [i == g - 1 for i, g in zip(ids, grid)])

        @pl.when(first)
        def _():
            comm.start(c_ins, c_outs, sems)

        body(*ins, *outs, *scr)

        @pl.when(last)
        def _():
            comm.finish(c_ins, c_outs, sems)

    any_spec = pl.BlockSpec(memory_space=pl.ANY)
    res = pl.pallas_call(
        with_comm, name=name, grid=grid, in_specs=list(in_specs) + [any_spec] * n_c,
        out_specs=list(out_specs) + [any_spec] * n_c, out_shape=list(out_shape) + comm.out_shape,
        scratch_shapes=list(scratch_shapes) + comm.scratch,
        compiler_params=_params(*(["arbitrary"] * len(grid))))(*args, *comm.inputs)
    return res[:n_out], res[n_out:]


def _small_all_reduce(packs):
    n = len(packs)
    pieces = [p.shape[0] // N_DEV for p in packs]

    def body(*refs):
        p_refs, o_refs, lands = refs[:n], refs[n:2 * n], refs[2 * n:3 * n]
        s1, r1, s2, r2 = refs[3 * n:]
        _, me = _peer(0)

        def piece(t, ref, idx):
            return ref.at[pl.ds(pl.multiple_of(idx * pieces[t], 8), pieces[t]), :]

        def scatter(t, d):
            peer, idx = _peer(d)
            return pltpu.make_async_remote_copy(
                src_ref=piece(t, p_refs[t], idx), dst_ref=lands[t].at[d - 1], send_sem=s1.at[t, d - 1],
                recv_sem=r1.at[t, d - 1], device_id=peer, device_id_type=MESH_ID)

        def gather(t, d, from_idx):
            peer, _ = _peer(d)
            return pltpu.make_async_remote_copy(
                src_ref=piece(t, o_refs[t], from_idx), dst_ref=piece(t, o_refs[t], from_idx), send_sem=s2.at[t, d - 1],
                recv_sem=r2.at[t, d - 1], device_id=peer, device_id_type=MESH_ID)

        first = [[scatter(t, d) for d in range(1, N_DEV)] for t in range(n)]
        for cp in sum(first, []):
            cp.start()
        second = []
        for t in range(n):
            acc = piece(t, p_refs[t], me)[...]
            for d in range(1, N_DEV):
                first[t][d - 1].wait_recv()
                acc = acc + lands[t][d - 1]
            piece(t, o_refs[t], me)[...] = acc
            second += [gather(t, d, me) for d in range(1, N_DEV)]
            for cp in second[-(N_DEV - 1):]:
                cp.start()
        for t in range(n):
            for d in range(1, N_DEV):
                gather(t, d, _peer(d)[1]).wait_recv()
        for cp in sum(first, []) + second:
            cp.wait_send()

    vmem = pl.BlockSpec(memory_space=pltpu.VMEM)
    return pl.pallas_call(
        body, name="small_all_reduce",
        out_shape=[_sds(p.shape, F32) for p in packs], in_specs=[vmem] * n, out_specs=[vmem] * n,
        scratch_shapes=[pltpu.VMEM((N_DEV - 1, pc, p.shape[1]), F32) for pc, p in zip(pieces, packs)]
        + [pltpu.SemaphoreType.DMA((n, N_DEV - 1))] * 4,
    )(*packs)


def _norm_in_proj(x, g_mix, w_in, tm, comm):
    s = x.shape[0]

    def body(x_ref, g_ref, w_ref, z_ref, h_ref):
        xn, _ = _rms(x_ref[...])
        h = (xn * g_ref[...]).astype(BF16)
        h_ref[...] = h
        for j in range(N_SEG):
            cols = slice(j * D_MODEL, (j + 1) * D_MODEL)
            z_ref[:, cols] = _dot(h, w_ref[:, cols]).astype(BF16)

    return _host_call(
        body, name="norm_in_proj", grid=(s // tm,),
        in_specs=[pl.BlockSpec((tm, D_MODEL), lambda i: (i, 0)), pl.BlockSpec((1, D_MODEL), lambda i: (0, 0)),
                  _resident(w_in.shape)],
        out_specs=[pl.BlockSpec((tm, N_SEG * D_MODEL), lambda i: (i, 0)), pl.BlockSpec((tm, D_MODEL), lambda i: (i, 0))],
        out_shape=[_sds((s, N_SEG * D_MODEL), BF16), _sds((s, D_MODEL), BF16)],
        semantics=("parallel",), args=(x, g_mix, w_in), comm=comm)


def _gate_chunk(xr_ext, w4_ref, cb_ref, wa, wx, ba_ref, bx_ref, sp, ch):
    xc = cb_ref[...] + w4_ref[3:4, :] * _shift_dn(xr_ext, 0, ch)
    for k in range(3):
        xc = xc + w4_ref[k:k + 1, :] * _shift_dn(xr_ext, 3 - k, ch)
    xcb = xc.astype(BF16)
    ra = _sigmoid(_dot(xcb, wa) + ba_ref[...])
    ia = _sigmoid(_dot(xcb, wx) + bx_ref[...])
    log_a = (-LRU_C) * ra * sp
    return xc, ra, ia, log_a


def _conv3(q_ext, w3_ref, ch):
    y = w3_ref[2:3, :] * _shift_dn(q_ext, 0, ch)
    for k in range(2):
        y = y + w3_ref[k:k + 1, :] * _shift_dn(q_ext, 2 - k, ch)
    return y


def _z_block_specs(s, cw, segs):
    nb = D_MODEL // cw
    return [pl.BlockSpec((s, cw), functools.partial(lambda j, seg: (0, seg * nb + j), seg=seg)) for seg in segs]


def _mixers_fwd(z, w4, b4, w_a, b_a, w_x, b_x, lam, w3, ch, comm):
    s = z.shape[0]
    cw = RNN_BW
    n_chunks = s // ch

    def body(xr_ref, gr_ref, cb_ref, cc_ref, cx_ref, w4_ref, b4_ref, wa_ref, ba_ref, wx_ref, bx_ref, lam_ref, w3_ref,
             ya_ref, yb_ref, h_ref, a_scr, b_scr):
        sp = _softplus(-lam_ref[...])
        wa = wa_ref[0].astype(BF16)
        wx = wx_ref[0].astype(BF16)

        def gates(c, _):
            xc, _, ia, log_a = _gate_chunk(_ext_before(xr_ref, c, ch), w4_ref, b4_ref, wa, wx, ba_ref, bx_ref, sp, ch)
            rows = pl.ds(pl.multiple_of(c * ch, HALO), ch)
            a = jnp.exp(log_a)
            a_scr[rows, :] = a
            b_scr[rows, :] = jnp.sqrt(_one_minus_sq(log_a, a)) * (ia * xc)
            q_ext = _ext_before(cc_ref, c, ch) * _ext_before(cx_ref, c, ch)
            yb_ref[rows, :] = (_rows(cb_ref, c, ch).astype(F32) * _conv3(q_ext, w3_ref, ch)).astype(BF16)
            return 0

        lax.fori_loop(0, n_chunks, gates, 0)
        _scan_fwd(a_scr, b_scr, h_ref, s, cw)

        def outputs(c, _):
            rows = pl.ds(pl.multiple_of(c * ch, HALO), ch)
            ya_ref[rows, :] = (h_ref[rows, :] * _gelu(gr_ref[rows, :].astype(F32))).astype(BF16)
            return 0

        lax.fori_loop(0, n_chunks, outputs, 0)

    col = lambda j: (0, j)
    vec = pl.BlockSpec((1, cw), col)
    sq = pl.BlockSpec((1, cw, cw), lambda j: (j, 0, 0))
    act = pl.BlockSpec((s, cw), col)
    return _host_call(
        body, name="mixers_fwd", grid=(D_MODEL // cw,),
        in_specs=_z_block_specs(s, cw, range(5)) + [pl.BlockSpec((4, cw), col), vec, sq, vec, sq, vec, vec,
                                                     pl.BlockSpec((3, cw), col)],
        out_specs=[act, act, act],
        out_shape=[_sds((s, D_MODEL), BF16), _sds((s, D_MODEL), BF16), _sds((s, D_MODEL), F32)],
        scratch_shapes=[pltpu.VMEM((s, cw), F32), pltpu.VMEM((s, cw), F32)],
        semantics=("parallel",), args=(z, z, z, z, z, w4, b4, w_a, b_a, w_x, b_x, lam, w3), comm=comm)


def _merge_out(ya, yb, z, x, w_pa, w_pb, w_out, g_ffn, tm):
    s = x.shape[0]

    def body(ya_ref, yb_ref, ga_ref, gb_ref, x_ref, wa_ref, wb_ref, wo_ref, g_ref, pa_ref, pb_ref, m_ref, x2_ref, h2_ref):
        pa = _dot(ya_ref[...], wa_ref[...])
        pb = _dot(yb_ref[...], wb_ref[...])
        pa_ref[...] = pa.astype(BF16)
        pb_ref[...] = pb.astype(BF16)
        m = (_sigmoid(ga_ref[...].astype(F32)) * pa + _sigmoid(gb_ref[...].astype(F32)) * pb).astype(BF16)
        m_ref[...] = m
        x2 = x_ref[...] + _dot(m, wo_ref[...])
        x2_ref[...] = x2
        xn, _ = _rms(x2)
        h2_ref[...] = (xn * g_ref[...]).astype(BF16)

    tile = pl.BlockSpec((tm, D_MODEL), lambda i: (i, 0))
    full = _resident((D_MODEL, D_MODEL))
    return pl.pallas_call(
        body, name="merge_out", grid=(s // tm,),
        in_specs=[tile, tile, pl.BlockSpec((tm, D_MODEL), lambda i: (i, 5)), pl.BlockSpec((tm, D_MODEL), lambda i: (i, 6)),
                  tile, full, full, full, pl.BlockSpec((1, D_MODEL), lambda i: (0, 0))],
        out_specs=[tile] * 5,
        out_shape=[_sds((s, D_MODEL), BF16)] * 3 + [_sds((s, D_MODEL), F32), _sds((s, D_MODEL), BF16)],
        compiler_params=_params("parallel"),
    )(ya, yb, z, z, x, w_pa, w_pb, w_out, g_ffn)


def _up_proj_act(h2, w_up_g, wc, bc, tm):
    s = h2.shape[0]
    bw = w_up_g.shape[2]
    per_half = N_DEV // 2

    def body(h_ref, w_ref, wc_ref, bc_ref, u_ref, f_ref, halo):
        @pl.when(pl.program_id(0) == 0)
        def _():
            halo[...] = jnp.zeros_like(halo)

        h = h_ref[...]
        for j in range(per_half):
            cols = slice(j * bw, (j + 1) * bw)
            conv = []
            for half in range(2):
                u = _dot(h, w_ref[half * per_half + j]).astype(BF16)
                u_ref[half, :, cols] = u
                u = u.astype(F32)
                ext = jnp.concatenate([halo[half, :, cols], u], axis=0)
                halo[half, :, cols] = u[tm - HALO:tm]
                w = wc_ref.at[half]
                acc = bc_ref[half, :, cols] + w[2:3, cols] * u
                for k in range(2):
                    acc = acc + w[k:k + 1, cols] * _shift_dn(ext, 2 - k, tm)
                conv.append(acc)
            f_ref[:, cols] = (_gelu(conv[0]) * conv[1]).astype(BF16)

    return pl.pallas_call(
        body, name="up_proj_act", grid=(s // tm,),
        in_specs=[pl.BlockSpec((tm, D_MODEL), lambda i: (i, 0)), _resident(w_up_g.shape), _resident(wc.shape),
                  _resident(bc.shape)],
        out_specs=[pl.BlockSpec((2, tm, D_FF), lambda i: (0, i, 0)), pl.BlockSpec((tm, D_FF), lambda i: (i, 0))],
        out_shape=[_sds((2, s, D_FF), BF16), _sds((s, D_FF), BF16)],
        scratch_shapes=[pltpu.VMEM((2, HALO, D_FF), F32)],
        compiler_params=_params("arbitrary"),
    )(h2, w_up_g, wc, bc)


def _ffn_conv(u_ref, w_ref, b_ref, half, c, ch):
    ext = _ext_before(u_ref.at[half], c, ch)
    w = w_ref.at[half]
    return b_ref[half] + _conv3(ext, w, ch), ext


def _head(x2, f, p, target, w_down, w_gate, w_pp, g_ple, g_final, tm):
    s = x2.shape[0]

    def body(x2_ref, f_ref, p_ref, t_ref, wd_ref, wg_ref, wp_ref, gp_ref, gf_ref,
             x3_ref, dgt_ref, de0_ref, dx3_ref, df_ref, sums_ref):
        @pl.when(pl.program_id(0) == 0)
        def _():
            sums_ref[...] = jnp.zeros_like(sums_ref)

        x3 = x2_ref[...] + _dot(f_ref[...], wd_ref[...])
        x3b = x3.astype(BF16)
        x3_ref[...] = x3b
        e0n, re = _rms(_dot(p_ref[...].astype(BF16), wp_ref[...]))
        e = e0n * gp_ref[...]
        sg = _sigmoid(_dot(x3b, wg_ref[...]))
        x4n, r4 = _rms(x3 + sg * e)
        diff = x4n * gf_ref[...] - t_ref[...]
        sums_ref[0:1, :] += jnp.sum(diff * diff, axis=0, keepdims=True)
        dy = diff * (1.0 / D_MODEL)
        sums_ref[1:2, :] += jnp.sum(dy * x4n, axis=0, keepdims=True)
        dx4 = _rms_bwd(dy * gf_ref[...], x4n, r4)
        de = dx4 * sg
        dgt = ((dx4 * e) * (sg * (1.0 - sg))).astype(BF16)
        dgt_ref[...] = dgt
        sums_ref[2:3, :] += jnp.sum(de * e0n, axis=0, keepdims=True)
        de0_ref[...] = _rms_bwd(de * gp_ref[...], e0n, re).astype(BF16)
        dx3 = dx4 + _dot_nt(dgt, wg_ref[...])
        dx3_ref[...] = dx3
        df_ref[...] = _dot_nt(dx3.astype(BF16), wd_ref[...]).astype(BF16)

    tile = pl.BlockSpec((tm, D_MODEL), lambda i: (i, 0))
    vec = pl.BlockSpec((1, D_MODEL), lambda i: (0, 0))
    wide = pl.BlockSpec((tm, D_FF), lambda i: (i, 0))
    return pl.pallas_call(
        body, name="loss_head", grid=(s // tm,),
        in_specs=[tile, wide, pl.BlockSpec((tm, D_PLE), lambda i: (i, 0)), tile,
                  _resident((D_FF, D_MODEL)), _resident((D_MODEL, D_MODEL)), _resident((D_PLE, D_MODEL)), vec, vec],
        out_specs=[tile, tile, tile, tile, wide, pl.BlockSpec((8, D_MODEL), lambda i: (0, 0))],
        out_shape=[_sds((s, D_MODEL), BF16)] * 3 + [_sds((s, D_MODEL), F32), _sds((s, D_FF), BF16),
                                                    _sds((8, D_MODEL), F32)],
        compiler_params=_params("arbitrary"),
    )(x2, f, p, target, w_down, w_gate, w_pp, g_ple, g_final)


def _ffn_act_bwd(u0, df, wc, bc, cw, ch, comm):
    s = u0.shape[1]
    n_chunks = s // ch

    def body(u_ref, df_ref, w_ref, b_ref, du0_ref, dw_ref, db_ref, du_scr):
        dw_ref[...] = jnp.zeros_like(dw_ref)
        db_ref[...] = jnp.zeros_like(db_ref)

        def first(c, _):
            ug, ext_g = _ffn_conv(u_ref, w_ref, b_ref, 0, c, ch)
            uv, ext_v = _ffn_conv(u_ref, w_ref, b_ref, 1, c, ch)
            gel, dgel = _gelu_and_grad(ug)
            d = _rows(df_ref, c, ch).astype(F32)
            rows = pl.ds(pl.multiple_of(c * ch, HALO), ch)
            for half, du, ext in ((0, d * uv * dgel, ext_g), (1, d * gel, ext_v)):
                du_scr[half, rows, :] = du
                db_ref[half] += jnp.sum(du, axis=0, keepdims=True)
                for k in range(3):
                    dw_ref[half, k:k + 1, :] += jnp.sum(du * _shift_dn(ext, 2 - k, ch), axis=0, keepdims=True)
            return 0

        lax.fori_loop(0, n_chunks, first, 0)

        def second(c, _):
            rows = pl.ds(pl.multiple_of(c * ch, HALO), ch)
            for half in range(2):
                ext = _ext_after(du_scr.at[half], c, ch, n_chunks)
                w = w_ref.at[half]
                acc = w[2:3, :] * _shift_up(ext, 0, ch)
                for k in range(2):
                    acc = acc + w[k:k + 1, :] * _shift_up(ext, 2 - k, ch)
                du0_ref[half, rows, :] = acc.astype(BF16)
            return 0

        lax.fori_loop(0, n_chunks, second, 0)

    blk = lambda r: pl.BlockSpec((2, r, cw), lambda j: (0, 0, j))
    return _host_call(
        body, name="ffn_act_bwd", grid=(D_FF // cw,),
        in_specs=[blk(s), pl.BlockSpec((s, cw), lambda j: (0, j)), blk(3), blk(1)],
        out_specs=[blk(s), blk(3), blk(1)],
        out_shape=[_sds((2, s, D_FF), BF16), _sds((2, 3, D_FF), F32), _sds((2, 1, D_FF), F32)],
        scratch_shapes=[pltpu.VMEM((2, s, cw), F32)],
        semantics=("parallel",), args=(u0, df, wc, bc), comm=comm)


def _up_bwd_merge_bwd(du0, w_up_g, x2, dx3, z, pa, pb, w_out, w_pa, w_pb, g_ffn, tm, comm):
    s = x2.shape[0]
    bw = w_up_g.shape[2]
    per_half = N_DEV // 2

    def body(du_ref, wu_ref, x2_ref, dx3_ref, ga_ref, gb_ref, pa_ref, pb_ref, wo_ref, wa_ref, wb_ref, g_ref,
             dx2_ref, dzg_ref, dpa_ref, dpb_ref, dya_ref, dyb_ref, sums_ref):
        @pl.when(pl.program_id(0) == 0)
        def _():
            sums_ref[...] = jnp.zeros_like(sums_ref)

        dh2 = None
        for j in range(N_DEV):
            cols = slice((j % per_half) * bw, (j % per_half + 1) * bw)
            term = _dot_nt(du_ref[j // per_half, :, cols], wu_ref[j])
            dh2 = term if dh2 is None else dh2 + term
        x2n, r2 = _rms(x2_ref[...])
        sums_ref[0:1, :] += jnp.sum(dh2 * x2n, axis=0, keepdims=True)
        dx2 = dx3_ref[...] + _rms_bwd(dh2 * g_ref[...], x2n, r2)
        dx2_ref[...] = dx2
        dm = _dot_nt(dx2.astype(BF16), wo_ref[...])
        sa = _sigmoid(ga_ref[...].astype(F32))
        sb = _sigmoid(gb_ref[...].astype(F32))
        dzg_ref[0] = (dm * pa_ref[...].astype(F32) * (sa * (1.0 - sa))).astype(BF16)
        dzg_ref[1] = (dm * pb_ref[...].astype(F32) * (sb * (1.0 - sb))).astype(BF16)
        dpa = (dm * sa).astype(BF16)
        dpb = (dm * sb).astype(BF16)
        dpa_ref[...] = dpa
        dpb_ref[...] = dpb
        dya_ref[...] = _dot_nt(dpa, wa_ref[...]).astype(BF16)
        dyb_ref[...] = _dot_nt(dpb, wb_ref[...]).astype(BF16)

    tile = pl.BlockSpec((tm, D_MODEL), lambda i: (i, 0))
    full = _resident((D_MODEL, D_MODEL))
    return _host_call(
        body, name="up_bwd_merge_bwd", grid=(s // tm,),
        in_specs=[pl.BlockSpec((2, tm, D_FF), lambda i: (0, i, 0)), _resident(w_up_g.shape),
                  tile, tile, pl.BlockSpec((tm, D_MODEL), lambda i: (i, 5)),
                  pl.BlockSpec((tm, D_MODEL), lambda i: (i, 6)), tile, tile, full, full, full,
                  pl.BlockSpec((1, D_MODEL), lambda i: (0, 0))],
        out_specs=[tile, pl.BlockSpec((2, tm, D_MODEL), lambda i: (0, i, 0)), tile, tile, tile, tile,
                   pl.BlockSpec((8, D_MODEL), lambda i: (0, 0))],
        out_shape=[_sds((s, D_MODEL), F32), _sds((2, s, D_MODEL), BF16)] + [_sds((s, D_MODEL), BF16)] * 4
        + [_sds((8, D_MODEL), F32)],
        semantics=("arbitrary",), args=(du0, w_up_g, x2, dx3, z, z, pa, pb, w_out, w_pa, w_pb, g_ffn),
        comm=comm)


def _mixers_bwd(z, h, dya, dyb, w4, b4, w_a, b_a, w_x, b_x, lam, w3, ch, comm):
    s = z.shape[0]
    cw = RNN_BW
    n_chunks = s // ch

    def body(xr_ref, gr_ref, cb_ref, cc_ref, cx_ref, h_ref, dya_ref, dyb_ref, w4_ref, b4_ref, wa_ref, ba_ref, wx_ref,
             bx_ref, lam_ref, w3_ref,
             dz_ref, dw4_ref, db4_ref, dwa_ref, dba_ref, dwx_ref, dbx_ref, dlam_ref, dw3_ref,
             a_scr, g_scr, q_scr, xc_scr, ra_scr, ia_scr, dh_scr):
        sp = _softplus(-lam_ref[...])
        wa = wa_ref[0].astype(BF16)
        wx = wx_ref[0].astype(BF16)
        for ref in (dw4_ref, db4_ref, dwa_ref, dba_ref, dwx_ref, dbx_ref, dlam_ref, dw3_ref):
            ref[...] = jnp.zeros_like(ref)

        def first(c, _):
            rows = pl.ds(pl.multiple_of(c * ch, HALO), ch)
            xc, ra, ia, log_a = _gate_chunk(_ext_before(xr_ref, c, ch), w4_ref, b4_ref, wa, wx, ba_ref, bx_ref, sp, ch)
            xc_scr[rows, :] = xc
            ra_scr[rows, :] = ra
            ia_scr[rows, :] = ia
            a_scr[rows, :] = jnp.exp(log_a)
            gel, dgel = _gelu_and_grad(gr_ref[rows, :].astype(F32))
            dya_c = dya_ref[rows, :].astype(F32)
            g_scr[rows, :] = dya_c * gel
            dz_ref[1, rows, :] = (dya_c * h_ref[rows, :] * dgel).astype(BF16)
            q_ext = _ext_before(cc_ref, c, ch) * _ext_before(cx_ref, c, ch)
            dyb_c = dyb_ref[rows, :].astype(F32)
            dz_ref[2, rows, :] = (dyb_c * _conv3(q_ext, w3_ref, ch)).astype(BF16)
            dyq = dyb_c * cb_ref[rows, :].astype(F32)
            q_scr[rows, :] = dyq
            for k in range(3):
                dw3_ref[k:k + 1, :] += jnp.sum(dyq * _shift_dn(q_ext, 2 - k, ch), axis=0, keepdims=True)
            return 0

        lax.fori_loop(0, n_chunks, first, 0)
        _scan_rev(a_scr, g_scr, dh_scr, s, cw)

        def second(c, _):
            rows = pl.ds(pl.multiple_of(c * ch, HALO), ch)
            xc, ra, ia, a = xc_scr[rows, :], ra_scr[rows, :], ia_scr[rows, :], a_scr[rows, :]
            m2 = _one_minus_sq((-LRU_C) * ra * sp, a)
            inv_mult = lax.rsqrt(m2)
            mult = m2 * inv_mult
            dh = dh_scr[rows, :]
            h_prev = _shift_dn(_ext_before(h_ref, c, ch), 1, ch)
            dlog_a = dh * h_prev * a - (dh * ia * xc) * (a * a) * inv_mult
            dlam_ref[...] += jnp.sum(dlog_a * ra, axis=0, keepdims=True)
            dpre_a = (dlog_a * ((-LRU_C) * sp)) * (ra * (1.0 - ra))
            dpre_x = (dh * mult * xc) * (ia * (1.0 - ia))
            dba_ref[...] += jnp.sum(dpre_a, axis=0, keepdims=True)
            dbx_ref[...] += jnp.sum(dpre_x, axis=0, keepdims=True)
            xcb = xc.astype(BF16)
            dpa_b = dpre_a.astype(BF16)
            dpx_b = dpre_x.astype(BF16)
            dwa_ref[0] += _dot_tn(xcb, dpa_b)
            dwx_ref[0] += _dot_tn(xcb, dpx_b)
            a_scr[rows, :] = dh * mult * ia + _dot_nt(dpa_b, wa) + _dot_nt(dpx_b, wx)
            return 0

        lax.fori_loop(0, n_chunks, second, 0)
        dlam_ref[...] = dlam_ref[...] * (LRU_C * _sigmoid(-lam_ref[...]))

        def third(c, _):
            rows = pl.ds(pl.multiple_of(c * ch, HALO), ch)
            dxc_ext = _ext_after(a_scr, c, ch, n_chunks)
            dxc = _shift_up(dxc_ext, 0, ch)
            xr_ext = _ext_before(xr_ref, c, ch)
            db4_ref[...] += jnp.sum(dxc, axis=0, keepdims=True)
            dxr = w4_ref[3:4, :] * dxc
            dw4_ref[3:4, :] += jnp.sum(dxc * _shift_dn(xr_ext, 0, ch), axis=0, keepdims=True)
            for k in range(3):
                dxr = dxr + w4_ref[k:k + 1, :] * _shift_up(dxc_ext, 3 - k, ch)
                dw4_ref[k:k + 1, :] += jnp.sum(dxc * _shift_dn(xr_ext, 3 - k, ch), axis=0, keepdims=True)
            dz_ref[0, rows, :] = dxr.astype(BF16)
            dyq_ext = _ext_after(q_scr, c, ch, n_chunks)
            dq = w3_ref[2:3, :] * _shift_up(dyq_ext, 0, ch)
            for k in range(2):
                dq = dq + w3_ref[k:k + 1, :] * _shift_up(dyq_ext, 2 - k, ch)
            dz_ref[3, rows, :] = (dq * cx_ref[rows, :].astype(F32)).astype(BF16)
            dz_ref[4, rows, :] = (dq * cc_ref[rows, :].astype(F32)).astype(BF16)
            return 0

        lax.fori_loop(0, n_chunks, third, 0)

    col = lambda j: (0, j)
    vec = pl.BlockSpec((1, cw), col)
    sq = pl.BlockSpec((1, cw, cw), lambda j: (j, 0, 0))
    act = pl.BlockSpec((s, cw), col)
    w4s, w3s = pl.BlockSpec((4, cw), col), pl.BlockSpec((3, cw), col)
    vec_shape = _sds((1, D_MODEL), F32)
    sq_shape = _sds((D_MODEL // cw, cw, cw), F32)
    return _host_call(
        body, name="mixers_bwd", grid=(D_MODEL // cw,),
        in_specs=_z_block_specs(s, cw, range(5)) + [act, act, act, w4s, vec, sq, vec, sq, vec, vec, w3s],
        out_specs=[pl.BlockSpec((5, s, cw), lambda j: (0, 0, j)), w4s, vec, sq, vec, sq, vec, vec, w3s],
        out_shape=[_sds((5, s, D_MODEL), BF16), _sds((4, D_MODEL), F32), vec_shape, sq_shape, vec_shape, sq_shape,
                   vec_shape, vec_shape, _sds((3, D_MODEL), F32)],
        scratch_shapes=[pltpu.VMEM((s, cw), F32)] * 7,
        semantics=("parallel",), args=(z, z, z, z, z, h, dya, dyb, w4, b4, w_a, b_a, w_x, b_x, lam, w3), comm=comm)


def _in_proj_bwd(dz5, dzg, w_in, x, dx2, g_mix, tm, comm):
    s = x.shape[0]

    def body(d5_ref, dg_ref, w_ref, x_ref, dx2_ref, g_ref, dx_ref, sums_ref):
        @pl.when(pl.program_id(0) == 0)
        def _():
            sums_ref[...] = jnp.zeros_like(sums_ref)

        dh1 = None
        for k in range(N_SEG):
            d = d5_ref[k] if k < 5 else dg_ref[k - 5]
            term = _dot_nt(d, w_ref[:, k * D_MODEL:(k + 1) * D_MODEL])
            dh1 = term if dh1 is None else dh1 + term
        xn, r1 = _rms(x_ref[...])
        sums_ref[0:1, :] += jnp.sum(dh1 * xn, axis=0, keepdims=True)
        dx_ref[...] = dx2_ref[...] + _rms_bwd(dh1 * g_ref[...], xn, r1)

    tile = pl.BlockSpec((tm, D_MODEL), lambda i: (i, 0))
    return _host_call(
        body, name="in_proj_bwd", grid=(s // tm,),
        in_specs=[pl.BlockSpec((5, tm, D_MODEL), lambda i: (0, i, 0)), pl.BlockSpec((2, tm, D_MODEL), lambda i: (0, i, 0)),
                  _resident(w_in.shape), tile, tile, pl.BlockSpec((1, D_MODEL), lambda i: (0, 0))],
        out_specs=[tile, pl.BlockSpec((8, D_MODEL), lambda i: (0, 0))],
        out_shape=[_sds((s, D_MODEL), F32), _sds((8, D_MODEL), F32)],
        semantics=("arbitrary",), args=(dz5, dzg, w_in, x, dx2, g_mix), comm=comm)


def _weight_grad(a, b, b_spec, out_spec, out_shape, *, n_blocks, chunks, width, tm, tk, name):
    s, m = a.shape
    nk = s // tk
    b_chunked = len([d for d in b_spec.block_shape if d is not None]) == 3
    blocks_out = len([d for d in out_spec.block_shape if d is not None]) == 3

    def body(a_ref, b_ref, o_ref, acc):
        k = pl.program_id(2)

        @pl.when(k == 0)
        def _():
            acc[...] = jnp.zeros_like(acc)

        at = a_ref[...].astype(BF16).T
        for j in range(chunks):
            cols = slice(j * width, (j + 1) * width)
            acc[:, cols] += _dot(at, (b_ref[j] if b_chunked else b_ref[:, cols]).astype(BF16))

        @pl.when(k == nk - 1)
        def _():
            if blocks_out:
                for j in range(chunks):
                    o_ref[j] = acc[:, j * width:(j + 1) * width].astype(o_ref.dtype)
            else:
                o_ref[...] = acc[...].astype(o_ref.dtype)

    return pl.pallas_call(
        body, name=name, grid=(m // tm, n_blocks, nk),
        in_specs=[pl.BlockSpec((tk, tm), lambda i, j, k: (k, i)), b_spec], out_specs=out_spec, out_shape=out_shape,
        scratch_shapes=[pltpu.VMEM((tm, chunks * width), F32)],
        compiler_params=_params("parallel", "parallel", "arbitrary"),
    )(a, b)


def _wgrad_2d(a, b, name, tk):
    m, n = a.shape[1], b.shape[1]
    tm = m
    return _weight_grad(a, b, pl.BlockSpec((tk, n), lambda i, j, k: (k, 0)), pl.BlockSpec((tm, n), lambda i, j, k: (i, 0)),
                        _sds((m, n), BF16), n_blocks=1, chunks=1, width=n, tm=tm, tk=tk, name=name)


def _wgrad_in(h1, dz5, dzg, tm, tk):
    s, m = h1.shape
    nk = s // tk
    g5, gg = dz5.shape[0], dzg.shape[0]

    def body(a_ref, b5_ref, bg_ref, o_ref, acc):
        k = pl.program_id(1)

        @pl.when(k == 0)
        def _():
            acc[...] = jnp.zeros_like(acc)

        at = a_ref[...].T
        for j in range(g5 + gg):
            b = b5_ref[j] if j < g5 else bg_ref[j - g5]
            acc[:, j * D_MODEL:(j + 1) * D_MODEL] += _dot(at, b)

        @pl.when(k == nk - 1)
        def _():
            o_ref[...] = acc[...].astype(BF16)

    return pl.pallas_call(
        body, name="wgrad_in", grid=(m // tm, nk),
        in_specs=[pl.BlockSpec((tk, tm), lambda i, k: (k, i)), pl.BlockSpec((g5, tk, D_MODEL), lambda i, k: (0, k, 0)),
                  pl.BlockSpec((gg, tk, D_MODEL), lambda i, k: (0, k, 0))],
        out_specs=pl.BlockSpec((tm, (g5 + gg) * D_MODEL), lambda i, k: (i, 0)),
        out_shape=_sds((m, (g5 + gg) * D_MODEL), BF16),
        scratch_shapes=[pltpu.VMEM((tm, (g5 + gg) * D_MODEL), F32)],
        compiler_params=_params("parallel", "arbitrary"),
    )(h1, dz5, dzg)


def _wgrad_up(h2, du0, bw, tk):
    per_half = N_DEV // 2
    return _weight_grad(h2, du0, pl.BlockSpec((None, tk, D_FF), lambda i, j, k: (j, k, 0)),
                        pl.BlockSpec((per_half, D_MODEL, bw), lambda i, j, k: (j, 0, 0)), _sds((N_DEV, D_MODEL, bw), BF16),
                        n_blocks=2, chunks=per_half, width=bw, tm=D_MODEL, tk=tk, name="wgrad_up")


def _adam_math(w, g, m, v):
    m = ADAM_B1 * m + (1.0 - ADAM_B1) * g
    v = ADAM_B2 * v + (1.0 - ADAM_B2) * jnp.square(g)
    m_hat = m / (1.0 - ADAM_B1 ** ADAM_STEP)
    v_hat = v / (1.0 - ADAM_B2 ** ADAM_STEP)
    delta = -ADAM_LR * (m_hat / (jnp.sqrt(v_hat) + ADAM_EPS) + ADAM_WD * w)
    return delta, m, v


def _adam_shard(parts, w, m, v, name):
    r, c = w.shape
    tr = min(r, 128)
    n_arrays = len(parts)

    def body(*refs):
        p_refs = refs[:n_arrays]
        w_ref, m_ref, v_ref, g_ref, d_ref, nm_ref, nv_ref = refs[n_arrays:]
        g = None
        for p_ref, (arr, first) in zip(p_refs, parts):
            for k in range(first, arr.shape[0]):
                term = p_ref[k].astype(F32)
                g = term if g is None else g + term
        g_ref[...] = g
        d_ref[...], nm_ref[...], nv_ref[...] = _adam_math(w_ref[...], g, m_ref[...], v_ref[...])

    tile = pl.BlockSpec((tr, c), lambda i: (i, 0))
    return pl.pallas_call(
        body, name=name, grid=(r // tr,),
        in_specs=[pl.BlockSpec((arr.shape[0], tr, c), lambda i: (0, i, 0)) for arr, _ in parts] + [tile, tile, tile],
        out_specs=[tile] * 4, out_shape=[_sds((r, c), F32)] * 4,
        compiler_params=_params("parallel"),
    )(*[arr for arr, _ in parts], w, m, v)


def _adam_small(grads, ws, ms, vs, name):
    n = len(ws)
    g_arrays, g_slot = [], []
    for g in grads:
        arr = g[0] if isinstance(g, tuple) else g
        if not any(arr is a for a in g_arrays):
            g_arrays.append(arr)
        g_slot.append([arr is a for a in g_arrays].index(True))
    n_g = len(g_arrays)

    def body(*refs):
        g_refs, refs = refs[:n_g], refs[n_g:]
        w_refs, m_refs, v_refs, outs = refs[:n], refs[n:2 * n], refs[2 * n:3 * n], refs[3 * n:]
        for i in range(n):
            og, od, om, ov = outs[4 * i:4 * i + 4]
            g_ref = g_refs[g_slot[i]]
            if isinstance(grads[i], tuple):
                row = grads[i][1]
                for j in range(ws[i].shape[1] // D_MODEL):
                    cols = slice(j * D_MODEL, (j + 1) * D_MODEL)
                    g = g_ref[row + j:row + j + 1, :]
                    og[:, cols] = g
                    od[:, cols], om[:, cols], ov[:, cols] = _adam_math(w_refs[i][:, cols], g, m_refs[i][:, cols],
                                                                       v_refs[i][:, cols])
            else:
                g = g_ref[...]
                og[...] = g
                od[...], om[...], ov[...] = _adam_math(w_refs[i][...], g, m_refs[i][...], v_refs[i][...])

    vmem = pl.BlockSpec(memory_space=pltpu.VMEM)
    res = pl.pallas_call(
        body, name=name, in_specs=[vmem] * (n_g + 3 * n), out_specs=[vmem] * (4 * n),
        out_shape=[_sds(a.shape, F32) for a in ws for _ in range(4)],
    )(*g_arrays, *ws, *ms, *vs)
    return [res[4 * i:4 * i + 4] for i in range(n)]


def _pack_conv_shard(rnn, sc, ffn):
    top = jnp.concatenate([rnn[:3], sc, ffn], axis=1)
    row3 = jnp.concatenate([rnn[3:4], jnp.zeros((1, D_MODEL - RNN_BW), F32)], axis=1)
    return jnp.concatenate([top, row3, jnp.zeros((4, D_MODEL), F32)], axis=0)


_VECTORS = ("g_mix", "rnn_conv_b", "b_rg_a", "b_rg_x", "lru_lambda", "g_ffn", "g_ple", "g_final", "ffn_conv_b")
VEC_ROWS = 64
ROW_LOSS, ROW_RNN_CONV, ROW_SC_CONV, ROW_FFN_CONV = 14, 16, 20, 24


_SHARDED_BIG = ("w_in", "w_proj_a", "w_proj_b", "w_out", "w_up", "w_down", "w_ple_gate", "w_ple_proj")
_NAMES = ("g_mix", "w_in", "rnn_conv_w", "rnn_conv_b", "w_rg_a", "b_rg_a", "w_rg_x", "b_rg_x", "lru_lambda", "sc_conv_w",
          "w_proj_a", "w_proj_b", "w_out", "g_ffn", "w_up", "ffn_conv_w", "ffn_conv_b", "w_down", "w_ple_gate",
          "w_ple_proj", "g_ple", "g_final")


def _step(x, p, target, w, m, v):
    s = x.shape[0]
    tm = min(s, 256)
    tm_wide = min(s, 512)
    ch = min(s, 256)
    my_index = 4 * lax.axis_index("x") + 2 * lax.axis_index("y") + lax.axis_index("c")

    big = {n: w[n][0] for n in _SHARDED_BIG}
    conv_shard = _pack_conv_shard(w["rnn_conv_w"][0], w["sc_conv_w"][0], w["ffn_conv_w"][0])
    shards = {n: big[n].astype(BF16) for n in _SHARDED_BIG}
    w_in, conv_all = _comm_only(_gather_comm([shards["w_in"], conv_shard], by_columns=(0,)), "in_proj_weight_gather")
    w4 =jnp.transpose(jnp.concatenate([conv_all[:, :3, :RNN_BW], conv_all[:, 3:4, :RNN_BW]], axis=1),
                       (1, 0, 2)).reshape(4, D_MODEL)
    w3 = jnp.transpose(conv_all[:, :3, RNN_BW:2 * RNN_BW], (1, 0, 2)).reshape(3, D_MODEL)
    wc = jnp.transpose(conv_all[:, :3, 2 * RNN_BW:], (1, 0, 2)).reshape(3, 2, D_FF).transpose(1, 0, 2)
    bc = w["ffn_conv_b"].reshape(2, 1, D_FF)
    b4, b_a, b_x, lam = w["rnn_conv_b"], w["b_rg_a"], w["b_rg_x"], w["lru_lambda"]
    w_a, w_x = w["w_rg_a"][0], w["w_rg_x"][0]
    g_final = w["g_final"].reshape(1, D_MODEL)

    tk = min(s, 512)
    received = {}

    comm = _gather_comm([shards[n] for n in ("w_proj_a", "w_proj_b", "w_out", "w_down")])
    (z, h1), (w_pa, w_pb, w_out, w_down) = _norm_in_proj(x, w["g_mix"], w_in, tm_wide, comm)
    w_pa, w_pb, w_out = (a.reshape(D_MODEL, D_MODEL) for a in (w_pa, w_pb, w_out))
    w_down = w_down.reshape(D_FF, D_MODEL)
    comm = _gather_comm([shards[n] for n in ("w_up", "w_ple_gate", "w_ple_proj")], by_columns=(2,))
    (ya, yb, h), (w_up_g, w_gate, w_pp) = _mixers_fwd(z, w4, b4, w_a, b_a, w_x, b_x, lam, w3, ch, comm)
    w_gate = w_gate.reshape(D_MODEL, D_MODEL)
    pa, pb, mm, x2, h2 = _merge_out(ya, yb, z, x, w_pa, w_pb, w_out, w["g_ffn"], tm)
    u0, f = _up_proj_act(h2, w_up_g, wc, bc, tm)
    x3, dgt, de0, dx3, df, head_sums = _head(x2, f, p, target, w_down, w_gate, w_pp, w["g_ple"], g_final, tm)
    parts = [_wgrad_2d(x3, dgt, "wgrad_ple_gate", tk).reshape(N_DEV, RNN_BW, D_MODEL),
             jnp.transpose(_wgrad_2d(p, de0, "wgrad_ple_proj", tk).reshape(D_PLE, N_DEV, RNN_BW), (1, 0, 2)),
             _wgrad_2d(f, dx3, "wgrad_down", tk).reshape(N_DEV, D_FF // N_DEV, D_MODEL)]
    (du0, dwc, dbc), got = _ffn_act_bwd(u0, df, wc, bc, 256, ch, _exchange_comm(parts))
    received.update({n: [(g, 0)] for n, g in zip(("w_ple_gate", "w_ple_proj", "w_down"), got)})
    dw_up = _wgrad_up(h2, du0, w_up_g.shape[2], tk)
    (dx2, dzg, dpa, dpb, dya, dyb, ffn_sums), got_up = _up_bwd_merge_bwd(
        du0, w_up_g, x2, dx3, z, pa, pb, w_out, w_pa, w_pb, w["g_ffn"], tm, _exchange_comm([dw_up], (4, 5, 6, 2)))
    parts = [_wgrad_2d(mm, dx2, "wgrad_out", tk).reshape(N_DEV, RNN_BW, D_MODEL),
             _wgrad_2d(ya, dpa, "wgrad_proj_a", tk).reshape(N_DEV, RNN_BW, D_MODEL),
             _wgrad_2d(yb, dpb, "wgrad_proj_b", tk).reshape(N_DEV, RNN_BW, D_MODEL)]
    comm = _exchange_comm(parts)
    rest = _exchange_comm([dw_up], (1, 3, 7))
    (dz5, dw4, db4, dwa, dba, dwx, dbx, dlam, dw3), got = _mixers_bwd(z, h, dya, dyb, w4, b4, w_a, b_a, w_x, b_x, lam, w3,
                                                                       ch, _join_comms(comm, rest))
    received.update({n: [(g, 0)] for n, g in zip(("w_out", "w_proj_a", "w_proj_b"), got[:3])})
    received["w_up"] = [(got_up[0], 0), (got[3], 1)]
    chip_sum = _pair_reduce(_wgrad_in(h1, dz5, dzg, 512, tk), "in_proj_grad_pair_reduce")
    (grad_x, mix_sums), got = _in_proj_bwd(dz5, dzg, w_in, x, dx2, w["g_mix"], tm, _chip_exchange_comm([chip_sum]))
    received["w_in"] = [(got[0], 0)]

    zero_row = jnp.zeros((1, D_MODEL), F32)
    vec_pack = jnp.concatenate(
        [mix_sums[0:1], db4, dba, dbx, dlam, ffn_sums[0:1], head_sums[2:3], head_sums[1:2], dbc.reshape(6, D_MODEL),
         head_sums[0:1], zero_row, dw4, dw3, zero_row, jnp.transpose(dwc, (1, 0, 2)).reshape(18, D_MODEL),
         jnp.zeros((VEC_ROWS - ROW_FFN_CONV - 18, D_MODEL), F32)], axis=0)
    gate_pack = jnp.concatenate([dwa.reshape(D_MODEL, RNN_BW), dwx.reshape(D_MODEL, RNN_BW)], axis=0)
    vec_total, gate_total = _small_all_reduce([vec_pack, gate_pack])
    loss = jnp.sum(vec_total[ROW_LOSS]) * (0.5 / D_MODEL)

    out = {}
    for n in _SHARDED_BIG:
        res = _adam_shard(received[n], big[n], m[n][0], v[n][0], "adam_" + n)
        out[n] = [r[None] for r in res]

    def flat(t, n):
        return t[n].reshape(1, -1)

    res = _adam_small([(vec_total, i) for i in range(len(_VECTORS))],
                      [flat(w, n) for n in _VECTORS], [flat(m, n) for n in _VECTORS], [flat(v, n) for n in _VECTORS],
                      "adam_vectors")
    for n, r in zip(_VECTORS, res):
        out[n] = [a.reshape(w[n].shape) for a in r]
    gates = ("w_rg_a", "w_rg_x")
    res = _adam_small([gate_total[:D_MODEL], gate_total[D_MODEL:]], *[[t[n].reshape(D_MODEL, RNN_BW) for n in gates]
                                                                      for t in (w, m, v)], "adam_gate_maps")
    for n, r in zip(gates, res):
        out[n] = [a.reshape(w[n].shape) for a in r]
    convs = ("rnn_conv_w", "sc_conv_w", "ffn_conv_w")
    bw_ffn = 2 * D_FF // N_DEV
    g_conv = [lax.dynamic_slice(vec_total[ROW_RNN_CONV:ROW_RNN_CONV + 4], (0, my_index * RNN_BW), (4, RNN_BW)),
              lax.dynamic_slice(vec_total[ROW_SC_CONV:ROW_SC_CONV + 3], (0, my_index * RNN_BW), (3, RNN_BW)),
              lax.dynamic_slice(vec_total[ROW_FFN_CONV:ROW_FFN_CONV + 18].reshape(3, 2 * D_FF), (0, my_index * bw_ffn),
                                (3, bw_ffn))]
    res = _adam_small(g_conv, *[[t[n][0] for n in convs] for t in (w, m, v)], "adam_conv")
    for n, r in zip(convs, res):
        out[n] = [a[None] for a in r]

    return (loss, grad_x[None]) + tuple(out[n][k] for k in range(4) for n in _NAMES)


def kernel(x, p, g_mix, w_in, rnn_conv_w, rnn_conv_b, w_rg_a, b_rg_a, w_rg_x, b_rg_x, lru_lambda, sc_conv_w, w_proj_a, w_proj_b, w_out, g_ffn, w_up, ffn_conv_w, ffn_conv_b, w_down, w_ple_gate, w_ple_proj, g_ple, g_final, loss_target, m_g_mix, m_w_in, m_rnn_conv_w, m_rnn_conv_b, m_w_rg_a, m_b_rg_a, m_w_rg_x, m_b_rg_x, m_lru_lambda, m_sc_conv_w, m_w_proj_a, m_w_proj_b, m_w_out, m_g_ffn, m_w_up, m_ffn_conv_w, m_ffn_conv_b, m_w_down, m_w_ple_gate, m_w_ple_proj, m_g_ple, m_g_final, v_g_mix, v_w_in, v_rnn_conv_w, v_rnn_conv_b, v_w_rg_a, v_b_rg_a, v_w_rg_x, v_b_rg_x, v_lru_lambda, v_sc_conv_w, v_w_proj_a, v_w_proj_b, v_w_out, v_g_ffn, v_w_up, v_ffn_conv_w, v_ffn_conv_b, v_w_down, v_w_ple_gate, v_w_ple_proj, v_g_ple, v_g_final):
    given = dict(locals())
    w = {n: given[n] for n in _NAMES}
    m = {n: given["m_" + n] for n in _NAMES}
    v = {n: given["v_" + n] for n in _NAMES}
    return _step(x[0], p[0, 0], loss_target[0], w, m, v)
```

```python
import functools

import jax
import jax.numpy as jnp
from jax import lax
from jax.experimental import pallas as pl
from jax.experimental.pallas import tpu as pltpu

F32 = jnp.float32
BF16 = jnp.bfloat16
MESH_ID = pl.DeviceIdType.MESH

N_DEV = 8
D_MODEL = 1024
D_PLE = 256
RNN_BW = 128
N_SEG = 7
D_FF = 3072
LRU_C = 8.0
EPS = 1e-6
ADAM_LR = 0.001
ADAM_B1 = 0.9
ADAM_B2 = 0.999
ADAM_EPS = 1e-08
ADAM_WD = 0.01
ADAM_STEP = 10

HALO = 16
SCAN_UNROLL = 4
VMEM_LIMIT = 56 * 1024 * 1024


def _dot(a, b):
    return jnp.dot(a, b, preferred_element_type=F32)


def _dot_nt(a, b):
    return lax.dot_general(a, b, (((1,), (1,)), ((), ())), preferred_element_type=F32)


def _dot_tn(a, b):
    return lax.dot_general(a, b, (((0,), (0,)), ((), ())), preferred_element_type=F32)


def _sigmoid(x):
    return 0.5 * jnp.tanh(0.5 * x) + 0.5


_GELU_C = 0.7978845608028654
_GELU_K = 0.044715


def _gelu(x):
    return 0.5 * x * (1.0 + jnp.tanh(_GELU_C * (x + _GELU_K * (x * x * x))))


def _gelu_and_grad(x):
    x2 = x * x
    t = jnp.tanh(_GELU_C * (x + _GELU_K * (x * x2)))
    g = 0.5 * x * (1.0 + t)
    dg = 0.5 * (1.0 + t) + 0.5 * x * (1.0 - t * t) * (_GELU_C * (1.0 + 3.0 * _GELU_K * x2))
    return g, dg


def _one_minus_sq(log_a, a):
    series = (-2.0 * log_a) * (1.0 + log_a * (1.0 + log_a * (2.0 / 3.0)))
    return jnp.where(log_a > -0.005, series, 1.0 - a * a)


def _softplus(x):
    return jnp.maximum(x, 0.0) + jnp.log1p(jnp.exp(-jnp.abs(x)))


def _rms(u):
    r = lax.rsqrt(jnp.mean(u * u, axis=-1, keepdims=True) + EPS)
    return u * r, r


def _rms_bwd(dn, un, r):
    return r * (dn - un * jnp.mean(dn * un, axis=-1, keepdims=True))


def _shift_dn(ext, s, n):
    if s == 0:
        return ext[HALO:HALO + n]
    return pltpu.roll(ext, s, 0)[HALO:HALO + n]


def _shift_up(ext, s, n):
    if s == 0:
        return ext[0:n]
    return pltpu.roll(ext, n + HALO - s, 0)[0:n]


def _ext_before(ref, c, ch):
    t0 = c * ch
    prev = pl.multiple_of(jnp.maximum(t0 - HALO, 0), HALO)
    halo = jnp.where(c > 0, ref[pl.ds(prev, HALO), :].astype(F32), 0.0)
    cur = ref[pl.ds(pl.multiple_of(t0, HALO), ch), :].astype(F32)
    return jnp.concatenate([halo, cur], axis=0)


def _ext_after(ref, c, ch, n_chunks):
    t0 = c * ch
    nxt = pl.multiple_of(jnp.minimum(t0 + ch, (n_chunks - 1) * ch + ch - HALO), HALO)
    halo = jnp.where(c < n_chunks - 1, ref[pl.ds(nxt, HALO), :].astype(F32), 0.0)
    cur = ref[pl.ds(pl.multiple_of(t0, HALO), ch), :].astype(F32)
    return jnp.concatenate([cur, halo], axis=0)


def _rows(ref, c, ch):
    return ref[pl.ds(pl.multiple_of(c * ch, HALO), ch), :]


def _scan_fwd(a_ref, b_ref, h_ref, n_rows, cw):
    rows = lax.broadcasted_iota(jnp.int32, (8, cw), 0)

    def body(i, carry):
        for u in range(SCAN_UNROLL):
            t = pl.multiple_of((i * SCAN_UNROLL + u) * 8, 8)
            a = a_ref[pl.ds(t, 8), :]
            b = b_ref[pl.ds(t, 8), :]
            for d in (1, 2, 4):
                m = rows >= d
                b = jnp.where(m, a * pltpu.roll(b, d, 0) + b, b)
                a = jnp.where(m, a * pltpu.roll(a, d, 0), a)
            h_ref[pl.ds(t, 8), :] = a * carry + b
            carry = jnp.broadcast_to(a[7:8, :], (8, cw)) * carry + jnp.broadcast_to(b[7:8, :], (8, cw))
        return carry

    lax.fori_loop(0, n_rows // (8 * SCAN_UNROLL), body, jnp.zeros((8, cw), F32))


def _scan_rev(a_ref, g_ref, o_ref, n_rows, cw):
    rows = lax.broadcasted_iota(jnp.int32, (8, cw), 0)
    n_groups = n_rows // 8

    def body(i, carry):
        dh_next, a_next = carry
        for u in range(SCAN_UNROLL):
            t = pl.multiple_of((n_groups - 1 - (i * SCAN_UNROLL + u)) * 8, 8)
            a = a_ref[pl.ds(t, 8), :]
            g = g_ref[pl.ds(t, 8), :]
            an = jnp.where(rows < 7, pltpu.roll(a, 7, 0), a_next)
            a_next = jnp.broadcast_to(a[0:1, :], (8, cw))
            for d in (1, 2, 4):
                m = rows < 8 - d
                g = jnp.where(m, an * pltpu.roll(g, 8 - d, 0) + g, g)
                an = jnp.where(m, an * pltpu.roll(an, 8 - d, 0), an)
            o_ref[pl.ds(t, 8), :] = an * dh_next + g
            dh_next = jnp.broadcast_to(an[0:1, :], (8, cw)) * dh_next + jnp.broadcast_to(g[0:1, :], (8, cw))
        return dh_next, a_next

    zero = jnp.zeros((8, cw), F32)
    lax.fori_loop(0, n_groups // SCAN_UNROLL, body, (zero, zero))


def _params(*sem):
    return pltpu.CompilerParams(dimension_semantics=sem, vmem_limit_bytes=VMEM_LIMIT)


def _sds(shape, dtype):
    return jax.ShapeDtypeStruct(shape, dtype)


def _resident(shape):
    zeros = (0,) * len(shape)
    return pl.BlockSpec(shape, lambda *_: zeros, pipeline_mode=pl.Buffered(1))


class _Comm:
    def __init__(self, inputs, out_shape, scratch, start, finish):
        self.inputs, self.out_shape, self.scratch, self.start, self.finish = inputs, out_shape, scratch, start, finish


def _join_comms(a, b):
    n, k = len(a.inputs), len(a.scratch)

    def start(ins, outs, sems):
        a.start(ins[:n], outs[:n], sems[:k])
        b.start(ins[n:], outs[n:], sems[k:])

    def finish(ins, outs, sems):
        a.finish(ins[:n], outs[:n], sems[:k])
        b.finish(ins[n:], outs[n:], sems[k:])

    return _Comm(a.inputs + b.inputs, a.out_shape + b.out_shape, a.scratch + b.scratch, start, finish)


def _sem_scratch(n, copies=7):
    return [pltpu.SemaphoreType.DMA((n, copies)), pltpu.SemaphoreType.DMA((n, copies)), pltpu.SemaphoreType.DMA((n,))]


def _gather_comm(shards, by_columns=()):
    n = len(shards)

    def plan(ins, outs, sems):
        send_sems, recv_sems, local_sems = sems
        x, y, c = lax.axis_index("x"), lax.axis_index("y"), lax.axis_index("c")
        me, sibling = (x, y, c), (x, y, 1 - c)
        chips = [(1 - x, y), (x, 1 - y), (1 - x, 1 - y)]

        def slot(t, dev):
            idx = 4 * dev[0] + 2 * dev[1] + dev[2]
            if t in by_columns:
                width = shards[t].shape[1]
                return outs[t].at[:, pl.ds(pl.multiple_of(idx * width, 128), width)]
            return outs[t].at[idx]

        def copy(t, k, block, to, src=None):
            return pltpu.make_async_remote_copy(
                src_ref=slot(t, block) if src is None else src, dst_ref=slot(t, block),
                send_sem=send_sems.at[t, k], recv_sem=recv_sems.at[t, k],
                device_id=to, device_id_type=MESH_ID)

        mine = [pltpu.make_async_copy(ins[t], slot(t, me), local_sems.at[t]) for t in range(n)]
        first = []
        for t in range(n):
            first.append(copy(t, 0, me, sibling, src=ins[t]))
            first += [copy(t, 1 + j, me, (*chip, c), src=ins[t]) for j, chip in enumerate(chips)]
        return me, sibling, chips, c, copy, mine, first

    def start(ins, outs, sems):
        *_, mine, first = plan(ins, outs, sems)
        for cp in mine + first:
            cp.start()

    def finish(ins, outs, sems):
        me, sibling, chips, c, copy, mine, first = plan(ins, outs, sems)
        passed = []
        for t in range(n):
            for j, chip in enumerate(chips):
                copy(t, 1 + j, (*chip, c), me).wait_recv()
                cp = copy(t, 4 + j, (*chip, c), sibling)
                cp.start()
                passed.append(cp)
        for t in range(n):
            copy(t, 0, sibling, me).wait_recv()
            for j, chip in enumerate(chips):
                copy(t, 4 + j, (*chip, 1 - c), me).wait_recv()
        for cp in first + passed:
            cp.wait_send()
        for cp in mine:
            cp.wait()

    out_shape = [_sds((s.shape[0], N_DEV * s.shape[1]) if t in by_columns else (N_DEV,) + s.shape, s.dtype)
                 for t, s in enumerate(shards)]
    return _Comm(list(shards), out_shape, _sem_scratch(n), start, finish)


def _peer(d):
    x, y, c = lax.axis_index("x"), lax.axis_index("y"), lax.axis_index("c")
    px = 1 - x if d & 4 else x
    py = 1 - y if d & 2 else y
    pc = 1 - c if d & 1 else c
    return (px, py, pc), 4 * px + 2 * py + pc


def _exchange_comm(parts, deltas=tuple(range(1, N_DEV))):
    n = len(parts)

    def plan(ins, outs, sems):
        send_sems, recv_sems, local_sems = sems
        _, me = _peer(0)

        def copy(t, i):
            peer, idx = _peer(deltas[i])
            return pltpu.make_async_remote_copy(
                src_ref=ins[t].at[idx], dst_ref=outs[t].at[1 + i],
                send_sem=send_sems.at[t, i], recv_sem=recv_sems.at[t, i],
                device_id=peer, device_id_type=MESH_ID)

        mine = [pltpu.make_async_copy(ins[t].at[me], outs[t].at[0], local_sems.at[t]) for t in range(n)]
        return mine, [copy(t, i) for t in range(n) for i in range(len(deltas))]

    def start(ins, outs, sems):
        mine, copies = plan(ins, outs, sems)
        for cp in mine + copies:
            cp.start()

    def finish(ins, outs, sems):
        mine, copies = plan(ins, outs, sems)
        for cp in copies:
            cp.wait_recv()
        for cp in copies:
            cp.wait_send()
        for cp in mine:
            cp.wait()

    return _Comm(list(parts), [_sds((1 + len(deltas),) + p.shape[1:], p.dtype) for p in parts],
                 _sem_scratch(n, len(deltas)), start, finish)


def _chip_exchange_comm(parts):
    n = len(parts)

    def plan(ins, outs, sems):
        send_sems, recv_sems, local_sems = sems
        x, y, c = lax.axis_index("x"), lax.axis_index("y"), lax.axis_index("c")

        def copy(t, dq):
            px = 1 - x if dq & 2 else x
            py = 1 - y if dq & 1 else y
            return pltpu.make_async_remote_copy(
                src_ref=ins[t].at[2 * px + py], dst_ref=outs[t].at[dq],
                send_sem=send_sems.at[t, dq - 1], recv_sem=recv_sems.at[t, dq - 1],
                device_id=(px, py, c), device_id_type=MESH_ID)

        mine = [pltpu.make_async_copy(ins[t].at[2 * x + y], outs[t].at[0], local_sems.at[t]) for t in range(n)]
        return mine, [copy(t, dq) for t in range(n) for dq in range(1, 4)]

    def start(ins, outs, sems):
        mine, copies = plan(ins, outs, sems)
        for cp in mine + copies:
            cp.start()

    def finish(ins, outs, sems):
        mine, copies = plan(ins, outs, sems)
        for cp in copies:
            cp.wait_recv()
        for cp in copies:
            cp.wait_send()
        for cp in mine:
            cp.wait()

    return _Comm(list(parts), [_sds(p.shape, p.dtype) for p in parts], _sem_scratch(n, 3), start, finish)


def _pair_reduce(part, name):
    r, c = part.shape[0], part.shape[1] // N_DEV
    n_chips = N_DEV // 2
    tr = 128

    def body(p_ref, o_ref, own, land, send_sems, recv_sems, local_sems):
        x, y, core = lax.axis_index("x"), lax.axis_index("y"), lax.axis_index("c")

        def block(k):
            return p_ref.at[:, pl.ds(pl.multiple_of(k * c, 128), c)]

        sends = [pltpu.make_async_remote_copy(
            src_ref=block(2 * q + (1 - core)), dst_ref=land.at[q], send_sem=send_sems.at[q], recv_sem=recv_sems.at[q],
            device_id=(x, y, 1 - core), device_id_type=MESH_ID) for q in range(n_chips)]
        loads = [pltpu.make_async_copy(block(2 * q + core), own.at[q], local_sems.at[q]) for q in range(n_chips)]
        for cp in sends + loads:
            cp.start()
        for q in range(n_chips):
            loads[q].wait()
            sends[q].wait_recv()

            def add_rows(i, _):
                rows = pl.ds(pl.multiple_of(i * tr, tr), tr)
                o_ref[q, rows, :] = (own[q, rows, :].astype(F32) + land[q, rows, :].astype(F32)).astype(BF16)
                return 0

            lax.fori_loop(0, r // tr, add_rows, 0)
        for cp in sends:
            cp.wait_send()

    return pl.pallas_call(
        body, name=name, out_shape=_sds((n_chips, r, c), BF16),
        in_specs=[pl.BlockSpec(memory_space=pl.ANY)], out_specs=pl.BlockSpec(memory_space=pltpu.VMEM),
        scratch_shapes=[pltpu.VMEM((n_chips, r, c), BF16), pltpu.VMEM((n_chips, r, c), BF16)]
        + [pltpu.SemaphoreType.DMA((n_chips,))] * 3,
        compiler_params=pltpu.CompilerParams(vmem_limit_bytes=VMEM_LIMIT),
    )(part)


def _comm_only(comm, name):
    n = len(comm.inputs)

    def body(*refs):
        ins, outs, sems = refs[:n], refs[n:2 * n], refs[2 * n:]
        comm.start(ins, outs, sems)
        comm.finish(ins, outs, sems)

    any_spec = pl.BlockSpec(memory_space=pl.ANY)
    return pl.pallas_call(body, name=name, out_shape=comm.out_shape, in_specs=[any_spec] * n, out_specs=[any_spec] * n,
                          scratch_shapes=comm.scratch)(*comm.inputs)


def _host_call(body, *, name, grid, in_specs, out_specs, out_shape, scratch_shapes=(), semantics, args, comm=None):
    if comm is None:
        res = pl.pallas_call(body, name=name, grid=grid, in_specs=in_specs, out_specs=out_specs, out_shape=out_shape,
                             scratch_shapes=list(scratch_shapes), compiler_params=_params(*semantics))(*args)
        return res, []
    n_in, n_out, n_scr, n_c = len(in_specs), len(out_specs), len(scratch_shapes), len(comm.inputs)

    def with_comm(*refs):
        ins, refs = refs[:n_in], refs[n_in:]
        c_ins, refs = refs[:n_c], refs[n_c:]
        outs, refs = refs[:n_out], refs[n_out:]
        c_outs, refs = refs[:n_c], refs[n_c:]
        scr, sems = refs[:n_scr], refs[n_scr:]
        ids = [pl.program_id(a) for a in range(len(grid))]
        first = functools.reduce(jnp.logical_and, [i == 0 for i in ids])
        last = functools.reduce(jnp.logical_and, [i == g - 1 for i, g in zip(ids, grid)])

        @pl.when(first)
        def _():
            comm.start(c_ins, c_outs, sems)

        body(*ins, *outs, *scr)

        @pl.when(last)
        def _():
            comm.finish(c_ins, c_outs, sems)

    any_spec = pl.BlockSpec(memory_space=pl.ANY)
    res = pl.pallas_call(
        with_comm, name=name, grid=grid, in_specs=list(in_specs) + [any_spec] * n_c,
        out_specs=list(out_specs) + [any_spec] * n_c, out_shape=list(out_shape) + comm.out_shape,
        scratch_shapes=list(scratch_shapes) + comm.scratch,
        compiler_params=_params(*(["arbitrary"] * len(grid))))(*args, *comm.inputs)
    return res[:n_out], res[n_out:]


def _small_all_reduce(packs):
    n = len(packs)
    pieces = [p.shape[0] // N_DEV for p in packs]

    def body(*refs):
        p_refs, o_refs, lands = refs[:n], refs[n:2 * n], refs[2 * n:3 * n]
        s1, r1, s2, r2 = refs[3 * n:]
        _, me = _peer(0)

        def piece(t, ref, idx):
            return ref.at[pl.ds(pl.multiple_of(idx * pieces[t], 8), pieces[t]), :]

        def scatter(t, d):
            peer, idx = _peer(d)
            return pltpu.make_async_remote_copy(
                src_ref=piece(t, p_refs[t], idx), dst_ref=lands[t].at[d - 1], send_sem=s1.at[t, d - 1],
                recv_sem=r1.at[t, d - 1], device_id=peer, device_id_type=MESH_ID)

        def gather(t, d, from_idx):
            peer, _ = _peer(d)
            return pltpu.make_async_remote_copy(
                src_ref=piece(t, o_refs[t], from_idx), dst_ref=piece(t, o_refs[t], from_idx), send_sem=s2.at[t, d - 1],
                recv_sem=r2.at[t, d - 1], device_id=peer, device_id_type=MESH_ID)

        first = [[scatter(t, d) for d in range(1, N_DEV)] for t in range(n)]
        for cp in sum(first, []):
            cp.start()
        second = []
        for t in range(n):
            acc = piece(t, p_refs[t], me)[...]
            for d in range(1, N_DEV):
                first[t][d - 1].wait_recv()
                acc = acc + lands[t][d - 1]
            piece(t, o_refs[t], me)[...] = acc
            second += [gather(t, d, me) for d in range(1, N_DEV)]
            for cp in second[-(N_DEV - 1):]:
                cp.start()
        for t in range(n):
            for d in range(1, N_DEV):
                gather(t, d, _peer(d)[1]).wait_recv()
        for cp in sum(first, []) + second:
            cp.wait_send()

    vmem = pl.BlockSpec(memory_space=pltpu.VMEM)
    return pl.pallas_call(
        body, name="small_all_reduce",
        out_shape=[_sds(p.shape, F32) for p in packs], in_specs=[vmem] * n, out_specs=[vmem] * n,
        scratch_shapes=[pltpu.VMEM((N_DEV - 1, pc, p.shape[1]), F32) for pc, p in zip(pieces, packs)]
        + [pltpu.SemaphoreType.DMA((n, N_DEV - 1))] * 4,
    )(*packs)


def _norm_in_proj(x, g_mix, w_in, tm, comm):
    s = x.shape[0]

    def body(x_ref, g_ref, w_ref, z_ref, h_ref):
        xn, _ = _rms(x_ref[...])
        h = (xn * g_ref[...]).astype(BF16)
        h_ref[...] = h
        for j in range(N_SEG):
            cols = slice(j * D_MODEL, (j + 1) * D_MODEL)
            z_ref[:, cols] = _dot(h, w_ref[:, cols]).astype(BF16)

    return _host_call(
        body, name="norm_in_proj", grid=(s // tm,),
        in_specs=[pl.BlockSpec((tm, D_MODEL), lambda i: (i, 0)), pl.BlockSpec((1, D_MODEL), lambda i: (0, 0)),
                  _resident(w_in.shape)],
        out_specs=[pl.BlockSpec((tm, N_SEG * D_MODEL), lambda i: (i, 0)), pl.BlockSpec((tm, D_MODEL), lambda i: (i, 0))],
        out_shape=[_sds((s, N_SEG * D_MODEL), BF16), _sds((s, D_MODEL), BF16)],
        semantics=("parallel",), args=(x, g_mix, w_in), comm=comm)


def _gate_chunk(xr_ext, w4_ref, cb_ref, wa, wx, ba_ref, bx_ref, sp, ch):
    xc = cb_ref[...] + w4_ref[3:4, :] * _shift_dn(xr_ext, 0, ch)
    for k in range(3):
        xc = xc + w4_ref[k:k + 1, :] * _shift_dn(xr_ext, 3 - k, ch)
    xcb = xc.astype(BF16)
    ra = _sigmoid(_dot(xcb, wa) + ba_ref[...])
    ia = _sigmoid(_dot(xcb, wx) + bx_ref[...])
    log_a = (-LRU_C) * ra * sp
    return xc, ra, ia, log_a


def _conv3(q_ext, w3_ref, ch):
    y = w3_ref[2:3, :] * _shift_dn(q_ext, 0, ch)
    for k in range(2):
        y = y + w3_ref[k:k + 1, :] * _shift_dn(q_ext, 2 - k, ch)
    return y


def _z_block_specs(s, cw, segs):
    nb = D_MODEL // cw
    return [pl.BlockSpec((s, cw), functools.partial(lambda j, seg: (0, seg * nb + j), seg=seg)) for seg in segs]


def _mixers_fwd(z, w4, b4, w_a, b_a, w_x, b_x, lam, w3, ch, comm):
    s = z.shape[0]
    cw = RNN_BW
    n_chunks = s // ch

    def body(xr_ref, gr_ref, cb_ref, cc_ref, cx_ref, w4_ref, b4_ref, wa_ref, ba_ref, wx_ref, bx_ref, lam_ref, w3_ref,
             ya_ref, yb_ref, h_ref, a_scr, b_scr):
        sp = _softplus(-lam_ref[...])
        wa = wa_ref[0].astype(BF16)
        wx = wx_ref[0].astype(BF16)

        def gates(c, _):
            xc, _, ia, log_a = _gate_chunk(_ext_before(xr_ref, c, ch), w4_ref, b4_ref, wa, wx, ba_ref, bx_ref, sp, ch)
            rows = pl.ds(pl.multiple_of(c * ch, HALO), ch)
            a = jnp.exp(log_a)
            a_scr[rows, :] = a
            b_scr[rows, :] = jnp.sqrt(_one_minus_sq(log_a, a)) * (ia * xc)
            q_ext = _ext_before(cc_ref, c, ch) * _ext_before(cx_ref, c, ch)
            yb_ref[rows, :] = (_rows(cb_ref, c, ch).astype(F32) * _conv3(q_ext, w3_ref, ch)).astype(BF16)
            return 0

        lax.fori_loop(0, n_chunks, gates, 0)
        _scan_fwd(a_scr, b_scr, h_ref, s, cw)

        def outputs(c, _):
            rows = pl.ds(pl.multiple_of(c * ch, HALO), ch)
            ya_ref[rows, :] = (h_ref[rows, :] * _gelu(gr_ref[rows, :].astype(F32))).astype(BF16)
            return 0

        lax.fori_loop(0, n_chunks, outputs, 0)

    col = lambda j: (0, j)
    vec = pl.BlockSpec((1, cw), col)
    sq = pl.BlockSpec((1, cw, cw), lambda j: (j, 0, 0))
    act = pl.BlockSpec((s, cw), col)
    return _host_call(
        body, name="mixers_fwd", grid=(D_MODEL // cw,),
        in_specs=_z_block_specs(s, cw, range(5)) + [pl.BlockSpec((4, cw), col), vec, sq, vec, sq, vec, vec,
                                                     pl.BlockSpec((3, cw), col)],
        out_specs=[act, act, act],
        out_shape=[_sds((s, D_MODEL), BF16), _sds((s, D_MODEL), BF16), _sds((s, D_MODEL), F32)],
        scratch_shapes=[pltpu.VMEM((s, cw), F32), pltpu.VMEM((s, cw), F32)],
        semantics=("parallel",), args=(z, z, z, z, z, w4, b4, w_a, b_a, w_x, b_x, lam, w3), comm=comm)


def _merge_out(ya, yb, z, x, w_pa, w_pb, w_out, g_ffn, tm):
    s = x.shape[0]

    def body(ya_ref, yb_ref, ga_ref, gb_ref, x_ref, wa_ref, wb_ref, wo_ref, g_ref, pa_ref, pb_ref, m_ref, x2_ref, h2_ref):
        pa = _dot(ya_ref[...], wa_ref[...])
        pb = _dot(yb_ref[...], wb_ref[...])
        pa_ref[...] = pa.astype(BF16)
        pb_ref[...] = pb.astype(BF16)
        m = (_sigmoid(ga_ref[...].astype(F32)) * pa + _sigmoid(gb_ref[...].astype(F32)) * pb).astype(BF16)
        m_ref[...] = m
        x2 = x_ref[...] + _dot(m, wo_ref[...])
        x2_ref[...] = x2
        xn, _ = _rms(x2)
        h2_ref[...] = (xn * g_ref[...]).astype(BF16)

    tile = pl.BlockSpec((tm, D_MODEL), lambda i: (i, 0))
    full = _resident((D_MODEL, D_MODEL))
    return pl.pallas_call(
        body, name="merge_out", grid=(s // tm,),
        in_specs=[tile, tile, pl.BlockSpec((tm, D_MODEL), lambda i: (i, 5)), pl.BlockSpec((tm, D_MODEL), lambda i: (i, 6)),
                  tile, full, full, full, pl.BlockSpec((1, D_MODEL), lambda i: (0, 0))],
        out_specs=[tile] * 5,
        out_shape=[_sds((s, D_MODEL), BF16)] * 3 + [_sds((s, D_MODEL), F32), _sds((s, D_MODEL), BF16)],
        compiler_params=_params("parallel"),
    )(ya, yb, z, z, x, w_pa, w_pb, w_out, g_ffn)


def _up_proj_act(h2, w_up_g, wc, bc, tm, comm):
    s = h2.shape[0]
    bw = w_up_g.shape[2]
    per_half = N_DEV // 2

    def body(h_ref, w_ref, wc_ref, bc_ref, u_ref, f_ref, halo):
        @pl.when(pl.program_id(0) == 0)
        def _():
            halo[...] = jnp.zeros_like(halo)

        h = h_ref[...]
        for j in range(per_half):
            cols = slice(j * bw, (j + 1) * bw)
            conv = []
            for half in range(2):
                u = _dot(h, w_ref[half * per_half + j]).astype(BF16)
                u_ref[half, :, cols] = u
                u = u.astype(F32)
                ext = jnp.concatenate([halo[half, :, cols], u], axis=0)
                halo[half, :, cols] = u[tm - HALO:tm]
                w = wc_ref.at[half]
                acc = bc_ref[half, :, cols] + w[2:3, cols] * u
                for k in range(2):
                    acc = acc + w[k:k + 1, cols] * _shift_dn(ext, 2 - k, tm)
                conv.append(acc)
            f_ref[:, cols] = (_gelu(conv[0]) * conv[1]).astype(BF16)

    return _host_call(
        body, name="up_proj_act", grid=(s // tm,),
        in_specs=[pl.BlockSpec((tm, D_MODEL), lambda i: (i, 0)), _resident(w_up_g.shape), _resident(wc.shape),
                  _resident(bc.shape)],
        out_specs=[pl.BlockSpec((2, tm, D_FF), lambda i: (0, i, 0)), pl.BlockSpec((tm, D_FF), lambda i: (i, 0))],
        out_shape=[_sds((2, s, D_FF), BF16), _sds((s, D_FF), BF16)],
        scratch_shapes=[pltpu.VMEM((2, HALO, D_FF), F32)],
        semantics=("arbitrary",), args=(h2, w_up_g, wc, bc), comm=comm)


def _ffn_conv(u_ref, w_ref, b_ref, half, c, ch):
    ext = _ext_before(u_ref.at[half], c, ch)
    w = w_ref.at[half]
    return b_ref[half] + _conv3(ext, w, ch), ext


def _head(x2, f, p, target, w_down, w_gate, w_pp, g_ple, g_final, tm):
    s = x2.shape[0]

    def body(x2_ref, f_ref, p_ref, t_ref, wd_ref, wg_ref, wp_ref, gp_ref, gf_ref,
             x3_ref, dgt_ref, de0_ref, dx3_ref, df_ref, sums_ref):
        @pl.when(pl.program_id(0) == 0)
        def _():
            sums_ref[...] = jnp.zeros_like(sums_ref)

        x3 = x2_ref[...] + _dot(f_ref[...], wd_ref[...])
        x3b = x3.astype(BF16)
        x3_ref[...] = x3b
        e0n, re = _rms(_dot(p_ref[...].astype(BF16), wp_ref[...]))
        e = e0n * gp_ref[...]
        sg = _sigmoid(_dot(x3b, wg_ref[...]))
        x4n, r4 = _rms(x3 + sg * e)
        diff = x4n * gf_ref[...] - t_ref[...]
        sums_ref[0:1, :] += jnp.sum(diff * diff, axis=0, keepdims=True)
        dy = diff * (1.0 / D_MODEL)
        sums_ref[1:2, :] += jnp.sum(dy * x4n, axis=0, keepdims=True)
        dx4 = _rms_bwd(dy * gf_ref[...], x4n, r4)
        de = dx4 * sg
        dgt = ((dx4 * e) * (sg * (1.0 - sg))).astype(BF16)
        dgt_ref[...] = dgt
        sums_ref[2:3, :] += jnp.sum(de * e0n, axis=0, keepdims=True)
        de0_ref[...] = _rms_bwd(de * gp_ref[...], e0n, re).astype(BF16)
        dx3 = dx4 + _dot_nt(dgt, wg_ref[...])
        dx3_ref[...] = dx3
        df_ref[...] = _dot_nt(dx3.astype(BF16), wd_ref[...]).astype(BF16)

    tile = pl.BlockSpec((tm, D_MODEL), lambda i: (i, 0))
    vec = pl.BlockSpec((1, D_MODEL), lambda i: (0, 0))
    wide = pl.BlockSpec((tm, D_FF), lambda i: (i, 0))
    return pl.pallas_call(
        body, name="loss_head", grid=(s // tm,),
        in_specs=[tile, wide, pl.BlockSpec((tm, D_PLE), lambda i: (i, 0)), tile,
                  _resident((D_FF, D_MODEL)), _resident((D_MODEL, D_MODEL)), _resident((D_PLE, D_MODEL)), vec, vec],
        out_specs=[tile, tile, tile, tile, wide, pl.BlockSpec((8, D_MODEL), lambda i: (0, 0))],
        out_shape=[_sds((s, D_MODEL), BF16)] * 3 + [_sds((s, D_MODEL), F32), _sds((s, D_FF), BF16),
                                                    _sds((8, D_MODEL), F32)],
        compiler_params=_params("arbitrary"),
    )(x2, f, p, target, w_down, w_gate, w_pp, g_ple, g_final)


def _ffn_act_bwd(u0, df, wc, bc, cw, ch, comm):
    s = u0.shape[1]
    n_chunks = s // ch

    def body(u_ref, df_ref, w_ref, b_ref, du0_ref, dw_ref, db_ref, du_scr):
        dw_ref[...] = jnp.zeros_like(dw_ref)
        db_ref[...] = jnp.zeros_like(db_ref)

        def first(c, _):
            ug, ext_g = _ffn_conv(u_ref, w_ref, b_ref, 0, c, ch)
            uv, ext_v = _ffn_conv(u_ref, w_ref, b_ref, 1, c, ch)
            gel, dgel = _gelu_and_grad(ug)
            d = _rows(df_ref, c, ch).astype(F32)
            rows = pl.ds(pl.multiple_of(c * ch, HALO), ch)
            for half, du, ext in ((0, d * uv * dgel, ext_g), (1, d * gel, ext_v)):
                du_scr[half, rows, :] = du
                db_ref[half] += jnp.sum(du, axis=0, keepdims=True)
                for k in range(3):
                    dw_ref[half, k:k + 1, :] += jnp.sum(du * _shift_dn(ext, 2 - k, ch), axis=0, keepdims=True)
            return 0

        lax.fori_loop(0, n_chunks, first, 0)

        def second(c, _):
            rows = pl.ds(pl.multiple_of(c * ch, HALO), ch)
            for half in range(2):
                ext = _ext_after(du_scr.at[half], c, ch, n_chunks)
                w = w_ref.at[half]
                acc = w[2:3, :] * _shift_up(ext, 0, ch)
                for k in range(2):
                    acc = acc + w[k:k + 1, :] * _shift_up(ext, 2 - k, ch)
                du0_ref[half, rows, :] = acc.astype(BF16)
            return 0

        lax.fori_loop(0, n_chunks, second, 0)

    blk = lambda r: pl.BlockSpec((2, r, cw), lambda j: (0, 0, j))
    return _host_call(
        body, name="ffn_act_bwd", grid=(D_FF // cw,),
        in_specs=[blk(s), pl.BlockSpec((s, cw), lambda j: (0, j)), blk(3), blk(1)],
        out_specs=[blk(s), blk(3), blk(1)],
        out_shape=[_sds((2, s, D_FF), BF16), _sds((2, 3, D_FF), F32), _sds((2, 1, D_FF), F32)],
        scratch_shapes=[pltpu.VMEM((2, s, cw), F32)],
        semantics=("parallel",), args=(u0, df, wc, bc), comm=comm)


def _up_bwd_merge_bwd(du0, w_up_g, x2, dx3, z, pa, pb, w_out, w_pa, w_pb, g_ffn, tm, comm):
    s = x2.shape[0]
    bw = w_up_g.shape[2]
    per_half = N_DEV // 2

    def body(du_ref, wu_ref, x2_ref, dx3_ref, ga_ref, gb_ref, pa_ref, pb_ref, wo_ref, wa_ref, wb_ref, g_ref,
             dx2_ref, dzg_ref, dpa_ref, dpb_ref, dya_ref, dyb_ref, sums_ref):
        @pl.when(pl.program_id(0) == 0)
        def _():
            sums_ref[...] = jnp.zeros_like(sums_ref)

        dh2 = None
        for j in range(N_DEV):
            cols = slice((j % per_half) * bw, (j % per_half + 1) * bw)
            term = _dot_nt(du_ref[j // per_half, :, cols], wu_ref[j])
            dh2 = term if dh2 is None else dh2 + term
        x2n, r2 = _rms(x2_ref[...])
        sums_ref[0:1, :] += jnp.sum(dh2 * x2n, axis=0, keepdims=True)
        dx2 = dx3_ref[...] + _rms_bwd(dh2 * g_ref[...], x2n, r2)
        dx2_ref[...] = dx2
        dm = _dot_nt(dx2.astype(BF16), wo_ref[...])
        sa = _sigmoid(ga_ref[...].astype(F32))
        sb = _sigmoid(gb_ref[...].astype(F32))
        dzg_ref[0] = (dm * pa_ref[...].astype(F32) * (sa * (1.0 - sa))).astype(BF16)
        dzg_ref[1] = (dm * pb_ref[...].astype(F32) * (sb * (1.0 - sb))).astype(BF16)
        dpa = (dm * sa).astype(BF16)
        dpb = (dm * sb).astype(BF16)
        dpa_ref[...] = dpa
        dpb_ref[...] = dpb
        dya_ref[...] = _dot_nt(dpa, wa_ref[...]).astype(BF16)
        dyb_ref[...] = _dot_nt(dpb, wb_ref[...]).astype(BF16)

    tile = pl.BlockSpec((tm, D_MODEL), lambda i: (i, 0))
    full = _resident((D_MODEL, D_MODEL))
    return _host_call(
        body, name="up_bwd_merge_bwd", grid=(s // tm,),
        in_specs=[pl.BlockSpec((2, tm, D_FF), lambda i: (0, i, 0)), _resident(w_up_g.shape),
                  tile, tile, pl.BlockSpec((tm, D_MODEL), lambda i: (i, 5)),
                  pl.BlockSpec((tm, D_MODEL), lambda i: (i, 6)), tile, tile, full, full, full,
                  pl.BlockSpec((1, D_MODEL), lambda i: (0, 0))],
        out_specs=[tile, pl.BlockSpec((2, tm, D_MODEL), lambda i: (0, i, 0)), tile, tile, tile, tile,
                   pl.BlockSpec((8, D_MODEL), lambda i: (0, 0))],
        out_shape=[_sds((s, D_MODEL), F32), _sds((2, s, D_MODEL), BF16)] + [_sds((s, D_MODEL), BF16)] * 4
        + [_sds((8, D_MODEL), F32)],
        semantics=("arbitrary",), args=(du0, w_up_g, x2, dx3, z, z, pa, pb, w_out, w_pa, w_pb, g_ffn),
        comm=comm)


def _mixers_bwd(z, h, dya, dyb, w4, b4, w_a, b_a, w_x, b_x, lam, w3, ch, comm):
    s = z.shape[0]
    cw = RNN_BW
    ch1, ch2, ch3 = ch

    def body(xr_ref, gr_ref, cb_ref, cc_ref, cx_ref, h_ref, dya_ref, dyb_ref, w4_ref, b4_ref, wa_ref, ba_ref, wx_ref,
             bx_ref, lam_ref, w3_ref,
             dz_ref, dw4_ref, db4_ref, dwa_ref, dba_ref, dwx_ref, dbx_ref, dlam_ref, dw3_ref,
             a_scr, g_scr, q_scr, xc_scr, ra_scr, ia_scr, dh_scr):
        sp = _softplus(-lam_ref[...])
        wa = wa_ref[0].astype(BF16)
        wx = wx_ref[0].astype(BF16)
        for ref in (dw4_ref, db4_ref, dwa_ref, dba_ref, dwx_ref, dbx_ref, dlam_ref, dw3_ref):
            ref[...] = jnp.zeros_like(ref)

        def first(c, _, ch=ch1):
            rows = pl.ds(pl.multiple_of(c * ch, HALO), ch)
            xc, ra, ia, log_a = _gate_chunk(_ext_before(xr_ref, c, ch), w4_ref, b4_ref, wa, wx, ba_ref, bx_ref, sp, ch)
            xc_scr[rows, :] = xc
            ra_scr[rows, :] = ra
            ia_scr[rows, :] = ia
            a_scr[rows, :] = jnp.exp(log_a)
            gel, dgel = _gelu_and_grad(gr_ref[rows, :].astype(F32))
            dya_c = dya_ref[rows, :].astype(F32)
            g_scr[rows, :] = dya_c * gel
            dz_ref[1, rows, :] = (dya_c * h_ref[rows, :] * dgel).astype(BF16)
            q_ext = _ext_before(cc_ref, c, ch) * _ext_before(cx_ref, c, ch)
            dyb_c = dyb_ref[rows, :].astype(F32)
            dz_ref[2, rows, :] = (dyb_c * _conv3(q_ext, w3_ref, ch)).astype(BF16)
            dyq = dyb_c * cb_ref[rows, :].astype(F32)
            q_scr[rows, :] = dyq
            for k in range(3):
                dw3_ref[k:k + 1, :] += jnp.sum(dyq * _shift_dn(q_ext, 2 - k, ch), axis=0, keepdims=True)
            return 0

        lax.fori_loop(0, s // ch1, first, 0)
        _scan_rev(a_scr, g_scr, dh_scr, s, cw)

        def second(c, _, ch=ch2):
            rows = pl.ds(pl.multiple_of(c * ch, HALO), ch)
            xc, ra, ia, a = xc_scr[rows, :], ra_scr[rows, :], ia_scr[rows, :], a_scr[rows, :]
            m2 = _one_minus_sq((-LRU_C) * ra * sp, a)
            inv_mult = lax.rsqrt(m2)
            mult = m2 * inv_mult
            dh = dh_scr[rows, :]
            h_prev = _shift_dn(_ext_before(h_ref, c, ch), 1, ch)
            dlog_a = dh * h_prev * a - (dh * ia * xc) * (a * a) * inv_mult
            dlam_ref[...] += jnp.sum(dlog_a * ra, axis=0, keepdims=True)
            dpre_a = (dlog_a * ((-LRU_C) * sp)) * (ra * (1.0 - ra))
            dpre_x = (dh * mult * xc) * (ia * (1.0 - ia))
            dba_ref[...] += jnp.sum(dpre_a, axis=0, keepdims=True)
            dbx_ref[...] += jnp.sum(dpre_x, axis=0, keepdims=True)
            xcb = xc.astype(BF16)
            dpa_b = dpre_a.astype(BF16)
            dpx_b = dpre_x.astype(BF16)
            dwa_ref[0] += _dot_tn(xcb, dpa_b)
            dwx_ref[0] += _dot_tn(xcb, dpx_b)
            a_scr[rows, :] = dh * mult * ia + _dot_nt(dpa_b, wa) + _dot_nt(dpx_b, wx)
            return 0

        lax.fori_loop(0, s // ch2, second, 0)
        dlam_ref[...] = dlam_ref[...] * (LRU_C * _sigmoid(-lam_ref[...]))

        def third(c, _, ch=ch3, n_chunks=s // ch3):
            rows = pl.ds(pl.multiple_of(c * ch, HALO), ch)
            dxc_ext = _ext_after(a_scr, c, ch, n_chunks)
            dxc = _shift_up(dxc_ext, 0, ch)
            xr_ext = _ext_before(xr_ref, c, ch)
            db4_ref[...] += jnp.sum(dxc, axis=0, keepdims=True)
            dxr = w4_ref[3:4, :] * dxc
            dw4_ref[3:4, :] += jnp.sum(dxc * _shift_dn(xr_ext, 0, ch), axis=0, keepdims=True)
            for k in range(3):
                dxr = dxr + w4_ref[k:k + 1, :] * _shift_up(dxc_ext, 3 - k, ch)
                dw4_ref[k:k + 1, :] += jnp.sum(dxc * _shift_dn(xr_ext, 3 - k, ch), axis=0, keepdims=True)
            dz_ref[0, rows, :] = dxr.astype(BF16)
            dyq_ext = _ext_after(q_scr, c, ch, n_chunks)
            dq = w3_ref[2:3, :] * _shift_up(dyq_ext, 0, ch)
            for k in range(2):
                dq = dq + w3_ref[k:k + 1, :] * _shift_up(dyq_ext, 2 - k, ch)
            dz_ref[3, rows, :] = (dq * cx_ref[rows, :].astype(F32)).astype(BF16)
            dz_ref[4, rows, :] = (dq * cc_ref[rows, :].astype(F32)).astype(BF16)
            return 0

        lax.fori_loop(0, s // ch3, third, 0)

    col = lambda j: (0, j)
    vec = pl.BlockSpec((1, cw), col)
    sq = pl.BlockSpec((1, cw, cw), lambda j: (j, 0, 0))
    act = pl.BlockSpec((s, cw), col)
    w4s, w3s = pl.BlockSpec((4, cw), col), pl.BlockSpec((3, cw), col)
    vec_shape = _sds((1, D_MODEL), F32)
    sq_shape = _sds((D_MODEL // cw, cw, cw), F32)
    return _host_call(
        body, name="mixers_bwd", grid=(D_MODEL // cw,),
        in_specs=_z_block_specs(s, cw, range(5)) + [act, act, act, w4s, vec, sq, vec, sq, vec, vec, w3s],
        out_specs=[pl.BlockSpec((5, s, cw), lambda j: (0, 0, j)), w4s, vec, sq, vec, sq, vec, vec, w3s],
        out_shape=[_sds((5, s, D_MODEL), BF16), _sds((4, D_MODEL), F32), vec_shape, sq_shape, vec_shape, sq_shape,
                   vec_shape, vec_shape, _sds((3, D_MODEL), F32)],
        scratch_shapes=[pltpu.VMEM((s, cw), F32)] * 7,
        semantics=("parallel",), args=(z, z, z, z, z, h, dya, dyb, w4, b4, w_a, b_a, w_x, b_x, lam, w3), comm=comm)


def _in_proj_bwd(dz5, dzg, w_in, x, dx2, g_mix, tm, comm):
    s = x.shape[0]

    def body(d5_ref, dg_ref, w_ref, x_ref, dx2_ref, g_ref, dx_ref, sums_ref):
        @pl.when(pl.program_id(0) == 0)
        def _():
            sums_ref[...] = jnp.zeros_like(sums_ref)

        dh1 = None
        for k in range(N_SEG):
            d = d5_ref[k] if k < 5 else dg_ref[k - 5]
            term = _dot_nt(d, w_ref[:, k * D_MODEL:(k + 1) * D_MODEL])
            dh1 = term if dh1 is None else dh1 + term
        xn, r1 = _rms(x_ref[...])
        sums_ref[0:1, :] += jnp.sum(dh1 * xn, axis=0, keepdims=True)
        dx_ref[...] = dx2_ref[...] + _rms_bwd(dh1 * g_ref[...], xn, r1)

    tile = pl.BlockSpec((tm, D_MODEL), lambda i: (i, 0))
    return _host_call(
        body, name="in_proj_bwd", grid=(s // tm,),
        in_specs=[pl.BlockSpec((5, tm, D_MODEL), lambda i: (0, i, 0)), pl.BlockSpec((2, tm, D_MODEL), lambda i: (0, i, 0)),
                  _resident(w_in.shape), tile, tile, pl.BlockSpec((1, D_MODEL), lambda i: (0, 0))],
        out_specs=[tile, pl.BlockSpec((8, D_MODEL), lambda i: (0, 0))],
        out_shape=[_sds((s, D_MODEL), F32), _sds((8, D_MODEL), F32)],
        semantics=("arbitrary",), args=(dz5, dzg, w_in, x, dx2, g_mix), comm=comm)


def _weight_grad(a, b, b_spec, out_spec, out_shape, *, n_blocks, chunks, width, tm, tk, name):
    s, m = a.shape
    nk = s // tk
    b_chunked = len([d for d in b_spec.block_shape if d is not None]) == 3
    blocks_out = len([d for d in out_spec.block_shape if d is not None]) == 3

    def body(a_ref, b_ref, o_ref, acc):
        k = pl.program_id(2)

        @pl.when(k == 0)
        def _():
            acc[...] = jnp.zeros_like(acc)

        at = a_ref[...].astype(BF16).T
        for j in range(chunks):
            cols = slice(j * width, (j + 1) * width)
            acc[:, cols] += _dot(at, (b_ref[j] if b_chunked else b_ref[:, cols]).astype(BF16))

        @pl.when(k == nk - 1)
        def _():
            if blocks_out:
                for j in range(chunks):
                    o_ref[j] = acc[:, j * width:(j + 1) * width].astype(o_ref.dtype)
            else:
                o_ref[...] = acc[...].astype(o_ref.dtype)

    return pl.pallas_call(
        body, name=name, grid=(m // tm, n_blocks, nk),
        in_specs=[pl.BlockSpec((tk, tm), lambda i, j, k: (k, i)), b_spec], out_specs=out_spec, out_shape=out_shape,
        scratch_shapes=[pltpu.VMEM((tm, chunks * width), F32)],
        compiler_params=_params("parallel", "parallel", "arbitrary"),
    )(a, b)


def _wgrad_2d(a, b, name, tk):
    m, n = a.shape[1], b.shape[1]
    tm = m
    return _weight_grad(a, b, pl.BlockSpec((tk, n), lambda i, j, k: (k, 0)), pl.BlockSpec((tm, n), lambda i, j, k: (i, 0)),
                        _sds((m, n), BF16), n_blocks=1, chunks=1, width=n, tm=tm, tk=tk, name=name)


def _wgrad_in(h1, dz5, dzg, tm, tk):
    s, m = h1.shape
    nk = s // tk
    g5, gg = dz5.shape[0], dzg.shape[0]

    def body(a_ref, b5_ref, bg_ref, o_ref, acc):
        k = pl.program_id(1)

        @pl.when(k == 0)
        def _():
            acc[...] = jnp.zeros_like(acc)

        at = a_ref[...].T
        for j in range(g5 + gg):
            b = b5_ref[j] if j < g5 else bg_ref[j - g5]
            acc[:, j * D_MODEL:(j + 1) * D_MODEL] += _dot(at, b)

        @pl.when(k == nk - 1)
        def _():
            o_ref[...] = acc[...].astype(BF16)

    return pl.pallas_call(
        body, name="wgrad_in", grid=(m // tm, nk),
        in_specs=[pl.BlockSpec((tk, tm), lambda i, k: (k, i)), pl.BlockSpec((g5, tk, D_MODEL), lambda i, k: (0, k, 0)),
                  pl.BlockSpec((gg, tk, D_MODEL), lambda i, k: (0, k, 0))],
        out_specs=pl.BlockSpec((tm, (g5 + gg) * D_MODEL), lambda i, k: (i, 0)),
        out_shape=_sds((m, (g5 + gg) * D_MODEL), BF16),
        scratch_shapes=[pltpu.VMEM((tm, (g5 + gg) * D_MODEL), F32)],
        compiler_params=_params("parallel", "arbitrary"),
    )(h1, dz5, dzg)


def _wgrad_up(h2, du0, bw, tk):
    per_half = N_DEV // 2
    return _weight_grad(h2, du0, pl.BlockSpec((None, tk, D_FF), lambda i, j, k: (j, k, 0)),
                        pl.BlockSpec((per_half, D_MODEL, bw), lambda i, j, k: (j, 0, 0)), _sds((N_DEV, D_MODEL, bw), BF16),
                        n_blocks=2, chunks=per_half, width=bw, tm=D_MODEL, tk=tk, name="wgrad_up")


def _adam_math(w, g, m, v):
    m = ADAM_B1 * m + (1.0 - ADAM_B1) * g
    v = ADAM_B2 * v + (1.0 - ADAM_B2) * jnp.square(g)
    m_hat = m / (1.0 - ADAM_B1 ** ADAM_STEP)
    v_hat = v / (1.0 - ADAM_B2 ** ADAM_STEP)
    delta = -ADAM_LR * (m_hat / (jnp.sqrt(v_hat) + ADAM_EPS) + ADAM_WD * w)
    return delta, m, v


def _adam_shard(parts, w, m, v, name):
    r, c = w.shape
    tr = min(r, 128)
    n_arrays = len(parts)

    def body(*refs):
        p_refs = refs[:n_arrays]
        w_ref, m_ref, v_ref, g_ref, d_ref, nm_ref, nv_ref = refs[n_arrays:]
        g = None
        for p_ref, (arr, first) in zip(p_refs, parts):
            for k in range(first, arr.shape[0]):
                term = p_ref[k].astype(F32)
                g = term if g is None else g + term
        g_ref[...] = g
        d_ref[...], nm_ref[...], nv_ref[...] = _adam_math(w_ref[...], g, m_ref[...], v_ref[...])

    tile = pl.BlockSpec((tr, c), lambda i: (i, 0))
    return pl.pallas_call(
        body, name=name, grid=(r // tr,),
        in_specs=[pl.BlockSpec((arr.shape[0], tr, c), lambda i: (0, i, 0)) for arr, _ in parts] + [tile, tile, tile],
        out_specs=[tile] * 4, out_shape=[_sds((r, c), F32)] * 4,
        compiler_params=_params("parallel"),
    )(*[arr for arr, _ in parts], w, m, v)


def _adam_small(grads, ws, ms, vs, name):
    n = len(ws)
    g_arrays, g_slot = [], []
    for g in grads:
        arr = g[0] if isinstance(g, tuple) else g
        if not any(arr is a for a in g_arrays):
            g_arrays.append(arr)
        g_slot.append([arr is a for a in g_arrays].index(True))
    n_g = len(g_arrays)

    def body(*refs):
        g_refs, refs = refs[:n_g], refs[n_g:]
        w_refs, m_refs, v_refs, outs = refs[:n], refs[n:2 * n], refs[2 * n:3 * n], refs[3 * n:]
        for i in range(n):
            og, od, om, ov = outs[4 * i:4 * i + 4]
            g_ref = g_refs[g_slot[i]]
            if isinstance(grads[i], tuple):
                row = grads[i][1]
                for j in range(ws[i].shape[1] // D_MODEL):
                    cols = slice(j * D_MODEL, (j + 1) * D_MODEL)
                    g = g_ref[row + j:row + j + 1, :]
                    og[:, cols] = g
                    od[:, cols], om[:, cols], ov[:, cols] = _adam_math(w_refs[i][:, cols], g, m_refs[i][:, cols],
                                                                       v_refs[i][:, cols])
            else:
                g = g_ref[...]
                og[...] = g
                od[...], om[...], ov[...] = _adam_math(w_refs[i][...], g, m_refs[i][...], v_refs[i][...])

    vmem = pl.BlockSpec(memory_space=pltpu.VMEM)
    res = pl.pallas_call(
        body, name=name, in_specs=[vmem] * (n_g + 3 * n), out_specs=[vmem] * (4 * n),
        out_shape=[_sds(a.shape, F32) for a in ws for _ in range(4)],
    )(*g_arrays, *ws, *ms, *vs)
    return [res[4 * i:4 * i + 4] for i in range(n)]


def _pack_conv_shard(rnn, sc, ffn):
    top = jnp.concatenate([rnn[:3], sc, ffn], axis=1)
    row3 = jnp.concatenate([rnn[3:4], jnp.zeros((1, D_MODEL - RNN_BW), F32)], axis=1)
    return jnp.concatenate([top, row3, jnp.zeros((4, D_MODEL), F32)], axis=0)


_VECTORS = ("g_mix", "rnn_conv_b", "b_rg_a", "b_rg_x", "lru_lambda", "g_ffn", "g_ple", "g_final", "ffn_conv_b")
VEC_ROWS = 64
ROW_LOSS, ROW_RNN_CONV, ROW_SC_CONV, ROW_FFN_CONV = 14, 16, 20, 24


_SHARDED_BIG = ("w_in", "w_proj_a", "w_proj_b", "w_out", "w_up", "w_down", "w_ple_gate", "w_ple_proj")
_NAMES = ("g_mix", "w_in", "rnn_conv_w", "rnn_conv_b", "w_rg_a", "b_rg_a", "w_rg_x", "b_rg_x", "lru_lambda", "sc_conv_w",
          "w_proj_a", "w_proj_b", "w_out", "g_ffn", "w_up", "ffn_conv_w", "ffn_conv_b", "w_down", "w_ple_gate",
          "w_ple_proj", "g_ple", "g_final")


def _step(x, p, target, w, m, v):
    s = x.shape[0]
    tm = min(s, 256)
    tm_wide = min(s, 512)
    ch = min(s, 256)
    ch_mix = min(s, 512)
    ch_bwd = (min(s, 512), min(s, 1024), min(s, 256))
    my_index = 4 * lax.axis_index("x") + 2 * lax.axis_index("y") + lax.axis_index("c")

    big = {n: w[n][0] for n in _SHARDED_BIG}
    conv_shard = _pack_conv_shard(w["rnn_conv_w"][0], w["sc_conv_w"][0], w["ffn_conv_w"][0])
    shards = {n: big[n].astype(BF16) for n in _SHARDED_BIG}
    w_in, conv_all = _comm_only(_gather_comm([shards["w_in"], conv_shard], by_columns=(0,)), "in_proj_weight_gather")
    w4 =jnp.transpose(jnp.concatenate([conv_all[:, :3, :RNN_BW], conv_all[:, 3:4, :RNN_BW]], axis=1),
                       (1, 0, 2)).reshape(4, D_MODEL)
    w3 = jnp.transpose(conv_all[:, :3, RNN_BW:2 * RNN_BW], (1, 0, 2)).reshape(3, D_MODEL)
    wc = jnp.transpose(conv_all[:, :3, 2 * RNN_BW:], (1, 0, 2)).reshape(3, 2, D_FF).transpose(1, 0, 2)
    bc = w["ffn_conv_b"].reshape(2, 1, D_FF)
    b4, b_a, b_x, lam = w["rnn_conv_b"], w["b_rg_a"], w["b_rg_x"], w["lru_lambda"]
    w_a, w_x = w["w_rg_a"][0], w["w_rg_x"][0]
    g_final = w["g_final"].reshape(1, D_MODEL)

    tk = min(s, 512)
    received = {}

    comm = _gather_comm([shards[n] for n in ("w_proj_a", "w_proj_b", "w_out")])
    (z, h1), (w_pa, w_pb, w_out) = _norm_in_proj(x, w["g_mix"], w_in, tm_wide, comm)
    w_pa, w_pb, w_out = (a.reshape(D_MODEL, D_MODEL) for a in (w_pa, w_pb, w_out))
    (ya, yb, h), (w_up_g,) = _mixers_fwd(z, w4, b4, w_a, b_a, w_x, b_x, lam, w3, ch_mix, _gather_comm([shards["w_up"]]))
    pa, pb, mm, x2, h2 = _merge_out(ya, yb, z, x, w_pa, w_pb, w_out, w["g_ffn"], tm)
    comm = _gather_comm([shards[n] for n in ("w_down", "w_ple_gate", "w_ple_proj")], by_columns=(2,))
    (u0, f), (w_down, w_gate, w_pp) = _up_proj_act(h2, w_up_g, wc, bc, tm, comm)
    w_down = w_down.reshape(D_FF, D_MODEL)
    w_gate = w_gate.reshape(D_MODEL, D_MODEL)
    x3, dgt, de0, dx3, df, head_sums = _head(x2, f, p, target, w_down, w_gate, w_pp, w["g_ple"], g_final, tm)
    parts = [_wgrad_2d(x3, dgt, "wgrad_ple_gate", tk).reshape(N_DEV, RNN_BW, D_MODEL),
             jnp.transpose(_wgrad_2d(p, de0, "wgrad_ple_proj", tk).reshape(D_PLE, N_DEV, RNN_BW), (1, 0, 2)),
             _wgrad_2d(f, dx3, "wgrad_down", tk).reshape(N_DEV, D_FF // N_DEV, D_MODEL)]
    (du0, dwc, dbc), got = _ffn_act_bwd(u0, df, wc, bc, 256, ch, _exchange_comm(parts))
    received.update({n: [(g, 0)] for n, g in zip(("w_ple_gate", "w_ple_proj", "w_down"), got)})
    dw_up = _wgrad_up(h2, du0, w_up_g.shape[2], tk)
    (dx2, dzg, dpa, dpb, dya, dyb, ffn_sums), got_up = _up_bwd_merge_bwd(
        du0, w_up_g, x2, dx3, z, pa, pb, w_out, w_pa, w_pb, w["g_ffn"], tm, _exchange_comm([dw_up], (4, 6, 2)))
    parts = [_wgrad_2d(mm, dx2, "wgrad_out", tk).reshape(N_DEV, RNN_BW, D_MODEL),
             _wgrad_2d(ya, dpa, "wgrad_proj_a", tk).reshape(N_DEV, RNN_BW, D_MODEL),
             _wgrad_2d(yb, dpb, "wgrad_proj_b", tk).reshape(N_DEV, RNN_BW, D_MODEL)]
    comm = _exchange_comm(parts)
    rest = _exchange_comm([dw_up], (1, 3, 5, 7))
    (dz5, dw4, db4, dwa, dba, dwx, dbx, dlam, dw3), got = _mixers_bwd(z, h, dya, dyb, w4, b4, w_a, b_a, w_x, b_x, lam, w3,
                                                                       ch_bwd, _join_comms(comm, rest))
    received.update({n: [(g, 0)] for n, g in zip(("w_out", "w_proj_a", "w_proj_b"), got[:3])})
    received["w_up"] = [(got_up[0], 0), (got[3], 1)]
    chip_sum = _pair_reduce(_wgrad_in(h1, dz5, dzg, 512, tk), "in_proj_grad_pair_reduce")
    (grad_x, mix_sums), got = _in_proj_bwd(dz5, dzg, w_in, x, dx2, w["g_mix"], tm, _chip_exchange_comm([chip_sum]))
    received["w_in"] = [(got[0], 0)]

    zero_row = jnp.zeros((1, D_MODEL), F32)
    vec_pack = jnp.concatenate(
        [mix_sums[0:1], db4, dba, dbx, dlam, ffn_sums[0:1], head_sums[2:3], head_sums[1:2], dbc.reshape(6, D_MODEL),
         head_sums[0:1], zero_row, dw4, dw3, zero_row, jnp.transpose(dwc, (1, 0, 2)).reshape(18, D_MODEL),
         jnp.zeros((VEC_ROWS - ROW_FFN_CONV - 18, D_MODEL), F32)], axis=0)
    gate_pack = jnp.concatenate([dwa.reshape(D_MODEL, RNN_BW), dwx.reshape(D_MODEL, RNN_BW)], axis=0)
    vec_total, gate_total = _small_all_reduce([vec_pack, gate_pack])
    loss = jnp.sum(vec_total[ROW_LOSS]) * (0.5 / D_MODEL)

    out = {}
    for n in _SHARDED_BIG:
        res = _adam_shard(received[n], big[n], m[n][0], v[n][0], "adam_" + n)
        out[n] = [r[None] for r in res]

    def flat(t, n):
        return t[n].reshape(1, -1)

    res = _adam_small([(vec_total, i) for i in range(len(_VECTORS))],
                      [flat(w, n) for n in _VECTORS], [flat(m, n) for n in _VECTORS], [flat(v, n) for n in _VECTORS],
                      "adam_vectors")
    for n, r in zip(_VECTORS, res):
        out[n] = [a.reshape(w[n].shape) for a in r]
    gates = ("w_rg_a", "w_rg_x")
    res = _adam_small([gate_total[:D_MODEL], gate_total[D_MODEL:]], *[[t[n].reshape(D_MODEL, RNN_BW) for n in gates]
                                                                      for t in (w, m, v)], "adam_gate_maps")
    for n, r in zip(gates, res):
        out[n] = [a.reshape(w[n].shape) for a in r]
    convs = ("rnn_conv_w", "sc_conv_w", "ffn_conv_w")
    bw_ffn = 2 * D_FF // N_DEV
    g_conv = [lax.dynamic_slice(vec_total[ROW_RNN_CONV:ROW_RNN_CONV + 4], (0, my_index * RNN_BW), (4, RNN_BW)),
              lax.dynamic_slice(vec_total[ROW_SC_CONV:ROW_SC_CONV + 3], (0, my_index * RNN_BW), (3, RNN_BW)),
              lax.dynamic_slice(vec_total[ROW_FFN_CONV:ROW_FFN_CONV + 18].reshape(3, 2 * D_FF), (0, my_index * bw_ffn),
                                (3, bw_ffn))]
    res = _adam_small(g_conv, *[[t[n][0] for n in convs] for t in (w, m, v)], "adam_conv")
    for n, r in zip(convs, res):
        out[n] = [a[None] for a in r]

    return (loss, grad_x[None]) + tuple(out[n][k] for k in range(4) for n in _NAMES)


def kernel(x, p, g_mix, w_in, rnn_conv_w, rnn_conv_b, w_rg_a, b_rg_a, w_rg_x, b_rg_x, lru_lambda, sc_conv_w, w_proj_a, w_proj_b, w_out, g_ffn, w_up, ffn_conv_w, ffn_conv_b, w_down, w_ple_gate, w_ple_proj, g_ple, g_final, loss_target, m_g_mix, m_w_in, m_rnn_conv_w, m_rnn_conv_b, m_w_rg_a, m_b_rg_a, m_w_rg_x, m_b_rg_x, m_lru_lambda, m_sc_conv_w, m_w_proj_a, m_w_proj_b, m_w_out, m_g_ffn, m_w_up, m_ffn_conv_w, m_ffn_conv_b, m_w_down, m_w_ple_gate, m_w_ple_proj, m_g_ple, m_g_final, v_g_mix, v_w_in, v_rnn_conv_w, v_rnn_conv_b, v_w_rg_a, v_b_rg_a, v_w_rg_x, v_b_rg_x, v_lru_lambda, v_sc_conv_w, v_w_proj_a, v_w_proj_b, v_w_out, v_g_ffn, v_w_up, v_ffn_conv_w, v_ffn_conv_b, v_w_down, v_w_ple_gate, v_w_ple_proj, v_g_ple, v_g_final):
    given = dict(locals())
    w = {n: given[n] for n in _NAMES}
    m = {n: given["m_" + n] for n in _NAMES}
    v = {n: given["v_" + n] for n in _NAMES}
    return _step(x[0], p[0, 0], loss_target[0], w, m, v)
```

```python
import functools

import jax
import jax.numpy as jnp
from jax import lax
from jax.experimental import pallas as pl
from jax.experimental.pallas import tpu as pltpu

F32 = jnp.float32
BF16 = jnp.bfloat16
MESH_ID = pl.DeviceIdType.MESH

N_DEV = 8
D_MODEL = 1024
D_PLE = 256
RNN_BW = 128
N_SEG = 7
D_FF = 3072
LRU_C = 8.0
EPS = 1e-6
ADAM_LR = 0.001
ADAM_B1 = 0.9
ADAM_B2 = 0.999
ADAM_EPS = 1e-08
ADAM_WD = 0.01
ADAM_STEP = 10

HALO = 16
SCAN_UNROLL = 4
VMEM_LIMIT = 56 * 1024 * 1024


def _dot(a, b):
    return jnp.dot(a, b, preferred_element_type=F32)


def _dot_nt(a, b):
    return lax.dot_general(a, b, (((1,), (1,)), ((), ())), preferred_element_type=F32)


def _dot_tn(a, b):
    return lax.dot_general(a, b, (((0,), (0,)), ((), ())), preferred_element_type=F32)


def _sigmoid(x):
    return 0.5 * jnp.tanh(0.5 * x) + 0.5


_GELU_C = 0.7978845608028654
_GELU_K = 0.044715


def _gelu(x):
    return 0.5 * x * (1.0 + jnp.tanh(_GELU_C * (x + _GELU_K * (x * x * x))))


def _gelu_and_grad(x):
    x2 = x * x
    t = jnp.tanh(_GELU_C * (x + _GELU_K * (x * x2)))
    g = 0.5 * x * (1.0 + t)
    dg = 0.5 * (1.0 + t) + 0.5 * x * (1.0 - t * t) * (_GELU_C * (1.0 + 3.0 * _GELU_K * x2))
    return g, dg


def _one_minus_sq(log_a, a):
    series = (-2.0 * log_a) * (1.0 + log_a * (1.0 + log_a * (2.0 / 3.0)))
    return jnp.where(log_a > -0.005, series, 1.0 - a * a)


def _softplus(x):
    return jnp.maximum(x, 0.0) + jnp.log1p(jnp.exp(-jnp.abs(x)))


def _rms(u):
    r = lax.rsqrt(jnp.mean(u * u, axis=-1, keepdims=True) + EPS)
    return u * r, r


def _rms_bwd(dn, un, r):
    return r * (dn - un * jnp.mean(dn * un, axis=-1, keepdims=True))


def _shift_dn(ext, s, n):
    if s == 0:
        return ext[HALO:HALO + n]
    return pltpu.roll(ext, s, 0)[HALO:HALO + n]


def _shift_up(ext, s, n):
    if s == 0:
        return ext[0:n]
    return pltpu.roll(ext, n + HALO - s, 0)[0:n]


def _ext_before(ref, c, ch):
    t0 = c * ch
    prev = pl.multiple_of(jnp.maximum(t0 - HALO, 0), HALO)
    halo = jnp.where(c > 0, ref[pl.ds(prev, HALO), :].astype(F32), 0.0)
    cur = ref[pl.ds(pl.multiple_of(t0, HALO), ch), :].astype(F32)
    return jnp.concatenate([halo, cur], axis=0)


def _ext_after(ref, c, ch, n_chunks):
    t0 = c * ch
    nxt = pl.multiple_of(jnp.minimum(t0 + ch, (n_chunks - 1) * ch + ch - HALO), HALO)
    halo = jnp.where(c < n_chunks - 1, ref[pl.ds(nxt, HALO), :].astype(F32), 0.0)
    cur = ref[pl.ds(pl.multiple_of(t0, HALO), ch), :].astype(F32)
    return jnp.concatenate([cur, halo], axis=0)


def _rows(ref, c, ch):
    return ref[pl.ds(pl.multiple_of(c * ch, HALO), ch), :]


def _scan_fwd(a_ref, b_ref, h_ref, n_rows, cw):
    rows = lax.broadcasted_iota(jnp.int32, (8, cw), 0)

    def body(i, carry):
        for u in range(SCAN_UNROLL):
            t = pl.multiple_of((i * SCAN_UNROLL + u) * 8, 8)
            a = a_ref[pl.ds(t, 8), :]
            b = b_ref[pl.ds(t, 8), :]
            for d in (1, 2, 4):
                m = rows >= d
                b = jnp.where(m, a * pltpu.roll(b, d, 0) + b, b)
                a = jnp.where(m, a * pltpu.roll(a, d, 0), a)
            h_ref[pl.ds(t, 8), :] = a * carry + b
            carry = jnp.broadcast_to(a[7:8, :], (8, cw)) * carry + jnp.broadcast_to(b[7:8, :], (8, cw))
        return carry

    lax.fori_loop(0, n_rows // (8 * SCAN_UNROLL), body, jnp.zeros((8, cw), F32))


def _scan_rev(a_ref, g_ref, o_ref, n_rows, cw):
    rows = lax.broadcasted_iota(jnp.int32, (8, cw), 0)
    n_groups = n_rows // 8

    def body(i, carry):
        dh_next, a_next = carry
        for u in range(SCAN_UNROLL):
            t = pl.multiple_of((n_groups - 1 - (i * SCAN_UNROLL + u)) * 8, 8)
            a = a_ref[pl.ds(t, 8), :]
            g = g_ref[pl.ds(t, 8), :]
            an = jnp.where(rows < 7, pltpu.roll(a, 7, 0), a_next)
            a_next = jnp.broadcast_to(a[0:1, :], (8, cw))
            for d in (1, 2, 4):
                m = rows < 8 - d
                g = jnp.where(m, an * pltpu.roll(g, 8 - d, 0) + g, g)
                an = jnp.where(m, an * pltpu.roll(an, 8 - d, 0), an)
            o_ref[pl.ds(t, 8), :] = an * dh_next + g
            dh_next = jnp.broadcast_to(an[0:1, :], (8, cw)) * dh_next + jnp.broadcast_to(g[0:1, :], (8, cw))
        return dh_next, a_next

    zero = jnp.zeros((8, cw), F32)
    lax.fori_loop(0, n_groups // SCAN_UNROLL, body, (zero, zero))


def _params(*sem):
    return pltpu.CompilerParams(dimension_semantics=sem, vmem_limit_bytes=VMEM_LIMIT)


def _sds(shape, dtype):
    return jax.ShapeDtypeStruct(shape, dtype)


def _resident(shape):
    zeros = (0,) * len(shape)
    return pl.BlockSpec(shape, lambda *_: zeros, pipeline_mode=pl.Buffered(1))


class _Comm:
    def __init__(self, inputs, out_shape, scratch, start, finish):
        self.inputs, self.out_shape, self.scratch, self.start, self.finish = inputs, out_shape, scratch, start, finish


def _join_comms(a, b):
    n, k = len(a.inputs), len(a.scratch)

    def start(ins, outs, sems):
        a.start(ins[:n], outs[:n], sems[:k])
        b.start(ins[n:], outs[n:], sems[k:])

    def finish(ins, outs, sems):
        a.finish(ins[:n], outs[:n], sems[:k])
        b.finish(ins[n:], outs[n:], sems[k:])

    return _Comm(a.inputs + b.inputs, a.out_shape + b.out_shape, a.scratch + b.scratch, start, finish)


def _sem_scratch(n, copies=7):
    return [pltpu.SemaphoreType.DMA((n, copies)), pltpu.SemaphoreType.DMA((n, copies)), pltpu.SemaphoreType.DMA((n,))]


def _gather_comm(shards, by_columns=()):
    n = len(shards)

    def plan(ins, outs, sems):
        send_sems, recv_sems, local_sems = sems
        x, y, c = lax.axis_index("x"), lax.axis_index("y"), lax.axis_index("c")
        me, sibling = (x, y, c), (x, y, 1 - c)
        chips = [(1 - x, y), (x, 1 - y), (1 - x, 1 - y)]

        def slot(t, dev):
            idx = 4 * dev[0] + 2 * dev[1] + dev[2]
            if t in by_columns:
                width = shards[t].shape[1]
                return outs[t].at[:, pl.ds(pl.multiple_of(idx * width, 128), width)]
            return outs[t].at[idx]

        def copy(t, k, block, to, src=None):
            return pltpu.make_async_remote_copy(
                src_ref=slot(t, block) if src is None else src, dst_ref=slot(t, block),
                send_sem=send_sems.at[t, k], recv_sem=recv_sems.at[t, k],
                device_id=to, device_id_type=MESH_ID)

        mine = [pltpu.make_async_copy(ins[t], slot(t, me), local_sems.at[t]) for t in range(n)]
        first = []
        for t in range(n):
            first.append(copy(t, 0, me, sibling, src=ins[t]))
            first += [copy(t, 1 + j, me, (*chip, c), src=ins[t]) for j, chip in enumerate(chips)]
        return me, sibling, chips, c, copy, mine, first

    def start(ins, outs, sems):
        *_, mine, first = plan(ins, outs, sems)
        for cp in mine + first:
            cp.start()

    def finish(ins, outs, sems):
        me, sibling, chips, c, copy, mine, first = plan(ins, outs, sems)
        passed = []
        for t in range(n):
            for j, chip in enumerate(chips):
                copy(t, 1 + j, (*chip, c), me).wait_recv()
                cp = copy(t, 4 + j, (*chip, c), sibling)
                cp.start()
                passed.append(cp)
        for t in range(n):
            copy(t, 0, sibling, me).wait_recv()
            for j, chip in enumerate(chips):
                copy(t, 4 + j, (*chip, 1 - c), me).wait_recv()
        for cp in first + passed:
            cp.wait_send()
        for cp in mine:
            cp.wait()

    out_shape = [_sds((s.shape[0], N_DEV * s.shape[1]) if t in by_columns else (N_DEV,) + s.shape, s.dtype)
                 for t, s in enumerate(shards)]
    return _Comm(list(shards), out_shape, _sem_scratch(n), start, finish)


def _peer(d):
    x, y, c = lax.axis_index("x"), lax.axis_index("y"), lax.axis_index("c")
    px = 1 - x if d & 4 else x
    py = 1 - y if d & 2 else y
    pc = 1 - c if d & 1 else c
    return (px, py, pc), 4 * px + 2 * py + pc


def _exchange_comm(parts, deltas=tuple(range(1, N_DEV))):
    n = len(parts)

    def plan(ins, outs, sems):
        send_sems, recv_sems, local_sems = sems
        _, me = _peer(0)

        def copy(t, i):
            peer, idx = _peer(deltas[i])
            return pltpu.make_async_remote_copy(
                src_ref=ins[t].at[idx], dst_ref=outs[t].at[1 + i],
                send_sem=send_sems.at[t, i], recv_sem=recv_sems.at[t, i],
                device_id=peer, device_id_type=MESH_ID)

        mine = [pltpu.make_async_copy(ins[t].at[me], outs[t].at[0], local_sems.at[t]) for t in range(n)]
        return mine, [copy(t, i) for t in range(n) for i in range(len(deltas))]

    def start(ins, outs, sems):
        mine, copies = plan(ins, outs, sems)
        for cp in mine + copies:
            cp.start()

    def finish(ins, outs, sems):
        mine, copies = plan(ins, outs, sems)
        for cp in copies:
            cp.wait_recv()
        for cp in copies:
            cp.wait_send()
        for cp in mine:
            cp.wait()

    return _Comm(list(parts), [_sds((1 + len(deltas),) + p.shape[1:], p.dtype) for p in parts],
                 _sem_scratch(n, len(deltas)), start, finish)


def _chip_exchange_comm(parts):
    n = len(parts)

    def plan(ins, outs, sems):
        send_sems, recv_sems, local_sems = sems
        x, y, c = lax.axis_index("x"), lax.axis_index("y"), lax.axis_index("c")

        def copy(t, dq):
            px = 1 - x if dq & 2 else x
            py = 1 - y if dq & 1 else y
            return pltpu.make_async_remote_copy(
                src_ref=ins[t].at[2 * px + py], dst_ref=outs[t].at[dq],
                send_sem=send_sems.at[t, dq - 1], recv_sem=recv_sems.at[t, dq - 1],
                device_id=(px, py, c), device_id_type=MESH_ID)

        mine = [pltpu.make_async_copy(ins[t].at[2 * x + y], outs[t].at[0], local_sems.at[t]) for t in range(n)]
        return mine, [copy(t, dq) for t in range(n) for dq in range(1, 4)]

    def start(ins, outs, sems):
        mine, copies = plan(ins, outs, sems)
        for cp in mine + copies:
            cp.start()

    def finish(ins, outs, sems):
        mine, copies = plan(ins, outs, sems)
        for cp in copies:
            cp.wait_recv()
        for cp in copies:
            cp.wait_send()
        for cp in mine:
            cp.wait()

    return _Comm(list(parts), [_sds(p.shape, p.dtype) for p in parts], _sem_scratch(n, 3), start, finish)


def _pair_reduce(part, name):
    by_columns = part.ndim == 2
    r, c = (part.shape[0], part.shape[1] // N_DEV) if by_columns else part.shape[1:]
    n_chips = N_DEV // 2
    tr = 128

    def body(p_ref, o_ref, own, land, send_sems, recv_sems, local_sems):
        x, y, core = lax.axis_index("x"), lax.axis_index("y"), lax.axis_index("c")

        def block(k):
            return p_ref.at[:, pl.ds(pl.multiple_of(k * c, 128), c)] if by_columns else p_ref.at[k]

        sends = [pltpu.make_async_remote_copy(
            src_ref=block(2 * q + (1 - core)), dst_ref=land.at[q], send_sem=send_sems.at[q], recv_sem=recv_sems.at[q],
            device_id=(x, y, 1 - core), device_id_type=MESH_ID) for q in range(n_chips)]
        loads = [pltpu.make_async_copy(block(2 * q + core), own.at[q], local_sems.at[q]) for q in range(n_chips)]
        for cp in sends + loads:
            cp.start()
        for q in range(n_chips):
            loads[q].wait()
            sends[q].wait_recv()

            def add_rows(i, _):
                rows = pl.ds(pl.multiple_of(i * tr, tr), tr)
                o_ref[q, rows, :] = (own[q, rows, :].astype(F32) + land[q, rows, :].astype(F32)).astype(BF16)
                return 0

            lax.fori_loop(0, r // tr, add_rows, 0)
        for cp in sends:
            cp.wait_send()

    return pl.pallas_call(
        body, name=name, out_shape=_sds((n_chips, r, c), BF16),
        in_specs=[pl.BlockSpec(memory_space=pl.ANY)], out_specs=pl.BlockSpec(memory_space=pltpu.VMEM),
        scratch_shapes=[pltpu.VMEM((n_chips, r, c), BF16), pltpu.VMEM((n_chips, r, c), BF16)]
        + [pltpu.SemaphoreType.DMA((n_chips,))] * 3,
        compiler_params=pltpu.CompilerParams(vmem_limit_bytes=VMEM_LIMIT),
    )(part)


def _comm_only(comm, name):
    n = len(comm.inputs)

    def body(*refs):
        ins, outs, sems = refs[:n], refs[n:2 * n], refs[2 * n:]
        comm.start(ins, outs, sems)
        comm.finish(ins, outs, sems)

    any_spec = pl.BlockSpec(memory_space=pl.ANY)
    return pl.pallas_call(body, name=name, out_shape=comm.out_shape, in_specs=[any_spec] * n, out_specs=[any_spec] * n,
                          scratch_shapes=comm.scratch)(*comm.inputs)


def _host_call(body, *, name, grid, in_specs, out_specs, out_shape, scratch_shapes=(), semantics, args, comm=None):
    if comm is None:
        res = pl.pallas_call(body, name=name, grid=grid, in_specs=in_specs, out_specs=out_specs, out_shape=out_shape,
                             scratch_shapes=list(scratch_shapes), compiler_params=_params(*semantics))(*args)
        return res, []
    n_in, n_out, n_scr, n_c = len(in_specs), len(out_specs), len(scratch_shapes), len(comm.inputs)

    def with_comm(*refs):
        ins, refs = refs[:n_in], refs[n_in:]
        c_ins, refs = refs[:n_c], refs[n_c:]
        outs, refs = refs[:n_out], refs[n_out:]
        c_outs, refs = refs[:n_c], refs[n_c:]
        scr, sems = refs[:n_scr], refs[n_scr:]
        ids = [pl.program_id(a) for a in range(len(grid))]
        first = functools.reduce(jnp.logical_and, [i == 0 for i in ids])
        last = functools.reduce(jnp.logical_and, [i == g - 1 for i, g in zip(ids, grid)])

        @pl.when(first)
        def _():
            comm.start(c_ins, c_outs, sems)

        body(*ins, *outs, *scr)

        @pl.when(last)
        def _():
            comm.finish(c_ins, c_outs, sems)

    any_spec = pl.BlockSpec(memory_space=pl.ANY)
    res = pl.pallas_call(
        with_comm, name=name, grid=grid, in_specs=list(in_specs) + [any_spec] * n_c,
        out_specs=list(out_specs) + [any_spec] * n_c, out_shape=list(out_shape) + comm.out_shape,
        scratch_shapes=list(scratch_shapes) + comm.scratch,
        compiler_params=_params(*(["arbitrary"] * len(grid))))(*args, *comm.inputs)
    return res[:n_out], res[n_out:]


def _small_all_reduce(packs):
    n = len(packs)
    pieces = [p.shape[0] // N_DEV for p in packs]

    def body(*refs):
        p_refs, o_refs, lands = refs[:n], refs[n:2 * n], refs[2 * n:3 * n]
        s1, r1, s2, r2 = refs[3 * n:]
        _, me = _peer(0)

        def piece(t, ref, idx):
            return ref.at[pl.ds(pl.multiple_of(idx * pieces[t], 8), pieces[t]), :]

        def scatter(t, d):
            peer, idx = _peer(d)
            return pltpu.make_async_remote_copy(
                src_ref=piece(t, p_refs[t], idx), dst_ref=lands[t].at[d - 1], send_sem=s1.at[t, d - 1],
                recv_sem=r1.at[t, d - 1], device_id=peer, device_id_type=MESH_ID)

        def gather(t, d, from_idx):
            peer, _ = _peer(d)
            return pltpu.make_async_remote_copy(
                src_ref=piece(t, o_refs[t], from_idx), dst_ref=piece(t, o_refs[t], from_idx), send_sem=s2.at[t, d - 1],
                recv_sem=r2.at[t, d - 1], device_id=peer, device_id_type=MESH_ID)

        first = [[scatter(t, d) for d in range(1, N_DEV)] for t in range(n)]
        for cp in sum(first, []):
            cp.start()
        second = []
        for t in range(n):
            acc = piece(t, p_refs[t], me)[...]
            for d in range(1, N_DEV):
                first[t][d - 1].wait_recv()
                acc = acc + lands[t][d - 1]
            piece(t, o_refs[t], me)[...] = acc
            second += [gather(t, d, me) for d in range(1, N_DEV)]
            for cp in second[-(N_DEV - 1):]:
                cp.start()
        for t in range(n):
            for d in range(1, N_DEV):
                gather(t, d, _peer(d)[1]).wait_recv()
        for cp in sum(first, []) + second:
            cp.wait_send()

    vmem = pl.BlockSpec(memory_space=pltpu.VMEM)
    return pl.pallas_call(
        body, name="small_all_reduce",
        out_shape=[_sds(p.shape, F32) for p in packs], in_specs=[vmem] * n, out_specs=[vmem] * n,
        scratch_shapes=[pltpu.VMEM((N_DEV - 1, pc, p.shape[1]), F32) for pc, p in zip(pieces, packs)]
        + [pltpu.SemaphoreType.DMA((n, N_DEV - 1))] * 4,
    )(*packs)


def _norm_in_proj(x, g_mix, w_in, tm, comm):
    s = x.shape[0]

    def body(x_ref, g_ref, w_ref, z_ref, h_ref):
        xn, _ = _rms(x_ref[...])
        h = (xn * g_ref[...]).astype(BF16)
        h_ref[...] = h
        for j in range(N_SEG):
            cols = slice(j * D_MODEL, (j + 1) * D_MODEL)
            z_ref[:, cols] = _dot(h, w_ref[:, cols]).astype(BF16)

    return _host_call(
        body, name="norm_in_proj", grid=(s // tm,),
        in_specs=[pl.BlockSpec((tm, D_MODEL), lambda i: (i, 0)), pl.BlockSpec((1, D_MODEL), lambda i: (0, 0)),
                  _resident(w_in.shape)],
        out_specs=[pl.BlockSpec((tm, N_SEG * D_MODEL), lambda i: (i, 0)), pl.BlockSpec((tm, D_MODEL), lambda i: (i, 0))],
        out_shape=[_sds((s, N_SEG * D_MODEL), BF16), _sds((s, D_MODEL), BF16)],
        semantics=("parallel",), args=(x, g_mix, w_in), comm=comm)


def _gate_chunk(xr_ext, w4_ref, cb_ref, wa, wx, ba_ref, bx_ref, sp, ch):
    xc = cb_ref[...] + w4_ref[3:4, :] * _shift_dn(xr_ext, 0, ch)
    for k in range(3):
        xc = xc + w4_ref[k:k + 1, :] * _shift_dn(xr_ext, 3 - k, ch)
    xcb = xc.astype(BF16)
    ra = _sigmoid(_dot(xcb, wa) + ba_ref[...])
    ia = _sigmoid(_dot(xcb, wx) + bx_ref[...])
    log_a = (-LRU_C) * ra * sp
    return xc, ra, ia, log_a


def _conv3(q_ext, w3_ref, ch):
    y = w3_ref[2:3, :] * _shift_dn(q_ext, 0, ch)
    for k in range(2):
        y = y + w3_ref[k:k + 1, :] * _shift_dn(q_ext, 2 - k, ch)
    return y


def _z_block_specs(s, cw, segs):
    nb = D_MODEL // cw
    return [pl.BlockSpec((s, cw), functools.partial(lambda j, seg: (0, seg * nb + j), seg=seg)) for seg in segs]


def _mixers_fwd(z, w4, b4, w_a, b_a, w_x, b_x, lam, w3, ch, comm):
    s = z.shape[0]
    cw = RNN_BW
    n_chunks = s // ch

    def body(xr_ref, gr_ref, cb_ref, cc_ref, cx_ref, w4_ref, b4_ref, wa_ref, ba_ref, wx_ref, bx_ref, lam_ref, w3_ref,
             ya_ref, yb_ref, h_ref, a_scr, b_scr):
        sp = _softplus(-lam_ref[...])
        wa = wa_ref[0].astype(BF16)
        wx = wx_ref[0].astype(BF16)

        def gates(c, _):
            xc, _, ia, log_a = _gate_chunk(_ext_before(xr_ref, c, ch), w4_ref, b4_ref, wa, wx, ba_ref, bx_ref, sp, ch)
            rows = pl.ds(pl.multiple_of(c * ch, HALO), ch)
            a = jnp.exp(log_a)
            a_scr[rows, :] = a
            b_scr[rows, :] = jnp.sqrt(_one_minus_sq(log_a, a)) * (ia * xc)
            q_ext = _ext_before(cc_ref, c, ch) * _ext_before(cx_ref, c, ch)
            yb_ref[rows, :] = (_rows(cb_ref, c, ch).astype(F32) * _conv3(q_ext, w3_ref, ch)).astype(BF16)
            return 0

        lax.fori_loop(0, n_chunks, gates, 0)
        _scan_fwd(a_scr, b_scr, h_ref, s, cw)

        def outputs(c, _):
            rows = pl.ds(pl.multiple_of(c * ch, HALO), ch)
            ya_ref[rows, :] = (h_ref[rows, :] * _gelu(gr_ref[rows, :].astype(F32))).astype(BF16)
            return 0

        lax.fori_loop(0, n_chunks, outputs, 0)

    col = lambda j: (0, j)
    vec = pl.BlockSpec((1, cw), col)
    sq = pl.BlockSpec((1, cw, cw), lambda j: (j, 0, 0))
    act = pl.BlockSpec((s, cw), col)
    return _host_call(
        body, name="mixers_fwd", grid=(D_MODEL // cw,),
        in_specs=_z_block_specs(s, cw, range(5)) + [pl.BlockSpec((4, cw), col), vec, sq, vec, sq, vec, vec,
                                                     pl.BlockSpec((3, cw), col)],
        out_specs=[act, act, act],
        out_shape=[_sds((s, D_MODEL), BF16), _sds((s, D_MODEL), BF16), _sds((s, D_MODEL), F32)],
        scratch_shapes=[pltpu.VMEM((s, cw), F32), pltpu.VMEM((s, cw), F32)],
        semantics=("parallel",), args=(z, z, z, z, z, w4, b4, w_a, b_a, w_x, b_x, lam, w3), comm=comm)


def _merge_out(ya, yb, z, x, w_pa, w_pb, w_out, g_ffn, tm):
    s = x.shape[0]

    def body(ya_ref, yb_ref, ga_ref, gb_ref, x_ref, wa_ref, wb_ref, wo_ref, g_ref, pa_ref, pb_ref, m_ref, x2_ref, h2_ref):
        pa = _dot(ya_ref[...], wa_ref[...])
        pb = _dot(yb_ref[...], wb_ref[...])
        pa_ref[...] = pa.astype(BF16)
        pb_ref[...] = pb.astype(BF16)
        m = (_sigmoid(ga_ref[...].astype(F32)) * pa + _sigmoid(gb_ref[...].astype(F32)) * pb).astype(BF16)
        m_ref[...] = m
        x2 = x_ref[...] + _dot(m, wo_ref[...])
        x2_ref[...] = x2
        xn, _ = _rms(x2)
        h2_ref[...] = (xn * g_ref[...]).astype(BF16)

    tile = pl.BlockSpec((tm, D_MODEL), lambda i: (i, 0))
    full = _resident((D_MODEL, D_MODEL))
    return pl.pallas_call(
        body, name="merge_out", grid=(s // tm,),
        in_specs=[tile, tile, pl.BlockSpec((tm, D_MODEL), lambda i: (i, 5)), pl.BlockSpec((tm, D_MODEL), lambda i: (i, 6)),
                  tile, full, full, full, pl.BlockSpec((1, D_MODEL), lambda i: (0, 0))],
        out_specs=[tile] * 5,
        out_shape=[_sds((s, D_MODEL), BF16)] * 3 + [_sds((s, D_MODEL), F32), _sds((s, D_MODEL), BF16)],
        compiler_params=_params("parallel"),
    )(ya, yb, z, z, x, w_pa, w_pb, w_out, g_ffn)


def _up_proj_act(h2, w_up_g, wc, bc, tm, comm):
    s = h2.shape[0]
    bw = w_up_g.shape[2]
    per_half = N_DEV // 2

    def body(h_ref, w_ref, wc_ref, bc_ref, u_ref, f_ref, halo):
        @pl.when(pl.program_id(0) == 0)
        def _():
            halo[...] = jnp.zeros_like(halo)

        h = h_ref[...]
        for j in range(per_half):
            cols = slice(j * bw, (j + 1) * bw)
            conv = []
            for half in range(2):
                u = _dot(h, w_ref[half * per_half + j]).astype(BF16)
                u_ref[half, :, cols] = u
                u = u.astype(F32)
                ext = jnp.concatenate([halo[half, :, cols], u], axis=0)
                halo[half, :, cols] = u[tm - HALO:tm]
                w = wc_ref.at[half]
                acc = bc_ref[half, :, cols] + w[2:3, cols] * u
                for k in range(2):
                    acc = acc + w[k:k + 1, cols] * _shift_dn(ext, 2 - k, tm)
                conv.append(acc)
            f_ref[:, cols] = (_gelu(conv[0]) * conv[1]).astype(BF16)

    return _host_call(
        body, name="up_proj_act", grid=(s // tm,),
        in_specs=[pl.BlockSpec((tm, D_MODEL), lambda i: (i, 0)), _resident(w_up_g.shape), _resident(wc.shape),
                  _resident(bc.shape)],
        out_specs=[pl.BlockSpec((2, tm, D_FF), lambda i: (0, i, 0)), pl.BlockSpec((tm, D_FF), lambda i: (i, 0))],
        out_shape=[_sds((2, s, D_FF), BF16), _sds((s, D_FF), BF16)],
        scratch_shapes=[pltpu.VMEM((2, HALO, D_FF), F32)],
        semantics=("arbitrary",), args=(h2, w_up_g, wc, bc), comm=comm)


def _head(x2, f, u0, wc, bc, p, target, w_down, w_gate, w_pp, g_ple, g_final, tm):
    s = x2.shape[0]
    bw = 256

    def body(x2_ref, f_ref, u_ref, uh_ref, wc_ref, bc_ref, p_ref, t_ref, wd_ref, wg_ref, wp_ref, gp_ref, gf_ref,
             x3_ref, dgt_ref, de0_ref, dx3_ref, du_ref, sums_ref, dwc_ref, dbc_ref):
        i = pl.program_id(0)

        @pl.when(i == 0)
        def _():
            sums_ref[...] = jnp.zeros_like(sums_ref)
            dwc_ref[...] = jnp.zeros_like(dwc_ref)
            dbc_ref[...] = jnp.zeros_like(dbc_ref)

        x3 = x2_ref[...] + _dot(f_ref[...], wd_ref[...])
        x3b = x3.astype(BF16)
        x3_ref[...] = x3b
        e0n, re = _rms(_dot(p_ref[...].astype(BF16), wp_ref[...]))
        e = e0n * gp_ref[...]
        sg = _sigmoid(_dot(x3b, wg_ref[...]))
        x4n, r4 = _rms(x3 + sg * e)
        diff = x4n * gf_ref[...] - t_ref[...]
        sums_ref[0:1, :] += jnp.sum(diff * diff, axis=0, keepdims=True)
        dy = diff * (1.0 / D_MODEL)
        sums_ref[1:2, :] += jnp.sum(dy * x4n, axis=0, keepdims=True)
        dx4 = _rms_bwd(dy * gf_ref[...], x4n, r4)
        de = dx4 * sg
        dgt = ((dx4 * e) * (sg * (1.0 - sg))).astype(BF16)
        dgt_ref[...] = dgt
        sums_ref[2:3, :] += jnp.sum(de * e0n, axis=0, keepdims=True)
        de0_ref[...] = _rms_bwd(de * gp_ref[...], e0n, re).astype(BF16)
        dx3 = dx4 + _dot_nt(dgt, wg_ref[...])
        dx3_ref[...] = dx3
        dx3b = dx3.astype(BF16)
        for j in range(D_FF // bw):
            cols = slice(j * bw, (j + 1) * bw)
            df = _dot_nt(dx3b, wd_ref[cols, :])
            conv, exts = [], []
            for half in range(2):
                halo = jnp.where(i > 0, uh_ref[half, :, cols].astype(F32), 0.0)
                ext = jnp.concatenate([halo, u_ref[half, :, cols].astype(F32)], axis=0)
                w = wc_ref.at[half]
                acc = bc_ref[half, :, cols] + w[2:3, cols] * _shift_dn(ext, 0, tm)
                for k in range(2):
                    acc = acc + w[k:k + 1, cols] * _shift_dn(ext, 2 - k, tm)
                conv.append(acc)
                exts.append(ext)
            gel, dgel = _gelu_and_grad(conv[0])
            for half, du in ((0, df * conv[1] * dgel), (1, df * gel)):
                du_ref[half, :, cols] = du.astype(BF16)
                dbc_ref[half, :, cols] += jnp.sum(du, axis=0, keepdims=True)
                for k in range(3):
                    dwc_ref[half, k:k + 1, cols] += jnp.sum(du * _shift_dn(exts[half], 2 - k, tm), axis=0, keepdims=True)

    tile = pl.BlockSpec((tm, D_MODEL), lambda i: (i, 0))
    vec = pl.BlockSpec((1, D_MODEL), lambda i: (0, 0))
    halves = pl.BlockSpec((2, tm, D_FF), lambda i: (0, i, 0))
    before = pl.BlockSpec((2, HALO, D_FF), lambda i: (0, jnp.maximum(i * (tm // HALO) - 1, 0), 0))
    return pl.pallas_call(
        body, name="loss_head", grid=(s // tm,),
        in_specs=[tile, pl.BlockSpec((tm, D_FF), lambda i: (i, 0)), halves, before, _resident(wc.shape),
                  _resident(bc.shape), pl.BlockSpec((tm, D_PLE), lambda i: (i, 0)), tile,
                  _resident((D_FF, D_MODEL)), _resident((D_MODEL, D_MODEL)), _resident((D_PLE, D_MODEL)), vec, vec],
        out_specs=[tile, tile, tile, tile, halves, pl.BlockSpec((8, D_MODEL), lambda i: (0, 0)),
                   pl.BlockSpec(wc.shape, lambda i: (0, 0, 0)), pl.BlockSpec(bc.shape, lambda i: (0, 0, 0))],
        out_shape=[_sds((s, D_MODEL), BF16)] * 3 + [_sds((s, D_MODEL), F32), _sds((2, s, D_FF), BF16),
                                                    _sds((8, D_MODEL), F32), _sds(wc.shape, F32), _sds(bc.shape, F32)],
        compiler_params=_params("arbitrary"),
    )(x2, f, u0, u0, wc, bc, p, target, w_down, w_gate, w_pp, g_ple, g_final)


def _up_bwd_merge_bwd(du, wc, w_up_g, x2, dx3, z, pa, pb, w_out, w_pa, w_pb, g_ffn, tm, comm):
    s = x2.shape[0]
    bw = w_up_g.shape[2]
    per_half = N_DEV // 2
    n_tiles = s // tm

    def body(du_ref, dua_ref, wc_ref, wu_ref, x2_ref, dx3_ref, ga_ref, gb_ref, pa_ref, pb_ref, wo_ref, wa_ref, wb_ref, g_ref,
             dx2_ref, dzg_ref, dpa_ref, dpb_ref, dya_ref, dyb_ref, du0_ref, sums_ref):
        i = pl.program_id(0)

        @pl.when(i == 0)
        def _():
            sums_ref[...] = jnp.zeros_like(sums_ref)

        dh2 = None
        for j in range(N_DEV):
            half = j // per_half
            cols = slice((j % per_half) * bw, (j % per_half + 1) * bw)
            halo = jnp.where(i < n_tiles - 1, dua_ref[half, :, cols].astype(F32), 0.0)
            ext = jnp.concatenate([du_ref[half, :, cols].astype(F32), halo], axis=0)
            w = wc_ref.at[half]
            acc = w[2:3, cols] * _shift_up(ext, 0, tm)
            for k in range(2):
                acc = acc + w[k:k + 1, cols] * _shift_up(ext, 2 - k, tm)
            du0 = acc.astype(BF16)
            du0_ref[half, :, cols] = du0
            term = _dot_nt(du0, wu_ref[j])
            dh2 = term if dh2 is None else dh2 + term
        x2n, r2 = _rms(x2_ref[...])
        sums_ref[0:1, :] += jnp.sum(dh2 * x2n, axis=0, keepdims=True)
        dx2 = dx3_ref[...] + _rms_bwd(dh2 * g_ref[...], x2n, r2)
        dx2_ref[...] = dx2
        dm = _dot_nt(dx2.astype(BF16), wo_ref[...])
        sa = _sigmoid(ga_ref[...].astype(F32))
        sb = _sigmoid(gb_ref[...].astype(F32))
        dzg_ref[0] = (dm * pa_ref[...].astype(F32) * (sa * (1.0 - sa))).astype(BF16)
        dzg_ref[1] = (dm * pb_ref[...].astype(F32) * (sb * (1.0 - sb))).astype(BF16)
        dpa = (dm * sa).astype(BF16)
        dpb = (dm * sb).astype(BF16)
        dpa_ref[...] = dpa
        dpb_ref[...] = dpb
        dya_ref[...] = _dot_nt(dpa, wa_ref[...]).astype(BF16)
        dyb_ref[...] = _dot_nt(dpb, wb_ref[...]).astype(BF16)

    tile = pl.BlockSpec((tm, D_MODEL), lambda i: (i, 0))
    full = _resident((D_MODEL, D_MODEL))
    halves = pl.BlockSpec((2, tm, D_FF), lambda i: (0, i, 0))
    after = pl.BlockSpec((2, HALO, D_FF), lambda i: (0, jnp.minimum((i + 1) * (tm // HALO), s // HALO - 1), 0))
    return _host_call(
        body, name="up_bwd_merge_bwd", grid=(n_tiles,),
        in_specs=[halves, after, _resident(wc.shape), _resident(w_up_g.shape),
                  tile, tile, pl.BlockSpec((tm, D_MODEL), lambda i: (i, 5)),
                  pl.BlockSpec((tm, D_MODEL), lambda i: (i, 6)), tile, tile, full, full, full,
                  pl.BlockSpec((1, D_MODEL), lambda i: (0, 0))],
        out_specs=[tile, pl.BlockSpec((2, tm, D_MODEL), lambda i: (0, i, 0)), tile, tile, tile, tile, halves,
                   pl.BlockSpec((8, D_MODEL), lambda i: (0, 0))],
        out_shape=[_sds((s, D_MODEL), F32), _sds((2, s, D_MODEL), BF16)] + [_sds((s, D_MODEL), BF16)] * 4
        + [_sds((2, s, D_FF), BF16), _sds((8, D_MODEL), F32)],
        semantics=("arbitrary",), args=(du, du, wc, w_up_g, x2, dx3, z, z, pa, pb, w_out, w_pa, w_pb, g_ffn),
        comm=comm)


def _mixers_bwd(z, h, dya, dyb, w4, b4, w_a, b_a, w_x, b_x, lam, w3, ch, comm):
    s = z.shape[0]
    cw = RNN_BW
    ch1, ch2, ch3 = ch

    def body(xr_ref, gr_ref, cb_ref, cc_ref, cx_ref, h_ref, dya_ref, dyb_ref, w4_ref, b4_ref, wa_ref, ba_ref, wx_ref,
             bx_ref, lam_ref, w3_ref,
             dz_ref, dw4_ref, db4_ref, dwa_ref, dba_ref, dwx_ref, dbx_ref, dlam_ref, dw3_ref,
             a_scr, g_scr, q_scr, xc_scr, ra_scr, ia_scr, dh_scr):
        sp = _softplus(-lam_ref[...])
        wa = wa_ref[0].astype(BF16)
        wx = wx_ref[0].astype(BF16)
        for ref in (dw4_ref, db4_ref, dwa_ref, dba_ref, dwx_ref, dbx_ref, dlam_ref, dw3_ref):
            ref[...] = jnp.zeros_like(ref)

        def first(c, _, ch=ch1):
            rows = pl.ds(pl.multiple_of(c * ch, HALO), ch)
            xc, ra, ia, log_a = _gate_chunk(_ext_before(xr_ref, c, ch), w4_ref, b4_ref, wa, wx, ba_ref, bx_ref, sp, ch)
            xc_scr[rows, :] = xc
            ra_scr[rows, :] = ra
            ia_scr[rows, :] = ia
            a_scr[rows, :] = jnp.exp(log_a)
            gel, dgel = _gelu_and_grad(gr_ref[rows, :].astype(F32))
            dya_c = dya_ref[rows, :].astype(F32)
            g_scr[rows, :] = dya_c * gel
            dz_ref[1, rows, :] = (dya_c * h_ref[rows, :] * dgel).astype(BF16)
            q_ext = _ext_before(cc_ref, c, ch) * _ext_before(cx_ref, c, ch)
            dyb_c = dyb_ref[rows, :].astype(F32)
            dz_ref[2, rows, :] = (dyb_c * _conv3(q_ext, w3_ref, ch)).astype(BF16)
            dyq = dyb_c * cb_ref[rows, :].astype(F32)
            q_scr[rows, :] = dyq
            for k in range(3):
                dw3_ref[k:k + 1, :] += jnp.sum(dyq * _shift_dn(q_ext, 2 - k, ch), axis=0, keepdims=True)
            return 0

        lax.fori_loop(0, s // ch1, first, 0)
        _scan_rev(a_scr, g_scr, dh_scr, s, cw)

        def second(c, _, ch=ch2):
            rows = pl.ds(pl.multiple_of(c * ch, HALO), ch)
            xc, ra, ia, a = xc_scr[rows, :], ra_scr[rows, :], ia_scr[rows, :], a_scr[rows, :]
            m2 = _one_minus_sq((-LRU_C) * ra * sp, a)
            inv_mult = lax.rsqrt(m2)
            mult = m2 * inv_mult
            dh = dh_scr[rows, :]
            h_prev = _shift_dn(_ext_before(h_ref, c, ch), 1, ch)
            dlog_a = dh * h_prev * a - (dh * ia * xc) * (a * a) * inv_mult
            dlam_ref[...] += jnp.sum(dlog_a * ra, axis=0, keepdims=True)
            dpre_a = (dlog_a * ((-LRU_C) * sp)) * (ra * (1.0 - ra))
            dpre_x = (dh * mult * xc) * (ia * (1.0 - ia))
            dba_ref[...] += jnp.sum(dpre_a, axis=0, keepdims=True)
            dbx_ref[...] += jnp.sum(dpre_x, axis=0, keepdims=True)
            xcb = xc.astype(BF16)
            dpa_b = dpre_a.astype(BF16)
            dpx_b = dpre_x.astype(BF16)
            dwa_ref[0] += _dot_tn(xcb, dpa_b)
            dwx_ref[0] += _dot_tn(xcb, dpx_b)
            a_scr[rows, :] = dh * mult * ia + _dot_nt(dpa_b, wa) + _dot_nt(dpx_b, wx)
            return 0

        lax.fori_loop(0, s // ch2, second, 0)
        dlam_ref[...] = dlam_ref[...] * (LRU_C * _sigmoid(-lam_ref[...]))

        def third(c, _, ch=ch3, n_chunks=s // ch3):
            rows = pl.ds(pl.multiple_of(c * ch, HALO), ch)
            dxc_ext = _ext_after(a_scr, c, ch, n_chunks)
            dxc = _shift_up(dxc_ext, 0, ch)
            xr_ext = _ext_before(xr_ref, c, ch)
            db4_ref[...] += jnp.sum(dxc, axis=0, keepdims=True)
            dxr = w4_ref[3:4, :] * dxc
            dw4_ref[3:4, :] += jnp.sum(dxc * _shift_dn(xr_ext, 0, ch), axis=0, keepdims=True)
            for k in range(3):
                dxr = dxr + w4_ref[k:k + 1, :] * _shift_up(dxc_ext, 3 - k, ch)
                dw4_ref[k:k + 1, :] += jnp.sum(dxc * _shift_dn(xr_ext, 3 - k, ch), axis=0, keepdims=True)
            dz_ref[0, rows, :] = dxr.astype(BF16)
            dyq_ext = _ext_after(q_scr, c, ch, n_chunks)
            dq = w3_ref[2:3, :] * _shift_up(dyq_ext, 0, ch)
            for k in range(2):
                dq = dq + w3_ref[k:k + 1, :] * _shift_up(dyq_ext, 2 - k, ch)
            dz_ref[3, rows, :] = (dq * cx_ref[rows, :].astype(F32)).astype(BF16)
            dz_ref[4, rows, :] = (dq * cc_ref[rows, :].astype(F32)).astype(BF16)
            return 0

        lax.fori_loop(0, s // ch3, third, 0)

    col = lambda j: (0, j)
    vec = pl.BlockSpec((1, cw), col)
    sq = pl.BlockSpec((1, cw, cw), lambda j: (j, 0, 0))
    act = pl.BlockSpec((s, cw), col)
    w4s, w3s = pl.BlockSpec((4, cw), col), pl.BlockSpec((3, cw), col)
    vec_shape = _sds((1, D_MODEL), F32)
    sq_shape = _sds((D_MODEL // cw, cw, cw), F32)
    return _host_call(
        body, name="mixers_bwd", grid=(D_MODEL // cw,),
        in_specs=_z_block_specs(s, cw, range(5)) + [act, act, act, w4s, vec, sq, vec, sq, vec, vec, w3s],
        out_specs=[pl.BlockSpec((5, s, cw), lambda j: (0, 0, j)), w4s, vec, sq, vec, sq, vec, vec, w3s],
        out_shape=[_sds((5, s, D_MODEL), BF16), _sds((4, D_MODEL), F32), vec_shape, sq_shape, vec_shape, sq_shape,
                   vec_shape, vec_shape, _sds((3, D_MODEL), F32)],
        scratch_shapes=[pltpu.VMEM((s, cw), F32)] * 7,
        semantics=("parallel",), args=(z, z, z, z, z, h, dya, dyb, w4, b4, w_a, b_a, w_x, b_x, lam, w3), comm=comm)


def _in_proj_bwd(dz5, dzg, w_in, x, dx2, g_mix, tm, comm):
    s = x.shape[0]

    def body(d5_ref, dg_ref, w_ref, x_ref, dx2_ref, g_ref, dx_ref, sums_ref):
        @pl.when(pl.program_id(0) == 0)
        def _():
            sums_ref[...] = jnp.zeros_like(sums_ref)

        dh1 = None
        for k in range(N_SEG):
            d = d5_ref[k] if k < 5 else dg_ref[k - 5]
            term = _dot_nt(d, w_ref[:, k * D_MODEL:(k + 1) * D_MODEL])
            dh1 = term if dh1 is None else dh1 + term
        xn, r1 = _rms(x_ref[...])
        sums_ref[0:1, :] += jnp.sum(dh1 * xn, axis=0, keepdims=True)
        dx_ref[...] = dx2_ref[...] + _rms_bwd(dh1 * g_ref[...], xn, r1)

    tile = pl.BlockSpec((tm, D_MODEL), lambda i: (i, 0))
    return _host_call(
        body, name="in_proj_bwd", grid=(s // tm,),
        in_specs=[pl.BlockSpec((5, tm, D_MODEL), lambda i: (0, i, 0)), pl.BlockSpec((2, tm, D_MODEL), lambda i: (0, i, 0)),
                  _resident(w_in.shape), tile, tile, pl.BlockSpec((1, D_MODEL), lambda i: (0, 0))],
        out_specs=[tile, pl.BlockSpec((8, D_MODEL), lambda i: (0, 0))],
        out_shape=[_sds((s, D_MODEL), F32), _sds((8, D_MODEL), F32)],
        semantics=("arbitrary",), args=(dz5, dzg, w_in, x, dx2, g_mix), comm=comm)


def _weight_grad(a, b, b_spec, out_spec, out_shape, *, n_blocks, chunks, width, tm, tk, name):
    s, m = a.shape
    nk = s // tk
    b_chunked = len([d for d in b_spec.block_shape if d is not None]) == 3
    blocks_out = len([d for d in out_spec.block_shape if d is not None]) == 3

    def body(a_ref, b_ref, o_ref, acc):
        k = pl.program_id(2)

        @pl.when(k == 0)
        def _():
            acc[...] = jnp.zeros_like(acc)

        at = a_ref[...].astype(BF16).T
        for j in range(chunks):
            cols = slice(j * width, (j + 1) * width)
            acc[:, cols] += _dot(at, (b_ref[j] if b_chunked else b_ref[:, cols]).astype(BF16))

        @pl.when(k == nk - 1)
        def _():
            if blocks_out:
                for j in range(chunks):
                    o_ref[j] = acc[:, j * width:(j + 1) * width].astype(o_ref.dtype)
            else:
                o_ref[...] = acc[...].astype(o_ref.dtype)

    return pl.pallas_call(
        body, name=name, grid=(m // tm, n_blocks, nk),
        in_specs=[pl.BlockSpec((tk, tm), lambda i, j, k: (k, i)), b_spec], out_specs=out_spec, out_shape=out_shape,
        scratch_shapes=[pltpu.VMEM((tm, chunks * width), F32)],
        compiler_params=_params("parallel", "parallel", "arbitrary"),
    )(a, b)


def _wgrad_2d(a, b, name, tk):
    m, n = a.shape[1], b.shape[1]
    tm = m
    return _weight_grad(a, b, pl.BlockSpec((tk, n), lambda i, j, k: (k, 0)), pl.BlockSpec((tm, n), lambda i, j, k: (i, 0)),
                        _sds((m, n), BF16), n_blocks=1, chunks=1, width=n, tm=tm, tk=tk, name=name)


def _wgrad_in(h1, dz5, dzg, tm, tk):
    s, m = h1.shape
    nk = s // tk
    g5, gg = dz5.shape[0], dzg.shape[0]

    def body(a_ref, b5_ref, bg_ref, o_ref, acc):
        k = pl.program_id(1)

        @pl.when(k == 0)
        def _():
            acc[...] = jnp.zeros_like(acc)

        at = a_ref[...].T
        for j in range(g5 + gg):
            b = b5_ref[j] if j < g5 else bg_ref[j - g5]
            acc[:, j * D_MODEL:(j + 1) * D_MODEL] += _dot(at, b)

        @pl.when(k == nk - 1)
        def _():
            o_ref[...] = acc[...].astype(BF16)

    return pl.pallas_call(
        body, name="wgrad_in", grid=(m // tm, nk),
        in_specs=[pl.BlockSpec((tk, tm), lambda i, k: (k, i)), pl.BlockSpec((g5, tk, D_MODEL), lambda i, k: (0, k, 0)),
                  pl.BlockSpec((gg, tk, D_MODEL), lambda i, k: (0, k, 0))],
        out_specs=pl.BlockSpec((tm, (g5 + gg) * D_MODEL), lambda i, k: (i, 0)),
        out_shape=_sds((m, (g5 + gg) * D_MODEL), BF16),
        scratch_shapes=[pltpu.VMEM((tm, (g5 + gg) * D_MODEL), F32)],
        compiler_params=_params("parallel", "arbitrary"),
    )(h1, dz5, dzg)


def _wgrad_up(h2, du0, bw, tk):
    per_half = N_DEV // 2
    return _weight_grad(h2, du0, pl.BlockSpec((None, tk, D_FF), lambda i, j, k: (j, k, 0)),
                        pl.BlockSpec((per_half, D_MODEL, bw), lambda i, j, k: (j, 0, 0)), _sds((N_DEV, D_MODEL, bw), BF16),
                        n_blocks=2, chunks=per_half, width=bw, tm=D_MODEL, tk=tk, name="wgrad_up")


def _adam_math(w, g, m, v):
    m = ADAM_B1 * m + (1.0 - ADAM_B1) * g
    v = ADAM_B2 * v + (1.0 - ADAM_B2) * jnp.square(g)
    m_hat = m / (1.0 - ADAM_B1 ** ADAM_STEP)
    v_hat = v / (1.0 - ADAM_B2 ** ADAM_STEP)
    delta = -ADAM_LR * (m_hat / (jnp.sqrt(v_hat) + ADAM_EPS) + ADAM_WD * w)
    return delta, m, v


def _adam_shard(parts, w, m, v, name):
    r, c = w.shape
    tr = min(r, 128)
    n_arrays = len(parts)

    def body(*refs):
        p_refs = refs[:n_arrays]
        w_ref, m_ref, v_ref, g_ref, d_ref, nm_ref, nv_ref = refs[n_arrays:]
        g = None
        for p_ref, (arr, first) in zip(p_refs, parts):
            for k in range(first, arr.shape[0]):
                term = p_ref[k].astype(F32)
                g = term if g is None else g + term
        g_ref[...] = g
        d_ref[...], nm_ref[...], nv_ref[...] = _adam_math(w_ref[...], g, m_ref[...], v_ref[...])

    tile = pl.BlockSpec((tr, c), lambda i: (i, 0))
    return pl.pallas_call(
        body, name=name, grid=(r // tr,),
        in_specs=[pl.BlockSpec((arr.shape[0], tr, c), lambda i: (0, i, 0)) for arr, _ in parts] + [tile, tile, tile],
        out_specs=[tile] * 4, out_shape=[_sds((r, c), F32)] * 4,
        compiler_params=_params("parallel"),
    )(*[arr for arr, _ in parts], w, m, v)


def _adam_small(grads, ws, ms, vs, name):
    n = len(ws)
    g_arrays, g_slot = [], []
    for g in grads:
        arr = g[0] if isinstance(g, tuple) else g
        if not any(arr is a for a in g_arrays):
            g_arrays.append(arr)
        g_slot.append([arr is a for a in g_arrays].index(True))
    n_g = len(g_arrays)

    def body(*refs):
        g_refs, refs = refs[:n_g], refs[n_g:]
        w_refs, m_refs, v_refs, outs = refs[:n], refs[n:2 * n], refs[2 * n:3 * n], refs[3 * n:]
        for i in range(n):
            og, od, om, ov = outs[4 * i:4 * i + 4]
            g_ref = g_refs[g_slot[i]]
            if isinstance(grads[i], tuple):
                row = grads[i][1]
                for j in range(ws[i].shape[1] // D_MODEL):
                    cols = slice(j * D_MODEL, (j + 1) * D_MODEL)
                    g = g_ref[row + j:row + j + 1, :]
                    og[:, cols] = g
                    od[:, cols], om[:, cols], ov[:, cols] = _adam_math(w_refs[i][:, cols], g, m_refs[i][:, cols],
                                                                       v_refs[i][:, cols])
            else:
                g = g_ref[...]
                og[...] = g
                od[...], om[...], ov[...] = _adam_math(w_refs[i][...], g, m_refs[i][...], v_refs[i][...])

    vmem = pl.BlockSpec(memory_space=pltpu.VMEM)
    res = pl.pallas_call(
        body, name=name, in_specs=[vmem] * (n_g + 3 * n), out_specs=[vmem] * (4 * n),
        out_shape=[_sds(a.shape, F32) for a in ws for _ in range(4)],
    )(*g_arrays, *ws, *ms, *vs)
    return [res[4 * i:4 * i + 4] for i in range(n)]


def _pack_conv_shard(rnn, sc, ffn):
    top = jnp.concatenate([rnn[:3], sc, ffn], axis=1)
    row3 = jnp.concatenate([rnn[3:4], jnp.zeros((1, D_MODEL - RNN_BW), F32)], axis=1)
    return jnp.concatenate([top, row3, jnp.zeros((4, D_MODEL), F32)], axis=0)


_VECTORS = ("g_mix", "rnn_conv_b", "b_rg_a", "b_rg_x", "lru_lambda", "g_ffn", "g_ple", "g_final", "ffn_conv_b")
VEC_ROWS = 64
ROW_LOSS, ROW_RNN_CONV, ROW_SC_CONV, ROW_FFN_CONV = 14, 16, 20, 24


_SHARDED_BIG = ("w_in", "w_proj_a", "w_proj_b", "w_out", "w_up", "w_down", "w_ple_gate", "w_ple_proj")
_NAMES = ("g_mix", "w_in", "rnn_conv_w", "rnn_conv_b", "w_rg_a", "b_rg_a", "w_rg_x", "b_rg_x", "lru_lambda", "sc_conv_w",
          "w_proj_a", "w_proj_b", "w_out", "g_ffn", "w_up", "ffn_conv_w", "ffn_conv_b", "w_down", "w_ple_gate",
          "w_ple_proj", "g_ple", "g_final")


def _step(x, p, target, w, m, v):
    s = x.shape[0]
    tm = min(s, 256)
    tm_wide = min(s, 512)
    ch = min(s, 256)
    ch_mix = min(s, 512)
    ch_bwd = (min(s, 512), min(s, 1024), min(s, 256))
    my_index = 4 * lax.axis_index("x") + 2 * lax.axis_index("y") + lax.axis_index("c")

    big = {n: w[n][0] for n in _SHARDED_BIG}
    conv_shard = _pack_conv_shard(w["rnn_conv_w"][0], w["sc_conv_w"][0], w["ffn_conv_w"][0])
    shards = {n: big[n].astype(BF16) for n in _SHARDED_BIG}
    w_in, conv_all = _comm_only(_gather_comm([shards["w_in"], conv_shard], by_columns=(0,)), "in_proj_weight_gather")
    w4 =jnp.transpose(jnp.concatenate([conv_all[:, :3, :RNN_BW], conv_all[:, 3:4, :RNN_BW]], axis=1),
                       (1, 0, 2)).reshape(4, D_MODEL)
    w3 = jnp.transpose(conv_all[:, :3, RNN_BW:2 * RNN_BW], (1, 0, 2)).reshape(3, D_MODEL)
    wc = jnp.transpose(conv_all[:, :3, 2 * RNN_BW:], (1, 0, 2)).reshape(3, 2, D_FF).transpose(1, 0, 2)
    bc = w["ffn_conv_b"].reshape(2, 1, D_FF)
    b4, b_a, b_x, lam = w["rnn_conv_b"], w["b_rg_a"], w["b_rg_x"], w["lru_lambda"]
    w_a, w_x = w["w_rg_a"][0], w["w_rg_x"][0]
    g_final = w["g_final"].reshape(1, D_MODEL)

    tk = min(s, 512)
    received = {}

    comm = _gather_comm([shards[n] for n in ("w_proj_a", "w_proj_b", "w_out")])
    (z, h1), (w_pa, w_pb, w_out) = _norm_in_proj(x, w["g_mix"], w_in, tm_wide, comm)
    w_pa, w_pb, w_out = (a.reshape(D_MODEL, D_MODEL) for a in (w_pa, w_pb, w_out))
    (ya, yb, h), (w_up_g,) = _mixers_fwd(z, w4, b4, w_a, b_a, w_x, b_x, lam, w3, ch_mix, _gather_comm([shards["w_up"]]))
    pa, pb, mm, x2, h2 = _merge_out(ya, yb, z, x, w_pa, w_pb, w_out, w["g_ffn"], tm)
    comm = _gather_comm([shards[n] for n in ("w_down", "w_ple_gate", "w_ple_proj")], by_columns=(2,))
    (u0, f), (w_down, w_gate, w_pp) = _up_proj_act(h2, w_up_g, wc, bc, tm, comm)
    w_down = w_down.reshape(D_FF, D_MODEL)
    w_gate = w_gate.reshape(D_MODEL, D_MODEL)
    x3, dgt, de0, dx3, du, head_sums, dwc, dbc = _head(x2, f, u0, wc, bc, p, target, w_down, w_gate, w_pp, w["g_ple"],
                                                       g_final, tm)
    parts = [_wgrad_2d(x3, dgt, "wgrad_ple_gate", tk).reshape(N_DEV, RNN_BW, D_MODEL),
             jnp.transpose(_wgrad_2d(p, de0, "wgrad_ple_proj", tk).reshape(D_PLE, N_DEV, RNN_BW), (1, 0, 2)),
             _wgrad_2d(f, dx3, "wgrad_down", tk).reshape(N_DEV, D_FF // N_DEV, D_MODEL)]
    (dx2, dzg, dpa, dpb, dya, dyb, du0, ffn_sums), got = _up_bwd_merge_bwd(
        du, wc, w_up_g, x2, dx3, z, pa, pb, w_out, w_pa, w_pb, w["g_ffn"], tm, _exchange_comm(parts))
    received.update({n: [(g, 0)] for n, g in zip(("w_ple_gate", "w_ple_proj", "w_down"), got)})
    up_sum = _pair_reduce(_wgrad_up(h2, du0, w_up_g.shape[2], tk), "up_grad_pair_reduce")
    parts = [_wgrad_2d(mm, dx2, "wgrad_out", tk).reshape(N_DEV, RNN_BW, D_MODEL),
             _wgrad_2d(ya, dpa, "wgrad_proj_a", tk).reshape(N_DEV, RNN_BW, D_MODEL),
             _wgrad_2d(yb, dpb, "wgrad_proj_b", tk).reshape(N_DEV, RNN_BW, D_MODEL)]
    comm = _join_comms(_exchange_comm(parts), _chip_exchange_comm([up_sum]))
    (dz5, dw4, db4, dwa, dba, dwx, dbx, dlam, dw3), got = _mixers_bwd(z, h, dya, dyb, w4, b4, w_a, b_a, w_x, b_x, lam, w3,
                                                                       ch_bwd, comm)
    received.update({n: [(g, 0)] for n, g in zip(("w_out", "w_proj_a", "w_proj_b", "w_up"), got)})
    chip_sum = _pair_reduce(_wgrad_in(h1, dz5, dzg, 512, tk), "in_proj_grad_pair_reduce")
    (grad_x, mix_sums), got = _in_proj_bwd(dz5, dzg, w_in, x, dx2, w["g_mix"], tm, _chip_exchange_comm([chip_sum]))
    received["w_in"] = [(got[0], 0)]

    zero_row = jnp.zeros((1, D_MODEL), F32)
    vec_pack = jnp.concatenate(
        [mix_sums[0:1], db4, dba, dbx, dlam, ffn_sums[0:1], head_sums[2:3], head_sums[1:2], dbc.reshape(6, D_MODEL),
         head_sums[0:1], zero_row, dw4, dw3, zero_row, jnp.transpose(dwc, (1, 0, 2)).reshape(18, D_MODEL),
         jnp.zeros((VEC_ROWS - ROW_FFN_CONV - 18, D_MODEL), F32)], axis=0)
    gate_pack = jnp.concatenate([dwa.reshape(D_MODEL, RNN_BW), dwx.reshape(D_MODEL, RNN_BW)], axis=0)
    vec_total, gate_total = _small_all_reduce([vec_pack, gate_pack])
    loss = jnp.sum(vec_total[ROW_LOSS]) * (0.5 / D_MODEL)

    out = {}
    for n in _SHARDED_BIG:
        res = _adam_shard(received[n], big[n], m[n][0], v[n][0], "adam_" + n)
        out[n] = [r[None] for r in res]

    def flat(t, n):
        return t[n].reshape(1, -1)

    res = _adam_small([(vec_total, i) for i in range(len(_VECTORS))],
                      [flat(w, n) for n in _VECTORS], [flat(m, n) for n in _VECTORS], [flat(v, n) for n in _VECTORS],
                      "adam_vectors")
    for n, r in zip(_VECTORS, res):
        out[n] = [a.reshape(w[n].shape) for a in r]
    gates = ("w_rg_a", "w_rg_x")
    res = _adam_small([gate_total[:D_MODEL], gate_total[D_MODEL:]], *[[t[n].reshape(D_MODEL, RNN_BW) for n in gates]
                                                                      for t in (w, m, v)], "adam_gate_maps")
    for n, r in zip(gates, res):
        out[n] = [a.reshape(w[n].shape) for a in r]
    convs = ("rnn_conv_w", "sc_conv_w", "ffn_conv_w")
    bw_ffn = 2 * D_FF // N_DEV
    g_conv = [lax.dynamic_slice(vec_total[ROW_RNN_CONV:ROW_RNN_CONV + 4], (0, my_index * RNN_BW), (4, RNN_BW)),
              lax.dynamic_slice(vec_total[ROW_SC_CONV:ROW_SC_CONV + 3], (0, my_index * RNN_BW), (3, RNN_BW)),
              lax.dynamic_slice(vec_total[ROW_FFN_CONV:ROW_FFN_CONV + 18].reshape(3, 2 * D_FF), (0, my_index * bw_ffn),
                                (3, bw_ffn))]
    res = _adam_small(g_conv, *[[t[n][0] for n in convs] for t in (w, m, v)], "adam_conv")
    for n, r in zip(convs, res):
        out[n] = [a[None] for a in r]

    return (loss, grad_x[None]) + tuple(out[n][k] for k in range(4) for n in _NAMES)


def kernel(x, p, g_mix, w_in, rnn_conv_w, rnn_conv_b, w_rg_a, b_rg_a, w_rg_x, b_rg_x, lru_lambda, sc_conv_w, w_proj_a, w_proj_b, w_out, g_ffn, w_up, ffn_conv_w, ffn_conv_b, w_down, w_ple_gate, w_ple_proj, g_ple, g_final, loss_target, m_g_mix, m_w_in, m_rnn_conv_w, m_rnn_conv_b, m_w_rg_a, m_b_rg_a, m_w_rg_x, m_b_rg_x, m_lru_lambda, m_sc_conv_w, m_w_proj_a, m_w_proj_b, m_w_out, m_g_ffn, m_w_up, m_ffn_conv_w, m_ffn_conv_b, m_w_down, m_w_ple_gate, m_w_ple_proj, m_g_ple, m_g_final, v_g_mix, v_w_in, v_rnn_conv_w, v_rnn_conv_b, v_w_rg_a, v_b_rg_a, v_w_rg_x, v_b_rg_x, v_lru_lambda, v_sc_conv_w, v_w_proj_a, v_w_proj_b, v_w_out, v_g_ffn, v_w_up, v_ffn_conv_w, v_ffn_conv_b, v_w_down, v_w_ple_gate, v_w_ple_proj, v_g_ple, v_g_final):
    given = dict(locals())
    w = {n: given[n] for n in _NAMES}
    m = {n: given["m_" + n] for n in _NAMES}
    v = {n: given["v_" + n] for n in _NAMES}
    return _step(x[0], p[0, 0], loss_target[0], w, m, v)
```

```python
import functools

import jax
import jax.numpy as jnp
from jax import lax
from jax.experimental import pallas as pl
from jax.experimental.pallas import tpu as pltpu

F32 = jnp.float32
BF16 = jnp.bfloat16
MESH_ID = pl.DeviceIdType.MESH

N_DEV = 8
D_MODEL = 1024
D_PLE = 256
RNN_BW = 128
N_SEG = 7
D_FF = 3072
LRU_C = 8.0
EPS = 1e-6
ADAM_LR = 0.001
ADAM_B1 = 0.9
ADAM_B2 = 0.999
ADAM_EPS = 1e-08
ADAM_WD = 0.01
ADAM_STEP = 10

HALO = 16
SCAN_UNROLL = 4
VMEM_LIMIT = 56 * 1024 * 1024


def _dot(a, b):
    return jnp.dot(a, b, preferred_element_type=F32)


def _dot_nt(a, b):
    return lax.dot_general(a, b, (((1,), (1,)), ((), ())), preferred_element_type=F32)


def _dot_tn(a, b):
    return lax.dot_general(a, b, (((0,), (0,)), ((), ())), preferred_element_type=F32)


def _sigmoid(x):
    return 0.5 * jnp.tanh(0.5 * x) + 0.5


_GELU_C = 0.7978845608028654
_GELU_K = 0.044715


def _gelu(x):
    return 0.5 * x * (1.0 + jnp.tanh(_GELU_C * (x + _GELU_K * (x * x * x))))


def _gelu_and_grad(x):
    x2 = x * x
    t = jnp.tanh(_GELU_C * (x + _GELU_K * (x * x2)))
    g = 0.5 * x * (1.0 + t)
    dg = 0.5 * (1.0 + t) + 0.5 * x * (1.0 - t * t) * (_GELU_C * (1.0 + 3.0 * _GELU_K * x2))
    return g, dg


def _one_minus_sq(log_a, a):
    series = (-2.0 * log_a) * (1.0 + log_a * (1.0 + log_a * (2.0 / 3.0)))
    return jnp.where(log_a > -0.005, series, 1.0 - a * a)


def _softplus(x):
    return jnp.maximum(x, 0.0) + jnp.log1p(jnp.exp(-jnp.abs(x)))


def _rms(u):
    r = lax.rsqrt(jnp.mean(u * u, axis=-1, keepdims=True) + EPS)
    return u * r, r


def _rms_bwd(dn, un, r):
    return r * (dn - un * jnp.mean(dn * un, axis=-1, keepdims=True))


def _shift_dn(ext, s, n):
    if s == 0:
        return ext[HALO:HALO + n]
    return pltpu.roll(ext, s, 0)[HALO:HALO + n]


def _shift_up(ext, s, n):
    if s == 0:
        return ext[0:n]
    return pltpu.roll(ext, n + HALO - s, 0)[0:n]


def _ext_before(ref, c, ch):
    t0 = c * ch
    prev = pl.multiple_of(jnp.maximum(t0 - HALO, 0), HALO)
    halo = jnp.where(c > 0, ref[pl.ds(prev, HALO), :].astype(F32), 0.0)
    cur = ref[pl.ds(pl.multiple_of(t0, HALO), ch), :].astype(F32)
    return jnp.concatenate([halo, cur], axis=0)


def _ext_after(ref, c, ch, n_chunks):
    t0 = c * ch
    nxt = pl.multiple_of(jnp.minimum(t0 + ch, (n_chunks - 1) * ch + ch - HALO), HALO)
    halo = jnp.where(c < n_chunks - 1, ref[pl.ds(nxt, HALO), :].astype(F32), 0.0)
    cur = ref[pl.ds(pl.multiple_of(t0, HALO), ch), :].astype(F32)
    return jnp.concatenate([cur, halo], axis=0)


def _rows(ref, c, ch):
    return ref[pl.ds(pl.multiple_of(c * ch, HALO), ch), :]


def _scan_fwd(a_ref, b_ref, h_ref, n_rows, cw):
    rows = lax.broadcasted_iota(jnp.int32, (8, cw), 0)

    def body(i, carry):
        for u in range(SCAN_UNROLL):
            t = pl.multiple_of((i * SCAN_UNROLL + u) * 8, 8)
            a = a_ref[pl.ds(t, 8), :]
            b = b_ref[pl.ds(t, 8), :]
            for d in (1, 2, 4):
                m = rows >= d
                b = jnp.where(m, a * pltpu.roll(b, d, 0) + b, b)
                a = jnp.where(m, a * pltpu.roll(a, d, 0), a)
            h_ref[pl.ds(t, 8), :] = a * carry + b
            carry = jnp.broadcast_to(a[7:8, :], (8, cw)) * carry + jnp.broadcast_to(b[7:8, :], (8, cw))
        return carry

    lax.fori_loop(0, n_rows // (8 * SCAN_UNROLL), body, jnp.zeros((8, cw), F32))


def _scan_rev(a_ref, g_ref, o_ref, n_rows, cw):
    rows = lax.broadcasted_iota(jnp.int32, (8, cw), 0)
    n_groups = n_rows // 8

    def body(i, carry):
        dh_next, a_next = carry
        for u in range(SCAN_UNROLL):
            t = pl.multiple_of((n_groups - 1 - (i * SCAN_UNROLL + u)) * 8, 8)
            a = a_ref[pl.ds(t, 8), :]
            g = g_ref[pl.ds(t, 8), :]
            an = jnp.where(rows < 7, pltpu.roll(a, 7, 0), a_next)
            a_next = jnp.broadcast_to(a[0:1, :], (8, cw))
            for d in (1, 2, 4):
                m = rows < 8 - d
                g = jnp.where(m, an * pltpu.roll(g, 8 - d, 0) + g, g)
                an = jnp.where(m, an * pltpu.roll(an, 8 - d, 0), an)
            o_ref[pl.ds(t, 8), :] = an * dh_next + g
            dh_next = jnp.broadcast_to(an[0:1, :], (8, cw)) * dh_next + jnp.broadcast_to(g[0:1, :], (8, cw))
        return dh_next, a_next

    zero = jnp.zeros((8, cw), F32)
    lax.fori_loop(0, n_groups // SCAN_UNROLL, body, (zero, zero))


def _params(*sem):
    return pltpu.CompilerParams(dimension_semantics=sem, vmem_limit_bytes=VMEM_LIMIT)


def _sds(shape, dtype):
    return jax.ShapeDtypeStruct(shape, dtype)


def _resident(shape):
    zeros = (0,) * len(shape)
    return pl.BlockSpec(shape, lambda *_: zeros, pipeline_mode=pl.Buffered(1))


class _Comm:
    def __init__(self, inputs, out_shape, scratch, start, finish, middle=None):
        self.inputs, self.out_shape, self.scratch, self.start, self.finish = inputs, out_shape, scratch, start, finish
        self.middle = middle


def _join_comms(a, b):
    n, k = len(a.inputs), len(a.scratch)

    def start(ins, outs, sems):
        a.start(ins[:n], outs[:n], sems[:k])
        b.start(ins[n:], outs[n:], sems[k:])

    def finish(ins, outs, sems):
        a.finish(ins[:n], outs[:n], sems[:k])
        b.finish(ins[n:], outs[n:], sems[k:])

    return _Comm(a.inputs + b.inputs, a.out_shape + b.out_shape, a.scratch + b.scratch, start, finish)


def _sem_scratch(n, copies=7):
    return [pltpu.SemaphoreType.DMA((n, copies)), pltpu.SemaphoreType.DMA((n, copies)), pltpu.SemaphoreType.DMA((n,))]


def _gather_comm(shards, by_columns=(), early_pass_on=False):
    n = len(shards)

    def plan(ins, outs, sems):
        send_sems, recv_sems, local_sems = sems
        x, y, c = lax.axis_index("x"), lax.axis_index("y"), lax.axis_index("c")
        me, sibling = (x, y, c), (x, y, 1 - c)
        chips = [(1 - x, y), (x, 1 - y), (1 - x, 1 - y)]

        def slot(t, dev):
            idx = 4 * dev[0] + 2 * dev[1] + dev[2]
            if t in by_columns:
                width = shards[t].shape[1]
                return outs[t].at[:, pl.ds(pl.multiple_of(idx * width, 128), width)]
            return outs[t].at[idx]

        def copy(t, k, block, to, src=None):
            return pltpu.make_async_remote_copy(
                src_ref=slot(t, block) if src is None else src, dst_ref=slot(t, block),
                send_sem=send_sems.at[t, k], recv_sem=recv_sems.at[t, k],
                device_id=to, device_id_type=MESH_ID)

        mine = [pltpu.make_async_copy(ins[t], slot(t, me), local_sems.at[t]) for t in range(n)]
        first = []
        for t in range(n):
            first.append(copy(t, 0, me, sibling, src=ins[t]))
            first += [copy(t, 1 + j, me, (*chip, c), src=ins[t]) for j, chip in enumerate(chips)]
        return me, sibling, chips, c, copy, mine, first

    def start(ins, outs, sems):
        *_, mine, first = plan(ins, outs, sems)
        for cp in mine + first:
            cp.start()

    def pass_on(ins, outs, sems):
        me, sibling, chips, c, copy, _, _ = plan(ins, outs, sems)
        passed = []
        for t in range(n):
            for j, chip in enumerate(chips):
                copy(t, 1 + j, (*chip, c), me).wait_recv()
                passed.append(copy(t, 4 + j, (*chip, c), sibling))
                passed[-1].start()
        return passed

    def finish(ins, outs, sems, passed=None):
        me, sibling, chips, c, copy, mine, first = plan(ins, outs, sems)
        if passed is None:
            passed = [copy(t, 4 + j, (*chip, c), sibling) for t in range(n) for j, chip in enumerate(chips)]
        for t in range(n):
            copy(t, 0, sibling, me).wait_recv()
            for j, chip in enumerate(chips):
                copy(t, 4 + j, (*chip, 1 - c), me).wait_recv()
        for cp in first + passed:
            cp.wait_send()
        for cp in mine:
            cp.wait()

    out_shape = [_sds((s.shape[0], N_DEV * s.shape[1]) if t in by_columns else (N_DEV,) + s.shape, s.dtype)
                 for t, s in enumerate(shards)]
    if early_pass_on:
        return _Comm(list(shards), out_shape, _sem_scratch(n), start, finish, middle=pass_on)
    return _Comm(list(shards), out_shape, _sem_scratch(n), start,
                 lambda ins, outs, sems: finish(ins, outs, sems, pass_on(ins, outs, sems)))


def _peer(d):
    x, y, c = lax.axis_index("x"), lax.axis_index("y"), lax.axis_index("c")
    px = 1 - x if d & 4 else x
    py = 1 - y if d & 2 else y
    pc = 1 - c if d & 1 else c
    return (px, py, pc), 4 * px + 2 * py + pc


def _exchange_comm(parts, deltas=tuple(range(1, N_DEV))):
    n = len(parts)

    def plan(ins, outs, sems):
        send_sems, recv_sems, local_sems = sems
        _, me = _peer(0)

        def copy(t, i):
            peer, idx = _peer(deltas[i])
            return pltpu.make_async_remote_copy(
                src_ref=ins[t].at[idx], dst_ref=outs[t].at[1 + i],
                send_sem=send_sems.at[t, i], recv_sem=recv_sems.at[t, i],
                device_id=peer, device_id_type=MESH_ID)

        mine = [pltpu.make_async_copy(ins[t].at[me], outs[t].at[0], local_sems.at[t]) for t in range(n)]
        return mine, [copy(t, i) for t in range(n) for i in range(len(deltas))]

    def start(ins, outs, sems):
        mine, copies = plan(ins, outs, sems)
        for cp in mine + copies:
            cp.start()

    def finish(ins, outs, sems):
        mine, copies = plan(ins, outs, sems)
        for cp in copies:
            cp.wait_recv()
        for cp in copies:
            cp.wait_send()
        for cp in mine:
            cp.wait()

    return _Comm(list(parts), [_sds((1 + len(deltas),) + p.shape[1:], p.dtype) for p in parts],
                 _sem_scratch(n, len(deltas)), start, finish)


def _chip_exchange_comm(parts):
    n = len(parts)

    def plan(ins, outs, sems):
        send_sems, recv_sems, local_sems = sems
        x, y, c = lax.axis_index("x"), lax.axis_index("y"), lax.axis_index("c")

        def copy(t, dq):
            px = 1 - x if dq & 2 else x
            py = 1 - y if dq & 1 else y
            return pltpu.make_async_remote_copy(
                src_ref=ins[t].at[2 * px + py], dst_ref=outs[t].at[dq],
                send_sem=send_sems.at[t, dq - 1], recv_sem=recv_sems.at[t, dq - 1],
                device_id=(px, py, c), device_id_type=MESH_ID)

        mine = [pltpu.make_async_copy(ins[t].at[2 * x + y], outs[t].at[0], local_sems.at[t]) for t in range(n)]
        return mine, [copy(t, dq) for t in range(n) for dq in range(1, 4)]

    def start(ins, outs, sems):
        mine, copies = plan(ins, outs, sems)
        for cp in mine + copies:
            cp.start()

    def finish(ins, outs, sems):
        mine, copies = plan(ins, outs, sems)
        for cp in copies:
            cp.wait_recv()
        for cp in copies:
            cp.wait_send()
        for cp in mine:
            cp.wait()

    return _Comm(list(parts), [_sds(p.shape, p.dtype) for p in parts], _sem_scratch(n, 3), start, finish)


def _pair_reduce(part, name):
    by_columns = part.ndim == 2
    r, c = (part.shape[0], part.shape[1] // N_DEV) if by_columns else part.shape[1:]
    n_chips = N_DEV // 2
    tr = 128

    def body(p_ref, o_ref, own, land, send_sems, recv_sems, local_sems):
        x, y, core = lax.axis_index("x"), lax.axis_index("y"), lax.axis_index("c")

        def block(k):
            return p_ref.at[:, pl.ds(pl.multiple_of(k * c, 128), c)] if by_columns else p_ref.at[k]

        sends = [pltpu.make_async_remote_copy(
            src_ref=block(2 * q + (1 - core)), dst_ref=land.at[q], send_sem=send_sems.at[q], recv_sem=recv_sems.at[q],
            device_id=(x, y, 1 - core), device_id_type=MESH_ID) for q in range(n_chips)]
        loads = [pltpu.make_async_copy(block(2 * q + core), own.at[q], local_sems.at[q]) for q in range(n_chips)]
        for cp in sends + loads:
            cp.start()
        for q in range(n_chips):
            loads[q].wait()
            sends[q].wait_recv()

            def add_rows(i, _):
                rows = pl.ds(pl.multiple_of(i * tr, tr), tr)
                o_ref[q, rows, :] = (own[q, rows, :].astype(F32) + land[q, rows, :].astype(F32)).astype(BF16)
                return 0

            lax.fori_loop(0, r // tr, add_rows, 0)
        for cp in sends:
            cp.wait_send()

    return pl.pallas_call(
        body, name=name, out_shape=_sds((n_chips, r, c), BF16),
        in_specs=[pl.BlockSpec(memory_space=pl.ANY)], out_specs=pl.BlockSpec(memory_space=pltpu.VMEM),
        scratch_shapes=[pltpu.VMEM((n_chips, r, c), BF16), pltpu.VMEM((n_chips, r, c), BF16)]
        + [pltpu.SemaphoreType.DMA((n_chips,))] * 3,
        compiler_params=pltpu.CompilerParams(vmem_limit_bytes=VMEM_LIMIT),
    )(part)


def _comm_only(comm, name):
    n = len(comm.inputs)

    def body(*refs):
        ins, outs, sems = refs[:n], refs[n:2 * n], refs[2 * n:]
        comm.start(ins, outs, sems)
        comm.finish(ins, outs, sems)

    any_spec = pl.BlockSpec(memory_space=pl.ANY)
    return pl.pallas_call(body, name=name, out_shape=comm.out_shape, in_specs=[any_spec] * n, out_specs=[any_spec] * n,
                          scratch_shapes=comm.scratch)(*comm.inputs)


def _host_call(body, *, name, grid, in_specs, out_specs, out_shape, scratch_shapes=(), semantics, args, comm=None):
    if comm is None:
        res = pl.pallas_call(body, name=name, grid=grid, in_specs=in_specs, out_specs=out_specs, out_shape=out_shape,
                             scratch_shapes=list(scratch_shapes), compiler_params=_params(*semantics))(*args)
        return res, []
    n_in, n_out, n_scr, n_c = len(in_specs), len(out_specs), len(scratch_shapes), len(comm.inputs)

    def with_comm(*refs):
        ins, refs = refs[:n_in], refs[n_in:]
        c_ins, refs = refs[:n_c], refs[n_c:]
        outs, refs = refs[:n_out], refs[n_out:]
        c_outs, refs = refs[:n_c], refs[n_c:]
        scr, sems = refs[:n_scr], refs[n_scr:]
        ids = [pl.program_id(a) for a in range(len(grid))]
        first = functools.reduce(jnp.logical_and, [i == 0 for i in ids])
        last = functools.reduce(jnp.logical_and, [i == g - 1 for i, g in zip(ids, grid)])

        @pl.when(first)
        def _():
            comm.start(c_ins, c_outs, sems)

        body(*ins, *outs, *scr)

        if comm.middle is not None:
            assert len(grid) == 1

            @pl.when(ids[0] == (3 * grid[0]) // 4)
            def _():
                comm.middle(c_ins, c_outs, sems)

        @pl.when(last)
        def _():
            comm.finish(c_ins, c_outs, sems)

    any_spec = pl.BlockSpec(memory_space=pl.ANY)
    res = pl.pallas_call(
        with_comm, name=name, grid=grid, in_specs=list(in_specs) + [any_spec] * n_c,
        out_specs=list(out_specs) + [any_spec] * n_c, out_shape=list(out_shape) + comm.out_shape,
        scratch_shapes=list(scratch_shapes) + comm.scratch,
        compiler_params=_params(*(["arbitrary"] * len(grid))))(*args, *comm.inputs)
    return res[:n_out], res[n_out:]


def _small_all_reduce(packs):
    n = len(packs)
    pieces = [p.shape[0] // N_DEV for p in packs]

    def body(*refs):
        p_refs, o_refs, lands = refs[:n], refs[n:2 * n], refs[2 * n:3 * n]
        s1, r1, s2, r2 = refs[3 * n:]
        _, me = _peer(0)

        def piece(t, ref, idx):
            return ref.at[pl.ds(pl.multiple_of(idx * pieces[t], 8), pieces[t]), :]

        def scatter(t, d):
            peer, idx = _peer(d)
            return pltpu.make_async_remote_copy(
                src_ref=piece(t, p_refs[t], idx), dst_ref=lands[t].at[d - 1], send_sem=s1.at[t, d - 1],
                recv_sem=r1.at[t, d - 1], device_id=peer, device_id_type=MESH_ID)

        def gather(t, d, from_idx):
            peer, _ = _peer(d)
            return pltpu.make_async_remote_copy(
                src_ref=piece(t, o_refs[t], from_idx), dst_ref=piece(t, o_refs[t], from_idx), send_sem=s2.at[t, d - 1],
                recv_sem=r2.at[t, d - 1], device_id=peer, device_id_type=MESH_ID)

        first = [[scatter(t, d) for d in range(1, N_DEV)] for t in range(n)]
        for cp in sum(first, []):
            cp.start()
        second = []
        for t in range(n):
            acc = piece(t, p_refs[t], me)[...]
            for d in range(1, N_DEV):
                first[t][d - 1].wait_recv()
                acc = acc + lands[t][d - 1]
            piece(t, o_refs[t], me)[...] = acc
            second += [gather(t, d, me) for d in range(1, N_DEV)]
            for cp in second[-(N_DEV - 1):]:
                cp.start()
        for t in range(n):
            for d in range(1, N_DEV):
                gather(t, d, _peer(d)[1]).wait_recv()
        for cp in sum(first, []) + second:
            cp.wait_send()

    vmem = pl.BlockSpec(memory_space=pltpu.VMEM)
    return pl.pallas_call(
        body, name="small_all_reduce",
        out_shape=[_sds(p.shape, F32) for p in packs], in_specs=[vmem] * n, out_specs=[vmem] * n,
        scratch_shapes=[pltpu.VMEM((N_DEV - 1, pc, p.shape[1]), F32) for pc, p in zip(pieces, packs)]
        + [pltpu.SemaphoreType.DMA((n, N_DEV - 1))] * 4,
    )(*packs)


def _norm_in_proj(x, g_mix, w_in, tm, comm):
    s = x.shape[0]

    def body(x_ref, g_ref, w_ref, z_ref, h_ref):
        xn, _ = _rms(x_ref[...])
        h = (xn * g_ref[...]).astype(BF16)
        h_ref[...] = h
        for j in range(N_SEG):
            cols = slice(j * D_MODEL, (j + 1) * D_MODEL)
            z_ref[:, cols] = _dot(h, w_ref[:, cols]).astype(BF16)

    return _host_call(
        body, name="norm_in_proj", grid=(s // tm,),
        in_specs=[pl.BlockSpec((tm, D_MODEL), lambda i: (i, 0)), pl.BlockSpec((1, D_MODEL), lambda i: (0, 0)),
                  _resident(w_in.shape)],
        out_specs=[pl.BlockSpec((tm, N_SEG * D_MODEL), lambda i: (i, 0)), pl.BlockSpec((tm, D_MODEL), lambda i: (i, 0))],
        out_shape=[_sds((s, N_SEG * D_MODEL), BF16), _sds((s, D_MODEL), BF16)],
        semantics=("parallel",), args=(x, g_mix, w_in), comm=comm)


def _gate_chunk(xr_ext, w4_ref, cb_ref, wa, wx, ba_ref, bx_ref, sp, ch):
    xc = cb_ref[...] + w4_ref[3:4, :] * _shift_dn(xr_ext, 0, ch)
    for k in range(3):
        xc = xc + w4_ref[k:k + 1, :] * _shift_dn(xr_ext, 3 - k, ch)
    xcb = xc.astype(BF16)
    ra = _sigmoid(_dot(xcb, wa) + ba_ref[...])
    ia = _sigmoid(_dot(xcb, wx) + bx_ref[...])
    log_a = (-LRU_C) * ra * sp
    return xc, ra, ia, log_a


def _conv3(q_ext, w3_ref, ch):
    y = w3_ref[2:3, :] * _shift_dn(q_ext, 0, ch)
    for k in range(2):
        y = y + w3_ref[k:k + 1, :] * _shift_dn(q_ext, 2 - k, ch)
    return y


def _z_block_specs(s, cw, segs):
    nb = D_MODEL // cw
    return [pl.BlockSpec((s, cw), functools.partial(lambda j, seg: (0, seg * nb + j), seg=seg)) for seg in segs]


def _mixers_fwd(z, w4, b4, w_a, b_a, w_x, b_x, lam, w3, ch, comm):
    s = z.shape[0]
    cw = RNN_BW
    n_chunks = s // ch

    def body(xr_ref, gr_ref, cb_ref, cc_ref, cx_ref, w4_ref, b4_ref, wa_ref, ba_ref, wx_ref, bx_ref, lam_ref, w3_ref,
             ya_ref, yb_ref, h_ref, a_scr, b_scr):
        sp = _softplus(-lam_ref[...])
        wa = wa_ref[0].astype(BF16)
        wx = wx_ref[0].astype(BF16)

        def gates(c, _):
            xc, _, ia, log_a = _gate_chunk(_ext_before(xr_ref, c, ch), w4_ref, b4_ref, wa, wx, ba_ref, bx_ref, sp, ch)
            rows = pl.ds(pl.multiple_of(c * ch, HALO), ch)
            a = jnp.exp(log_a)
            a_scr[rows, :] = a
            b_scr[rows, :] = jnp.sqrt(_one_minus_sq(log_a, a)) * (ia * xc)
            q_ext = _ext_before(cc_ref, c, ch) * _ext_before(cx_ref, c, ch)
            yb_ref[rows, :] = (_rows(cb_ref, c, ch).astype(F32) * _conv3(q_ext, w3_ref, ch)).astype(BF16)
            return 0

        lax.fori_loop(0, n_chunks, gates, 0)
        _scan_fwd(a_scr, b_scr, h_ref, s, cw)

        def outputs(c, _):
            rows = pl.ds(pl.multiple_of(c * ch, HALO), ch)
            ya_ref[rows, :] = (h_ref[rows, :] * _gelu(gr_ref[rows, :].astype(F32))).astype(BF16)
            return 0

        lax.fori_loop(0, n_chunks, outputs, 0)

    col = lambda j: (0, j)
    vec = pl.BlockSpec((1, cw), col)
    sq = pl.BlockSpec((1, cw, cw), lambda j: (j, 0, 0))
    act = pl.BlockSpec((s, cw), col)
    return _host_call(
        body, name="mixers_fwd", grid=(D_MODEL // cw,),
        in_specs=_z_block_specs(s, cw, range(5)) + [pl.BlockSpec((4, cw), col), vec, sq, vec, sq, vec, vec,
                                                     pl.BlockSpec((3, cw), col)],
        out_specs=[act, act, act],
        out_shape=[_sds((s, D_MODEL), BF16), _sds((s, D_MODEL), BF16), _sds((s, D_MODEL), F32)],
        scratch_shapes=[pltpu.VMEM((s, cw), F32), pltpu.VMEM((s, cw), F32)],
        semantics=("parallel",), args=(z, z, z, z, z, w4, b4, w_a, b_a, w_x, b_x, lam, w3), comm=comm)


def _merge_out(ya, yb, z, x, w_pa, w_pb, w_out, g_ffn, tm):
    s = x.shape[0]

    def body(ya_ref, yb_ref, ga_ref, gb_ref, x_ref, wa_ref, wb_ref, wo_ref, g_ref, pa_ref, pb_ref, m_ref, x2_ref, h2_ref):
        pa = _dot(ya_ref[...], wa_ref[...])
        pb = _dot(yb_ref[...], wb_ref[...])
        pa_ref[...] = pa.astype(BF16)
        pb_ref[...] = pb.astype(BF16)
        m = (_sigmoid(ga_ref[...].astype(F32)) * pa + _sigmoid(gb_ref[...].astype(F32)) * pb).astype(BF16)
        m_ref[...] = m
        x2 = x_ref[...] + _dot(m, wo_ref[...])
        x2_ref[...] = x2
        xn, _ = _rms(x2)
        h2_ref[...] = (xn * g_ref[...]).astype(BF16)

    tile = pl.BlockSpec((tm, D_MODEL), lambda i: (i, 0))
    full = _resident((D_MODEL, D_MODEL))
    return pl.pallas_call(
        body, name="merge_out", grid=(s // tm,),
        in_specs=[tile, tile, pl.BlockSpec((tm, D_MODEL), lambda i: (i, 5)), pl.BlockSpec((tm, D_MODEL), lambda i: (i, 6)),
                  tile, full, full, full, pl.BlockSpec((1, D_MODEL), lambda i: (0, 0))],
        out_specs=[tile] * 5,
        out_shape=[_sds((s, D_MODEL), BF16)] * 3 + [_sds((s, D_MODEL), F32), _sds((s, D_MODEL), BF16)],
        compiler_params=_params("parallel"),
    )(ya, yb, z, z, x, w_pa, w_pb, w_out, g_ffn)


def _up_proj_act(h2, w_up_g, wc, bc, tm, comm):
    s = h2.shape[0]
    bw = w_up_g.shape[2]
    per_half = N_DEV // 2

    def body(h_ref, w_ref, wc_ref, bc_ref, u_ref, f_ref, halo):
        @pl.when(pl.program_id(0) == 0)
        def _():
            halo[...] = jnp.zeros_like(halo)

        h = h_ref[...]
        for j in range(per_half):
            cols = slice(j * bw, (j + 1) * bw)
            conv = []
            for half in range(2):
                u = _dot(h, w_ref[half * per_half + j]).astype(BF16)
                u_ref[half, :, cols] = u
                u = u.astype(F32)
                ext = jnp.concatenate([halo[half, :, cols], u], axis=0)
                halo[half, :, cols] = u[tm - HALO:tm]
                w = wc_ref.at[half]
                acc = bc_ref[half, :, cols] + w[2:3, cols] * u
                for k in range(2):
                    acc = acc + w[k:k + 1, cols] * _shift_dn(ext, 2 - k, tm)
                conv.append(acc)
            f_ref[:, cols] = (_gelu(conv[0]) * conv[1]).astype(BF16)

    return _host_call(
        body, name="up_proj_act", grid=(s // tm,),
        in_specs=[pl.BlockSpec((tm, D_MODEL), lambda i: (i, 0)), _resident(w_up_g.shape), _resident(wc.shape),
                  _resident(bc.shape)],
        out_specs=[pl.BlockSpec((2, tm, D_FF), lambda i: (0, i, 0)), pl.BlockSpec((tm, D_FF), lambda i: (i, 0))],
        out_shape=[_sds((2, s, D_FF), BF16), _sds((s, D_FF), BF16)],
        scratch_shapes=[pltpu.VMEM((2, HALO, D_FF), F32)],
        semantics=("arbitrary",), args=(h2, w_up_g, wc, bc), comm=comm)


def _head(x2, f, u0, wc, bc, p, target, w_down, w_gate, w_pp, g_ple, g_final, tm):
    s = x2.shape[0]
    bw = 256

    def body(x2_ref, f_ref, u_ref, uh_ref, wc_ref, bc_ref, p_ref, t_ref, wd_ref, wg_ref, wp_ref, gp_ref, gf_ref,
             x3_ref, dgt_ref, de0_ref, dx3_ref, du_ref, sums_ref, dwc_ref, dbc_ref):
        i = pl.program_id(0)

        @pl.when(i == 0)
        def _():
            sums_ref[...] = jnp.zeros_like(sums_ref)
            dwc_ref[...] = jnp.zeros_like(dwc_ref)
            dbc_ref[...] = jnp.zeros_like(dbc_ref)

        x3 = x2_ref[...] + _dot(f_ref[...], wd_ref[...])
        x3b = x3.astype(BF16)
        x3_ref[...] = x3b
        e0n, re = _rms(_dot(p_ref[...].astype(BF16), wp_ref[...]))
        e = e0n * gp_ref[...]
        sg = _sigmoid(_dot(x3b, wg_ref[...]))
        x4n, r4 = _rms(x3 + sg * e)
        diff = x4n * gf_ref[...] - t_ref[...]
        sums_ref[0:1, :] += jnp.sum(diff * diff, axis=0, keepdims=True)
        dy = diff * (1.0 / D_MODEL)
        sums_ref[1:2, :] += jnp.sum(dy * x4n, axis=0, keepdims=True)
        dx4 = _rms_bwd(dy * gf_ref[...], x4n, r4)
        de = dx4 * sg
        dgt = ((dx4 * e) * (sg * (1.0 - sg))).astype(BF16)
        dgt_ref[...] = dgt
        sums_ref[2:3, :] += jnp.sum(de * e0n, axis=0, keepdims=True)
        de0_ref[...] = _rms_bwd(de * gp_ref[...], e0n, re).astype(BF16)
        dx3 = dx4 + _dot_nt(dgt, wg_ref[...])
        dx3_ref[...] = dx3
        dx3b = dx3.astype(BF16)
        for j in range(D_FF // bw):
            cols = slice(j * bw, (j + 1) * bw)
            df = _dot_nt(dx3b, wd_ref[cols, :])
            conv, exts = [], []
            for half in range(2):
                halo = jnp.where(i > 0, uh_ref[half, :, cols].astype(F32), 0.0)
                ext = jnp.concatenate([halo, u_ref[half, :, cols].astype(F32)], axis=0)
                w = wc_ref.at[half]
                acc = bc_ref[half, :, cols] + w[2:3, cols] * _shift_dn(ext, 0, tm)
                for k in range(2):
                    acc = acc + w[k:k + 1, cols] * _shift_dn(ext, 2 - k, tm)
                conv.append(acc)
                exts.append(ext)
            gel, dgel = _gelu_and_grad(conv[0])
            for half, du in ((0, df * conv[1] * dgel), (1, df * gel)):
                du_ref[half, :, cols] = du.astype(BF16)
                dbc_ref[half, :, cols] += jnp.sum(du, axis=0, keepdims=True)
                for k in range(3):
                    dwc_ref[half, k:k + 1, cols] += jnp.sum(du * _shift_dn(exts[half], 2 - k, tm), axis=0, keepdims=True)

    tile = pl.BlockSpec((tm, D_MODEL), lambda i: (i, 0))
    vec = pl.BlockSpec((1, D_MODEL), lambda i: (0, 0))
    halves = pl.BlockSpec((2, tm, D_FF), lambda i: (0, i, 0))
    before = pl.BlockSpec((2, HALO, D_FF), lambda i: (0, jnp.maximum(i * (tm // HALO) - 1, 0), 0))
    return pl.pallas_call(
        body, name="loss_head", grid=(s // tm,),
        in_specs=[tile, pl.BlockSpec((tm, D_FF), lambda i: (i, 0)), halves, before, _resident(wc.shape),
                  _resident(bc.shape), pl.BlockSpec((tm, D_PLE), lambda i: (i, 0)), tile,
                  _resident((D_FF, D_MODEL)), _resident((D_MODEL, D_MODEL)), _resident((D_PLE, D_MODEL)), vec, vec],
        out_specs=[tile, tile, tile, tile, halves, pl.BlockSpec((8, D_MODEL), lambda i: (0, 0)),
                   pl.BlockSpec(wc.shape, lambda i: (0, 0, 0)), pl.BlockSpec(bc.shape, lambda i: (0, 0, 0))],
        out_shape=[_sds((s, D_MODEL), BF16)] * 3 + [_sds((s, D_MODEL), F32), _sds((2, s, D_FF), BF16),
                                                    _sds((8, D_MODEL), F32), _sds(wc.shape, F32), _sds(bc.shape, F32)],
        compiler_params=_params("arbitrary"),
    )(x2, f, u0, u0, wc, bc, p, target, w_down, w_gate, w_pp, g_ple, g_final)


def _up_bwd_merge_bwd(du, wc, w_up_g, x2, dx3, z, pa, pb, w_out, w_pa, w_pb, g_ffn, tm, comm):
    s = x2.shape[0]
    bw = w_up_g.shape[2]
    per_half = N_DEV // 2
    n_tiles = s // tm

    def body(du_ref, dua_ref, wc_ref, wu_ref, x2_ref, dx3_ref, ga_ref, gb_ref, pa_ref, pb_ref, wo_ref, wa_ref, wb_ref, g_ref,
             dx2_ref, dzg_ref, dpa_ref, dpb_ref, dya_ref, dyb_ref, du0_ref, sums_ref):
        i = pl.program_id(0)

        @pl.when(i == 0)
        def _():
            sums_ref[...] = jnp.zeros_like(sums_ref)

        dh2 = None
        for j in range(N_DEV):
            half = j // per_half
            cols = slice((j % per_half) * bw, (j % per_half + 1) * bw)
            halo = jnp.where(i < n_tiles - 1, dua_ref[half, :, cols].astype(F32), 0.0)
            ext = jnp.concatenate([du_ref[half, :, cols].astype(F32), halo], axis=0)
            w = wc_ref.at[half]
            acc = w[2:3, cols] * _shift_up(ext, 0, tm)
            for k in range(2):
                acc = acc + w[k:k + 1, cols] * _shift_up(ext, 2 - k, tm)
            du0 = acc.astype(BF16)
            du0_ref[half, :, cols] = du0
            term = _dot_nt(du0, wu_ref[j])
            dh2 = term if dh2 is None else dh2 + term
        x2n, r2 = _rms(x2_ref[...])
        sums_ref[0:1, :] += jnp.sum(dh2 * x2n, axis=0, keepdims=True)
        dx2 = dx3_ref[...] + _rms_bwd(dh2 * g_ref[...], x2n, r2)
        dx2_ref[...] = dx2
        dm = _dot_nt(dx2.astype(BF16), wo_ref[...])
        sa = _sigmoid(ga_ref[...].astype(F32))
        sb = _sigmoid(gb_ref[...].astype(F32))
        dzg_ref[0] = (dm * pa_ref[...].astype(F32) * (sa * (1.0 - sa))).astype(BF16)
        dzg_ref[1] = (dm * pb_ref[...].astype(F32) * (sb * (1.0 - sb))).astype(BF16)
        dpa = (dm * sa).astype(BF16)
        dpb = (dm * sb).astype(BF16)
        dpa_ref[...] = dpa
        dpb_ref[...] = dpb
        dya_ref[...] = _dot_nt(dpa, wa_ref[...]).astype(BF16)
        dyb_ref[...] = _dot_nt(dpb, wb_ref[...]).astype(BF16)

    tile = pl.BlockSpec((tm, D_MODEL), lambda i: (i, 0))
    full = _resident((D_MODEL, D_MODEL))
    halves = pl.BlockSpec((2, tm, D_FF), lambda i: (0, i, 0))
    after = pl.BlockSpec((2, HALO, D_FF), lambda i: (0, jnp.minimum((i + 1) * (tm // HALO), s // HALO - 1), 0))
    return _host_call(
        body, name="up_bwd_merge_bwd", grid=(n_tiles,),
        in_specs=[halves, after, _resident(wc.shape), _resident(w_up_g.shape),
                  tile, tile, pl.BlockSpec((tm, D_MODEL), lambda i: (i, 5)),
                  pl.BlockSpec((tm, D_MODEL), lambda i: (i, 6)), tile, tile, full, full, full,
                  pl.BlockSpec((1, D_MODEL), lambda i: (0, 0))],
        out_specs=[tile, pl.BlockSpec((2, tm, D_MODEL), lambda i: (0, i, 0)), tile, tile, tile, tile, halves,
                   pl.BlockSpec((8, D_MODEL), lambda i: (0, 0))],
        out_shape=[_sds((s, D_MODEL), F32), _sds((2, s, D_MODEL), BF16)] + [_sds((s, D_MODEL), BF16)] * 4
        + [_sds((2, s, D_FF), BF16), _sds((8, D_MODEL), F32)],
        semantics=("arbitrary",), args=(du, du, wc, w_up_g, x2, dx3, z, z, pa, pb, w_out, w_pa, w_pb, g_ffn),
        comm=comm)


def _mixers_bwd(z, h, dya, dyb, w4, b4, w_a, b_a, w_x, b_x, lam, w3, ch, comm):
    s = z.shape[0]
    cw = RNN_BW
    ch1, ch2, ch3 = ch

    def body(xr_ref, gr_ref, cb_ref, cc_ref, cx_ref, h_ref, dya_ref, dyb_ref, w4_ref, b4_ref, wa_ref, ba_ref, wx_ref,
             bx_ref, lam_ref, w3_ref,
             dz_ref, dw4_ref, db4_ref, dwa_ref, dba_ref, dwx_ref, dbx_ref, dlam_ref, dw3_ref,
             a_scr, g_scr, q_scr, xc_scr, ra_scr, ia_scr, dh_scr):
        sp = _softplus(-lam_ref[...])
        wa = wa_ref[0].astype(BF16)
        wx = wx_ref[0].astype(BF16)
        for ref in (dw4_ref, db4_ref, dwa_ref, dba_ref, dwx_ref, dbx_ref, dlam_ref, dw3_ref):
            ref[...] = jnp.zeros_like(ref)

        def first(c, _, ch=ch1):
            rows = pl.ds(pl.multiple_of(c * ch, HALO), ch)
            xc, ra, ia, log_a = _gate_chunk(_ext_before(xr_ref, c, ch), w4_ref, b4_ref, wa, wx, ba_ref, bx_ref, sp, ch)
            xc_scr[rows, :] = xc
            ra_scr[rows, :] = ra
            ia_scr[rows, :] = ia
            a_scr[rows, :] = jnp.exp(log_a)
            gel, dgel = _gelu_and_grad(gr_ref[rows, :].astype(F32))
            dya_c = dya_ref[rows, :].astype(F32)
            g_scr[rows, :] = dya_c * gel
            dz_ref[1, rows, :] = (dya_c * h_ref[rows, :] * dgel).astype(BF16)
            q_ext = _ext_before(cc_ref, c, ch) * _ext_before(cx_ref, c, ch)
            dyb_c = dyb_ref[rows, :].astype(F32)
            dz_ref[2, rows, :] = (dyb_c * _conv3(q_ext, w3_ref, ch)).astype(BF16)
            dyq = dyb_c * cb_ref[rows, :].astype(F32)
            q_scr[rows, :] = dyq
            for k in range(3):
                dw3_ref[k:k + 1, :] += jnp.sum(dyq * _shift_dn(q_ext, 2 - k, ch), axis=0, keepdims=True)
            return 0

        lax.fori_loop(0, s // ch1, first, 0)
        _scan_rev(a_scr, g_scr, dh_scr, s, cw)

        def second(c, _, ch=ch2):
            rows = pl.ds(pl.multiple_of(c * ch, HALO), ch)
            xc, ra, ia, a = xc_scr[rows, :], ra_scr[rows, :], ia_scr[rows, :], a_scr[rows, :]
            m2 = _one_minus_sq((-LRU_C) * ra * sp, a)
            inv_mult = lax.rsqrt(m2)
            mult = m2 * inv_mult
            dh = dh_scr[rows, :]
            h_prev = _shift_dn(_ext_before(h_ref, c, ch), 1, ch)
            dlog_a = dh * h_prev * a - (dh * ia * xc) * (a * a) * inv_mult
            dlam_ref[...] += jnp.sum(dlog_a * ra, axis=0, keepdims=True)
            dpre_a = (dlog_a * ((-LRU_C) * sp)) * (ra * (1.0 - ra))
            dpre_x = (dh * mult * xc) * (ia * (1.0 - ia))
            dba_ref[...] += jnp.sum(dpre_a, axis=0, keepdims=True)
            dbx_ref[...] += jnp.sum(dpre_x, axis=0, keepdims=True)
            xcb = xc.astype(BF16)
            dpa_b = dpre_a.astype(BF16)
            dpx_b = dpre_x.astype(BF16)
            dwa_ref[0] += _dot_tn(xcb, dpa_b)
            dwx_ref[0] += _dot_tn(xcb, dpx_b)
            a_scr[rows, :] = dh * mult * ia + _dot_nt(dpa_b, wa) + _dot_nt(dpx_b, wx)
            return 0

        lax.fori_loop(0, s // ch2, second, 0)
        dlam_ref[...] = dlam_ref[...] * (LRU_C * _sigmoid(-lam_ref[...]))

        def third(c, _, ch=ch3, n_chunks=s // ch3):
            rows = pl.ds(pl.multiple_of(c * ch, HALO), ch)
            dxc_ext = _ext_after(a_scr, c, ch, n_chunks)
            dxc = _shift_up(dxc_ext, 0, ch)
            xr_ext = _ext_before(xr_ref, c, ch)
            db4_ref[...] += jnp.sum(dxc, axis=0, keepdims=True)
            dxr = w4_ref[3:4, :] * dxc
            dw4_ref[3:4, :] += jnp.sum(dxc * _shift_dn(xr_ext, 0, ch), axis=0, keepdims=True)
            for k in range(3):
                dxr = dxr + w4_ref[k:k + 1, :] * _shift_up(dxc_ext, 3 - k, ch)
                dw4_ref[k:k + 1, :] += jnp.sum(dxc * _shift_dn(xr_ext, 3 - k, ch), axis=0, keepdims=True)
            dz_ref[0, rows, :] = dxr.astype(BF16)
            dyq_ext = _ext_after(q_scr, c, ch, n_chunks)
            dq = w3_ref[2:3, :] * _shift_up(dyq_ext, 0, ch)
            for k in range(2):
                dq = dq + w3_ref[k:k + 1, :] * _shift_up(dyq_ext, 2 - k, ch)
            dz_ref[3, rows, :] = (dq * cx_ref[rows, :].astype(F32)).astype(BF16)
            dz_ref[4, rows, :] = (dq * cc_ref[rows, :].astype(F32)).astype(BF16)
            return 0

        lax.fori_loop(0, s // ch3, third, 0)

    col = lambda j: (0, j)
    vec = pl.BlockSpec((1, cw), col)
    sq = pl.BlockSpec((1, cw, cw), lambda j: (j, 0, 0))
    act = pl.BlockSpec((s, cw), col)
    w4s, w3s = pl.BlockSpec((4, cw), col), pl.BlockSpec((3, cw), col)
    vec_shape = _sds((1, D_MODEL), F32)
    sq_shape = _sds((D_MODEL // cw, cw, cw), F32)
    return _host_call(
        body, name="mixers_bwd", grid=(D_MODEL // cw,),
        in_specs=_z_block_specs(s, cw, range(5)) + [act, act, act, w4s, vec, sq, vec, sq, vec, vec, w3s],
        out_specs=[pl.BlockSpec((5, s, cw), lambda j: (0, 0, j)), w4s, vec, sq, vec, sq, vec, vec, w3s],
        out_shape=[_sds((5, s, D_MODEL), BF16), _sds((4, D_MODEL), F32), vec_shape, sq_shape, vec_shape, sq_shape,
                   vec_shape, vec_shape, _sds((3, D_MODEL), F32)],
        scratch_shapes=[pltpu.VMEM((s, cw), F32)] * 7,
        semantics=("parallel",), args=(z, z, z, z, z, h, dya, dyb, w4, b4, w_a, b_a, w_x, b_x, lam, w3), comm=comm)


def _in_proj_bwd(dz5, dzg, w_in, x, dx2, g_mix, tm, comm):
    s = x.shape[0]

    def body(d5_ref, dg_ref, w_ref, x_ref, dx2_ref, g_ref, dx_ref, sums_ref):
        @pl.when(pl.program_id(0) == 0)
        def _():
            sums_ref[...] = jnp.zeros_like(sums_ref)

        dh1 = None
        for k in range(N_SEG):
            d = d5_ref[k] if k < 5 else dg_ref[k - 5]
            term = _dot_nt(d, w_ref[:, k * D_MODEL:(k + 1) * D_MODEL])
            dh1 = term if dh1 is None else dh1 + term
        xn, r1 = _rms(x_ref[...])
        sums_ref[0:1, :] += jnp.sum(dh1 * xn, axis=0, keepdims=True)
        dx_ref[...] = dx2_ref[...] + _rms_bwd(dh1 * g_ref[...], xn, r1)

    tile = pl.BlockSpec((tm, D_MODEL), lambda i: (i, 0))
    return _host_call(
        body, name="in_proj_bwd", grid=(s // tm,),
        in_specs=[pl.BlockSpec((5, tm, D_MODEL), lambda i: (0, i, 0)), pl.BlockSpec((2, tm, D_MODEL), lambda i: (0, i, 0)),
                  _resident(w_in.shape), tile, tile, pl.BlockSpec((1, D_MODEL), lambda i: (0, 0))],
        out_specs=[tile, pl.BlockSpec((8, D_MODEL), lambda i: (0, 0))],
        out_shape=[_sds((s, D_MODEL), F32), _sds((8, D_MODEL), F32)],
        semantics=("arbitrary",), args=(dz5, dzg, w_in, x, dx2, g_mix), comm=comm)


def _weight_grad(a, b, b_spec, out_spec, out_shape, *, n_blocks, chunks, width, tm, tk, name):
    s, m = a.shape
    nk = s // tk
    b_chunked = len([d for d in b_spec.block_shape if d is not None]) == 3
    blocks_out = len([d for d in out_spec.block_shape if d is not None]) == 3

    def body(a_ref, b_ref, o_ref, acc):
        k = pl.program_id(2)

        @pl.when(k == 0)
        def _():
            acc[...] = jnp.zeros_like(acc)

        at = a_ref[...].astype(BF16).T
        for j in range(chunks):
            cols = slice(j * width, (j + 1) * width)
            acc[:, cols] += _dot(at, (b_ref[j] if b_chunked else b_ref[:, cols]).astype(BF16))

        @pl.when(k == nk - 1)
        def _():
            if blocks_out:
                for j in range(chunks):
                    o_ref[j] = acc[:, j * width:(j + 1) * width].astype(o_ref.dtype)
            else:
                o_ref[...] = acc[...].astype(o_ref.dtype)

    return pl.pallas_call(
        body, name=name, grid=(m // tm, n_blocks, nk),
        in_specs=[pl.BlockSpec((tk, tm), lambda i, j, k: (k, i)), b_spec], out_specs=out_spec, out_shape=out_shape,
        scratch_shapes=[pltpu.VMEM((tm, chunks * width), F32)],
        compiler_params=_params("parallel", "parallel", "arbitrary"),
    )(a, b)


def _wgrad_2d(a, b, name, tk):
    m, n = a.shape[1], b.shape[1]
    tm = m
    return _weight_grad(a, b, pl.BlockSpec((tk, n), lambda i, j, k: (k, 0)), pl.BlockSpec((tm, n), lambda i, j, k: (i, 0)),
                        _sds((m, n), BF16), n_blocks=1, chunks=1, width=n, tm=tm, tk=tk, name=name)


def _wgrad_multi(pairs, name, tk):
    n = len(pairs)
    s, m = pairs[0][0].shape
    nn = pairs[0][1].shape[1]
    nk = s // tk

    def body(*refs):
        a_refs, b_refs, o_refs, accs = refs[:n], refs[n:2 * n], refs[2 * n:3 * n], refs[3 * n:]
        k = pl.program_id(0)
        for a_ref, b_ref, o_ref, acc in zip(a_refs, b_refs, o_refs, accs):
            @pl.when(k == 0)
            def _():
                acc[...] = jnp.zeros_like(acc)

            acc[...] += _dot(a_ref[...].astype(BF16).T, b_ref[...].astype(BF16))

            @pl.when(k == nk - 1)
            def _():
                o_ref[...] = acc[...].astype(BF16)

    whole = pl.BlockSpec((m, nn), lambda k: (0, 0))
    return pl.pallas_call(
        body, name=name, grid=(nk,),
        in_specs=[pl.BlockSpec((tk, m), lambda k: (k, 0))] * n + [pl.BlockSpec((tk, nn), lambda k: (k, 0))] * n,
        out_specs=[whole] * n, out_shape=[_sds((m, nn), BF16)] * n,
        scratch_shapes=[pltpu.VMEM((m, nn), F32)] * n,
        compiler_params=_params("arbitrary"),
    )(*[a for a, _ in pairs], *[b for _, b in pairs])


def _wgrad_in(h1, dz5, dzg, tm, tk):
    s, m = h1.shape
    nk = s // tk
    g5, gg = dz5.shape[0], dzg.shape[0]

    def body(a_ref, b5_ref, bg_ref, o_ref, acc):
        k = pl.program_id(1)

        @pl.when(k == 0)
        def _():
            acc[...] = jnp.zeros_like(acc)

        at = a_ref[...].T
        for j in range(g5 + gg):
            b = b5_ref[j] if j < g5 else bg_ref[j - g5]
            acc[:, j * D_MODEL:(j + 1) * D_MODEL] += _dot(at, b)

        @pl.when(k == nk - 1)
        def _():
            o_ref[...] = acc[...].astype(BF16)

    return pl.pallas_call(
        body, name="wgrad_in", grid=(m // tm, nk),
        in_specs=[pl.BlockSpec((tk, tm), lambda i, k: (k, i)), pl.BlockSpec((g5, tk, D_MODEL), lambda i, k: (0, k, 0)),
                  pl.BlockSpec((gg, tk, D_MODEL), lambda i, k: (0, k, 0))],
        out_specs=pl.BlockSpec((tm, (g5 + gg) * D_MODEL), lambda i, k: (i, 0)),
        out_shape=_sds((m, (g5 + gg) * D_MODEL), BF16),
        scratch_shapes=[pltpu.VMEM((tm, (g5 + gg) * D_MODEL), F32)],
        compiler_params=_params("parallel", "arbitrary"),
    )(h1, dz5, dzg)


def _wgrad_up(h2, du0, bw, tk):
    per_half = N_DEV // 2
    return _weight_grad(h2, du0, pl.BlockSpec((None, tk, D_FF), lambda i, j, k: (j, k, 0)),
                        pl.BlockSpec((per_half, D_MODEL, bw), lambda i, j, k: (j, 0, 0)), _sds((N_DEV, D_MODEL, bw), BF16),
                        n_blocks=2, chunks=per_half, width=bw, tm=D_MODEL, tk=tk, name="wgrad_up")


def _adam_math(w, g, m, v):
    m = ADAM_B1 * m + (1.0 - ADAM_B1) * g
    v = ADAM_B2 * v + (1.0 - ADAM_B2) * jnp.square(g)
    m_hat = m / (1.0 - ADAM_B1 ** ADAM_STEP)
    v_hat = v / (1.0 - ADAM_B2 ** ADAM_STEP)
    delta = -ADAM_LR * (m_hat / (jnp.sqrt(v_hat) + ADAM_EPS) + ADAM_WD * w)
    return delta, m, v


def _adam_shard(parts, w, m, v, name):
    r, c = w.shape
    tr = min(r, 128)
    n_arrays = len(parts)

    def body(*refs):
        p_refs = refs[:n_arrays]
        w_ref, m_ref, v_ref, g_ref, d_ref, nm_ref, nv_ref = refs[n_arrays:]
        g = None
        for p_ref, (arr, first) in zip(p_refs, parts):
            for k in range(first, arr.shape[0]):
                term = p_ref[k].astype(F32)
                g = term if g is None else g + term
        g_ref[...] = g
        d_ref[...], nm_ref[...], nv_ref[...] = _adam_math(w_ref[...], g, m_ref[...], v_ref[...])

    tile = pl.BlockSpec((tr, c), lambda i: (i, 0))
    return pl.pallas_call(
        body, name=name, grid=(r // tr,),
        in_specs=[pl.BlockSpec((arr.shape[0], tr, c), lambda i: (0, i, 0)) for arr, _ in parts] + [tile, tile, tile],
        out_specs=[tile] * 4, out_shape=[_sds((r, c), F32)] * 4,
        compiler_params=_params("parallel"),
    )(*[arr for arr, _ in parts], w, m, v)


def _adam_small(grads, ws, ms, vs, name):
    n = len(ws)
    g_arrays, g_slot = [], []
    for g in grads:
        arr = g[0] if isinstance(g, tuple) else g
        if not any(arr is a for a in g_arrays):
            g_arrays.append(arr)
        g_slot.append([arr is a for a in g_arrays].index(True))
    n_g = len(g_arrays)

    def body(*refs):
        g_refs, refs = refs[:n_g], refs[n_g:]
        w_refs, m_refs, v_refs, outs = refs[:n], refs[n:2 * n], refs[2 * n:3 * n], refs[3 * n:]
        for i in range(n):
            og, od, om, ov = outs[4 * i:4 * i + 4]
            g_ref = g_refs[g_slot[i]]
            if isinstance(grads[i], tuple):
                row = grads[i][1]
                for j in range(ws[i].shape[1] // D_MODEL):
                    cols = slice(j * D_MODEL, (j + 1) * D_MODEL)
                    g = g_ref[row + j:row + j + 1, :]
                    og[:, cols] = g
                    od[:, cols], om[:, cols], ov[:, cols] = _adam_math(w_refs[i][:, cols], g, m_refs[i][:, cols],
                                                                       v_refs[i][:, cols])
            else:
                g = g_ref[...]
                og[...] = g
                od[...], om[...], ov[...] = _adam_math(w_refs[i][...], g, m_refs[i][...], v_refs[i][...])

    vmem = pl.BlockSpec(memory_space=pltpu.VMEM)
    res = pl.pallas_call(
        body, name=name, in_specs=[vmem] * (n_g + 3 * n), out_specs=[vmem] * (4 * n),
        out_shape=[_sds(a.shape, F32) for a in ws for _ in range(4)],
    )(*g_arrays, *ws, *ms, *vs)
    return [res[4 * i:4 * i + 4] for i in range(n)]


def _pack_conv_shard(rnn, sc, ffn):
    top = jnp.concatenate([rnn[:3], sc, ffn], axis=1)
    row3 = jnp.concatenate([rnn[3:4], jnp.zeros((1, D_MODEL - RNN_BW), F32)], axis=1)
    return jnp.concatenate([top, row3, jnp.zeros((4, D_MODEL), F32)], axis=0)


_VECTORS = ("g_mix", "rnn_conv_b", "b_rg_a", "b_rg_x", "lru_lambda", "g_ffn", "g_ple", "g_final", "ffn_conv_b")
VEC_ROWS = 64
ROW_LOSS, ROW_RNN_CONV, ROW_SC_CONV, ROW_FFN_CONV = 14, 16, 20, 24


_SHARDED_BIG = ("w_in", "w_proj_a", "w_proj_b", "w_out", "w_up", "w_down", "w_ple_gate", "w_ple_proj")
_NAMES = ("g_mix", "w_in", "rnn_conv_w", "rnn_conv_b", "w_rg_a", "b_rg_a", "w_rg_x", "b_rg_x", "lru_lambda", "sc_conv_w",
          "w_proj_a", "w_proj_b", "w_out", "g_ffn", "w_up", "ffn_conv_w", "ffn_conv_b", "w_down", "w_ple_gate",
          "w_ple_proj", "g_ple", "g_final")


def _step(x, p, target, w, m, v):
    s = x.shape[0]
    tm = min(s, 256)
    tm_wide = min(s, 512)
    ch_mix = min(s, 512)
    ch_bwd = (min(s, 512), min(s, 1024), min(s, 256))
    my_index = 4 * lax.axis_index("x") + 2 * lax.axis_index("y") + lax.axis_index("c")

    big = {n: w[n][0] for n in _SHARDED_BIG}
    conv_shard = _pack_conv_shard(w["rnn_conv_w"][0], w["sc_conv_w"][0], w["ffn_conv_w"][0])
    shards = {n: big[n].astype(BF16) for n in _SHARDED_BIG}
    w_in, conv_all = _comm_only(_gather_comm([shards["w_in"], conv_shard], by_columns=(0,)), "in_proj_weight_gather")
    w4 =jnp.transpose(jnp.concatenate([conv_all[:, :3, :RNN_BW], conv_all[:, 3:4, :RNN_BW]], axis=1),
                       (1, 0, 2)).reshape(4, D_MODEL)
    w3 = jnp.transpose(conv_all[:, :3, RNN_BW:2 * RNN_BW], (1, 0, 2)).reshape(3, D_MODEL)
    wc = jnp.transpose(conv_all[:, :3, 2 * RNN_BW:], (1, 0, 2)).reshape(3, 2, D_FF).transpose(1, 0, 2)
    bc = w["ffn_conv_b"].reshape(2, 1, D_FF)
    b4, b_a, b_x, lam = w["rnn_conv_b"], w["b_rg_a"], w["b_rg_x"], w["lru_lambda"]
    w_a, w_x = w["w_rg_a"][0], w["w_rg_x"][0]
    g_final = w["g_final"].reshape(1, D_MODEL)

    tk = min(s, 512)
    received = {}

    comm = _gather_comm([shards[n] for n in ("w_proj_a", "w_proj_b", "w_out")], early_pass_on=True)
    (z, h1), (w_pa, w_pb, w_out) = _norm_in_proj(x, w["g_mix"], w_in, tm_wide, comm)
    w_pa, w_pb, w_out = (a.reshape(D_MODEL, D_MODEL) for a in (w_pa, w_pb, w_out))
    comm = _gather_comm([shards["w_up"]], early_pass_on=True)
    (ya, yb, h), (w_up_g,) = _mixers_fwd(z, w4, b4, w_a, b_a, w_x, b_x, lam, w3, ch_mix, comm)
    pa, pb, mm, x2, h2 = _merge_out(ya, yb, z, x, w_pa, w_pb, w_out, w["g_ffn"], tm)
    comm = _gather_comm([shards[n] for n in ("w_down", "w_ple_gate", "w_ple_proj")], by_columns=(2,), early_pass_on=True)
    (u0, f), (w_down, w_gate, w_pp) = _up_proj_act(h2, w_up_g, wc, bc, tm, comm)
    w_down = w_down.reshape(D_FF, D_MODEL)
    w_gate = w_gate.reshape(D_MODEL, D_MODEL)
    x3, dgt, de0, dx3, du, head_sums, dwc, dbc = _head(x2, f, u0, wc, bc, p, target, w_down, w_gate, w_pp, w["g_ple"],
                                                       g_final, tm)
    parts = [_wgrad_2d(x3, dgt, "wgrad_ple_gate", tk).reshape(N_DEV, RNN_BW, D_MODEL),
             jnp.transpose(_wgrad_2d(p, de0, "wgrad_ple_proj", tk).reshape(D_PLE, N_DEV, RNN_BW), (1, 0, 2)),
             _wgrad_2d(f, dx3, "wgrad_down", tk).reshape(N_DEV, D_FF // N_DEV, D_MODEL)]
    (dx2, dzg, dpa, dpb, dya, dyb, du0, ffn_sums), got = _up_bwd_merge_bwd(
        du, wc, w_up_g, x2, dx3, z, pa, pb, w_out, w_pa, w_pb, w["g_ffn"], tm, _exchange_comm(parts))
    received.update({n: [(g, 0)] for n, g in zip(("w_ple_gate", "w_ple_proj", "w_down"), got)})
    up_sum = _pair_reduce(_wgrad_up(h2, du0, w_up_g.shape[2], tk), "up_grad_pair_reduce")
    parts = [g.reshape(N_DEV, RNN_BW, D_MODEL)
             for g in _wgrad_multi([(mm, dx2), (ya, dpa), (yb, dpb)], "wgrad_merge", tk)]
    comm = _join_comms(_exchange_comm(parts), _chip_exchange_comm([up_sum]))
    (dz5, dw4, db4, dwa, dba, dwx, dbx, dlam, dw3), got = _mixers_bwd(z, h, dya, dyb, w4, b4, w_a, b_a, w_x, b_x, lam, w3,
                                                                       ch_bwd, comm)
    received.update({n: [(g, 0)] for n, g in zip(("w_out", "w_proj_a", "w_proj_b", "w_up"), got)})
    chip_sum = _pair_reduce(_wgrad_in(h1, dz5, dzg, 512, tk), "in_proj_grad_pair_reduce")
    (grad_x, mix_sums), got = _in_proj_bwd(dz5, dzg, w_in, x, dx2, w["g_mix"], tm, _chip_exchange_comm([chip_sum]))
    received["w_in"] = [(got[0], 0)]

    zero_row = jnp.zeros((1, D_MODEL), F32)
    vec_pack = jnp.concatenate(
        [mix_sums[0:1], db4, dba, dbx, dlam, ffn_sums[0:1], head_sums[2:3], head_sums[1:2], dbc.reshape(6, D_MODEL),
         head_sums[0:1], zero_row, dw4, dw3, zero_row, jnp.transpose(dwc, (1, 0, 2)).reshape(18, D_MODEL),
         jnp.zeros((VEC_ROWS - ROW_FFN_CONV - 18, D_MODEL), F32)], axis=0)
    gate_pack = jnp.concatenate([dwa.reshape(D_MODEL, RNN_BW), dwx.reshape(D_MODEL, RNN_BW)], axis=0)
    vec_total, gate_total = _small_all_reduce([vec_pack, gate_pack])
    loss = jnp.sum(vec_total[ROW_LOSS]) * (0.5 / D_MODEL)

    out = {}
    for n in _SHARDED_BIG:
        res = _adam_shard(received[n], big[n], m[n][0], v[n][0], "adam_" + n)
        out[n] = [r[None] for r in res]

    def flat(t, n):
        return t[n].reshape(1, -1)

    res = _adam_small([(vec_total, i) for i in range(len(_VECTORS))],
                      [flat(w, n) for n in _VECTORS], [flat(m, n) for n in _VECTORS], [flat(v, n) for n in _VECTORS],
                      "adam_vectors")
    for n, r in zip(_VECTORS, res):
        out[n] = [a.reshape(w[n].shape) for a in r]
    gates = ("w_rg_a", "w_rg_x")
    res = _adam_small([gate_total[:D_MODEL], gate_total[D_MODEL:]], *[[t[n].reshape(D_MODEL, RNN_BW) for n in gates]
                                                                      for t in (w, m, v)], "adam_gate_maps")
    for n, r in zip(gates, res):
        out[n] = [a.reshape(w[n].shape) for a in r]
    convs = ("rnn_conv_w", "sc_conv_w", "ffn_conv_w")
    bw_ffn = 2 * D_FF // N_DEV
    g_conv = [lax.dynamic_slice(vec_total[ROW_RNN_CONV:ROW_RNN_CONV + 4], (0, my_index * RNN_BW), (4, RNN_BW)),
              lax.dynamic_slice(vec_total[ROW_SC_CONV:ROW_SC_CONV + 3], (0, my_index * RNN_BW), (3, RNN_BW)),
              lax.dynamic_slice(vec_total[ROW_FFN_CONV:ROW_FFN_CONV + 18].reshape(3, 2 * D_FF), (0, my_index * bw_ffn),
                                (3, bw_ffn))]
    res = _adam_small(g_conv, *[[t[n][0] for n in convs] for t in (w, m, v)], "adam_conv")
    for n, r in zip(convs, res):
        out[n] = [a[None] for a in r]

    return (loss, grad_x[None]) + tuple(out[n][k] for k in range(4) for n in _NAMES)


def kernel(x, p, g_mix, w_in, rnn_conv_w, rnn_conv_b, w_rg_a, b_rg_a, w_rg_x, b_rg_x, lru_lambda, sc_conv_w, w_proj_a, w_proj_b, w_out, g_ffn, w_up, ffn_conv_w, ffn_conv_b, w_down, w_ple_gate, w_ple_proj, g_ple, g_final, loss_target, m_g_mix, m_w_in, m_rnn_conv_w, m_rnn_conv_b, m_w_rg_a, m_b_rg_a, m_w_rg_x, m_b_rg_x, m_lru_lambda, m_sc_conv_w, m_w_proj_a, m_w_proj_b, m_w_out, m_g_ffn, m_w_up, m_ffn_conv_w, m_ffn_conv_b, m_w_down, m_w_ple_gate, m_w_ple_proj, m_g_ple, m_g_final, v_g_mix, v_w_in, v_rnn_conv_w, v_rnn_conv_b, v_w_rg_a, v_b_rg_a, v_w_rg_x, v_b_rg_x, v_lru_lambda, v_sc_conv_w, v_w_proj_a, v_w_proj_b, v_w_out, v_g_ffn, v_w_up, v_ffn_conv_w, v_ffn_conv_b, v_w_down, v_w_ple_gate, v_w_ple_proj, v_g_ple, v_g_final):
    given = dict(locals())
    w = {n: given[n] for n in _NAMES}
    m = {n: given["m_" + n] for n in _NAMES}
    v = {n: given["v_" + n] for n in _NAMES}
    return _step(x[0], p[0, 0], loss_target[0], w, m, v)
```

```python
import functools

import jax
import jax.numpy as jnp
from jax import lax
from jax.experimental import pallas as pl
from jax.experimental.pallas import tpu as pltpu

F32 = jnp.float32
BF16 = jnp.bfloat16
MESH_ID = pl.DeviceIdType.MESH

N_DEV = 8
D_MODEL = 1024
D_PLE = 256
RNN_BW = 128
N_SEG = 7
D_FF = 3072
LRU_C = 8.0
EPS = 1e-6
ADAM_LR = 0.001
ADAM_B1 = 0.9
ADAM_B2 = 0.999
ADAM_EPS = 1e-08
ADAM_WD = 0.01
ADAM_STEP = 10

HALO = 16
SCAN_UNROLL = 16
VMEM_LIMIT = 56 * 1024 * 1024


def _dot(a, b):
    return jnp.dot(a, b, preferred_element_type=F32)


def _dot_nt(a, b):
    return lax.dot_general(a, b, (((1,), (1,)), ((), ())), preferred_element_type=F32)


def _dot_tn(a, b):
    return lax.dot_general(a, b, (((0,), (0,)), ((), ())), preferred_element_type=F32)


def _sigmoid(x):
    return 0.5 * jnp.tanh(0.5 * x) + 0.5


_GELU_C = 0.7978845608028654
_GELU_K = 0.044715


def _gelu(x):
    return 0.5 * x * (1.0 + jnp.tanh(_GELU_C * (x + _GELU_K * (x * x * x))))


def _gelu_and_grad(x):
    x2 = x * x
    t = jnp.tanh(_GELU_C * (x + _GELU_K * (x * x2)))
    g = 0.5 * x * (1.0 + t)
    dg = 0.5 * (1.0 + t) + 0.5 * x * (1.0 - t * t) * (_GELU_C * (1.0 + 3.0 * _GELU_K * x2))
    return g, dg


def _one_minus_sq(log_a, a):
    series = (-2.0 * log_a) * (1.0 + log_a * (1.0 + log_a * (2.0 / 3.0)))
    return jnp.where(log_a > -0.005, series, 1.0 - a * a)


def _softplus(x):
    return jnp.maximum(x, 0.0) + jnp.log1p(jnp.exp(-jnp.abs(x)))


def _rms(u):
    r = lax.rsqrt(jnp.mean(u * u, axis=-1, keepdims=True) + EPS)
    return u * r, r


def _rms_bwd(dn, un, r):
    return r * (dn - un * jnp.mean(dn * un, axis=-1, keepdims=True))


def _shift_dn(ext, s, n):
    if s == 0:
        return ext[HALO:HALO + n]
    return pltpu.roll(ext, s, 0)[HALO:HALO + n]


def _shift_up(ext, s, n):
    if s == 0:
        return ext[0:n]
    return pltpu.roll(ext, n + HALO - s, 0)[0:n]


def _ext_before(ref, c, ch):
    t0 = c * ch
    prev = pl.multiple_of(jnp.maximum(t0 - HALO, 0), HALO)
    halo = jnp.where(c > 0, ref[pl.ds(prev, HALO), :].astype(F32), 0.0)
    cur = ref[pl.ds(pl.multiple_of(t0, HALO), ch), :].astype(F32)
    return jnp.concatenate([halo, cur], axis=0)


def _ext_after(ref, c, ch, n_chunks):
    t0 = c * ch
    nxt = pl.multiple_of(jnp.minimum(t0 + ch, (n_chunks - 1) * ch + ch - HALO), HALO)
    halo = jnp.where(c < n_chunks - 1, ref[pl.ds(nxt, HALO), :].astype(F32), 0.0)
    cur = ref[pl.ds(pl.multiple_of(t0, HALO), ch), :].astype(F32)
    return jnp.concatenate([cur, halo], axis=0)


def _rows(ref, c, ch):
    return ref[pl.ds(pl.multiple_of(c * ch, HALO), ch), :]


def _scan_fwd(a_ref, b_ref, h_ref, n_rows, cw):
    rows = lax.broadcasted_iota(jnp.int32, (8, cw), 0)

    def body(i, carry):
        for u in range(SCAN_UNROLL):
            t = pl.multiple_of((i * SCAN_UNROLL + u) * 8, 8)
            a = a_ref[pl.ds(t, 8), :]
            b = b_ref[pl.ds(t, 8), :]
            for d in (1, 2, 4):
                m = rows >= d
                b = jnp.where(m, a * pltpu.roll(b, d, 0) + b, b)
                a = jnp.where(m, a * pltpu.roll(a, d, 0), a)
            h_ref[pl.ds(t, 8), :] = a * carry + b
            carry = jnp.broadcast_to(a[7:8, :], (8, cw)) * carry + jnp.broadcast_to(b[7:8, :], (8, cw))
        return carry

    lax.fori_loop(0, n_rows // (8 * SCAN_UNROLL), body, jnp.zeros((8, cw), F32))


def _scan_rev(a_ref, g_ref, o_ref, n_rows, cw):
    rows = lax.broadcasted_iota(jnp.int32, (8, cw), 0)
    n_groups = n_rows // 8

    def body(i, carry):
        dh_next, a_next = carry
        for u in range(SCAN_UNROLL):
            t = pl.multiple_of((n_groups - 1 - (i * SCAN_UNROLL + u)) * 8, 8)
            a = a_ref[pl.ds(t, 8), :]
            g = g_ref[pl.ds(t, 8), :]
            an = jnp.where(rows < 7, pltpu.roll(a, 7, 0), a_next)
            a_next = jnp.broadcast_to(a[0:1, :], (8, cw))
            for d in (1, 2, 4):
                m = rows < 8 - d
                g = jnp.where(m, an * pltpu.roll(g, 8 - d, 0) + g, g)
                an = jnp.where(m, an * pltpu.roll(an, 8 - d, 0), an)
            o_ref[pl.ds(t, 8), :] = an * dh_next + g
            dh_next = jnp.broadcast_to(an[0:1, :], (8, cw)) * dh_next + jnp.broadcast_to(g[0:1, :], (8, cw))
        return dh_next, a_next

    zero = jnp.zeros((8, cw), F32)
    lax.fori_loop(0, n_groups // SCAN_UNROLL, body, (zero, zero))


def _params(*sem):
    return pltpu.CompilerParams(dimension_semantics=sem, vmem_limit_bytes=VMEM_LIMIT)


def _sds(shape, dtype):
    return jax.ShapeDtypeStruct(shape, dtype)


def _resident(shape):
    zeros = (0,) * len(shape)
    return pl.BlockSpec(shape, lambda *_: zeros, pipeline_mode=pl.Buffered(1))


class _Comm:
    def __init__(self, inputs, out_shape, scratch, start, finish, middle=None):
        self.inputs, self.out_shape, self.scratch, self.start, self.finish = inputs, out_shape, scratch, start, finish
        self.middle = middle


def _join_comms(a, b):
    n, k = len(a.inputs), len(a.scratch)

    def start(ins, outs, sems):
        a.start(ins[:n], outs[:n], sems[:k])
        b.start(ins[n:], outs[n:], sems[k:])

    def finish(ins, outs, sems):
        a.finish(ins[:n], outs[:n], sems[:k])
        b.finish(ins[n:], outs[n:], sems[k:])

    return _Comm(a.inputs + b.inputs, a.out_shape + b.out_shape, a.scratch + b.scratch, start, finish)


def _sem_scratch(n, copies=7):
    return [pltpu.SemaphoreType.DMA((n, copies)), pltpu.SemaphoreType.DMA((n, copies)), pltpu.SemaphoreType.DMA((n,))]


def _gather_comm(shards, by_columns=(), early_pass_on=False):
    n = len(shards)

    def plan(ins, outs, sems):
        send_sems, recv_sems, local_sems = sems
        x, y, c = lax.axis_index("x"), lax.axis_index("y"), lax.axis_index("c")
        me, sibling = (x, y, c), (x, y, 1 - c)
        chips = [(1 - x, y), (x, 1 - y), (1 - x, 1 - y)]

        def slot(t, dev):
            idx = 4 * dev[0] + 2 * dev[1] + dev[2]
            if t in by_columns:
                width = shards[t].shape[1]
                return outs[t].at[:, pl.ds(pl.multiple_of(idx * width, 128), width)]
            return outs[t].at[idx]

        def copy(t, k, block, to, src=None):
            return pltpu.make_async_remote_copy(
                src_ref=slot(t, block) if src is None else src, dst_ref=slot(t, block),
                send_sem=send_sems.at[t, k], recv_sem=recv_sems.at[t, k],
                device_id=to, device_id_type=MESH_ID)

        mine = [pltpu.make_async_copy(ins[t], slot(t, me), local_sems.at[t]) for t in range(n)]
        first = []
        for t in range(n):
            first.append(copy(t, 0, me, sibling, src=ins[t]))
            first += [copy(t, 1 + j, me, (*chip, c), src=ins[t]) for j, chip in enumerate(chips)]
        return me, sibling, chips, c, copy, mine, first

    def start(ins, outs, sems):
        *_, mine, first = plan(ins, outs, sems)
        for cp in mine + first:
            cp.start()

    def pass_on(ins, outs, sems):
        me, sibling, chips, c, copy, _, _ = plan(ins, outs, sems)
        passed = []
        for t in range(n):
            for j, chip in enumerate(chips):
                copy(t, 1 + j, (*chip, c), me).wait_recv()
                passed.append(copy(t, 4 + j, (*chip, c), sibling))
                passed[-1].start()
        return passed

    def finish(ins, outs, sems, passed=None):
        me, sibling, chips, c, copy, mine, first = plan(ins, outs, sems)
        if passed is None:
            passed = [copy(t, 4 + j, (*chip, c), sibling) for t in range(n) for j, chip in enumerate(chips)]
        for t in range(n):
            copy(t, 0, sibling, me).wait_recv()
            for j, chip in enumerate(chips):
                copy(t, 4 + j, (*chip, 1 - c), me).wait_recv()
        for cp in first + passed:
            cp.wait_send()
        for cp in mine:
            cp.wait()

    out_shape = [_sds((s.shape[0], N_DEV * s.shape[1]) if t in by_columns else (N_DEV,) + s.shape, s.dtype)
                 for t, s in enumerate(shards)]
    if early_pass_on:
        return _Comm(list(shards), out_shape, _sem_scratch(n), start, finish, middle=pass_on)
    return _Comm(list(shards), out_shape, _sem_scratch(n), start,
                 lambda ins, outs, sems: finish(ins, outs, sems, pass_on(ins, outs, sems)))


def _peer(d):
    x, y, c = lax.axis_index("x"), lax.axis_index("y"), lax.axis_index("c")
    px = 1 - x if d & 4 else x
    py = 1 - y if d & 2 else y
    pc = 1 - c if d & 1 else c
    return (px, py, pc), 4 * px + 2 * py + pc


def _exchange_comm(parts, deltas=tuple(range(1, N_DEV))):
    n = len(parts)

    def plan(ins, outs, sems):
        send_sems, recv_sems, local_sems = sems
        _, me = _peer(0)

        def copy(t, i):
            peer, idx = _peer(deltas[i])
            return pltpu.make_async_remote_copy(
                src_ref=ins[t].at[idx], dst_ref=outs[t].at[1 + i],
                send_sem=send_sems.at[t, i], recv_sem=recv_sems.at[t, i],
                device_id=peer, device_id_type=MESH_ID)

        mine = [pltpu.make_async_copy(ins[t].at[me], outs[t].at[0], local_sems.at[t]) for t in range(n)]
        return mine, [copy(t, i) for t in range(n) for i in range(len(deltas))]

    def start(ins, outs, sems):
        mine, copies = plan(ins, outs, sems)
        for cp in mine + copies:
            cp.start()

    def finish(ins, outs, sems):
        mine, copies = plan(ins, outs, sems)
        for cp in copies:
            cp.wait_recv()
        for cp in copies:
            cp.wait_send()
        for cp in mine:
            cp.wait()

    return _Comm(list(parts), [_sds((1 + len(deltas),) + p.shape[1:], p.dtype) for p in parts],
                 _sem_scratch(n, len(deltas)), start, finish)


def _chip_exchange_comm(parts):
    n = len(parts)

    def plan(ins, outs, sems):
        send_sems, recv_sems, local_sems = sems
        x, y, c = lax.axis_index("x"), lax.axis_index("y"), lax.axis_index("c")

        def copy(t, dq):
            px = 1 - x if dq & 2 else x
            py = 1 - y if dq & 1 else y
            return pltpu.make_async_remote_copy(
                src_ref=ins[t].at[2 * px + py], dst_ref=outs[t].at[dq],
                send_sem=send_sems.at[t, dq - 1], recv_sem=recv_sems.at[t, dq - 1],
                device_id=(px, py, c), device_id_type=MESH_ID)

        mine = [pltpu.make_async_copy(ins[t].at[2 * x + y], outs[t].at[0], local_sems.at[t]) for t in range(n)]
        return mine, [copy(t, dq) for t in range(n) for dq in range(1, 4)]

    def start(ins, outs, sems):
        mine, copies = plan(ins, outs, sems)
        for cp in mine + copies:
            cp.start()

    def finish(ins, outs, sems):
        mine, copies = plan(ins, outs, sems)
        for cp in copies:
            cp.wait_recv()
        for cp in copies:
            cp.wait_send()
        for cp in mine:
            cp.wait()

    return _Comm(list(parts), [_sds(p.shape, p.dtype) for p in parts], _sem_scratch(n, 3), start, finish)


def _pair_reduce(part, name):
    by_columns = part.ndim == 2
    r, c = (part.shape[0], part.shape[1] // N_DEV) if by_columns else part.shape[1:]
    n_chips = N_DEV // 2
    tr = 128

    def body(p_ref, o_ref, own, land, send_sems, recv_sems, local_sems):
        x, y, core = lax.axis_index("x"), lax.axis_index("y"), lax.axis_index("c")

        def block(k):
            return p_ref.at[:, pl.ds(pl.multiple_of(k * c, 128), c)] if by_columns else p_ref.at[k]

        sends = [pltpu.make_async_remote_copy(
            src_ref=block(2 * q + (1 - core)), dst_ref=land.at[q], send_sem=send_sems.at[q], recv_sem=recv_sems.at[q],
            device_id=(x, y, 1 - core), device_id_type=MESH_ID) for q in range(n_chips)]
        loads = [pltpu.make_async_copy(block(2 * q + core), own.at[q], local_sems.at[q]) for q in range(n_chips)]
        for cp in sends + loads:
            cp.start()
        for q in range(n_chips):
            loads[q].wait()
            sends[q].wait_recv()

            def add_rows(i, _):
                rows = pl.ds(pl.multiple_of(i * tr, tr), tr)
                o_ref[q, rows, :] = (own[q, rows, :].astype(F32) + land[q, rows, :].astype(F32)).astype(BF16)
                return 0

            lax.fori_loop(0, r // tr, add_rows, 0)
        for cp in sends:
            cp.wait_send()

    return pl.pallas_call(
        body, name=name, out_shape=_sds((n_chips, r, c), BF16),
        in_specs=[pl.BlockSpec(memory_space=pl.ANY)], out_specs=pl.BlockSpec(memory_space=pltpu.VMEM),
        scratch_shapes=[pltpu.VMEM((n_chips, r, c), BF16), pltpu.VMEM((n_chips, r, c), BF16)]
        + [pltpu.SemaphoreType.DMA((n_chips,))] * 3,
        compiler_params=pltpu.CompilerParams(vmem_limit_bytes=VMEM_LIMIT),
    )(part)


def _comm_only(comm, name):
    n = len(comm.inputs)

    def body(*refs):
        ins, outs, sems = refs[:n], refs[n:2 * n], refs[2 * n:]
        comm.start(ins, outs, sems)
        comm.finish(ins, outs, sems)

    any_spec = pl.BlockSpec(memory_space=pl.ANY)
    return pl.pallas_call(body, name=name, out_shape=comm.out_shape, in_specs=[any_spec] * n, out_specs=[any_spec] * n,
                          scratch_shapes=comm.scratch)(*comm.inputs)


def _host_call(body, *, name, grid, in_specs, out_specs, out_shape, scratch_shapes=(), semantics, args, comm=None):
    if comm is None:
        res = pl.pallas_call(body, name=name, grid=grid, in_specs=in_specs, out_specs=out_specs, out_shape=out_shape,
                             scratch_shapes=list(scratch_shapes), compiler_params=_params(*semantics))(*args)
        return res, []
    n_in, n_out, n_scr, n_c = len(in_specs), len(out_specs), len(scratch_shapes), len(comm.inputs)

    def with_comm(*refs):
        ins, refs = refs[:n_in], refs[n_in:]
        c_ins, refs = refs[:n_c], refs[n_c:]
        outs, refs = refs[:n_out], refs[n_out:]
        c_outs, refs = refs[:n_c], refs[n_c:]
        scr, sems = refs[:n_scr], refs[n_scr:]
        ids = [pl.program_id(a) for a in range(len(grid))]
        first = functools.reduce(jnp.logical_and, [i == 0 for i in ids])
        last = functools.reduce(jnp.logical_and, [i == g - 1 for i, g in zip(ids, grid)])

        @pl.when(first)
        def _():
            comm.start(c_ins, c_outs, sems)

        body(*ins, *outs, *scr)

        if comm.middle is not None:
            assert len(grid) == 1

            @pl.when(ids[0] == (3 * grid[0]) // 4)
            def _():
                comm.middle(c_ins, c_outs, sems)

        @pl.when(last)
        def _():
            comm.finish(c_ins, c_outs, sems)

    any_spec = pl.BlockSpec(memory_space=pl.ANY)
    res = pl.pallas_call(
        with_comm, name=name, grid=grid, in_specs=list(in_specs) + [any_spec] * n_c,
        out_specs=list(out_specs) + [any_spec] * n_c, out_shape=list(out_shape) + comm.out_shape,
        scratch_shapes=list(scratch_shapes) + comm.scratch,
        compiler_params=_params(*(["arbitrary"] * len(grid))))(*args, *comm.inputs)
    return res[:n_out], res[n_out:]


def _small_all_reduce(packs):
    n = len(packs)
    pieces = [p.shape[0] // N_DEV for p in packs]

    def body(*refs):
        p_refs, o_refs, lands = refs[:n], refs[n:2 * n], refs[2 * n:3 * n]
        s1, r1, s2, r2 = refs[3 * n:]
        _, me = _peer(0)

        def piece(t, ref, idx):
            return ref.at[pl.ds(pl.multiple_of(idx * pieces[t], 8), pieces[t]), :]

        def scatter(t, d):
            peer, idx = _peer(d)
            return pltpu.make_async_remote_copy(
                src_ref=piece(t, p_refs[t], idx), dst_ref=lands[t].at[d - 1], send_sem=s1.at[t, d - 1],
                recv_sem=r1.at[t, d - 1], device_id=peer, device_id_type=MESH_ID)

        def gather(t, d, from_idx):
            peer, _ = _peer(d)
            return pltpu.make_async_remote_copy(
                src_ref=piece(t, o_refs[t], from_idx), dst_ref=piece(t, o_refs[t], from_idx), send_sem=s2.at[t, d - 1],
                recv_sem=r2.at[t, d - 1], device_id=peer, device_id_type=MESH_ID)

        first = [[scatter(t, d) for d in range(1, N_DEV)] for t in range(n)]
        for cp in sum(first, []):
            cp.start()
        second = []
        for t in range(n):
            acc = piece(t, p_refs[t], me)[...]
            for d in range(1, N_DEV):
                first[t][d - 1].wait_recv()
                acc = acc + lands[t][d - 1]
            piece(t, o_refs[t], me)[...] = acc
            second += [gather(t, d, me) for d in range(1, N_DEV)]
            for cp in second[-(N_DEV - 1):]:
                cp.start()
        for t in range(n):
            for d in range(1, N_DEV):
                gather(t, d, _peer(d)[1]).wait_recv()
        for cp in sum(first, []) + second:
            cp.wait_send()

    vmem = pl.BlockSpec(memory_space=pltpu.VMEM)
    return pl.pallas_call(
        body, name="small_all_reduce",
        out_shape=[_sds(p.shape, F32) for p in packs], in_specs=[vmem] * n, out_specs=[vmem] * n,
        scratch_shapes=[pltpu.VMEM((N_DEV - 1, pc, p.shape[1]), F32) for pc, p in zip(pieces, packs)]
        + [pltpu.SemaphoreType.DMA((n, N_DEV - 1))] * 4,
    )(*packs)


def _norm_in_proj(x, g_mix, w_in, tm, comm):
    s = x.shape[0]

    def body(x_ref, g_ref, w_ref, z_ref, h_ref):
        xn, _ = _rms(x_ref[...])
        h = (xn * g_ref[...]).astype(BF16)
        h_ref[...] = h
        for j in range(N_SEG):
            cols = slice(j * D_MODEL, (j + 1) * D_MODEL)
            z_ref[:, cols] = _dot(h, w_ref[:, cols]).astype(BF16)

    return _host_call(
        body, name="norm_in_proj", grid=(s // tm,),
        in_specs=[pl.BlockSpec((tm, D_MODEL), lambda i: (i, 0)), pl.BlockSpec((1, D_MODEL), lambda i: (0, 0)),
                  _resident(w_in.shape)],
        out_specs=[pl.BlockSpec((tm, N_SEG * D_MODEL), lambda i: (i, 0)), pl.BlockSpec((tm, D_MODEL), lambda i: (i, 0))],
        out_shape=[_sds((s, N_SEG * D_MODEL), BF16), _sds((s, D_MODEL), BF16)],
        semantics=("parallel",), args=(x, g_mix, w_in), comm=comm)


def _gate_chunk(xr_ext, w4_ref, cb_ref, wa, wx, ba_ref, bx_ref, sp, ch):
    xc = cb_ref[...] + w4_ref[3:4, :] * _shift_dn(xr_ext, 0, ch)
    for k in range(3):
        xc = xc + w4_ref[k:k + 1, :] * _shift_dn(xr_ext, 3 - k, ch)
    xcb = xc.astype(BF16)
    ra = _sigmoid(_dot(xcb, wa) + ba_ref[...])
    ia = _sigmoid(_dot(xcb, wx) + bx_ref[...])
    log_a = (-LRU_C) * ra * sp
    return xc, ra, ia, log_a


def _conv3(q_ext, w3_ref, ch):
    y = w3_ref[2:3, :] * _shift_dn(q_ext, 0, ch)
    for k in range(2):
        y = y + w3_ref[k:k + 1, :] * _shift_dn(q_ext, 2 - k, ch)
    return y


def _z_block_specs(s, cw, segs):
    nb = D_MODEL // cw
    return [pl.BlockSpec((s, cw), functools.partial(lambda j, seg: (0, seg * nb + j), seg=seg)) for seg in segs]


def _mixers_fwd(z, w4, b4, w_a, b_a, w_x, b_x, lam, w3, ch, comm):
    s = z.shape[0]
    cw = RNN_BW
    n_chunks = s // ch

    def body(xr_ref, gr_ref, cb_ref, cc_ref, cx_ref, w4_ref, b4_ref, wa_ref, ba_ref, wx_ref, bx_ref, lam_ref, w3_ref,
             ya_ref, yb_ref, h_ref, a_scr, b_scr):
        sp = _softplus(-lam_ref[...])
        wa = wa_ref[0].astype(BF16)
        wx = wx_ref[0].astype(BF16)

        def gates(c, _):
            xc, _, ia, log_a = _gate_chunk(_ext_before(xr_ref, c, ch), w4_ref, b4_ref, wa, wx, ba_ref, bx_ref, sp, ch)
            rows = pl.ds(pl.multiple_of(c * ch, HALO), ch)
            a = jnp.exp(log_a)
            a_scr[rows, :] = a
            b_scr[rows, :] = jnp.sqrt(_one_minus_sq(log_a, a)) * (ia * xc)
            q_ext = _ext_before(cc_ref, c, ch) * _ext_before(cx_ref, c, ch)
            yb_ref[rows, :] = (_rows(cb_ref, c, ch).astype(F32) * _conv3(q_ext, w3_ref, ch)).astype(BF16)
            return 0

        lax.fori_loop(0, n_chunks, gates, 0)
        _scan_fwd(a_scr, b_scr, h_ref, s, cw)

        def outputs(c, _):
            rows = pl.ds(pl.multiple_of(c * ch, HALO), ch)
            ya_ref[rows, :] = (h_ref[rows, :] * _gelu(gr_ref[rows, :].astype(F32))).astype(BF16)
            return 0

        lax.fori_loop(0, n_chunks, outputs, 0)

    col = lambda j: (0, j)
    vec = pl.BlockSpec((1, cw), col)
    sq = pl.BlockSpec((1, cw, cw), lambda j: (j, 0, 0))
    act = pl.BlockSpec((s, cw), col)
    return _host_call(
        body, name="mixers_fwd", grid=(D_MODEL // cw,),
        in_specs=_z_block_specs(s, cw, range(5)) + [pl.BlockSpec((4, cw), col), vec, sq, vec, sq, vec, vec,
                                                     pl.BlockSpec((3, cw), col)],
        out_specs=[act, act, act],
        out_shape=[_sds((s, D_MODEL), BF16), _sds((s, D_MODEL), BF16), _sds((s, D_MODEL), F32)],
        scratch_shapes=[pltpu.VMEM((s, cw), F32), pltpu.VMEM((s, cw), F32)],
        semantics=("parallel",), args=(z, z, z, z, z, w4, b4, w_a, b_a, w_x, b_x, lam, w3), comm=comm)


def _merge_out(ya, yb, z, x, w_pa, w_pb, w_out, g_ffn, tm):
    s = x.shape[0]

    def body(ya_ref, yb_ref, ga_ref, gb_ref, x_ref, wa_ref, wb_ref, wo_ref, g_ref, pa_ref, pb_ref, m_ref, x2_ref, h2_ref):
        pa = _dot(ya_ref[...], wa_ref[...])
        pb = _dot(yb_ref[...], wb_ref[...])
        pa_ref[...] = pa.astype(BF16)
        pb_ref[...] = pb.astype(BF16)
        m = (_sigmoid(ga_ref[...].astype(F32)) * pa + _sigmoid(gb_ref[...].astype(F32)) * pb).astype(BF16)
        m_ref[...] = m
        x2 = x_ref[...] + _dot(m, wo_ref[...])
        x2_ref[...] = x2
        xn, _ = _rms(x2)
        h2_ref[...] = (xn * g_ref[...]).astype(BF16)

    tile = pl.BlockSpec((tm, D_MODEL), lambda i: (i, 0))
    full = _resident((D_MODEL, D_MODEL))
    return pl.pallas_call(
        body, name="merge_out", grid=(s // tm,),
        in_specs=[tile, tile, pl.BlockSpec((tm, D_MODEL), lambda i: (i, 5)), pl.BlockSpec((tm, D_MODEL), lambda i: (i, 6)),
                  tile, full, full, full, pl.BlockSpec((1, D_MODEL), lambda i: (0, 0))],
        out_specs=[tile] * 5,
        out_shape=[_sds((s, D_MODEL), BF16)] * 3 + [_sds((s, D_MODEL), F32), _sds((s, D_MODEL), BF16)],
        compiler_params=_params("parallel"),
    )(ya, yb, z, z, x, w_pa, w_pb, w_out, g_ffn)


def _up_proj_act(h2, w_up_g, wc, bc, tm, comm):
    s = h2.shape[0]
    bw = w_up_g.shape[2]
    per_half = N_DEV // 2

    def body(h_ref, w_ref, wc_ref, bc_ref, u_ref, f_ref, halo):
        @pl.when(pl.program_id(0) == 0)
        def _():
            halo[...] = jnp.zeros_like(halo)

        h = h_ref[...]
        for j in range(per_half):
            cols = slice(j * bw, (j + 1) * bw)
            conv = []
            for half in range(2):
                u = _dot(h, w_ref[half * per_half + j]).astype(BF16)
                u_ref[half, :, cols] = u
                u = u.astype(F32)
                ext = jnp.concatenate([halo[half, :, cols], u], axis=0)
                halo[half, :, cols] = u[tm - HALO:tm]
                w = wc_ref.at[half]
                acc = bc_ref[half, :, cols] + w[2:3, cols] * u
                for k in range(2):
                    acc = acc + w[k:k + 1, cols] * _shift_dn(ext, 2 - k, tm)
                conv.append(acc)
            f_ref[:, cols] = (_gelu(conv[0]) * conv[1]).astype(BF16)

    return _host_call(
        body, name="up_proj_act", grid=(s // tm,),
        in_specs=[pl.BlockSpec((tm, D_MODEL), lambda i: (i, 0)), _resident(w_up_g.shape), _resident(wc.shape),
                  _resident(bc.shape)],
        out_specs=[pl.BlockSpec((2, tm, D_FF), lambda i: (0, i, 0)), pl.BlockSpec((tm, D_FF), lambda i: (i, 0))],
        out_shape=[_sds((2, s, D_FF), BF16), _sds((s, D_FF), BF16)],
        scratch_shapes=[pltpu.VMEM((2, HALO, D_FF), F32)],
        semantics=("arbitrary",), args=(h2, w_up_g, wc, bc), comm=comm)


def _head(x2, f, u0, wc, bc, p, target, w_down, w_gate, w_pp, g_ple, g_final, tm):
    s = x2.shape[0]
    bw = 256

    def body(x2_ref, f_ref, u_ref, uh_ref, wc_ref, bc_ref, p_ref, t_ref, wd_ref, wg_ref, wp_ref, gp_ref, gf_ref,
             x3_ref, dgt_ref, de0_ref, dx3_ref, du_ref, sums_ref, dwc_ref, dbc_ref):
        i = pl.program_id(0)

        @pl.when(i == 0)
        def _():
            sums_ref[...] = jnp.zeros_like(sums_ref)
            dwc_ref[...] = jnp.zeros_like(dwc_ref)
            dbc_ref[...] = jnp.zeros_like(dbc_ref)

        x3 = x2_ref[...] + _dot(f_ref[...], wd_ref[...])
        x3b = x3.astype(BF16)
        x3_ref[...] = x3b
        e0n, re = _rms(_dot(p_ref[...].astype(BF16), wp_ref[...]))
        e = e0n * gp_ref[...]
        sg = _sigmoid(_dot(x3b, wg_ref[...]))
        x4n, r4 = _rms(x3 + sg * e)
        diff = x4n * gf_ref[...] - t_ref[...]
        sums_ref[0:1, :] += jnp.sum(diff * diff, axis=0, keepdims=True)
        dy = diff * (1.0 / D_MODEL)
        sums_ref[1:2, :] += jnp.sum(dy * x4n, axis=0, keepdims=True)
        dx4 = _rms_bwd(dy * gf_ref[...], x4n, r4)
        de = dx4 * sg
        dgt = ((dx4 * e) * (sg * (1.0 - sg))).astype(BF16)
        dgt_ref[...] = dgt
        sums_ref[2:3, :] += jnp.sum(de * e0n, axis=0, keepdims=True)
        de0_ref[...] = _rms_bwd(de * gp_ref[...], e0n, re).astype(BF16)
        dx3 = dx4 + _dot_nt(dgt, wg_ref[...])
        dx3_ref[...] = dx3
        dx3b = dx3.astype(BF16)
        for j in range(D_FF // bw):
            cols = slice(j * bw, (j + 1) * bw)
            df = _dot_nt(dx3b, wd_ref[cols, :])
            conv, taps = [], []
            for half in range(2):
                halo = jnp.where(i > 0, uh_ref[half, :, cols].astype(F32), 0.0)
                ext = jnp.concatenate([halo, u_ref[half, :, cols].astype(F32)], axis=0)
                shifted = [_shift_dn(ext, 2 - k, tm) for k in range(3)]
                w = wc_ref.at[half]
                acc = bc_ref[half, :, cols] + w[2:3, cols] * shifted[2]
                for k in range(2):
                    acc = acc + w[k:k + 1, cols] * shifted[k]
                conv.append(acc)
                taps.append(shifted)
            gel, dgel = _gelu_and_grad(conv[0])
            for half, du in ((0, df * conv[1] * dgel), (1, df * gel)):
                du_ref[half, :, cols] = du.astype(BF16)
                dbc_ref[half, :, cols] += jnp.sum(du, axis=0, keepdims=True)
                for k in range(3):
                    dwc_ref[half, k:k + 1, cols] += jnp.sum(du * taps[half][k], axis=0, keepdims=True)

    tile = pl.BlockSpec((tm, D_MODEL), lambda i: (i, 0))
    vec = pl.BlockSpec((1, D_MODEL), lambda i: (0, 0))
    halves = pl.BlockSpec((2, tm, D_FF), lambda i: (0, i, 0))
    before = pl.BlockSpec((2, HALO, D_FF), lambda i: (0, jnp.maximum(i * (tm // HALO) - 1, 0), 0))
    return pl.pallas_call(
        body, name="loss_head", grid=(s // tm,),
        in_specs=[tile, pl.BlockSpec((tm, D_FF), lambda i: (i, 0)), halves, before, _resident(wc.shape),
                  _resident(bc.shape), pl.BlockSpec((tm, D_PLE), lambda i: (i, 0)), tile,
                  _resident((D_FF, D_MODEL)), _resident((D_MODEL, D_MODEL)), _resident((D_PLE, D_MODEL)), vec, vec],
        out_specs=[tile, tile, tile, tile, halves, pl.BlockSpec((8, D_MODEL), lambda i: (0, 0)),
                   pl.BlockSpec(wc.shape, lambda i: (0, 0, 0)), pl.BlockSpec(bc.shape, lambda i: (0, 0, 0))],
        out_shape=[_sds((s, D_MODEL), BF16)] * 3 + [_sds((s, D_MODEL), F32), _sds((2, s, D_FF), BF16),
                                                    _sds((8, D_MODEL), F32), _sds(wc.shape, F32), _sds(bc.shape, F32)],
        compiler_params=_params("arbitrary"),
    )(x2, f, u0, u0, wc, bc, p, target, w_down, w_gate, w_pp, g_ple, g_final)


def _up_bwd_merge_bwd(du, wc, w_up_g, x2, dx3, z, pa, pb, w_out, w_pa, w_pb, g_ffn, tm, comm):
    s = x2.shape[0]
    bw = w_up_g.shape[2]
    per_half = N_DEV // 2
    n_tiles = s // tm

    def body(du_ref, dua_ref, wc_ref, wu_ref, x2_ref, dx3_ref, ga_ref, gb_ref, pa_ref, pb_ref, wo_ref, wa_ref, wb_ref, g_ref,
             dx2_ref, dzg_ref, dpa_ref, dpb_ref, dya_ref, dyb_ref, du0_ref, sums_ref):
        i = pl.program_id(0)

        @pl.when(i == 0)
        def _():
            sums_ref[...] = jnp.zeros_like(sums_ref)

        dh2 = None
        for j in range(N_DEV):
            half = j // per_half
            cols = slice((j % per_half) * bw, (j % per_half + 1) * bw)
            halo = jnp.where(i < n_tiles - 1, dua_ref[half, :, cols].astype(F32), 0.0)
            ext = jnp.concatenate([du_ref[half, :, cols].astype(F32), halo], axis=0)
            w = wc_ref.at[half]
            acc = w[2:3, cols] * _shift_up(ext, 0, tm)
            for k in range(2):
                acc = acc + w[k:k + 1, cols] * _shift_up(ext, 2 - k, tm)
            du0 = acc.astype(BF16)
            du0_ref[half, :, cols] = du0
            term = _dot_nt(du0, wu_ref[j])
            dh2 = term if dh2 is None else dh2 + term
        x2n, r2 = _rms(x2_ref[...])
        sums_ref[0:1, :] += jnp.sum(dh2 * x2n, axis=0, keepdims=True)
        dx2 = dx3_ref[...] + _rms_bwd(dh2 * g_ref[...], x2n, r2)
        dx2_ref[...] = dx2
        dm = _dot_nt(dx2.astype(BF16), wo_ref[...])
        sa = _sigmoid(ga_ref[...].astype(F32))
        sb = _sigmoid(gb_ref[...].astype(F32))
        dzg_ref[0] = (dm * pa_ref[...].astype(F32) * (sa * (1.0 - sa))).astype(BF16)
        dzg_ref[1] = (dm * pb_ref[...].astype(F32) * (sb * (1.0 - sb))).astype(BF16)
        dpa = (dm * sa).astype(BF16)
        dpb = (dm * sb).astype(BF16)
        dpa_ref[...] = dpa
        dpb_ref[...] = dpb
        dya_ref[...] = _dot_nt(dpa, wa_ref[...]).astype(BF16)
        dyb_ref[...] = _dot_nt(dpb, wb_ref[...]).astype(BF16)

    tile = pl.BlockSpec((tm, D_MODEL), lambda i: (i, 0))
    full = _resident((D_MODEL, D_MODEL))
    halves = pl.BlockSpec((2, tm, D_FF), lambda i: (0, i, 0))
    after = pl.BlockSpec((2, HALO, D_FF), lambda i: (0, jnp.minimum((i + 1) * (tm // HALO), s // HALO - 1), 0))
    return _host_call(
        body, name="up_bwd_merge_bwd", grid=(n_tiles,),
        in_specs=[halves, after, _resident(wc.shape), _resident(w_up_g.shape),
                  tile, tile, pl.BlockSpec((tm, D_MODEL), lambda i: (i, 5)),
                  pl.BlockSpec((tm, D_MODEL), lambda i: (i, 6)), tile, tile, full, full, full,
                  pl.BlockSpec((1, D_MODEL), lambda i: (0, 0))],
        out_specs=[tile, pl.BlockSpec((2, tm, D_MODEL), lambda i: (0, i, 0)), tile, tile, tile, tile, halves,
                   pl.BlockSpec((8, D_MODEL), lambda i: (0, 0))],
        out_shape=[_sds((s, D_MODEL), F32), _sds((2, s, D_MODEL), BF16)] + [_sds((s, D_MODEL), BF16)] * 4
        + [_sds((2, s, D_FF), BF16), _sds((8, D_MODEL), F32)],
        semantics=("arbitrary",), args=(du, du, wc, w_up_g, x2, dx3, z, z, pa, pb, w_out, w_pa, w_pb, g_ffn),
        comm=comm)


def _mixers_bwd(z, h, dya, dyb, w4, b4, w_a, b_a, w_x, b_x, lam, w3, ch, comm):
    s = z.shape[0]
    cw = RNN_BW
    ch1, ch2, ch3 = ch

    def body(xr_ref, gr_ref, cb_ref, cc_ref, cx_ref, h_ref, dya_ref, dyb_ref, w4_ref, b4_ref, wa_ref, ba_ref, wx_ref,
             bx_ref, lam_ref, w3_ref,
             dz_ref, dw4_ref, db4_ref, dwa_ref, dba_ref, dwx_ref, dbx_ref, dlam_ref, dw3_ref,
             a_scr, g_scr, q_scr, xc_scr, ra_scr, ia_scr, dh_scr):
        sp = _softplus(-lam_ref[...])
        wa = wa_ref[0].astype(BF16)
        wx = wx_ref[0].astype(BF16)
        for ref in (dw4_ref, db4_ref, dwa_ref, dba_ref, dwx_ref, dbx_ref, dlam_ref, dw3_ref):
            ref[...] = jnp.zeros_like(ref)

        def first(c, _, ch=ch1):
            rows = pl.ds(pl.multiple_of(c * ch, HALO), ch)
            xc, ra, ia, log_a = _gate_chunk(_ext_before(xr_ref, c, ch), w4_ref, b4_ref, wa, wx, ba_ref, bx_ref, sp, ch)
            xc_scr[rows, :] = xc
            ra_scr[rows, :] = ra
            ia_scr[rows, :] = ia
            a_scr[rows, :] = jnp.exp(log_a)
            gel, dgel = _gelu_and_grad(gr_ref[rows, :].astype(F32))
            dya_c = dya_ref[rows, :].astype(F32)
            g_scr[rows, :] = dya_c * gel
            dz_ref[1, rows, :] = (dya_c * h_ref[rows, :] * dgel).astype(BF16)
            q_ext = _ext_before(cc_ref, c, ch) * _ext_before(cx_ref, c, ch)
            dyb_c = dyb_ref[rows, :].astype(F32)
            q_taps = [_shift_dn(q_ext, 2 - k, ch) for k in range(3)]
            conv_q = w3_ref[2:3, :] * q_taps[2] + w3_ref[1:2, :] * q_taps[1] + w3_ref[0:1, :] * q_taps[0]
            dz_ref[2, rows, :] = (dyb_c * conv_q).astype(BF16)
            dyq = dyb_c * cb_ref[rows, :].astype(F32)
            q_scr[rows, :] = dyq
            for k in range(3):
                dw3_ref[k:k + 1, :] += jnp.sum(dyq * q_taps[k], axis=0, keepdims=True)
            return 0

        lax.fori_loop(0, s // ch1, first, 0)
        _scan_rev(a_scr, g_scr, dh_scr, s, cw)

        def second(c, _, ch=ch2):
            rows = pl.ds(pl.multiple_of(c * ch, HALO), ch)
            xc, ra, ia, a = xc_scr[rows, :], ra_scr[rows, :], ia_scr[rows, :], a_scr[rows, :]
            m2 = _one_minus_sq((-LRU_C) * ra * sp, a)
            inv_mult = lax.rsqrt(m2)
            mult = m2 * inv_mult
            dh = dh_scr[rows, :]
            h_prev = _shift_dn(_ext_before(h_ref, c, ch), 1, ch)
            dlog_a = dh * h_prev * a - (dh * ia * xc) * (a * a) * inv_mult
            dlam_ref[...] += jnp.sum(dlog_a * ra, axis=0, keepdims=True)
            dpre_a = (dlog_a * ((-LRU_C) * sp)) * (ra * (1.0 - ra))
            dpre_x = (dh * mult * xc) * (ia * (1.0 - ia))
            dba_ref[...] += jnp.sum(dpre_a, axis=0, keepdims=True)
            dbx_ref[...] += jnp.sum(dpre_x, axis=0, keepdims=True)
            xcb = xc.astype(BF16)
            dpa_b = dpre_a.astype(BF16)
            dpx_b = dpre_x.astype(BF16)
            dwa_ref[0] += _dot_tn(xcb, dpa_b)
            dwx_ref[0] += _dot_tn(xcb, dpx_b)
            a_scr[rows, :] = dh * mult * ia + _dot_nt(dpa_b, wa) + _dot_nt(dpx_b, wx)
            return 0

        lax.fori_loop(0, s // ch2, second, 0)
        dlam_ref[...] = dlam_ref[...] * (LRU_C * _sigmoid(-lam_ref[...]))

        def third(c, _, ch=ch3, n_chunks=s // ch3):
            rows = pl.ds(pl.multiple_of(c * ch, HALO), ch)
            dxc_ext = _ext_after(a_scr, c, ch, n_chunks)
            dxc = _shift_up(dxc_ext, 0, ch)
            xr_ext = _ext_before(xr_ref, c, ch)
            db4_ref[...] += jnp.sum(dxc, axis=0, keepdims=True)
            dxr = w4_ref[3:4, :] * dxc
            dw4_ref[3:4, :] += jnp.sum(dxc * _shift_dn(xr_ext, 0, ch), axis=0, keepdims=True)
            for k in range(3):
                dxr = dxr + w4_ref[k:k + 1, :] * _shift_up(dxc_ext, 3 - k, ch)
                dw4_ref[k:k + 1, :] += jnp.sum(dxc * _shift_dn(xr_ext, 3 - k, ch), axis=0, keepdims=True)
            dz_ref[0, rows, :] = dxr.astype(BF16)
            dyq_ext = _ext_after(q_scr, c, ch, n_chunks)
            dq = w3_ref[2:3, :] * _shift_up(dyq_ext, 0, ch)
            for k in range(2):
                dq = dq + w3_ref[k:k + 1, :] * _shift_up(dyq_ext, 2 - k, ch)
            dz_ref[3, rows, :] = (dq * cx_ref[rows, :].astype(F32)).astype(BF16)
            dz_ref[4, rows, :] = (dq * cc_ref[rows, :].astype(F32)).astype(BF16)
            return 0

        lax.fori_loop(0, s // ch3, third, 0)

    col = lambda j: (0, j)
    vec = pl.BlockSpec((1, cw), col)
    sq = pl.BlockSpec((1, cw, cw), lambda j: (j, 0, 0))
    act = pl.BlockSpec((s, cw), col)
    w4s, w3s = pl.BlockSpec((4, cw), col), pl.BlockSpec((3, cw), col)
    vec_shape = _sds((1, D_MODEL), F32)
    sq_shape = _sds((D_MODEL // cw, cw, cw), F32)
    return _host_call(
        body, name="mixers_bwd", grid=(D_MODEL // cw,),
        in_specs=_z_block_specs(s, cw, range(5)) + [act, act, act, w4s, vec, sq, vec, sq, vec, vec, w3s],
        out_specs=[pl.BlockSpec((5, s, cw), lambda j: (0, 0, j)), w4s, vec, sq, vec, sq, vec, vec, w3s],
        out_shape=[_sds((5, s, D_MODEL), BF16), _sds((4, D_MODEL), F32), vec_shape, sq_shape, vec_shape, sq_shape,
                   vec_shape, vec_shape, _sds((3, D_MODEL), F32)],
        scratch_shapes=[pltpu.VMEM((s, cw), F32)] * 7,
        semantics=("parallel",), args=(z, z, z, z, z, h, dya, dyb, w4, b4, w_a, b_a, w_x, b_x, lam, w3), comm=comm)


def _in_proj_bwd(dz5, dzg, w_in, x, dx2, g_mix, tm, comm):
    s = x.shape[0]

    def body(d5_ref, dg_ref, w_ref, x_ref, dx2_ref, g_ref, dx_ref, sums_ref):
        @pl.when(pl.program_id(0) == 0)
        def _():
            sums_ref[...] = jnp.zeros_like(sums_ref)

        dh1 = None
        for k in range(N_SEG):
            d = d5_ref[k] if k < 5 else dg_ref[k - 5]
            term = _dot_nt(d, w_ref[:, k * D_MODEL:(k + 1) * D_MODEL])
            dh1 = term if dh1 is None else dh1 + term
        xn, r1 = _rms(x_ref[...])
        sums_ref[0:1, :] += jnp.sum(dh1 * xn, axis=0, keepdims=True)
        dx_ref[...] = dx2_ref[...] + _rms_bwd(dh1 * g_ref[...], xn, r1)

    tile = pl.BlockSpec((tm, D_MODEL), lambda i: (i, 0))
    return _host_call(
        body, name="in_proj_bwd", grid=(s // tm,),
        in_specs=[pl.BlockSpec((5, tm, D_MODEL), lambda i: (0, i, 0)), pl.BlockSpec((2, tm, D_MODEL), lambda i: (0, i, 0)),
                  _resident(w_in.shape), tile, tile, pl.BlockSpec((1, D_MODEL), lambda i: (0, 0))],
        out_specs=[tile, pl.BlockSpec((8, D_MODEL), lambda i: (0, 0))],
        out_shape=[_sds((s, D_MODEL), F32), _sds((8, D_MODEL), F32)],
        semantics=("arbitrary",), args=(dz5, dzg, w_in, x, dx2, g_mix), comm=comm)


def _weight_grad(a, b, b_spec, out_spec, out_shape, *, n_blocks, chunks, width, tm, tk, name):
    s, m = a.shape
    nk = s // tk
    b_chunked = len([d for d in b_spec.block_shape if d is not None]) == 3
    blocks_out = len([d for d in out_spec.block_shape if d is not None]) == 3

    def body(a_ref, b_ref, o_ref, acc):
        k = pl.program_id(2)

        @pl.when(k == 0)
        def _():
            acc[...] = jnp.zeros_like(acc)

        at = a_ref[...].astype(BF16).T
        for j in range(chunks):
            cols = slice(j * width, (j + 1) * width)
            acc[:, cols] += _dot(at, (b_ref[j] if b_chunked else b_ref[:, cols]).astype(BF16))

        @pl.when(k == nk - 1)
        def _():
            if blocks_out:
                for j in range(chunks):
                    o_ref[j] = acc[:, j * width:(j + 1) * width].astype(o_ref.dtype)
            else:
                o_ref[...] = acc[...].astype(o_ref.dtype)

    return pl.pallas_call(
        body, name=name, grid=(m // tm, n_blocks, nk),
        in_specs=[pl.BlockSpec((tk, tm), lambda i, j, k: (k, i)), b_spec], out_specs=out_spec, out_shape=out_shape,
        scratch_shapes=[pltpu.VMEM((tm, chunks * width), F32)],
        compiler_params=_params("parallel", "parallel", "arbitrary"),
    )(a, b)


def _wgrad_2d(a, b, name, tk):
    m, n = a.shape[1], b.shape[1]
    tm = m
    return _weight_grad(a, b, pl.BlockSpec((tk, n), lambda i, j, k: (k, 0)), pl.BlockSpec((tm, n), lambda i, j, k: (i, 0)),
                        _sds((m, n), BF16), n_blocks=1, chunks=1, width=n, tm=tm, tk=tk, name=name)


def _wgrad_multi(pairs, name, tk):
    n = len(pairs)
    s, m = pairs[0][0].shape
    nn = pairs[0][1].shape[1]
    nk = s // tk

    def body(*refs):
        a_refs, b_refs, o_refs, accs = refs[:n], refs[n:2 * n], refs[2 * n:3 * n], refs[3 * n:]
        k = pl.program_id(0)
        for a_ref, b_ref, o_ref, acc in zip(a_refs, b_refs, o_refs, accs):
            @pl.when(k == 0)
            def _():
                acc[...] = jnp.zeros_like(acc)

            acc[...] += _dot(a_ref[...].astype(BF16).T, b_ref[...].astype(BF16))

            @pl.when(k == nk - 1)
            def _():
                o_ref[...] = acc[...].astype(BF16)

    whole = pl.BlockSpec((m, nn), lambda k: (0, 0))
    return pl.pallas_call(
        body, name=name, grid=(nk,),
        in_specs=[pl.BlockSpec((tk, m), lambda k: (k, 0))] * n + [pl.BlockSpec((tk, nn), lambda k: (k, 0))] * n,
        out_specs=[whole] * n, out_shape=[_sds((m, nn), BF16)] * n,
        scratch_shapes=[pltpu.VMEM((m, nn), F32)] * n,
        compiler_params=_params("arbitrary"),
    )(*[a for a, _ in pairs], *[b for _, b in pairs])


def _wgrad_in(h1, dz5, dzg, tm, tk):
    s, m = h1.shape
    nk = s // tk
    g5, gg = dz5.shape[0], dzg.shape[0]

    def body(a_ref, b5_ref, bg_ref, o_ref, acc):
        k = pl.program_id(1)

        @pl.when(k == 0)
        def _():
            acc[...] = jnp.zeros_like(acc)

        at = a_ref[...].T
        for j in range(g5 + gg):
            b = b5_ref[j] if j < g5 else bg_ref[j - g5]
            acc[:, j * D_MODEL:(j + 1) * D_MODEL] += _dot(at, b)

        @pl.when(k == nk - 1)
        def _():
            o_ref[...] = acc[...].astype(BF16)

    return pl.pallas_call(
        body, name="wgrad_in", grid=(m // tm, nk),
        in_specs=[pl.BlockSpec((tk, tm), lambda i, k: (k, i)), pl.BlockSpec((g5, tk, D_MODEL), lambda i, k: (0, k, 0)),
                  pl.BlockSpec((gg, tk, D_MODEL), lambda i, k: (0, k, 0))],
        out_specs=pl.BlockSpec((tm, (g5 + gg) * D_MODEL), lambda i, k: (i, 0)),
        out_shape=_sds((m, (g5 + gg) * D_MODEL), BF16),
        scratch_shapes=[pltpu.VMEM((tm, (g5 + gg) * D_MODEL), F32)],
        compiler_params=_params("parallel", "arbitrary"),
    )(h1, dz5, dzg)


def _wgrad_up(h2, du0, bw, tk):
    per_half = N_DEV // 2
    return _weight_grad(h2, du0, pl.BlockSpec((None, tk, D_FF), lambda i, j, k: (j, k, 0)),
                        pl.BlockSpec((per_half, D_MODEL, bw), lambda i, j, k: (j, 0, 0)), _sds((N_DEV, D_MODEL, bw), BF16),
                        n_blocks=2, chunks=per_half, width=bw, tm=D_MODEL, tk=tk, name="wgrad_up")


def _adam_math(w, g, m, v):
    m = ADAM_B1 * m + (1.0 - ADAM_B1) * g
    v = ADAM_B2 * v + (1.0 - ADAM_B2) * jnp.square(g)
    m_hat = m / (1.0 - ADAM_B1 ** ADAM_STEP)
    v_hat = v / (1.0 - ADAM_B2 ** ADAM_STEP)
    delta = -ADAM_LR * (m_hat / (jnp.sqrt(v_hat) + ADAM_EPS) + ADAM_WD * w)
    return delta, m, v


def _adam_shard(parts, w, m, v, name):
    r, c = w.shape
    tr = min(r, 128)
    n_arrays = len(parts)

    def body(*refs):
        p_refs = refs[:n_arrays]
        w_ref, m_ref, v_ref, g_ref, d_ref, nm_ref, nv_ref = refs[n_arrays:]
        g = None
        for p_ref, (arr, first) in zip(p_refs, parts):
            for k in range(first, arr.shape[0]):
                term = p_ref[k].astype(F32)
                g = term if g is None else g + term
        g_ref[...] = g
        d_ref[...], nm_ref[...], nv_ref[...] = _adam_math(w_ref[...], g, m_ref[...], v_ref[...])

    tile = pl.BlockSpec((tr, c), lambda i: (i, 0))
    return pl.pallas_call(
        body, name=name, grid=(r // tr,),
        in_specs=[pl.BlockSpec((arr.shape[0], tr, c), lambda i: (0, i, 0)) for arr, _ in parts] + [tile, tile, tile],
        out_specs=[tile] * 4, out_shape=[_sds((r, c), F32)] * 4,
        compiler_params=_params("parallel"),
    )(*[arr for arr, _ in parts], w, m, v)


def _adam_small(grads, ws, ms, vs, name):
    n = len(ws)
    g_arrays, g_slot = [], []
    for g in grads:
        arr = g[0] if isinstance(g, tuple) else g
        if not any(arr is a for a in g_arrays):
            g_arrays.append(arr)
        g_slot.append([arr is a for a in g_arrays].index(True))
    n_g = len(g_arrays)

    def body(*refs):
        g_refs, refs = refs[:n_g], refs[n_g:]
        w_refs, m_refs, v_refs, outs = refs[:n], refs[n:2 * n], refs[2 * n:3 * n], refs[3 * n:]
        for i in range(n):
            og, od, om, ov = outs[4 * i:4 * i + 4]
            g_ref = g_refs[g_slot[i]]
            if isinstance(grads[i], tuple):
                row = grads[i][1]
                for j in range(ws[i].shape[1] // D_MODEL):
                    cols = slice(j * D_MODEL, (j + 1) * D_MODEL)
                    g = g_ref[row + j:row + j + 1, :]
                    og[:, cols] = g
                    od[:, cols], om[:, cols], ov[:, cols] = _adam_math(w_refs[i][:, cols], g, m_refs[i][:, cols],
                                                                       v_refs[i][:, cols])
            else:
                g = g_ref[...]
                og[...] = g
                od[...], om[...], ov[...] = _adam_math(w_refs[i][...], g, m_refs[i][...], v_refs[i][...])

    vmem = pl.BlockSpec(memory_space=pltpu.VMEM)
    res = pl.pallas_call(
        body, name=name, in_specs=[vmem] * (n_g + 3 * n), out_specs=[vmem] * (4 * n),
        out_shape=[_sds(a.shape, F32) for a in ws for _ in range(4)],
    )(*g_arrays, *ws, *ms, *vs)
    return [res[4 * i:4 * i + 4] for i in range(n)]


def _pack_conv_shard(rnn, sc, ffn):
    top = jnp.concatenate([rnn[:3], sc, ffn], axis=1)
    row3 = jnp.concatenate([rnn[3:4], jnp.zeros((1, D_MODEL - RNN_BW), F32)], axis=1)
    return jnp.concatenate([top, row3, jnp.zeros((4, D_MODEL), F32)], axis=0)


_VECTORS = ("g_mix", "rnn_conv_b", "b_rg_a", "b_rg_x", "lru_lambda", "g_ffn", "g_ple", "g_final", "ffn_conv_b")
VEC_ROWS = 64
ROW_LOSS, ROW_RNN_CONV, ROW_SC_CONV, ROW_FFN_CONV = 14, 16, 20, 24


_SHARDED_BIG = ("w_in", "w_proj_a", "w_proj_b", "w_out", "w_up", "w_down", "w_ple_gate", "w_ple_proj")
_NAMES = ("g_mix", "w_in", "rnn_conv_w", "rnn_conv_b", "w_rg_a", "b_rg_a", "w_rg_x", "b_rg_x", "lru_lambda", "sc_conv_w",
          "w_proj_a", "w_proj_b", "w_out", "g_ffn", "w_up", "ffn_conv_w", "ffn_conv_b", "w_down", "w_ple_gate",
          "w_ple_proj", "g_ple", "g_final")


def _step(x, p, target, w, m, v):
    s = x.shape[0]
    tm = min(s, 256)
    tm_wide = min(s, 512)
    ch_mix = min(s, 512)
    ch_bwd = (min(s, 512), min(s, 1024), min(s, 256))
    my_index = 4 * lax.axis_index("x") + 2 * lax.axis_index("y") + lax.axis_index("c")

    big = {n: w[n][0] for n in _SHARDED_BIG}
    conv_shard = _pack_conv_shard(w["rnn_conv_w"][0], w["sc_conv_w"][0], w["ffn_conv_w"][0])
    shards = {n: big[n].astype(BF16) for n in _SHARDED_BIG}
    w_in, conv_all = _comm_only(_gather_comm([shards["w_in"], conv_shard], by_columns=(0,)), "in_proj_weight_gather")
    w4 =jnp.transpose(jnp.concatenate([conv_all[:, :3, :RNN_BW], conv_all[:, 3:4, :RNN_BW]], axis=1),
                       (1, 0, 2)).reshape(4, D_MODEL)
    w3 = jnp.transpose(conv_all[:, :3, RNN_BW:2 * RNN_BW], (1, 0, 2)).reshape(3, D_MODEL)
    wc = jnp.transpose(conv_all[:, :3, 2 * RNN_BW:], (1, 0, 2)).reshape(3, 2, D_FF).transpose(1, 0, 2)
    bc = w["ffn_conv_b"].reshape(2, 1, D_FF)
    b4, b_a, b_x, lam = w["rnn_conv_b"], w["b_rg_a"], w["b_rg_x"], w["lru_lambda"]
    w_a, w_x = w["w_rg_a"][0], w["w_rg_x"][0]
    g_final = w["g_final"].reshape(1, D_MODEL)

    tk = min(s, 512)
    received = {}

    comm = _gather_comm([shards[n] for n in ("w_proj_a", "w_proj_b", "w_out")], early_pass_on=True)
    (z, h1), (w_pa, w_pb, w_out) = _norm_in_proj(x, w["g_mix"], w_in, tm_wide, comm)
    w_pa, w_pb, w_out = (a.reshape(D_MODEL, D_MODEL) for a in (w_pa, w_pb, w_out))
    comm = _gather_comm([shards["w_up"]], early_pass_on=True)
    (ya, yb, h), (w_up_g,) = _mixers_fwd(z, w4, b4, w_a, b_a, w_x, b_x, lam, w3, ch_mix, comm)
    pa, pb, mm, x2, h2 = _merge_out(ya, yb, z, x, w_pa, w_pb, w_out, w["g_ffn"], tm)
    comm = _gather_comm([shards[n] for n in ("w_down", "w_ple_gate", "w_ple_proj")], by_columns=(2,), early_pass_on=True)
    (u0, f), (w_down, w_gate, w_pp) = _up_proj_act(h2, w_up_g, wc, bc, tm, comm)
    w_down = w_down.reshape(D_FF, D_MODEL)
    w_gate = w_gate.reshape(D_MODEL, D_MODEL)
    x3, dgt, de0, dx3, du, head_sums, dwc, dbc = _head(x2, f, u0, wc, bc, p, target, w_down, w_gate, w_pp, w["g_ple"],
                                                       g_final, tm)
    parts = [_wgrad_2d(x3, dgt, "wgrad_ple_gate", tk).reshape(N_DEV, RNN_BW, D_MODEL),
             jnp.transpose(_wgrad_2d(p, de0, "wgrad_ple_proj", tk).reshape(D_PLE, N_DEV, RNN_BW), (1, 0, 2)),
             _wgrad_2d(f, dx3, "wgrad_down", tk).reshape(N_DEV, D_FF // N_DEV, D_MODEL)]
    (dx2, dzg, dpa, dpb, dya, dyb, du0, ffn_sums), got = _up_bwd_merge_bwd(
        du, wc, w_up_g, x2, dx3, z, pa, pb, w_out, w_pa, w_pb, w["g_ffn"], tm, _exchange_comm(parts))
    received.update({n: [(g, 0)] for n, g in zip(("w_ple_gate", "w_ple_proj", "w_down"), got)})
    up_sum = _pair_reduce(_wgrad_up(h2, du0, w_up_g.shape[2], tk), "up_grad_pair_reduce")
    parts = [g.reshape(N_DEV, RNN_BW, D_MODEL)
             for g in _wgrad_multi([(mm, dx2), (ya, dpa), (yb, dpb)], "wgrad_merge", tk)]
    comm = _join_comms(_exchange_comm(parts), _chip_exchange_comm([up_sum]))
    (dz5, dw4, db4, dwa, dba, dwx, dbx, dlam, dw3), got = _mixers_bwd(z, h, dya, dyb, w4, b4, w_a, b_a, w_x, b_x, lam, w3,
                                                                       ch_bwd, comm)
    received.update({n: [(g, 0)] for n, g in zip(("w_out", "w_proj_a", "w_proj_b", "w_up"), got)})
    chip_sum = _pair_reduce(_wgrad_in(h1, dz5, dzg, 512, tk), "in_proj_grad_pair_reduce")
    (grad_x, mix_sums), got = _in_proj_bwd(dz5, dzg, w_in, x, dx2, w["g_mix"], tm, _chip_exchange_comm([chip_sum]))
    received["w_in"] = [(got[0], 0)]

    zero_row = jnp.zeros((1, D_MODEL), F32)
    vec_pack = jnp.concatenate(
        [mix_sums[0:1], db4, dba, dbx, dlam, ffn_sums[0:1], head_sums[2:3], head_sums[1:2], dbc.reshape(6, D_MODEL),
         head_sums[0:1], zero_row, dw4, dw3, zero_row, jnp.transpose(dwc, (1, 0, 2)).reshape(18, D_MODEL),
         jnp.zeros((VEC_ROWS - ROW_FFN_CONV - 18, D_MODEL), F32)], axis=0)
    gate_pack = jnp.concatenate([dwa.reshape(D_MODEL, RNN_BW), dwx.reshape(D_MODEL, RNN_BW)], axis=0)
    vec_total, gate_total = _small_all_reduce([vec_pack, gate_pack])
    loss = jnp.sum(vec_total[ROW_LOSS]) * (0.5 / D_MODEL)

    out = {}
    for n in _SHARDED_BIG:
        res = _adam_shard(received[n], big[n], m[n][0], v[n][0], "adam_" + n)
        out[n] = [r[None] for r in res]

    def flat(t, n):
        return t[n].reshape(1, -1)

    res = _adam_small([(vec_total, i) for i in range(len(_VECTORS))],
                      [flat(w, n) for n in _VECTORS], [flat(m, n) for n in _VECTORS], [flat(v, n) for n in _VECTORS],
                      "adam_vectors")
    for n, r in zip(_VECTORS, res):
        out[n] = [a.reshape(w[n].shape) for a in r]
    gates = ("w_rg_a", "w_rg_x")
    res = _adam_small([gate_total[:D_MODEL], gate_total[D_MODEL:]], *[[t[n].reshape(D_MODEL, RNN_BW) for n in gates]
                                                                      for t in (w, m, v)], "adam_gate_maps")
    for n, r in zip(gates, res):
        out[n] = [a.reshape(w[n].shape) for a in r]
    convs = ("rnn_conv_w", "sc_conv_w", "ffn_conv_w")
    bw_ffn = 2 * D_FF // N_DEV
    g_conv = [lax.dynamic_slice(vec_total[ROW_RNN_CONV:ROW_RNN_CONV + 4], (0, my_index * RNN_BW), (4, RNN_BW)),
              lax.dynamic_slice(vec_total[ROW_SC_CONV:ROW_SC_CONV + 3], (0, my_index * RNN_BW), (3, RNN_BW)),
              lax.dynamic_slice(vec_total[ROW_FFN_CONV:ROW_FFN_CONV + 18].reshape(3, 2 * D_FF), (0, my_index * bw_ffn),
                                (3, bw_ffn))]
    res = _adam_small(g_conv, *[[t[n][0] for n in convs] for t in (w, m, v)], "adam_conv")
    for n, r in zip(convs, res):
        out[n] = [a[None] for a in r]

    return (loss, grad_x[None]) + tuple(out[n][k] for k in range(4) for n in _NAMES)


def kernel(x, p, g_mix, w_in, rnn_conv_w, rnn_conv_b, w_rg_a, b_rg_a, w_rg_x, b_rg_x, lru_lambda, sc_conv_w, w_proj_a, w_proj_b, w_out, g_ffn, w_up, ffn_conv_w, ffn_conv_b, w_down, w_ple_gate, w_ple_proj, g_ple, g_final, loss_target, m_g_mix, m_w_in, m_rnn_conv_w, m_rnn_conv_b, m_w_rg_a, m_b_rg_a, m_w_rg_x, m_b_rg_x, m_lru_lambda, m_sc_conv_w, m_w_proj_a, m_w_proj_b, m_w_out, m_g_ffn, m_w_up, m_ffn_conv_w, m_ffn_conv_b, m_w_down, m_w_ple_gate, m_w_ple_proj, m_g_ple, m_g_final, v_g_mix, v_w_in, v_rnn_conv_w, v_rnn_conv_b, v_w_rg_a, v_b_rg_a, v_w_rg_x, v_b_rg_x, v_lru_lambda, v_sc_conv_w, v_w_proj_a, v_w_proj_b, v_w_out, v_g_ffn, v_w_up, v_ffn_conv_w, v_ffn_conv_b, v_w_down, v_w_ple_gate, v_w_ple_proj, v_g_ple, v_g_final):
    given = dict(locals())
    w = {n: given[n] for n in _NAMES}
    m = {n: given["m_" + n] for n in _NAMES}
    v = {n: given["v_" + n] for n in _NAMES}
    return _step(x[0], p[0, 0], loss_target[0], w, m, v)
```

```python
import functools

import jax
import jax.numpy as jnp
from jax import lax
from jax.experimental import pallas as pl
from jax.experimental.pallas import tpu as pltpu

F32 = jnp.float32
BF16 = jnp.bfloat16
MESH_ID = pl.DeviceIdType.MESH

N_DEV = 8
D_MODEL = 1024
D_PLE = 256
RNN_BW = 128
N_SEG = 7
D_FF = 3072
LRU_C = 8.0
EPS = 1e-6
ADAM_LR = 0.001
ADAM_B1 = 0.9
ADAM_B2 = 0.999
ADAM_EPS = 1e-08
ADAM_WD = 0.01
ADAM_STEP = 10

HALO = 16
SCAN_UNROLL = 16
VMEM_LIMIT = 56 * 1024 * 1024


def _dot(a, b):
    return jnp.dot(a, b, preferred_element_type=F32)


def _dot_nt(a, b):
    return lax.dot_general(a, b, (((1,), (1,)), ((), ())), preferred_element_type=F32)


def _dot_tn(a, b):
    return lax.dot_general(a, b, (((0,), (0,)), ((), ())), preferred_element_type=F32)


def _sigmoid(x):
    return 0.5 * jnp.tanh(0.5 * x) + 0.5


_GELU_C = 0.7978845608028654
_GELU_K = 0.044715


def _gelu(x):
    return 0.5 * x * (1.0 + jnp.tanh(_GELU_C * (x + _GELU_K * (x * x * x))))


def _gelu_and_grad(x):
    x2 = x * x
    t = jnp.tanh(_GELU_C * (x + _GELU_K * (x * x2)))
    g = 0.5 * x * (1.0 + t)
    dg = 0.5 * (1.0 + t) + 0.5 * x * (1.0 - t * t) * (_GELU_C * (1.0 + 3.0 * _GELU_K * x2))
    return g, dg


def _one_minus_sq(log_a, a):
    series = (-2.0 * log_a) * (1.0 + log_a * (1.0 + log_a * (2.0 / 3.0)))
    return jnp.where(log_a > -0.005, series, 1.0 - a * a)


def _softplus(x):
    return jnp.maximum(x, 0.0) + jnp.log1p(jnp.exp(-jnp.abs(x)))


def _rms(u):
    r = lax.rsqrt(jnp.mean(u * u, axis=-1, keepdims=True) + EPS)
    return u * r, r


def _rms_bwd(dn, un, r):
    return r * (dn - un * jnp.mean(dn * un, axis=-1, keepdims=True))


def _shift_dn(ext, s, n):
    if s == 0:
        return ext[HALO:HALO + n]
    return pltpu.roll(ext, s, 0)[HALO:HALO + n]


def _shift_up(ext, s, n):
    if s == 0:
        return ext[0:n]
    return pltpu.roll(ext, n + HALO - s, 0)[0:n]


def _ext_before(ref, c, ch):
    t0 = c * ch
    prev = pl.multiple_of(jnp.maximum(t0 - HALO, 0), HALO)
    halo = jnp.where(c > 0, ref[pl.ds(prev, HALO), :].astype(F32), 0.0)
    cur = ref[pl.ds(pl.multiple_of(t0, HALO), ch), :].astype(F32)
    return jnp.concatenate([halo, cur], axis=0)


def _ext_after(ref, c, ch, n_chunks):
    t0 = c * ch
    nxt = pl.multiple_of(jnp.minimum(t0 + ch, (n_chunks - 1) * ch + ch - HALO), HALO)
    halo = jnp.where(c < n_chunks - 1, ref[pl.ds(nxt, HALO), :].astype(F32), 0.0)
    cur = ref[pl.ds(pl.multiple_of(t0, HALO), ch), :].astype(F32)
    return jnp.concatenate([cur, halo], axis=0)


def _rows(ref, c, ch):
    return ref[pl.ds(pl.multiple_of(c * ch, HALO), ch), :]


def _scan_fwd(a_ref, b_ref, h_ref, n_rows, cw):
    rows = lax.broadcasted_iota(jnp.int32, (8, cw), 0)

    def body(i, carry):
        for u in range(SCAN_UNROLL):
            t = pl.multiple_of((i * SCAN_UNROLL + u) * 8, 8)
            a = a_ref[pl.ds(t, 8), :]
            b = b_ref[pl.ds(t, 8), :]
            for d in (1, 2, 4):
                m = rows >= d
                b = jnp.where(m, a * pltpu.roll(b, d, 0) + b, b)
                a = jnp.where(m, a * pltpu.roll(a, d, 0), a)
            h_ref[pl.ds(t, 8), :] = a * carry + b
            carry = jnp.broadcast_to(a[7:8, :], (8, cw)) * carry + jnp.broadcast_to(b[7:8, :], (8, cw))
        return carry

    lax.fori_loop(0, n_rows // (8 * SCAN_UNROLL), body, jnp.zeros((8, cw), F32))


def _scan_rev(a_ref, g_ref, o_ref, n_rows, cw):
    rows = lax.broadcasted_iota(jnp.int32, (8, cw), 0)
    n_groups = n_rows // 8

    def body(i, carry):
        dh_next, a_next = carry
        for u in range(SCAN_UNROLL):
            t = pl.multiple_of((n_groups - 1 - (i * SCAN_UNROLL + u)) * 8, 8)
            a = a_ref[pl.ds(t, 8), :]
            g = g_ref[pl.ds(t, 8), :]
            an = jnp.where(rows < 7, pltpu.roll(a, 7, 0), a_next)
            a_next = jnp.broadcast_to(a[0:1, :], (8, cw))
            for d in (1, 2, 4):
                m = rows < 8 - d
                g = jnp.where(m, an * pltpu.roll(g, 8 - d, 0) + g, g)
                an = jnp.where(m, an * pltpu.roll(an, 8 - d, 0), an)
            o_ref[pl.ds(t, 8), :] = an * dh_next + g
            dh_next = jnp.broadcast_to(an[0:1, :], (8, cw)) * dh_next + jnp.broadcast_to(g[0:1, :], (8, cw))
        return dh_next, a_next

    zero = jnp.zeros((8, cw), F32)
    lax.fori_loop(0, n_groups // SCAN_UNROLL, body, (zero, zero))


def _params(*sem):
    return pltpu.CompilerParams(dimension_semantics=sem, vmem_limit_bytes=VMEM_LIMIT)


def _sds(shape, dtype):
    return jax.ShapeDtypeStruct(shape, dtype)


def _resident(shape):
    zeros = (0,) * len(shape)
    return pl.BlockSpec(shape, lambda *_: zeros, pipeline_mode=pl.Buffered(1))


class _Comm:
    def __init__(self, inputs, out_shape, scratch, start, finish, middle=None):
        self.inputs, self.out_shape, self.scratch, self.start, self.finish = inputs, out_shape, scratch, start, finish
        self.middle = middle


def _join_comms(a, b):
    n, k = len(a.inputs), len(a.scratch)

    def start(ins, outs, sems):
        a.start(ins[:n], outs[:n], sems[:k])
        b.start(ins[n:], outs[n:], sems[k:])

    def finish(ins, outs, sems):
        a.finish(ins[:n], outs[:n], sems[:k])
        b.finish(ins[n:], outs[n:], sems[k:])

    return _Comm(a.inputs + b.inputs, a.out_shape + b.out_shape, a.scratch + b.scratch, start, finish)


def _sem_scratch(n, copies=7):
    return [pltpu.SemaphoreType.DMA((n, copies)), pltpu.SemaphoreType.DMA((n, copies)), pltpu.SemaphoreType.DMA((n,))]


def _gather_comm(shards, by_columns=(), early_pass_on=False):
    n = len(shards)

    def plan(ins, outs, sems):
        send_sems, recv_sems, local_sems = sems
        x, y, c = lax.axis_index("x"), lax.axis_index("y"), lax.axis_index("c")
        me, sibling = (x, y, c), (x, y, 1 - c)
        chips = [(1 - x, y), (x, 1 - y), (1 - x, 1 - y)]

        def slot(t, dev):
            idx = 4 * dev[0] + 2 * dev[1] + dev[2]
            if t in by_columns:
                width = shards[t].shape[1]
                return outs[t].at[:, pl.ds(pl.multiple_of(idx * width, 128), width)]
            return outs[t].at[idx]

        def copy(t, k, block, to, src=None):
            return pltpu.make_async_remote_copy(
                src_ref=slot(t, block) if src is None else src, dst_ref=slot(t, block),
                send_sem=send_sems.at[t, k], recv_sem=recv_sems.at[t, k],
                device_id=to, device_id_type=MESH_ID)

        mine = [pltpu.make_async_copy(ins[t], slot(t, me), local_sems.at[t]) for t in range(n)]
        first = []
        for t in range(n):
            first.append(copy(t, 0, me, sibling, src=ins[t]))
            first += [copy(t, 1 + j, me, (*chip, c), src=ins[t]) for j, chip in enumerate(chips)]
        return me, sibling, chips, c, copy, mine, first

    def start(ins, outs, sems):
        *_, mine, first = plan(ins, outs, sems)
        for cp in mine + first:
            cp.start()

    def pass_on(ins, outs, sems):
        me, sibling, chips, c, copy, _, _ = plan(ins, outs, sems)
        passed = []
        for t in range(n):
            for j, chip in enumerate(chips):
                copy(t, 1 + j, (*chip, c), me).wait_recv()
                passed.append(copy(t, 4 + j, (*chip, c), sibling))
                passed[-1].start()
        return passed

    def finish(ins, outs, sems, passed=None):
        me, sibling, chips, c, copy, mine, first = plan(ins, outs, sems)
        if passed is None:
            passed = [copy(t, 4 + j, (*chip, c), sibling) for t in range(n) for j, chip in enumerate(chips)]
        for t in range(n):
            copy(t, 0, sibling, me).wait_recv()
            for j, chip in enumerate(chips):
                copy(t, 4 + j, (*chip, 1 - c), me).wait_recv()
        for cp in first + passed:
            cp.wait_send()
        for cp in mine:
            cp.wait()

    out_shape = [_sds((s.shape[0], N_DEV * s.shape[1]) if t in by_columns else (N_DEV,) + s.shape, s.dtype)
                 for t, s in enumerate(shards)]
    if early_pass_on:
        return _Comm(list(shards), out_shape, _sem_scratch(n), start, finish, middle=pass_on)
    return _Comm(list(shards), out_shape, _sem_scratch(n), start,
                 lambda ins, outs, sems: finish(ins, outs, sems, pass_on(ins, outs, sems)))


def _peer(d):
    x, y, c = lax.axis_index("x"), lax.axis_index("y"), lax.axis_index("c")
    px = 1 - x if d & 4 else x
    py = 1 - y if d & 2 else y
    pc = 1 - c if d & 1 else c
    return (px, py, pc), 4 * px + 2 * py + pc


def _exchange_comm(parts, deltas=tuple(range(1, N_DEV))):
    n = len(parts)

    def plan(ins, outs, sems):
        send_sems, recv_sems, local_sems = sems
        _, me = _peer(0)

        def copy(t, i):
            peer, idx = _peer(deltas[i])
            return pltpu.make_async_remote_copy(
                src_ref=ins[t].at[idx], dst_ref=outs[t].at[1 + i],
                send_sem=send_sems.at[t, i], recv_sem=recv_sems.at[t, i],
                device_id=peer, device_id_type=MESH_ID)

        mine = [pltpu.make_async_copy(ins[t].at[me], outs[t].at[0], local_sems.at[t]) for t in range(n)]
        return mine, [copy(t, i) for t in range(n) for i in range(len(deltas))]

    def start(ins, outs, sems):
        mine, copies = plan(ins, outs, sems)
        for cp in mine + copies:
            cp.start()

    def finish(ins, outs, sems):
        mine, copies = plan(ins, outs, sems)
        for cp in copies:
            cp.wait_recv()
        for cp in copies:
            cp.wait_send()
        for cp in mine:
            cp.wait()

    return _Comm(list(parts), [_sds((1 + len(deltas),) + p.shape[1:], p.dtype) for p in parts],
                 _sem_scratch(n, len(deltas)), start, finish)


def _chip_exchange_comm(parts):
    n = len(parts)

    def plan(ins, outs, sems):
        send_sems, recv_sems, local_sems = sems
        x, y, c = lax.axis_index("x"), lax.axis_index("y"), lax.axis_index("c")

        def copy(t, dq):
            px = 1 - x if dq & 2 else x
            py = 1 - y if dq & 1 else y
            return pltpu.make_async_remote_copy(
                src_ref=ins[t].at[2 * px + py], dst_ref=outs[t].at[dq],
                send_sem=send_sems.at[t, dq - 1], recv_sem=recv_sems.at[t, dq - 1],
                device_id=(px, py, c), device_id_type=MESH_ID)

        mine = [pltpu.make_async_copy(ins[t].at[2 * x + y], outs[t].at[0], local_sems.at[t]) for t in range(n)]
        return mine, [copy(t, dq) for t in range(n) for dq in range(1, 4)]

    def start(ins, outs, sems):
        mine, copies = plan(ins, outs, sems)
        for cp in mine + copies:
            cp.start()

    def finish(ins, outs, sems):
        mine, copies = plan(ins, outs, sems)
        for cp in copies:
            cp.wait_recv()
        for cp in copies:
            cp.wait_send()
        for cp in mine:
            cp.wait()

    return _Comm(list(parts), [_sds(p.shape, p.dtype) for p in parts], _sem_scratch(n, 3), start, finish)


def _pair_reduce(part, name):
    by_columns = part.ndim == 2
    r, c = (part.shape[0], part.shape[1] // N_DEV) if by_columns else part.shape[1:]
    n_chips = N_DEV // 2
    tr = 128

    def body(p_ref, o_ref, own, land, send_sems, recv_sems, local_sems):
        x, y, core = lax.axis_index("x"), lax.axis_index("y"), lax.axis_index("c")

        def block(k):
            return p_ref.at[:, pl.ds(pl.multiple_of(k * c, 128), c)] if by_columns else p_ref.at[k]

        sends = [pltpu.make_async_remote_copy(
            src_ref=block(2 * q + (1 - core)), dst_ref=land.at[q], send_sem=send_sems.at[q], recv_sem=recv_sems.at[q],
            device_id=(x, y, 1 - core), device_id_type=MESH_ID) for q in range(n_chips)]
        loads = [pltpu.make_async_copy(block(2 * q + core), own.at[q], local_sems.at[q]) for q in range(n_chips)]
        for cp in sends + loads:
            cp.start()
        for q in range(n_chips):
            loads[q].wait()
            sends[q].wait_recv()

            def add_rows(i, _):
                rows = pl.ds(pl.multiple_of(i * tr, tr), tr)
                o_ref[q, rows, :] = (own[q, rows, :].astype(F32) + land[q, rows, :].astype(F32)).astype(BF16)
                return 0

            lax.fori_loop(0, r // tr, add_rows, 0)
        for cp in sends:
            cp.wait_send()

    return pl.pallas_call(
        body, name=name, out_shape=_sds((n_chips, r, c), BF16),
        in_specs=[pl.BlockSpec(memory_space=pl.ANY)], out_specs=pl.BlockSpec(memory_space=pltpu.VMEM),
        scratch_shapes=[pltpu.VMEM((n_chips, r, c), BF16), pltpu.VMEM((n_chips, r, c), BF16)]
        + [pltpu.SemaphoreType.DMA((n_chips,))] * 3,
        compiler_params=pltpu.CompilerParams(vmem_limit_bytes=VMEM_LIMIT),
    )(part)


def _comm_only(comm, name):
    n = len(comm.inputs)

    def body(*refs):
        ins, outs, sems = refs[:n], refs[n:2 * n], refs[2 * n:]
        comm.start(ins, outs, sems)
        comm.finish(ins, outs, sems)

    any_spec = pl.BlockSpec(memory_space=pl.ANY)
    return pl.pallas_call(body, name=name, out_shape=comm.out_shape, in_specs=[any_spec] * n, out_specs=[any_spec] * n,
                          scratch_shapes=comm.scratch)(*comm.inputs)


def _host_call(body, *, name, grid, in_specs, out_specs, out_shape, scratch_shapes=(), semantics, args, comm=None):
    if comm is None:
        res = pl.pallas_call(body, name=name, grid=grid, in_specs=in_specs, out_specs=out_specs, out_shape=out_shape,
                             scratch_shapes=list(scratch_shapes), compiler_params=_params(*semantics))(*args)
        return res, []
    n_in, n_out, n_scr, n_c = len(in_specs), len(out_specs), len(scratch_shapes), len(comm.inputs)

    def with_comm(*refs):
        ins, refs = refs[:n_in], refs[n_in:]
        c_ins, refs = refs[:n_c], refs[n_c:]
        outs, refs = refs[:n_out], refs[n_out:]
        c_outs, refs = refs[:n_c], refs[n_c:]
        scr, sems = refs[:n_scr], refs[n_scr:]
        ids = [pl.program_id(a) for a in range(len(grid))]
        first = functools.reduce(jnp.logical_and, [i == 0 for i in ids])
        last = functools.reduce(jnp.logical_and, [i == g - 1 for i, g in zip(ids, grid)])

        @pl.when(first)
        def _():
            comm.start(c_ins, c_outs, sems)

        body(*ins, *outs, *scr)

        if comm.middle is not None:
            assert len(grid) == 1

            @pl.when(ids[0] == (3 * grid[0]) // 4)
            def _():
                comm.middle(c_ins, c_outs, sems)

        @pl.when(last)
        def _():
            comm.finish(c_ins, c_outs, sems)

    any_spec = pl.BlockSpec(memory_space=pl.ANY)
    res = pl.pallas_call(
        with_comm, name=name, grid=grid, in_specs=list(in_specs) + [any_spec] * n_c,
        out_specs=list(out_specs) + [any_spec] * n_c, out_shape=list(out_shape) + comm.out_shape,
        scratch_shapes=list(scratch_shapes) + comm.scratch,
        compiler_params=_params(*(["arbitrary"] * len(grid))))(*args, *comm.inputs)
    return res[:n_out], res[n_out:]


def _small_all_reduce(packs):
    n = len(packs)
    pieces = [p.shape[0] // N_DEV for p in packs]

    def body(*refs):
        p_refs, o_refs, lands = refs[:n], refs[n:2 * n], refs[2 * n:3 * n]
        s1, r1, s2, r2 = refs[3 * n:]
        _, me = _peer(0)

        def piece(t, ref, idx):
            return ref.at[pl.ds(pl.multiple_of(idx * pieces[t], 8), pieces[t]), :]

        def scatter(t, d):
            peer, idx = _peer(d)
            return pltpu.make_async_remote_copy(
                src_ref=piece(t, p_refs[t], idx), dst_ref=lands[t].at[d - 1], send_sem=s1.at[t, d - 1],
                recv_sem=r1.at[t, d - 1], device_id=peer, device_id_type=MESH_ID)

        def gather(t, d, from_idx):
            peer, _ = _peer(d)
            return pltpu.make_async_remote_copy(
                src_ref=piece(t, o_refs[t], from_idx), dst_ref=piece(t, o_refs[t], from_idx), send_sem=s2.at[t, d - 1],
                recv_sem=r2.at[t, d - 1], device_id=peer, device_id_type=MESH_ID)

        first = [[scatter(t, d) for d in range(1, N_DEV)] for t in range(n)]
        for cp in sum(first, []):
            cp.start()
        second = []
        for t in range(n):
            acc = piece(t, p_refs[t], me)[...]
            for d in range(1, N_DEV):
                first[t][d - 1].wait_recv()
                acc = acc + lands[t][d - 1]
            piece(t, o_refs[t], me)[...] = acc
            second += [gather(t, d, me) for d in range(1, N_DEV)]
            for cp in second[-(N_DEV - 1):]:
                cp.start()
        for t in range(n):
            for d in range(1, N_DEV):
                gather(t, d, _peer(d)[1]).wait_recv()
        for cp in sum(first, []) + second:
            cp.wait_send()

    vmem = pl.BlockSpec(memory_space=pltpu.VMEM)
    return pl.pallas_call(
        body, name="small_all_reduce",
        out_shape=[_sds(p.shape, F32) for p in packs], in_specs=[vmem] * n, out_specs=[vmem] * n,
        scratch_shapes=[pltpu.VMEM((N_DEV - 1, pc, p.shape[1]), F32) for pc, p in zip(pieces, packs)]
        + [pltpu.SemaphoreType.DMA((n, N_DEV - 1))] * 4,
    )(*packs)


def _norm_in_proj(x, g_mix, w_in, tm, comm):
    s = x.shape[0]

    def body(x_ref, g_ref, w_ref, z_ref, h_ref):
        xn, _ = _rms(x_ref[...])
        h = (xn * g_ref[...]).astype(BF16)
        h_ref[...] = h
        for j in range(N_SEG):
            cols = slice(j * D_MODEL, (j + 1) * D_MODEL)
            z_ref[:, cols] = _dot(h, w_ref[:, cols]).astype(BF16)

    return _host_call(
        body, name="norm_in_proj", grid=(s // tm,),
        in_specs=[pl.BlockSpec((tm, D_MODEL), lambda i: (i, 0)), pl.BlockSpec((1, D_MODEL), lambda i: (0, 0)),
                  _resident(w_in.shape)],
        out_specs=[pl.BlockSpec((tm, N_SEG * D_MODEL), lambda i: (i, 0)), pl.BlockSpec((tm, D_MODEL), lambda i: (i, 0))],
        out_shape=[_sds((s, N_SEG * D_MODEL), BF16), _sds((s, D_MODEL), BF16)],
        semantics=("parallel",), args=(x, g_mix, w_in), comm=comm)


def _gate_chunk(xr_ext, w4_ref, cb_ref, wa, wx, ba_ref, bx_ref, sp, ch):
    xc = cb_ref[...] + w4_ref[3:4, :] * _shift_dn(xr_ext, 0, ch)
    for k in range(3):
        xc = xc + w4_ref[k:k + 1, :] * _shift_dn(xr_ext, 3 - k, ch)
    xcb = xc.astype(BF16)
    ra = _sigmoid(_dot(xcb, wa) + ba_ref[...])
    ia = _sigmoid(_dot(xcb, wx) + bx_ref[...])
    log_a = (-LRU_C) * ra * sp
    return xc, ra, ia, log_a


def _conv3(q_ext, w3_ref, ch):
    y = w3_ref[2:3, :] * _shift_dn(q_ext, 0, ch)
    for k in range(2):
        y = y + w3_ref[k:k + 1, :] * _shift_dn(q_ext, 2 - k, ch)
    return y


def _z_block_specs(s, cw, segs):
    nb = D_MODEL // cw
    return [pl.BlockSpec((s, cw), functools.partial(lambda j, seg: (0, seg * nb + j), seg=seg)) for seg in segs]


def _mixers_fwd(z, w4, b4, w_a, b_a, w_x, b_x, lam, w3, ch, comm):
    s = z.shape[0]
    cw = RNN_BW
    n_chunks = s // ch

    def body(xr_ref, gr_ref, cb_ref, cc_ref, cx_ref, w4_ref, b4_ref, wa_ref, ba_ref, wx_ref, bx_ref, lam_ref, w3_ref,
             ya_ref, yb_ref, h_ref, a_scr, b_scr):
        sp = _softplus(-lam_ref[...])
        wa = wa_ref[0].astype(BF16)
        wx = wx_ref[0].astype(BF16)

        def gates(c, _):
            xc, _, ia, log_a = _gate_chunk(_ext_before(xr_ref, c, ch), w4_ref, b4_ref, wa, wx, ba_ref, bx_ref, sp, ch)
            rows = pl.ds(pl.multiple_of(c * ch, HALO), ch)
            a = jnp.exp(log_a)
            a_scr[rows, :] = a
            b_scr[rows, :] = jnp.sqrt(_one_minus_sq(log_a, a)) * (ia * xc)
            q_ext = _ext_before(cc_ref, c, ch) * _ext_before(cx_ref, c, ch)
            yb_ref[rows, :] = (_rows(cb_ref, c, ch).astype(F32) * _conv3(q_ext, w3_ref, ch)).astype(BF16)
            return 0

        lax.fori_loop(0, n_chunks, gates, 0)
        _scan_fwd(a_scr, b_scr, h_ref, s, cw)

        def outputs(c, _):
            rows = pl.ds(pl.multiple_of(c * ch, HALO), ch)
            ya_ref[rows, :] = (h_ref[rows, :] * _gelu(gr_ref[rows, :].astype(F32))).astype(BF16)
            return 0

        lax.fori_loop(0, n_chunks, outputs, 0)

    col = lambda j: (0, j)
    vec = pl.BlockSpec((1, cw), col)
    sq = pl.BlockSpec((1, cw, cw), lambda j: (j, 0, 0))
    act = pl.BlockSpec((s, cw), col)
    return _host_call(
        body, name="mixers_fwd", grid=(D_MODEL // cw,),
        in_specs=_z_block_specs(s, cw, range(5)) + [pl.BlockSpec((4, cw), col), vec, sq, vec, sq, vec, vec,
                                                     pl.BlockSpec((3, cw), col)],
        out_specs=[act, act, act],
        out_shape=[_sds((s, D_MODEL), BF16), _sds((s, D_MODEL), BF16), _sds((s, D_MODEL), F32)],
        scratch_shapes=[pltpu.VMEM((s, cw), F32), pltpu.VMEM((s, cw), F32)],
        semantics=("parallel",), args=(z, z, z, z, z, w4, b4, w_a, b_a, w_x, b_x, lam, w3), comm=comm)


def _merge_out(ya, yb, z, x, w_pa, w_pb, w_out, g_ffn, tm):
    s = x.shape[0]

    def body(ya_ref, yb_ref, ga_ref, gb_ref, x_ref, wa_ref, wb_ref, wo_ref, g_ref, pa_ref, pb_ref, m_ref, x2_ref, h2_ref):
        pa = _dot(ya_ref[...], wa_ref[...])
        pb = _dot(yb_ref[...], wb_ref[...])
        pa_ref[...] = pa.astype(BF16)
        pb_ref[...] = pb.astype(BF16)
        m = (_sigmoid(ga_ref[...].astype(F32)) * pa + _sigmoid(gb_ref[...].astype(F32)) * pb).astype(BF16)
        m_ref[...] = m
        x2 = x_ref[...] + _dot(m, wo_ref[...])
        x2_ref[...] = x2
        xn, _ = _rms(x2)
        h2_ref[...] = (xn * g_ref[...]).astype(BF16)

    tile = pl.BlockSpec((tm, D_MODEL), lambda i: (i, 0))
    full = _resident((D_MODEL, D_MODEL))
    return pl.pallas_call(
        body, name="merge_out", grid=(s // tm,),
        in_specs=[tile, tile, pl.BlockSpec((tm, D_MODEL), lambda i: (i, 5)), pl.BlockSpec((tm, D_MODEL), lambda i: (i, 6)),
                  tile, full, full, full, pl.BlockSpec((1, D_MODEL), lambda i: (0, 0))],
        out_specs=[tile] * 5,
        out_shape=[_sds((s, D_MODEL), BF16)] * 3 + [_sds((s, D_MODEL), F32), _sds((s, D_MODEL), BF16)],
        compiler_params=_params("parallel"),
    )(ya, yb, z, z, x, w_pa, w_pb, w_out, g_ffn)


def _up_proj_act(h2, w_up_g, wc, bc, tm, comm):
    s = h2.shape[0]
    bw = w_up_g.shape[2]
    per_half = N_DEV // 2

    def body(h_ref, w_ref, wc_ref, bc_ref, u_ref, f_ref, halo):
        @pl.when(pl.program_id(0) == 0)
        def _():
            halo[...] = jnp.zeros_like(halo)

        h = h_ref[...]
        for j in range(per_half):
            cols = slice(j * bw, (j + 1) * bw)
            conv = []
            for half in range(2):
                u = _dot(h, w_ref[half * per_half + j]).astype(BF16)
                u_ref[half, :, cols] = u
                u = u.astype(F32)
                ext = jnp.concatenate([halo[half, :, cols], u], axis=0)
                halo[half, :, cols] = u[tm - HALO:tm]
                w = wc_ref.at[half]
                acc = bc_ref[half, :, cols] + w[2:3, cols] * u
                for k in range(2):
                    acc = acc + w[k:k + 1, cols] * _shift_dn(ext, 2 - k, tm)
                conv.append(acc)
            f_ref[:, cols] = (_gelu(conv[0]) * conv[1]).astype(BF16)

    return _host_call(
        body, name="up_proj_act", grid=(s // tm,),
        in_specs=[pl.BlockSpec((tm, D_MODEL), lambda i: (i, 0)), _resident(w_up_g.shape), _resident(wc.shape),
                  _resident(bc.shape)],
        out_specs=[pl.BlockSpec((2, tm, D_FF), lambda i: (0, i, 0)), pl.BlockSpec((tm, D_FF), lambda i: (i, 0))],
        out_shape=[_sds((2, s, D_FF), BF16), _sds((s, D_FF), BF16)],
        scratch_shapes=[pltpu.VMEM((2, HALO, D_FF), F32)],
        semantics=("arbitrary",), args=(h2, w_up_g, wc, bc), comm=comm)


def _head(x2, f, u0, wc, bc, p, target, w_down, w_gate, w_pp, g_ple, g_final, tm):
    s = x2.shape[0]
    bw = 256

    def body(x2_ref, f_ref, u_ref, uh_ref, wc_ref, bc_ref, p_ref, t_ref, wd_ref, wg_ref, wp_ref, gp_ref, gf_ref,
             x3_ref, dgt_ref, de0_ref, dx3_ref, du_ref, sums_ref, dwc_ref, dbc_ref):
        i = pl.program_id(0)

        @pl.when(i == 0)
        def _():
            sums_ref[...] = jnp.zeros_like(sums_ref)
            dwc_ref[...] = jnp.zeros_like(dwc_ref)
            dbc_ref[...] = jnp.zeros_like(dbc_ref)

        x3 = x2_ref[...] + _dot(f_ref[...], wd_ref[...])
        x3b = x3.astype(BF16)
        x3_ref[...] = x3b
        e0n, re = _rms(_dot(p_ref[...].astype(BF16), wp_ref[...]))
        e = e0n * gp_ref[...]
        sg = _sigmoid(_dot(x3b, wg_ref[...]))
        x4n, r4 = _rms(x3 + sg * e)
        diff = x4n * gf_ref[...] - t_ref[...]
        sums_ref[0:1, :] += jnp.sum(diff * diff, axis=0, keepdims=True)
        dy = diff * (1.0 / D_MODEL)
        sums_ref[1:2, :] += jnp.sum(dy * x4n, axis=0, keepdims=True)
        dx4 = _rms_bwd(dy * gf_ref[...], x4n, r4)
        de = dx4 * sg
        dgt = ((dx4 * e) * (sg * (1.0 - sg))).astype(BF16)
        dgt_ref[...] = dgt
        sums_ref[2:3, :] += jnp.sum(de * e0n, axis=0, keepdims=True)
        de0_ref[...] = _rms_bwd(de * gp_ref[...], e0n, re).astype(BF16)
        dx3 = dx4 + _dot_nt(dgt, wg_ref[...])
        dx3_ref[...] = dx3
        dx3b = dx3.astype(BF16)
        for j in range(D_FF // bw):
            cols = slice(j * bw, (j + 1) * bw)
            df = _dot_nt(dx3b, wd_ref[cols, :])
            conv, taps = [], []
            for half in range(2):
                halo = jnp.where(i > 0, uh_ref[half, :, cols].astype(F32), 0.0)
                ext = jnp.concatenate([halo, u_ref[half, :, cols].astype(F32)], axis=0)
                shifted = [_shift_dn(ext, 2 - k, tm) for k in range(3)]
                w = wc_ref.at[half]
                acc = bc_ref[half, :, cols] + w[2:3, cols] * shifted[2]
                for k in range(2):
                    acc = acc + w[k:k + 1, cols] * shifted[k]
                conv.append(acc)
                taps.append(shifted)
            gel, dgel = _gelu_and_grad(conv[0])
            for half, du in ((0, df * conv[1] * dgel), (1, df * gel)):
                du_ref[half, :, cols] = du.astype(BF16)
                dbc_ref[half, :, cols] += jnp.sum(du, axis=0, keepdims=True)
                for k in range(3):
                    dwc_ref[half, k:k + 1, cols] += jnp.sum(du * taps[half][k], axis=0, keepdims=True)

    tile = pl.BlockSpec((tm, D_MODEL), lambda i: (i, 0))
    vec = pl.BlockSpec((1, D_MODEL), lambda i: (0, 0))
    halves = pl.BlockSpec((2, tm, D_FF), lambda i: (0, i, 0))
    before = pl.BlockSpec((2, HALO, D_FF), lambda i: (0, jnp.maximum(i * (tm // HALO) - 1, 0), 0))
    return pl.pallas_call(
        body, name="loss_head", grid=(s // tm,),
        in_specs=[tile, pl.BlockSpec((tm, D_FF), lambda i: (i, 0)), halves, before, _resident(wc.shape),
                  _resident(bc.shape), pl.BlockSpec((tm, D_PLE), lambda i: (i, 0)), tile,
                  _resident((D_FF, D_MODEL)), _resident((D_MODEL, D_MODEL)), _resident((D_PLE, D_MODEL)), vec, vec],
        out_specs=[tile, tile, tile, tile, halves, pl.BlockSpec((8, D_MODEL), lambda i: (0, 0)),
                   pl.BlockSpec(wc.shape, lambda i: (0, 0, 0)), pl.BlockSpec(bc.shape, lambda i: (0, 0, 0))],
        out_shape=[_sds((s, D_MODEL), BF16)] * 3 + [_sds((s, D_MODEL), F32), _sds((2, s, D_FF), BF16),
                                                    _sds((8, D_MODEL), F32), _sds(wc.shape, F32), _sds(bc.shape, F32)],
        compiler_params=_params("arbitrary"),
    )(x2, f, u0, u0, wc, bc, p, target, w_down, w_gate, w_pp, g_ple, g_final)


def _up_bwd_merge_bwd(du, wc, w_up_g, x2, dx3, z, pa, pb, w_out, w_pa, w_pb, g_ffn, tm, comm):
    s = x2.shape[0]
    bw = w_up_g.shape[2]
    per_half = N_DEV // 2
    n_tiles = s // tm

    def body(du_ref, dua_ref, wc_ref, wu_ref, x2_ref, dx3_ref, ga_ref, gb_ref, pa_ref, pb_ref, wo_ref, wa_ref, wb_ref, g_ref,
             dx2_ref, dzg_ref, dpa_ref, dpb_ref, dya_ref, dyb_ref, du0_ref, sums_ref):
        i = pl.program_id(0)

        @pl.when(i == 0)
        def _():
            sums_ref[...] = jnp.zeros_like(sums_ref)

        dh2 = None
        for j in range(N_DEV):
            half = j // per_half
            cols = slice((j % per_half) * bw, (j % per_half + 1) * bw)
            halo = jnp.where(i < n_tiles - 1, dua_ref[half, :, cols].astype(F32), 0.0)
            ext = jnp.concatenate([du_ref[half, :, cols].astype(F32), halo], axis=0)
            w = wc_ref.at[half]
            acc = w[2:3, cols] * _shift_up(ext, 0, tm)
            for k in range(2):
                acc = acc + w[k:k + 1, cols] * _shift_up(ext, 2 - k, tm)
            du0 = acc.astype(BF16)
            du0_ref[half, :, cols] = du0
            term = _dot_nt(du0, wu_ref[j])
            dh2 = term if dh2 is None else dh2 + term
        x2n, r2 = _rms(x2_ref[...])
        sums_ref[0:1, :] += jnp.sum(dh2 * x2n, axis=0, keepdims=True)
        dx2 = dx3_ref[...] + _rms_bwd(dh2 * g_ref[...], x2n, r2)
        dx2_ref[...] = dx2
        dm = _dot_nt(dx2.astype(BF16), wo_ref[...])
        sa = _sigmoid(ga_ref[...].astype(F32))
        sb = _sigmoid(gb_ref[...].astype(F32))
        dzg_ref[0] = (dm * pa_ref[...].astype(F32) * (sa * (1.0 - sa))).astype(BF16)
        dzg_ref[1] = (dm * pb_ref[...].astype(F32) * (sb * (1.0 - sb))).astype(BF16)
        dpa = (dm * sa).astype(BF16)
        dpb = (dm * sb).astype(BF16)
        dpa_ref[...] = dpa
        dpb_ref[...] = dpb
        dya_ref[...] = _dot_nt(dpa, wa_ref[...]).astype(BF16)
        dyb_ref[...] = _dot_nt(dpb, wb_ref[...]).astype(BF16)

    tile = pl.BlockSpec((tm, D_MODEL), lambda i: (i, 0))
    full = _resident((D_MODEL, D_MODEL))
    halves = pl.BlockSpec((2, tm, D_FF), lambda i: (0, i, 0))
    after = pl.BlockSpec((2, HALO, D_FF), lambda i: (0, jnp.minimum((i + 1) * (tm // HALO), s // HALO - 1), 0))
    return _host_call(
        body, name="up_bwd_merge_bwd", grid=(n_tiles,),
        in_specs=[halves, after, _resident(wc.shape), _resident(w_up_g.shape),
                  tile, tile, pl.BlockSpec((tm, D_MODEL), lambda i: (i, 5)),
                  pl.BlockSpec((tm, D_MODEL), lambda i: (i, 6)), tile, tile, full, full, full,
                  pl.BlockSpec((1, D_MODEL), lambda i: (0, 0))],
        out_specs=[tile, pl.BlockSpec((2, tm, D_MODEL), lambda i: (0, i, 0)), tile, tile, tile, tile, halves,
                   pl.BlockSpec((8, D_MODEL), lambda i: (0, 0))],
        out_shape=[_sds((s, D_MODEL), F32), _sds((2, s, D_MODEL), BF16)] + [_sds((s, D_MODEL), BF16)] * 4
        + [_sds((2, s, D_FF), BF16), _sds((8, D_MODEL), F32)],
        semantics=("arbitrary",), args=(du, du, wc, w_up_g, x2, dx3, z, z, pa, pb, w_out, w_pa, w_pb, g_ffn),
        comm=comm)


def _mixers_bwd(z, h, dya, dyb, w4, b4, w_a, b_a, w_x, b_x, lam, w3, ch, comm):
    s = z.shape[0]
    cw = RNN_BW
    ch1, ch2, ch3 = ch

    def body(xr_ref, gr_ref, cb_ref, cc_ref, cx_ref, h_ref, dya_ref, dyb_ref, w4_ref, b4_ref, wa_ref, ba_ref, wx_ref,
             bx_ref, lam_ref, w3_ref,
             dz_ref, dw4_ref, db4_ref, dwa_ref, dba_ref, dwx_ref, dbx_ref, dlam_ref, dw3_ref,
             a_scr, g_scr, q_scr, xc_scr, ra_scr, ia_scr, dh_scr):
        sp = _softplus(-lam_ref[...])
        wa = wa_ref[0].astype(BF16)
        wx = wx_ref[0].astype(BF16)
        for ref in (dw4_ref, db4_ref, dwa_ref, dba_ref, dwx_ref, dbx_ref, dlam_ref, dw3_ref):
            ref[...] = jnp.zeros_like(ref)

        def first(c, _, ch=ch1):
            rows = pl.ds(pl.multiple_of(c * ch, HALO), ch)
            xc, ra, ia, log_a = _gate_chunk(_ext_before(xr_ref, c, ch), w4_ref, b4_ref, wa, wx, ba_ref, bx_ref, sp, ch)
            xc_scr[rows, :] = xc
            ra_scr[rows, :] = ra
            ia_scr[rows, :] = ia
            a_scr[rows, :] = jnp.exp(log_a)
            gel, dgel = _gelu_and_grad(gr_ref[rows, :].astype(F32))
            dya_c = dya_ref[rows, :].astype(F32)
            g_scr[rows, :] = dya_c * gel
            dz_ref[1, rows, :] = (dya_c * h_ref[rows, :] * dgel).astype(BF16)
            q_ext = _ext_before(cc_ref, c, ch) * _ext_before(cx_ref, c, ch)
            dyb_c = dyb_ref[rows, :].astype(F32)
            q_taps = [_shift_dn(q_ext, 2 - k, ch) for k in range(3)]
            conv_q = w3_ref[2:3, :] * q_taps[2] + w3_ref[1:2, :] * q_taps[1] + w3_ref[0:1, :] * q_taps[0]
            dz_ref[2, rows, :] = (dyb_c * conv_q).astype(BF16)
            dyq = dyb_c * cb_ref[rows, :].astype(F32)
            q_scr[rows, :] = dyq
            for k in range(3):
                dw3_ref[k:k + 1, :] += jnp.sum(dyq * q_taps[k], axis=0, keepdims=True)
            return 0

        lax.fori_loop(0, s // ch1, first, 0)
        _scan_rev(a_scr, g_scr, dh_scr, s, cw)

        def second(c, _, ch=ch2):
            rows = pl.ds(pl.multiple_of(c * ch, HALO), ch)
            xc, ra, ia, a = xc_scr[rows, :], ra_scr[rows, :], ia_scr[rows, :], a_scr[rows, :]
            m2 = _one_minus_sq((-LRU_C) * ra * sp, a)
            inv_mult = lax.rsqrt(m2)
            mult = m2 * inv_mult
            dh = dh_scr[rows, :]
            h_prev = _shift_dn(_ext_before(h_ref, c, ch), 1, ch)
            dlog_a = dh * h_prev * a - (dh * ia * xc) * (a * a) * inv_mult
            dlam_ref[...] += jnp.sum(dlog_a * ra, axis=0, keepdims=True)
            dpre_a = (dlog_a * ((-LRU_C) * sp)) * (ra * (1.0 - ra))
            dpre_x = (dh * mult * xc) * (ia * (1.0 - ia))
            dba_ref[...] += jnp.sum(dpre_a, axis=0, keepdims=True)
            dbx_ref[...] += jnp.sum(dpre_x, axis=0, keepdims=True)
            xcb = xc.astype(BF16)
            dpa_b = dpre_a.astype(BF16)
            dpx_b = dpre_x.astype(BF16)
            dwa_ref[0] += _dot_tn(xcb, dpa_b)
            dwx_ref[0] += _dot_tn(xcb, dpx_b)
            a_scr[rows, :] = dh * mult * ia + _dot_nt(dpa_b, wa) + _dot_nt(dpx_b, wx)
            return 0

        lax.fori_loop(0, s // ch2, second, 0)
        dlam_ref[...] = dlam_ref[...] * (LRU_C * _sigmoid(-lam_ref[...]))

        def third(c, _, ch=ch3, n_chunks=s // ch3):
            rows = pl.ds(pl.multiple_of(c * ch, HALO), ch)
            dxc_ext = _ext_after(a_scr, c, ch, n_chunks)
            dxc = _shift_up(dxc_ext, 0, ch)
            xr_ext = _ext_before(xr_ref, c, ch)
            db4_ref[...] += jnp.sum(dxc, axis=0, keepdims=True)
            dxr = w4_ref[3:4, :] * dxc
            dw4_ref[3:4, :] += jnp.sum(dxc * _shift_dn(xr_ext, 0, ch), axis=0, keepdims=True)
            for k in range(3):
                dxr = dxr + w4_ref[k:k + 1, :] * _shift_up(dxc_ext, 3 - k, ch)
                dw4_ref[k:k + 1, :] += jnp.sum(dxc * _shift_dn(xr_ext, 3 - k, ch), axis=0, keepdims=True)
            dz_ref[0, rows, :] = dxr.astype(BF16)
            dyq_ext = _ext_after(q_scr, c, ch, n_chunks)
            dq = w3_ref[2:3, :] * _shift_up(dyq_ext, 0, ch)
            for k in range(2):
                dq = dq + w3_ref[k:k + 1, :] * _shift_up(dyq_ext, 2 - k, ch)
            dz_ref[3, rows, :] = (dq * cx_ref[rows, :].astype(F32)).astype(BF16)
            dz_ref[4, rows, :] = (dq * cc_ref[rows, :].astype(F32)).astype(BF16)
            return 0

        lax.fori_loop(0, s // ch3, third, 0)

    col = lambda j: (0, j)
    vec = pl.BlockSpec((1, cw), col)
    sq = pl.BlockSpec((1, cw, cw), lambda j: (j, 0, 0))
    act = pl.BlockSpec((s, cw), col)
    w4s, w3s = pl.BlockSpec((4, cw), col), pl.BlockSpec((3, cw), col)
    vec_shape = _sds((1, D_MODEL), F32)
    sq_shape = _sds((D_MODEL // cw, cw, cw), F32)
    return _host_call(
        body, name="mixers_bwd", grid=(D_MODEL // cw,),
        in_specs=_z_block_specs(s, cw, range(5)) + [act, act, act, w4s, vec, sq, vec, sq, vec, vec, w3s],
        out_specs=[pl.BlockSpec((5, s, cw), lambda j: (0, 0, j)), w4s, vec, sq, vec, sq, vec, vec, w3s],
        out_shape=[_sds((5, s, D_MODEL), BF16), _sds((4, D_MODEL), F32), vec_shape, sq_shape, vec_shape, sq_shape,
                   vec_shape, vec_shape, _sds((3, D_MODEL), F32)],
        scratch_shapes=[pltpu.VMEM((s, cw), F32)] * 7,
        semantics=("parallel",), args=(z, z, z, z, z, h, dya, dyb, w4, b4, w_a, b_a, w_x, b_x, lam, w3), comm=comm)


def _in_proj_bwd(dz5, dzg, w_in, x, dx2, g_mix, tm, comm):
    s = x.shape[0]

    def body(d5_ref, dg_ref, w_ref, x_ref, dx2_ref, g_ref, dx_ref, sums_ref):
        @pl.when(pl.program_id(0) == 0)
        def _():
            sums_ref[...] = jnp.zeros_like(sums_ref)

        dh1 = None
        for k in range(N_SEG):
            d = d5_ref[k] if k < 5 else dg_ref[k - 5]
            term = _dot_nt(d, w_ref[:, k * D_MODEL:(k + 1) * D_MODEL])
            dh1 = term if dh1 is None else dh1 + term
        xn, r1 = _rms(x_ref[...])
        sums_ref[0:1, :] += jnp.sum(dh1 * xn, axis=0, keepdims=True)
        dx_ref[...] = dx2_ref[...] + _rms_bwd(dh1 * g_ref[...], xn, r1)

    tile = pl.BlockSpec((tm, D_MODEL), lambda i: (i, 0))
    return _host_call(
        body, name="in_proj_bwd", grid=(s // tm,),
        in_specs=[pl.BlockSpec((5, tm, D_MODEL), lambda i: (0, i, 0)), pl.BlockSpec((2, tm, D_MODEL), lambda i: (0, i, 0)),
                  _resident(w_in.shape), tile, tile, pl.BlockSpec((1, D_MODEL), lambda i: (0, 0))],
        out_specs=[tile, pl.BlockSpec((8, D_MODEL), lambda i: (0, 0))],
        out_shape=[_sds((s, D_MODEL), F32), _sds((8, D_MODEL), F32)],
        semantics=("arbitrary",), args=(dz5, dzg, w_in, x, dx2, g_mix), comm=comm)


def _weight_grad(a, b, b_spec, out_spec, out_shape, *, n_blocks, chunks, width, tm, tk, name, comm=None):
    s, m = a.shape
    nk = s // tk
    b_chunked = len([d for d in b_spec.block_shape if d is not None]) == 3
    blocks_out = len([d for d in out_spec.block_shape if d is not None]) == 3

    def body(a_ref, b_ref, o_ref, acc):
        k = pl.program_id(2)

        @pl.when(k == 0)
        def _():
            acc[...] = jnp.zeros_like(acc)

        at = a_ref[...].astype(BF16).T
        for j in range(chunks):
            cols = slice(j * width, (j + 1) * width)
            acc[:, cols] += _dot(at, (b_ref[j] if b_chunked else b_ref[:, cols]).astype(BF16))

        @pl.when(k == nk - 1)
        def _():
            if blocks_out:
                for j in range(chunks):
                    o_ref[j] = acc[:, j * width:(j + 1) * width].astype(o_ref.dtype)
            else:
                o_ref[...] = acc[...].astype(o_ref.dtype)

    (res,), got = _host_call(
        body, name=name, grid=(m // tm, n_blocks, nk),
        in_specs=[pl.BlockSpec((tk, tm), lambda i, j, k: (k, i)), b_spec], out_specs=[out_spec], out_shape=[out_shape],
        scratch_shapes=[pltpu.VMEM((tm, chunks * width), F32)],
        semantics=("parallel", "parallel", "arbitrary"), args=(a, b), comm=comm)
    return (res, got) if comm is not None else res


def _wgrad_2d(a, b, name, tk):
    m, n = a.shape[1], b.shape[1]
    tm = m
    return _weight_grad(a, b, pl.BlockSpec((tk, n), lambda i, j, k: (k, 0)), pl.BlockSpec((tm, n), lambda i, j, k: (i, 0)),
                        _sds((m, n), BF16), n_blocks=1, chunks=1, width=n, tm=tm, tk=tk, name=name)


def _wgrad_multi(pairs, name, tk):
    n = len(pairs)
    s, m = pairs[0][0].shape
    nn = pairs[0][1].shape[1]
    nk = s // tk

    def body(*refs):
        a_refs, b_refs, o_refs, accs = refs[:n], refs[n:2 * n], refs[2 * n:3 * n], refs[3 * n:]
        k = pl.program_id(0)
        for a_ref, b_ref, o_ref, acc in zip(a_refs, b_refs, o_refs, accs):
            @pl.when(k == 0)
            def _():
                acc[...] = jnp.zeros_like(acc)

            acc[...] += _dot(a_ref[...].astype(BF16).T, b_ref[...].astype(BF16))

            @pl.when(k == nk - 1)
            def _():
                o_ref[...] = acc[...].astype(BF16)

    whole = pl.BlockSpec((m, nn), lambda k: (0, 0))
    return pl.pallas_call(
        body, name=name, grid=(nk,),
        in_specs=[pl.BlockSpec((tk, m), lambda k: (k, 0))] * n + [pl.BlockSpec((tk, nn), lambda k: (k, 0))] * n,
        out_specs=[whole] * n, out_shape=[_sds((m, nn), BF16)] * n,
        scratch_shapes=[pltpu.VMEM((m, nn), F32)] * n,
        compiler_params=_params("arbitrary"),
    )(*[a for a, _ in pairs], *[b for _, b in pairs])


def _wgrad_in(h1, dz5, dzg, tm, tk):
    s, m = h1.shape
    nk = s // tk
    g5, gg = dz5.shape[0], dzg.shape[0]

    def body(a_ref, b5_ref, bg_ref, o_ref, acc):
        k = pl.program_id(1)

        @pl.when(k == 0)
        def _():
            acc[...] = jnp.zeros_like(acc)

        at = a_ref[...].T
        for j in range(g5 + gg):
            b = b5_ref[j] if j < g5 else bg_ref[j - g5]
            acc[:, j * D_MODEL:(j + 1) * D_MODEL] += _dot(at, b)

        @pl.when(k == nk - 1)
        def _():
            o_ref[...] = acc[...].astype(BF16)

    return pl.pallas_call(
        body, name="wgrad_in", grid=(m // tm, nk),
        in_specs=[pl.BlockSpec((tk, tm), lambda i, k: (k, i)), pl.BlockSpec((g5, tk, D_MODEL), lambda i, k: (0, k, 0)),
                  pl.BlockSpec((gg, tk, D_MODEL), lambda i, k: (0, k, 0))],
        out_specs=pl.BlockSpec((tm, (g5 + gg) * D_MODEL), lambda i, k: (i, 0)),
        out_shape=_sds((m, (g5 + gg) * D_MODEL), BF16),
        scratch_shapes=[pltpu.VMEM((tm, (g5 + gg) * D_MODEL), F32)],
        compiler_params=_params("parallel", "arbitrary"),
    )(h1, dz5, dzg)


def _wgrad_up(h2, du0, bw, tk, comm):
    per_half = N_DEV // 2
    return _weight_grad(h2, du0, pl.BlockSpec((None, tk, D_FF), lambda i, j, k: (j, k, 0)),
                        pl.BlockSpec((per_half, D_MODEL, bw), lambda i, j, k: (j, 0, 0)), _sds((N_DEV, D_MODEL, bw), BF16),
                        n_blocks=2, chunks=per_half, width=bw, tm=D_MODEL, tk=tk, name="wgrad_up", comm=comm)


def _adam_math(w, g, m, v):
    m = ADAM_B1 * m + (1.0 - ADAM_B1) * g
    v = ADAM_B2 * v + (1.0 - ADAM_B2) * jnp.square(g)
    m_hat = m / (1.0 - ADAM_B1 ** ADAM_STEP)
    v_hat = v / (1.0 - ADAM_B2 ** ADAM_STEP)
    delta = -ADAM_LR * (m_hat / (jnp.sqrt(v_hat) + ADAM_EPS) + ADAM_WD * w)
    return delta, m, v


def _adam_shard(parts, w, m, v, name):
    r, c = w.shape
    tr = min(r, 128)
    n_arrays = len(parts)

    def body(*refs):
        p_refs = refs[:n_arrays]
        w_ref, m_ref, v_ref, g_ref, d_ref, nm_ref, nv_ref = refs[n_arrays:]
        g = None
        for p_ref, (arr, first) in zip(p_refs, parts):
            for k in range(first, arr.shape[0]):
                term = p_ref[k].astype(F32)
                g = term if g is None else g + term
        g_ref[...] = g
        d_ref[...], nm_ref[...], nv_ref[...] = _adam_math(w_ref[...], g, m_ref[...], v_ref[...])

    tile = pl.BlockSpec((tr, c), lambda i: (i, 0))
    return pl.pallas_call(
        body, name=name, grid=(r // tr,),
        in_specs=[pl.BlockSpec((arr.shape[0], tr, c), lambda i: (0, i, 0)) for arr, _ in parts] + [tile, tile, tile],
        out_specs=[tile] * 4, out_shape=[_sds((r, c), F32)] * 4,
        compiler_params=_params("parallel"),
    )(*[arr for arr, _ in parts], w, m, v)


def _adam_small(grads, ws, ms, vs, name):
    n = len(ws)
    g_arrays, g_slot = [], []
    for g in grads:
        arr = g[0] if isinstance(g, tuple) else g
        if not any(arr is a for a in g_arrays):
            g_arrays.append(arr)
        g_slot.append([arr is a for a in g_arrays].index(True))
    n_g = len(g_arrays)

    def body(*refs):
        g_refs, refs = refs[:n_g], refs[n_g:]
        w_refs, m_refs, v_refs, outs = refs[:n], refs[n:2 * n], refs[2 * n:3 * n], refs[3 * n:]
        for i in range(n):
            og, od, om, ov = outs[4 * i:4 * i + 4]
            g_ref = g_refs[g_slot[i]]
            if isinstance(grads[i], tuple):
                row = grads[i][1]
                for j in range(ws[i].shape[1] // D_MODEL):
                    cols = slice(j * D_MODEL, (j + 1) * D_MODEL)
                    g = g_ref[row + j:row + j + 1, :]
                    og[:, cols] = g
                    od[:, cols], om[:, cols], ov[:, cols] = _adam_math(w_refs[i][:, cols], g, m_refs[i][:, cols],
                                                                       v_refs[i][:, cols])
            else:
                g = g_ref[...]
                og[...] = g
                od[...], om[...], ov[...] = _adam_math(w_refs[i][...], g, m_refs[i][...], v_refs[i][...])

    vmem = pl.BlockSpec(memory_space=pltpu.VMEM)
    res = pl.pallas_call(
        body, name=name, in_specs=[vmem] * (n_g + 3 * n), out_specs=[vmem] * (4 * n),
        out_shape=[_sds(a.shape, F32) for a in ws for _ in range(4)],
    )(*g_arrays, *ws, *ms, *vs)
    return [res[4 * i:4 * i + 4] for i in range(n)]


def _pack_conv_shard(rnn, sc, ffn):
    top = jnp.concatenate([rnn[:3], sc, ffn], axis=1)
    row3 = jnp.concatenate([rnn[3:4], jnp.zeros((1, D_MODEL - RNN_BW), F32)], axis=1)
    return jnp.concatenate([top, row3, jnp.zeros((4, D_MODEL), F32)], axis=0)


_VECTORS = ("g_mix", "rnn_conv_b", "b_rg_a", "b_rg_x", "lru_lambda", "g_ffn", "g_ple", "g_final", "ffn_conv_b")
VEC_ROWS = 64
ROW_LOSS, ROW_RNN_CONV, ROW_SC_CONV, ROW_FFN_CONV = 14, 16, 20, 24


_SHARDED_BIG = ("w_in", "w_proj_a", "w_proj_b", "w_out", "w_up", "w_down", "w_ple_gate", "w_ple_proj")
_NAMES = ("g_mix", "w_in", "rnn_conv_w", "rnn_conv_b", "w_rg_a", "b_rg_a", "w_rg_x", "b_rg_x", "lru_lambda", "sc_conv_w",
          "w_proj_a", "w_proj_b", "w_out", "g_ffn", "w_up", "ffn_conv_w", "ffn_conv_b", "w_down", "w_ple_gate",
          "w_ple_proj", "g_ple", "g_final")


def _step(x, p, target, w, m, v):
    s = x.shape[0]
    tm = min(s, 256)
    tm_wide = min(s, 512)
    ch_mix = min(s, 512)
    ch_bwd = (min(s, 512), min(s, 1024), min(s, 256))
    my_index = 4 * lax.axis_index("x") + 2 * lax.axis_index("y") + lax.axis_index("c")

    big = {n: w[n][0] for n in _SHARDED_BIG}
    conv_shard = _pack_conv_shard(w["rnn_conv_w"][0], w["sc_conv_w"][0], w["ffn_conv_w"][0])
    shards = {n: big[n].astype(BF16) for n in _SHARDED_BIG}
    w_in, conv_all = _comm_only(_gather_comm([shards["w_in"], conv_shard], by_columns=(0,)), "in_proj_weight_gather")
    w4 =jnp.transpose(jnp.concatenate([conv_all[:, :3, :RNN_BW], conv_all[:, 3:4, :RNN_BW]], axis=1),
                       (1, 0, 2)).reshape(4, D_MODEL)
    w3 = jnp.transpose(conv_all[:, :3, RNN_BW:2 * RNN_BW], (1, 0, 2)).reshape(3, D_MODEL)
    wc = jnp.transpose(conv_all[:, :3, 2 * RNN_BW:], (1, 0, 2)).reshape(3, 2, D_FF).transpose(1, 0, 2)
    bc = w["ffn_conv_b"].reshape(2, 1, D_FF)
    b4, b_a, b_x, lam = w["rnn_conv_b"], w["b_rg_a"], w["b_rg_x"], w["lru_lambda"]
    w_a, w_x = w["w_rg_a"][0], w["w_rg_x"][0]
    g_final = w["g_final"].reshape(1, D_MODEL)

    tk = min(s, 512)
    received = {}

    comm = _gather_comm([shards[n] for n in ("w_proj_a", "w_proj_b", "w_out")], early_pass_on=True)
    (z, h1), (w_pa, w_pb, w_out) = _norm_in_proj(x, w["g_mix"], w_in, tm_wide, comm)
    w_pa, w_pb, w_out = (a.reshape(D_MODEL, D_MODEL) for a in (w_pa, w_pb, w_out))
    comm = _gather_comm([shards["w_up"]], early_pass_on=True)
    (ya, yb, h), (w_up_g,) = _mixers_fwd(z, w4, b4, w_a, b_a, w_x, b_x, lam, w3, ch_mix, comm)
    pa, pb, mm, x2, h2 = _merge_out(ya, yb, z, x, w_pa, w_pb, w_out, w["g_ffn"], tm)
    comm = _gather_comm([shards[n] for n in ("w_down", "w_ple_gate", "w_ple_proj")], by_columns=(2,), early_pass_on=True)
    (u0, f), (w_down, w_gate, w_pp) = _up_proj_act(h2, w_up_g, wc, bc, tm, comm)
    w_down = w_down.reshape(D_FF, D_MODEL)
    w_gate = w_gate.reshape(D_MODEL, D_MODEL)
    x3, dgt, de0, dx3, du, head_sums, dwc, dbc = _head(x2, f, u0, wc, bc, p, target, w_down, w_gate, w_pp, w["g_ple"],
                                                       g_final, tm)
    parts = [_wgrad_2d(x3, dgt, "wgrad_ple_gate", tk).reshape(N_DEV, RNN_BW, D_MODEL),
             jnp.transpose(_wgrad_2d(p, de0, "wgrad_ple_proj", tk).reshape(D_PLE, N_DEV, RNN_BW), (1, 0, 2)),
             _wgrad_2d(f, dx3, "wgrad_down", tk).reshape(N_DEV, D_FF // N_DEV, D_MODEL)]
    (dx2, dzg, dpa, dpb, dya, dyb, du0, ffn_sums), got = _up_bwd_merge_bwd(
        du, wc, w_up_g, x2, dx3, z, pa, pb, w_out, w_pa, w_pb, w["g_ffn"], tm, _exchange_comm(parts[2:]))
    received["w_down"] = [(got[0], 0)]
    dw_up, got = _wgrad_up(h2, du0, w_up_g.shape[2], tk, _exchange_comm(parts[:2]))
    received.update({n: [(g, 0)] for n, g in zip(("w_ple_gate", "w_ple_proj"), got)})
    up_sum = _pair_reduce(dw_up, "up_grad_pair_reduce")
    parts = [g.reshape(N_DEV, RNN_BW, D_MODEL)
             for g in _wgrad_multi([(mm, dx2), (ya, dpa), (yb, dpb)], "wgrad_merge", tk)]
    comm = _join_comms(_exchange_comm(parts), _chip_exchange_comm([up_sum]))
    (dz5, dw4, db4, dwa, dba, dwx, dbx, dlam, dw3), got = _mixers_bwd(z, h, dya, dyb, w4, b4, w_a, b_a, w_x, b_x, lam, w3,
                                                                       ch_bwd, comm)
    received.update({n: [(g, 0)] for n, g in zip(("w_out", "w_proj_a", "w_proj_b", "w_up"), got)})
    chip_sum = _pair_reduce(_wgrad_in(h1, dz5, dzg, 512, tk), "in_proj_grad_pair_reduce")
    (grad_x, mix_sums), got = _in_proj_bwd(dz5, dzg, w_in, x, dx2, w["g_mix"], tm, _chip_exchange_comm([chip_sum]))
    received["w_in"] = [(got[0], 0)]

    zero_row = jnp.zeros((1, D_MODEL), F32)
    vec_pack = jnp.concatenate(
        [mix_sums[0:1], db4, dba, dbx, dlam, ffn_sums[0:1], head_sums[2:3], head_sums[1:2], dbc.reshape(6, D_MODEL),
         head_sums[0:1], zero_row, dw4, dw3, zero_row, jnp.transpose(dwc, (1, 0, 2)).reshape(18, D_MODEL),
         jnp.zeros((VEC_ROWS - ROW_FFN_CONV - 18, D_MODEL), F32)], axis=0)
    gate_pack = jnp.concatenate([dwa.reshape(D_MODEL, RNN_BW), dwx.reshape(D_MODEL, RNN_BW)], axis=0)
    vec_total, gate_total = _small_all_reduce([vec_pack, gate_pack])
    loss = jnp.sum(vec_total[ROW_LOSS]) * (0.5 / D_MODEL)

    out = {}
    for n in _SHARDED_BIG:
        res = _adam_shard(received[n], big[n], m[n][0], v[n][0], "adam_" + n)
        out[n] = [r[None] for r in res]

    def flat(t, n):
        return t[n].reshape(1, -1)

    res = _adam_small([(vec_total, i) for i in range(len(_VECTORS))],
                      [flat(w, n) for n in _VECTORS], [flat(m, n) for n in _VECTORS], [flat(v, n) for n in _VECTORS],
                      "adam_vectors")
    for n, r in zip(_VECTORS, res):
        out[n] = [a.reshape(w[n].shape) for a in r]
    gates = ("w_rg_a", "w_rg_x")
    res = _adam_small([gate_total[:D_MODEL], gate_total[D_MODEL:]], *[[t[n].reshape(D_MODEL, RNN_BW) for n in gates]
                                                                      for t in (w, m, v)], "adam_gate_maps")
    for n, r in zip(gates, res):
        out[n] = [a.reshape(w[n].shape) for a in r]
    convs = ("rnn_conv_w", "sc_conv_w", "ffn_conv_w")
    bw_ffn = 2 * D_FF // N_DEV
    g_conv = [lax.dynamic_slice(vec_total[ROW_RNN_CONV:ROW_RNN_CONV + 4], (0, my_index * RNN_BW), (4, RNN_BW)),
              lax.dynamic_slice(vec_total[ROW_SC_CONV:ROW_SC_CONV + 3], (0, my_index * RNN_BW), (3, RNN_BW)),
              lax.dynamic_slice(vec_total[ROW_FFN_CONV:ROW_FFN_CONV + 18].reshape(3, 2 * D_FF), (0, my_index * bw_ffn),
                                (3, bw_ffn))]
    res = _adam_small(g_conv, *[[t[n][0] for n in convs] for t in (w, m, v)], "adam_conv")
    for n, r in zip(convs, res):
        out[n] = [a[None] for a in r]

    return (loss, grad_x[None]) + tuple(out[n][k] for k in range(4) for n in _NAMES)


def kernel(x, p, g_mix, w_in, rnn_conv_w, rnn_conv_b, w_rg_a, b_rg_a, w_rg_x, b_rg_x, lru_lambda, sc_conv_w, w_proj_a, w_proj_b, w_out, g_ffn, w_up, ffn_conv_w, ffn_conv_b, w_down, w_ple_gate, w_ple_proj, g_ple, g_final, loss_target, m_g_mix, m_w_in, m_rnn_conv_w, m_rnn_conv_b, m_w_rg_a, m_b_rg_a, m_w_rg_x, m_b_rg_x, m_lru_lambda, m_sc_conv_w, m_w_proj_a, m_w_proj_b, m_w_out, m_g_ffn, m_w_up, m_ffn_conv_w, m_ffn_conv_b, m_w_down, m_w_ple_gate, m_w_ple_proj, m_g_ple, m_g_final, v_g_mix, v_w_in, v_rnn_conv_w, v_rnn_conv_b, v_w_rg_a, v_b_rg_a, v_w_rg_x, v_b_rg_x, v_lru_lambda, v_sc_conv_w, v_w_proj_a, v_w_proj_b, v_w_out, v_g_ffn, v_w_up, v_ffn_conv_w, v_ffn_conv_b, v_w_down, v_w_ple_gate, v_w_ple_proj, v_g_ple, v_g_final):
    given = dict(locals())
    w = {n: given[n] for n in _NAMES}
    m = {n: given["m_" + n] for n in _NAMES}
    v = {n: given["v_" + n] for n in _NAMES}
    return _step(x[0], p[0, 0], loss_target[0], w, m, v)
```

```python
import functools

import jax
import jax.numpy as jnp
from jax import lax
from jax.experimental import pallas as pl
from jax.experimental.pallas import tpu as pltpu

F32 = jnp.float32
BF16 = jnp.bfloat16
MESH_ID = pl.DeviceIdType.MESH

N_DEV = 8
D_MODEL = 1024
D_PLE = 256
RNN_BW = 128
N_SEG = 7
D_FF = 3072
LRU_C = 8.0
EPS = 1e-6
ADAM_LR = 0.001
ADAM_B1 = 0.9
ADAM_B2 = 0.999
ADAM_EPS = 1e-08
ADAM_WD = 0.01
ADAM_STEP = 10

HALO = 16
SCAN_UNROLL = 16
VMEM_LIMIT = 56 * 1024 * 1024


def _dot(a, b):
    return jnp.dot(a, b, preferred_element_type=F32)


def _dot_nt(a, b):
    return lax.dot_general(a, b, (((1,), (1,)), ((), ())), preferred_element_type=F32)


def _dot_tn(a, b):
    return lax.dot_general(a, b, (((0,), (0,)), ((), ())), preferred_element_type=F32)


def _sigmoid(x):
    return 0.5 * jnp.tanh(0.5 * x) + 0.5


_GELU_C = 0.7978845608028654
_GELU_K = 0.044715


def _gelu(x):
    return 0.5 * x * (1.0 + jnp.tanh(_GELU_C * (x + _GELU_K * (x * x * x))))


def _gelu_and_grad(x):
    x2 = x * x
    t = jnp.tanh(_GELU_C * (x + _GELU_K * (x * x2)))
    g = 0.5 * x * (1.0 + t)
    dg = 0.5 * (1.0 + t) + 0.5 * x * (1.0 - t * t) * (_GELU_C * (1.0 + 3.0 * _GELU_K * x2))
    return g, dg


def _one_minus_sq(log_a, a):
    series = (-2.0 * log_a) * (1.0 + log_a * (1.0 + log_a * (2.0 / 3.0)))
    return jnp.where(log_a > -0.005, series, 1.0 - a * a)


def _softplus(x):
    return jnp.maximum(x, 0.0) + jnp.log1p(jnp.exp(-jnp.abs(x)))


def _rms(u):
    r = lax.rsqrt(jnp.mean(u * u, axis=-1, keepdims=True) + EPS)
    return u * r, r


def _rms_bwd(dn, un, r):
    return r * (dn - un * jnp.mean(dn * un, axis=-1, keepdims=True))


def _shift_dn(ext, s, n):
    if s == 0:
        return ext[HALO:HALO + n]
    return pltpu.roll(ext, s, 0)[HALO:HALO + n]


def _shift_up(ext, s, n):
    if s == 0:
        return ext[0:n]
    return pltpu.roll(ext, n + HALO - s, 0)[0:n]


def _ext_before(ref, c, ch):
    t0 = c * ch
    prev = pl.multiple_of(jnp.maximum(t0 - HALO, 0), HALO)
    halo = jnp.where(c > 0, ref[pl.ds(prev, HALO), :].astype(F32), 0.0)
    cur = ref[pl.ds(pl.multiple_of(t0, HALO), ch), :].astype(F32)
    return jnp.concatenate([halo, cur], axis=0)


def _ext_after(ref, c, ch, n_chunks):
    t0 = c * ch
    nxt = pl.multiple_of(jnp.minimum(t0 + ch, (n_chunks - 1) * ch + ch - HALO), HALO)
    halo = jnp.where(c < n_chunks - 1, ref[pl.ds(nxt, HALO), :].astype(F32), 0.0)
    cur = ref[pl.ds(pl.multiple_of(t0, HALO), ch), :].astype(F32)
    return jnp.concatenate([cur, halo], axis=0)


def _rows(ref, c, ch):
    return ref[pl.ds(pl.multiple_of(c * ch, HALO), ch), :]


def _scan_fwd(a_ref, b_ref, h_ref, n_rows, cw):
    rows = lax.broadcasted_iota(jnp.int32, (8, cw), 0)

    def body(i, carry):
        for u in range(SCAN_UNROLL):
            t = pl.multiple_of((i * SCAN_UNROLL + u) * 8, 8)
            a = a_ref[pl.ds(t, 8), :]
            b = b_ref[pl.ds(t, 8), :]
            for d in (1, 2, 4):
                m = rows >= d
                b = jnp.where(m, a * pltpu.roll(b, d, 0) + b, b)
                a = jnp.where(m, a * pltpu.roll(a, d, 0), a)
            h_ref[pl.ds(t, 8), :] = a * carry + b
            carry = jnp.broadcast_to(a[7:8, :], (8, cw)) * carry + jnp.broadcast_to(b[7:8, :], (8, cw))
        return carry

    lax.fori_loop(0, n_rows // (8 * SCAN_UNROLL), body, jnp.zeros((8, cw), F32))


def _scan_rev(a_ref, g_ref, o_ref, n_rows, cw):
    rows = lax.broadcasted_iota(jnp.int32, (8, cw), 0)
    n_groups = n_rows // 8

    def body(i, carry):
        dh_next, a_next = carry
        for u in range(SCAN_UNROLL):
            t = pl.multiple_of((n_groups - 1 - (i * SCAN_UNROLL + u)) * 8, 8)
            a = a_ref[pl.ds(t, 8), :]
            g = g_ref[pl.ds(t, 8), :]
            an = jnp.where(rows < 7, pltpu.roll(a, 7, 0), a_next)
            a_next = jnp.broadcast_to(a[0:1, :], (8, cw))
            for d in (1, 2, 4):
                m = rows < 8 - d
                g = jnp.where(m, an * pltpu.roll(g, 8 - d, 0) + g, g)
                an = jnp.where(m, an * pltpu.roll(an, 8 - d, 0), an)
            o_ref[pl.ds(t, 8), :] = an * dh_next + g
            dh_next = jnp.broadcast_to(an[0:1, :], (8, cw)) * dh_next + jnp.broadcast_to(g[0:1, :], (8, cw))
        return dh_next, a_next

    zero = jnp.zeros((8, cw), F32)
    lax.fori_loop(0, n_groups // SCAN_UNROLL, body, (zero, zero))


def _params(*sem):
    return pltpu.CompilerParams(dimension_semantics=sem, vmem_limit_bytes=VMEM_LIMIT)


def _sds(shape, dtype):
    return jax.ShapeDtypeStruct(shape, dtype)


def _resident(shape):
    zeros = (0,) * len(shape)
    return pl.BlockSpec(shape, lambda *_: zeros, pipeline_mode=pl.Buffered(1))


class _Comm:
    def __init__(self, inputs, out_shape, scratch, start, finish, middle=None):
        self.inputs, self.out_shape, self.scratch, self.start, self.finish = inputs, out_shape, scratch, start, finish
        self.middle = middle


def _join_comms(a, b):
    n, k = len(a.inputs), len(a.scratch)

    def start(ins, outs, sems):
        a.start(ins[:n], outs[:n], sems[:k])
        b.start(ins[n:], outs[n:], sems[k:])

    def finish(ins, outs, sems):
        a.finish(ins[:n], outs[:n], sems[:k])
        b.finish(ins[n:], outs[n:], sems[k:])

    return _Comm(a.inputs + b.inputs, a.out_shape + b.out_shape, a.scratch + b.scratch, start, finish)


def _sem_scratch(n, copies=7):
    return [pltpu.SemaphoreType.DMA((n, copies)), pltpu.SemaphoreType.DMA((n, copies)), pltpu.SemaphoreType.DMA((n,))]


def _gather_comm(shards, by_columns=(), early_pass_on=False):
    n = len(shards)

    def plan(ins, outs, sems):
        send_sems, recv_sems, local_sems = sems
        x, y, c = lax.axis_index("x"), lax.axis_index("y"), lax.axis_index("c")
        me, sibling = (x, y, c), (x, y, 1 - c)
        chips = [(1 - x, y), (x, 1 - y), (1 - x, 1 - y)]

        def slot(t, dev):
            idx = 4 * dev[0] + 2 * dev[1] + dev[2]
            if t in by_columns:
                width = shards[t].shape[1]
                return outs[t].at[:, pl.ds(pl.multiple_of(idx * width, 128), width)]
            return outs[t].at[idx]

        def copy(t, k, block, to, src=None):
            return pltpu.make_async_remote_copy(
                src_ref=slot(t, block) if src is None else src, dst_ref=slot(t, block),
                send_sem=send_sems.at[t, k], recv_sem=recv_sems.at[t, k],
                device_id=to, device_id_type=MESH_ID)

        mine = [pltpu.make_async_copy(ins[t], slot(t, me), local_sems.at[t]) for t in range(n)]
        first = []
        for t in range(n):
            first.append(copy(t, 0, me, sibling, src=ins[t]))
            first += [copy(t, 1 + j, me, (*chip, c), src=ins[t]) for j, chip in enumerate(chips)]
        return me, sibling, chips, c, copy, mine, first

    def start(ins, outs, sems):
        *_, mine, first = plan(ins, outs, sems)
        for cp in mine + first:
            cp.start()

    def pass_on(ins, outs, sems):
        me, sibling, chips, c, copy, _, _ = plan(ins, outs, sems)
        passed = []
        for t in range(n):
            for j, chip in enumerate(chips):
                copy(t, 1 + j, (*chip, c), me).wait_recv()
                passed.append(copy(t, 4 + j, (*chip, c), sibling))
                passed[-1].start()
        return passed

    def finish(ins, outs, sems, passed=None):
        me, sibling, chips, c, copy, mine, first = plan(ins, outs, sems)
        if passed is None:
            passed = [copy(t, 4 + j, (*chip, c), sibling) for t in range(n) for j, chip in enumerate(chips)]
        for t in range(n):
            copy(t, 0, sibling, me).wait_recv()
            for j, chip in enumerate(chips):
                copy(t, 4 + j, (*chip, 1 - c), me).wait_recv()
        for cp in first + passed:
            cp.wait_send()
        for cp in mine:
            cp.wait()

    out_shape = [_sds((s.shape[0], N_DEV * s.shape[1]) if t in by_columns else (N_DEV,) + s.shape, s.dtype)
                 for t, s in enumerate(shards)]
    if early_pass_on:
        return _Comm(list(shards), out_shape, _sem_scratch(n), start, finish, middle=pass_on)
    return _Comm(list(shards), out_shape, _sem_scratch(n), start,
                 lambda ins, outs, sems: finish(ins, outs, sems, pass_on(ins, outs, sems)))


def _peer(d):
    x, y, c = lax.axis_index("x"), lax.axis_index("y"), lax.axis_index("c")
    px = 1 - x if d & 4 else x
    py = 1 - y if d & 2 else y
    pc = 1 - c if d & 1 else c
    return (px, py, pc), 4 * px + 2 * py + pc


def _exchange_comm(parts, deltas=tuple(range(1, N_DEV))):
    n = len(parts)

    def plan(ins, outs, sems):
        send_sems, recv_sems, local_sems = sems
        _, me = _peer(0)

        def copy(t, i):
            peer, idx = _peer(deltas[i])
            return pltpu.make_async_remote_copy(
                src_ref=ins[t].at[idx], dst_ref=outs[t].at[1 + i],
                send_sem=send_sems.at[t, i], recv_sem=recv_sems.at[t, i],
                device_id=peer, device_id_type=MESH_ID)

        mine = [pltpu.make_async_copy(ins[t].at[me], outs[t].at[0], local_sems.at[t]) for t in range(n)]
        return mine, [copy(t, i) for t in range(n) for i in range(len(deltas))]

    def start(ins, outs, sems):
        mine, copies = plan(ins, outs, sems)
        for cp in mine + copies:
            cp.start()

    def finish(ins, outs, sems):
        mine, copies = plan(ins, outs, sems)
        for cp in copies:
            cp.wait_recv()
        for cp in copies:
            cp.wait_send()
        for cp in mine:
            cp.wait()

    return _Comm(list(parts), [_sds((1 + len(deltas),) + p.shape[1:], p.dtype) for p in parts],
                 _sem_scratch(n, len(deltas)), start, finish)


def _chip_exchange_comm(parts):
    n = len(parts)

    def plan(ins, outs, sems):
        send_sems, recv_sems, local_sems = sems
        x, y, c = lax.axis_index("x"), lax.axis_index("y"), lax.axis_index("c")

        def copy(t, dq):
            px = 1 - x if dq & 2 else x
            py = 1 - y if dq & 1 else y
            return pltpu.make_async_remote_copy(
                src_ref=ins[t].at[2 * px + py], dst_ref=outs[t].at[dq],
                send_sem=send_sems.at[t, dq - 1], recv_sem=recv_sems.at[t, dq - 1],
                device_id=(px, py, c), device_id_type=MESH_ID)

        mine = [pltpu.make_async_copy(ins[t].at[2 * x + y], outs[t].at[0], local_sems.at[t]) for t in range(n)]
        return mine, [copy(t, dq) for t in range(n) for dq in range(1, 4)]

    def start(ins, outs, sems):
        mine, copies = plan(ins, outs, sems)
        for cp in mine + copies:
            cp.start()

    def finish(ins, outs, sems):
        mine, copies = plan(ins, outs, sems)
        for cp in copies:
            cp.wait_recv()
        for cp in copies:
            cp.wait_send()
        for cp in mine:
            cp.wait()

    return _Comm(list(parts), [_sds(p.shape, p.dtype) for p in parts], _sem_scratch(n, 3), start, finish)


def _pair_reduce(part, name):
    by_columns = part.ndim == 2
    r, c = (part.shape[0], part.shape[1] // N_DEV) if by_columns else part.shape[1:]
    n_chips = N_DEV // 2
    tr = 128

    def body(p_ref, o_ref, own, land, send_sems, recv_sems, local_sems):
        x, y, core = lax.axis_index("x"), lax.axis_index("y"), lax.axis_index("c")

        def block(k):
            return p_ref.at[:, pl.ds(pl.multiple_of(k * c, 128), c)] if by_columns else p_ref.at[k]

        sends = [pltpu.make_async_remote_copy(
            src_ref=block(2 * q + (1 - core)), dst_ref=land.at[q], send_sem=send_sems.at[q], recv_sem=recv_sems.at[q],
            device_id=(x, y, 1 - core), device_id_type=MESH_ID) for q in range(n_chips)]
        loads = [pltpu.make_async_copy(block(2 * q + core), own.at[q], local_sems.at[q]) for q in range(n_chips)]
        for cp in sends + loads:
            cp.start()
        for q in range(n_chips):
            loads[q].wait()
            sends[q].wait_recv()

            def add_rows(i, _):
                rows = pl.ds(pl.multiple_of(i * tr, tr), tr)
                o_ref[q, rows, :] = (own[q, rows, :].astype(F32) + land[q, rows, :].astype(F32)).astype(BF16)
                return 0

            lax.fori_loop(0, r // tr, add_rows, 0)
        for cp in sends:
            cp.wait_send()

    return pl.pallas_call(
        body, name=name, out_shape=_sds((n_chips, r, c), BF16),
        in_specs=[pl.BlockSpec(memory_space=pl.ANY)], out_specs=pl.BlockSpec(memory_space=pltpu.VMEM),
        scratch_shapes=[pltpu.VMEM((n_chips, r, c), BF16), pltpu.VMEM((n_chips, r, c), BF16)]
        + [pltpu.SemaphoreType.DMA((n_chips,))] * 3,
        compiler_params=pltpu.CompilerParams(vmem_limit_bytes=VMEM_LIMIT),
    )(part)


def _comm_only(comm, name):
    n = len(comm.inputs)

    def body(*refs):
        ins, outs, sems = refs[:n], refs[n:2 * n], refs[2 * n:]
        comm.start(ins, outs, sems)
        comm.finish(ins, outs, sems)

    any_spec = pl.BlockSpec(memory_space=pl.ANY)
    return pl.pallas_call(body, name=name, out_shape=comm.out_shape, in_specs=[any_spec] * n, out_specs=[any_spec] * n,
                          scratch_shapes=comm.scratch)(*comm.inputs)


def _host_call(body, *, name, grid, in_specs, out_specs, out_shape, scratch_shapes=(), semantics, args, comm=None):
    if comm is None:
        res = pl.pallas_call(body, name=name, grid=grid, in_specs=in_specs, out_specs=out_specs, out_shape=out_shape,
                             scratch_shapes=list(scratch_shapes), compiler_params=_params(*semantics))(*args)
        return res, []
    n_in, n_out, n_scr, n_c = len(in_specs), len(out_specs), len(scratch_shapes), len(comm.inputs)

    def with_comm(*refs):
        ins, refs = refs[:n_in], refs[n_in:]
        c_ins, refs = refs[:n_c], refs[n_c:]
        outs, refs = refs[:n_out], refs[n_out:]
        c_outs, refs = refs[:n_c], refs[n_c:]
        scr, sems = refs[:n_scr], refs[n_scr:]
        ids = [pl.program_id(a) for a in range(len(grid))]
        first = functools.reduce(jnp.logical_and, [i == 0 for i in ids])
        last = functools.reduce(jnp.logical_and, [i == g - 1 for i, g in zip(ids, grid)])

        @pl.when(first)
        def _():
            comm.start(c_ins, c_outs, sems)

        body(*ins, *outs, *scr)

        if comm.middle is not None:
            assert len(grid) == 1

            @pl.when(ids[0] == (3 * grid[0]) // 4)
            def _():
                comm.middle(c_ins, c_outs, sems)

        @pl.when(last)
        def _():
            comm.finish(c_ins, c_outs, sems)

    any_spec = pl.BlockSpec(memory_space=pl.ANY)
    res = pl.pallas_call(
        with_comm, name=name, grid=grid, in_specs=list(in_specs) + [any_spec] * n_c,
        out_specs=list(out_specs) + [any_spec] * n_c, out_shape=list(out_shape) + comm.out_shape,
        scratch_shapes=list(scratch_shapes) + comm.scratch,
        compiler_params=_params(*(["arbitrary"] * len(grid))))(*args, *comm.inputs)
    return res[:n_out], res[n_out:]


def _small_all_reduce(packs):
    n = len(packs)
    pieces = [p.shape[0] // N_DEV for p in packs]

    def body(*refs):
        p_refs, o_refs, lands = refs[:n], refs[n:2 * n], refs[2 * n:3 * n]
        s1, r1, s2, r2 = refs[3 * n:]
        _, me = _peer(0)

        def piece(t, ref, idx):
            return ref.at[pl.ds(pl.multiple_of(idx * pieces[t], 8), pieces[t]), :]

        def scatter(t, d):
            peer, idx = _peer(d)
            return pltpu.make_async_remote_copy(
                src_ref=piece(t, p_refs[t], idx), dst_ref=lands[t].at[d - 1], send_sem=s1.at[t, d - 1],
                recv_sem=r1.at[t, d - 1], device_id=peer, device_id_type=MESH_ID)

        def gather(t, d, from_idx):
            peer, _ = _peer(d)
            return pltpu.make_async_remote_copy(
                src_ref=piece(t, o_refs[t], from_idx), dst_ref=piece(t, o_refs[t], from_idx), send_sem=s2.at[t, d - 1],
                recv_sem=r2.at[t, d - 1], device_id=peer, device_id_type=MESH_ID)

        first = [[scatter(t, d) for d in range(1, N_DEV)] for t in range(n)]
        for cp in sum(first, []):
            cp.start()
        second = []
        for t in range(n):
            acc = piece(t, p_refs[t], me)[...]
            for d in range(1, N_DEV):
                first[t][d - 1].wait_recv()
                acc = acc + lands[t][d - 1]
            piece(t, o_refs[t], me)[...] = acc
            second += [gather(t, d, me) for d in range(1, N_DEV)]
            for cp in second[-(N_DEV - 1):]:
                cp.start()
        for t in range(n):
            for d in range(1, N_DEV):
                gather(t, d, _peer(d)[1]).wait_recv()
        for cp in sum(first, []) + second:
            cp.wait_send()

    vmem = pl.BlockSpec(memory_space=pltpu.VMEM)
    return pl.pallas_call(
        body, name="small_all_reduce",
        out_shape=[_sds(p.shape, F32) for p in packs], in_specs=[vmem] * n, out_specs=[vmem] * n,
        scratch_shapes=[pltpu.VMEM((N_DEV - 1, pc, p.shape[1]), F32) for pc, p in zip(pieces, packs)]
        + [pltpu.SemaphoreType.DMA((n, N_DEV - 1))] * 4,
    )(*packs)


def _norm_in_proj(x, g_mix, w_in, tm, comm):
    s = x.shape[0]

    def body(x_ref, g_ref, w_ref, z_ref, h_ref):
        xn, _ = _rms(x_ref[...])
        h = (xn * g_ref[...]).astype(BF16)
        h_ref[...] = h
        for j in range(N_SEG):
            cols = slice(j * D_MODEL, (j + 1) * D_MODEL)
            z_ref[:, cols] = _dot(h, w_ref[:, cols]).astype(BF16)

    return _host_call(
        body, name="norm_in_proj", grid=(s // tm,),
        in_specs=[pl.BlockSpec((tm, D_MODEL), lambda i: (i, 0)), pl.BlockSpec((1, D_MODEL), lambda i: (0, 0)),
                  _resident(w_in.shape)],
        out_specs=[pl.BlockSpec((tm, N_SEG * D_MODEL), lambda i: (i, 0)), pl.BlockSpec((tm, D_MODEL), lambda i: (i, 0))],
        out_shape=[_sds((s, N_SEG * D_MODEL), BF16), _sds((s, D_MODEL), BF16)],
        semantics=("parallel",), args=(x, g_mix, w_in), comm=comm)


def _gate_chunk(xr_ext, w4_ref, cb_ref, wa, wx, ba_ref, bx_ref, sp, ch):
    xc = cb_ref[...] + w4_ref[3:4, :] * _shift_dn(xr_ext, 0, ch)
    for k in range(3):
        xc = xc + w4_ref[k:k + 1, :] * _shift_dn(xr_ext, 3 - k, ch)
    xcb = xc.astype(BF16)
    ra = _sigmoid(_dot(xcb, wa) + ba_ref[...])
    ia = _sigmoid(_dot(xcb, wx) + bx_ref[...])
    log_a = (-LRU_C) * ra * sp
    return xc, ra, ia, log_a


def _conv3(q_ext, w3_ref, ch):
    y = w3_ref[2:3, :] * _shift_dn(q_ext, 0, ch)
    for k in range(2):
        y = y + w3_ref[k:k + 1, :] * _shift_dn(q_ext, 2 - k, ch)
    return y


def _z_block_specs(s, cw, segs):
    nb = D_MODEL // cw
    return [pl.BlockSpec((s, cw), functools.partial(lambda j, seg: (0, seg * nb + j), seg=seg)) for seg in segs]


def _mixers_fwd(z, w4, b4, w_a, b_a, w_x, b_x, lam, w3, ch, comm):
    s = z.shape[0]
    cw = RNN_BW
    n_chunks = s // ch

    def body(xr_ref, gr_ref, cb_ref, cc_ref, cx_ref, w4_ref, b4_ref, wa_ref, ba_ref, wx_ref, bx_ref, lam_ref, w3_ref,
             ya_ref, yb_ref, h_ref, a_scr, b_scr):
        sp = _softplus(-lam_ref[...])
        wa = wa_ref[0].astype(BF16)
        wx = wx_ref[0].astype(BF16)

        def gates(c, _):
            xc, _, ia, log_a = _gate_chunk(_ext_before(xr_ref, c, ch), w4_ref, b4_ref, wa, wx, ba_ref, bx_ref, sp, ch)
            rows = pl.ds(pl.multiple_of(c * ch, HALO), ch)
            a = jnp.exp(log_a)
            a_scr[rows, :] = a
            b_scr[rows, :] = jnp.sqrt(_one_minus_sq(log_a, a)) * (ia * xc)
            q_ext = _ext_before(cc_ref, c, ch) * _ext_before(cx_ref, c, ch)
            yb_ref[rows, :] = (_rows(cb_ref, c, ch).astype(F32) * _conv3(q_ext, w3_ref, ch)).astype(BF16)
            return 0

        lax.fori_loop(0, n_chunks, gates, 0)
        _scan_fwd(a_scr, b_scr, h_ref, s, cw)

        def outputs(c, _):
            rows = pl.ds(pl.multiple_of(c * ch, HALO), ch)
            ya_ref[rows, :] = (h_ref[rows, :] * _gelu(gr_ref[rows, :].astype(F32))).astype(BF16)
            return 0

        lax.fori_loop(0, n_chunks, outputs, 0)

    col = lambda j: (0, j)
    vec = pl.BlockSpec((1, cw), col)
    sq = pl.BlockSpec((1, cw, cw), lambda j: (j, 0, 0))
    act = pl.BlockSpec((s, cw), col)
    return _host_call(
        body, name="mixers_fwd", grid=(D_MODEL // cw,),
        in_specs=_z_block_specs(s, cw, range(5)) + [pl.BlockSpec((4, cw), col), vec, sq, vec, sq, vec, vec,
                                                     pl.BlockSpec((3, cw), col)],
        out_specs=[act, act, act],
        out_shape=[_sds((s, D_MODEL), BF16), _sds((s, D_MODEL), BF16), _sds((s, D_MODEL), F32)],
        scratch_shapes=[pltpu.VMEM((s, cw), F32), pltpu.VMEM((s, cw), F32)],
        semantics=("parallel",), args=(z, z, z, z, z, w4, b4, w_a, b_a, w_x, b_x, lam, w3), comm=comm)


def _merge_out(ya, yb, z, x, w_pa, w_pb, w_out, g_ffn, tm):
    s = x.shape[0]

    def body(ya_ref, yb_ref, ga_ref, gb_ref, x_ref, wa_ref, wb_ref, wo_ref, g_ref, pa_ref, pb_ref, m_ref, x2_ref, h2_ref):
        pa = _dot(ya_ref[...], wa_ref[...])
        pb = _dot(yb_ref[...], wb_ref[...])
        pa_ref[...] = pa.astype(BF16)
        pb_ref[...] = pb.astype(BF16)
        m = (_sigmoid(ga_ref[...].astype(F32)) * pa + _sigmoid(gb_ref[...].astype(F32)) * pb).astype(BF16)
        m_ref[...] = m
        x2 = x_ref[...] + _dot(m, wo_ref[...])
        x2_ref[...] = x2
        xn, _ = _rms(x2)
        h2_ref[...] = (xn * g_ref[...]).astype(BF16)

    tile = pl.BlockSpec((tm, D_MODEL), lambda i: (i, 0))
    full = _resident((D_MODEL, D_MODEL))
    return pl.pallas_call(
        body, name="merge_out", grid=(s // tm,),
        in_specs=[tile, tile, pl.BlockSpec((tm, D_MODEL), lambda i: (i, 5)), pl.BlockSpec((tm, D_MODEL), lambda i: (i, 6)),
                  tile, full, full, full, pl.BlockSpec((1, D_MODEL), lambda i: (0, 0))],
        out_specs=[tile] * 5,
        out_shape=[_sds((s, D_MODEL), BF16)] * 3 + [_sds((s, D_MODEL), F32), _sds((s, D_MODEL), BF16)],
        compiler_params=_params("parallel"),
    )(ya, yb, z, z, x, w_pa, w_pb, w_out, g_ffn)


def _up_proj_act(h2, w_up_g, wc, bc, tm, comm):
    s = h2.shape[0]
    bw = w_up_g.shape[2]
    per_half = N_DEV // 2

    def body(h_ref, w_ref, wc_ref, bc_ref, u_ref, f_ref, halo):
        @pl.when(pl.program_id(0) == 0)
        def _():
            halo[...] = jnp.zeros_like(halo)

        h = h_ref[...]
        for j in range(per_half):
            cols = slice(j * bw, (j + 1) * bw)
            conv = []
            for half in range(2):
                u = _dot(h, w_ref[half * per_half + j]).astype(BF16)
                u_ref[half, :, cols] = u
                u = u.astype(F32)
                ext = jnp.concatenate([halo[half, :, cols], u], axis=0)
                halo[half, :, cols] = u[tm - HALO:tm]
                w = wc_ref.at[half]
                acc = bc_ref[half, :, cols] + w[2:3, cols] * u
                for k in range(2):
                    acc = acc + w[k:k + 1, cols] * _shift_dn(ext, 2 - k, tm)
                conv.append(acc)
            f_ref[:, cols] = (_gelu(conv[0]) * conv[1]).astype(BF16)

    return _host_call(
        body, name="up_proj_act", grid=(s // tm,),
        in_specs=[pl.BlockSpec((tm, D_MODEL), lambda i: (i, 0)), _resident(w_up_g.shape), _resident(wc.shape),
                  _resident(bc.shape)],
        out_specs=[pl.BlockSpec((2, tm, D_FF), lambda i: (0, i, 0)), pl.BlockSpec((tm, D_FF), lambda i: (i, 0))],
        out_shape=[_sds((2, s, D_FF), BF16), _sds((s, D_FF), BF16)],
        scratch_shapes=[pltpu.VMEM((2, HALO, D_FF), F32)],
        semantics=("arbitrary",), args=(h2, w_up_g, wc, bc), comm=comm)


def _head(x2, f, u0, wc, bc, p, target, w_down, w_gate, w_pp, g_ple, g_final, tm):
    s = x2.shape[0]
    bw = 256

    def body(x2_ref, f_ref, u_ref, uh_ref, wc_ref, bc_ref, p_ref, t_ref, wd_ref, wg_ref, wp_ref, gp_ref, gf_ref,
             x3_ref, dgt_ref, de0_ref, dx3_ref, du_ref, sums_ref, dwc_ref, dbc_ref):
        i = pl.program_id(0)

        @pl.when(i == 0)
        def _():
            sums_ref[...] = jnp.zeros_like(sums_ref)
            dwc_ref[...] = jnp.zeros_like(dwc_ref)
            dbc_ref[...] = jnp.zeros_like(dbc_ref)

        x3 = x2_ref[...] + _dot(f_ref[...], wd_ref[...])
        x3b = x3.astype(BF16)
        x3_ref[...] = x3b
        e0n, re = _rms(_dot(p_ref[...].astype(BF16), wp_ref[...]))
        e = e0n * gp_ref[...]
        sg = _sigmoid(_dot(x3b, wg_ref[...]))
        x4n, r4 = _rms(x3 + sg * e)
        diff = x4n * gf_ref[...] - t_ref[...]
        sums_ref[0:1, :] += jnp.sum(diff * diff, axis=0, keepdims=True)
        dy = diff * (1.0 / D_MODEL)
        sums_ref[1:2, :] += jnp.sum(dy * x4n, axis=0, keepdims=True)
        dx4 = _rms_bwd(dy * gf_ref[...], x4n, r4)
        de = dx4 * sg
        dgt = ((dx4 * e) * (sg * (1.0 - sg))).astype(BF16)
        dgt_ref[...] = dgt
        sums_ref[2:3, :] += jnp.sum(de * e0n, axis=0, keepdims=True)
        de0_ref[...] = _rms_bwd(de * gp_ref[...], e0n, re).astype(BF16)
        dx3 = dx4 + _dot_nt(dgt, wg_ref[...])
        dx3_ref[...] = dx3
        dx3b = dx3.astype(BF16)
        for j in range(D_FF // bw):
            cols = slice(j * bw, (j + 1) * bw)
            df = _dot_nt(dx3b, wd_ref[cols, :])
            conv, taps = [], []
            for half in range(2):
                halo = jnp.where(i > 0, uh_ref[half, :, cols].astype(F32), 0.0)
                ext = jnp.concatenate([halo, u_ref[half, :, cols].astype(F32)], axis=0)
                shifted = [_shift_dn(ext, 2 - k, tm) for k in range(3)]
                w = wc_ref.at[half]
                acc = bc_ref[half, :, cols] + w[2:3, cols] * shifted[2]
                for k in range(2):
                    acc = acc + w[k:k + 1, cols] * shifted[k]
                conv.append(acc)
                taps.append(shifted)
            gel, dgel = _gelu_and_grad(conv[0])
            for half, du in ((0, df * conv[1] * dgel), (1, df * gel)):
                du_ref[half, :, cols] = du.astype(BF16)
                dbc_ref[half, :, cols] += jnp.sum(du, axis=0, keepdims=True)
                for k in range(3):
                    dwc_ref[half, k:k + 1, cols] += jnp.sum(du * taps[half][k], axis=0, keepdims=True)

    tile = pl.BlockSpec((tm, D_MODEL), lambda i: (i, 0))
    vec = pl.BlockSpec((1, D_MODEL), lambda i: (0, 0))
    halves = pl.BlockSpec((2, tm, D_FF), lambda i: (0, i, 0))
    before = pl.BlockSpec((2, HALO, D_FF), lambda i: (0, jnp.maximum(i * (tm // HALO) - 1, 0), 0))
    return pl.pallas_call(
        body, name="loss_head", grid=(s // tm,),
        in_specs=[tile, pl.BlockSpec((tm, D_FF), lambda i: (i, 0)), halves, before, _resident(wc.shape),
                  _resident(bc.shape), pl.BlockSpec((tm, D_PLE), lambda i: (i, 0)), tile,
                  _resident((D_FF, D_MODEL)), _resident((D_MODEL, D_MODEL)), _resident((D_PLE, D_MODEL)), vec, vec],
        out_specs=[tile, tile, tile, tile, halves, pl.BlockSpec((8, D_MODEL), lambda i: (0, 0)),
                   pl.BlockSpec(wc.shape, lambda i: (0, 0, 0)), pl.BlockSpec(bc.shape, lambda i: (0, 0, 0))],
        out_shape=[_sds((s, D_MODEL), BF16)] * 3 + [_sds((s, D_MODEL), F32), _sds((2, s, D_FF), BF16),
                                                    _sds((8, D_MODEL), F32), _sds(wc.shape, F32), _sds(bc.shape, F32)],
        compiler_params=_params("arbitrary"),
    )(x2, f, u0, u0, wc, bc, p, target, w_down, w_gate, w_pp, g_ple, g_final)


def _up_bwd_merge_bwd(du, wc, w_up_g, x2, dx3, z, pa, pb, w_out, w_pa, w_pb, g_ffn, tm, comm):
    s = x2.shape[0]
    bw = w_up_g.shape[2]
    per_half = N_DEV // 2
    n_tiles = s // tm

    def body(du_ref, dua_ref, wc_ref, wu_ref, x2_ref, dx3_ref, ga_ref, gb_ref, pa_ref, pb_ref, wo_ref, wa_ref, wb_ref, g_ref,
             dx2_ref, dzg_ref, dpa_ref, dpb_ref, dya_ref, dyb_ref, du0_ref, sums_ref):
        i = pl.program_id(0)

        @pl.when(i == 0)
        def _():
            sums_ref[...] = jnp.zeros_like(sums_ref)

        dh2 = None
        for j in range(N_DEV):
            half = j // per_half
            cols = slice((j % per_half) * bw, (j % per_half + 1) * bw)
            halo = jnp.where(i < n_tiles - 1, dua_ref[half, :, cols].astype(F32), 0.0)
            ext = jnp.concatenate([du_ref[half, :, cols].astype(F32), halo], axis=0)
            w = wc_ref.at[half]
            acc = w[2:3, cols] * _shift_up(ext, 0, tm)
            for k in range(2):
                acc = acc + w[k:k + 1, cols] * _shift_up(ext, 2 - k, tm)
            du0 = acc.astype(BF16)
            du0_ref[half, :, cols] = du0
            term = _dot_nt(du0, wu_ref[j])
            dh2 = term if dh2 is None else dh2 + term
        x2n, r2 = _rms(x2_ref[...])
        sums_ref[0:1, :] += jnp.sum(dh2 * x2n, axis=0, keepdims=True)
        dx2 = dx3_ref[...] + _rms_bwd(dh2 * g_ref[...], x2n, r2)
        dx2_ref[...] = dx2
        dm = _dot_nt(dx2.astype(BF16), wo_ref[...])
        sa = _sigmoid(ga_ref[...].astype(F32))
        sb = _sigmoid(gb_ref[...].astype(F32))
        dzg_ref[0] = (dm * pa_ref[...].astype(F32) * (sa * (1.0 - sa))).astype(BF16)
        dzg_ref[1] = (dm * pb_ref[...].astype(F32) * (sb * (1.0 - sb))).astype(BF16)
        dpa = (dm * sa).astype(BF16)
        dpb = (dm * sb).astype(BF16)
        dpa_ref[...] = dpa
        dpb_ref[...] = dpb
        dya_ref[...] = _dot_nt(dpa, wa_ref[...]).astype(BF16)
        dyb_ref[...] = _dot_nt(dpb, wb_ref[...]).astype(BF16)

    tile = pl.BlockSpec((tm, D_MODEL), lambda i: (i, 0))
    full = _resident((D_MODEL, D_MODEL))
    halves = pl.BlockSpec((2, tm, D_FF), lambda i: (0, i, 0))
    after = pl.BlockSpec((2, HALO, D_FF), lambda i: (0, jnp.minimum((i + 1) * (tm // HALO), s // HALO - 1), 0))
    return _host_call(
        body, name="up_bwd_merge_bwd", grid=(n_tiles,),
        in_specs=[halves, after, _resident(wc.shape), _resident(w_up_g.shape),
                  tile, tile, pl.BlockSpec((tm, D_MODEL), lambda i: (i, 5)),
                  pl.BlockSpec((tm, D_MODEL), lambda i: (i, 6)), tile, tile, full, full, full,
                  pl.BlockSpec((1, D_MODEL), lambda i: (0, 0))],
        out_specs=[tile, pl.BlockSpec((2, tm, D_MODEL), lambda i: (0, i, 0)), tile, tile, tile, tile, halves,
                   pl.BlockSpec((8, D_MODEL), lambda i: (0, 0))],
        out_shape=[_sds((s, D_MODEL), F32), _sds((2, s, D_MODEL), BF16)] + [_sds((s, D_MODEL), BF16)] * 4
        + [_sds((2, s, D_FF), BF16), _sds((8, D_MODEL), F32)],
        semantics=("arbitrary",), args=(du, du, wc, w_up_g, x2, dx3, z, z, pa, pb, w_out, w_pa, w_pb, g_ffn),
        comm=comm)


def _mixers_bwd(z, h, dya, dyb, w4, b4, w_a, b_a, w_x, b_x, lam, w3, ch, comm):
    s = z.shape[0]
    cw = RNN_BW
    ch1, ch2, ch3 = ch

    def body(xr_ref, gr_ref, cb_ref, cc_ref, cx_ref, h_ref, dya_ref, dyb_ref, w4_ref, b4_ref, wa_ref, ba_ref, wx_ref,
             bx_ref, lam_ref, w3_ref,
             dz_ref, dw4_ref, db4_ref, dwa_ref, dba_ref, dwx_ref, dbx_ref, dlam_ref, dw3_ref,
             a_scr, g_scr, q_scr, xc_scr, ra_scr, ia_scr, dh_scr):
        sp = _softplus(-lam_ref[...])
        wa = wa_ref[0].astype(BF16)
        wx = wx_ref[0].astype(BF16)
        for ref in (dw4_ref, db4_ref, dwa_ref, dba_ref, dwx_ref, dbx_ref, dlam_ref, dw3_ref):
            ref[...] = jnp.zeros_like(ref)

        def first(c, _, ch=ch1):
            rows = pl.ds(pl.multiple_of(c * ch, HALO), ch)
            xc, ra, ia, log_a = _gate_chunk(_ext_before(xr_ref, c, ch), w4_ref, b4_ref, wa, wx, ba_ref, bx_ref, sp, ch)
            xc_scr[rows, :] = xc
            ra_scr[rows, :] = ra
            ia_scr[rows, :] = ia
            a_scr[rows, :] = jnp.exp(log_a)
            gel, dgel = _gelu_and_grad(gr_ref[rows, :].astype(F32))
            dya_c = dya_ref[rows, :].astype(F32)
            g_scr[rows, :] = dya_c * gel
            dz_ref[1, rows, :] = (dya_c * h_ref[rows, :] * dgel).astype(BF16)
            q_ext = _ext_before(cc_ref, c, ch) * _ext_before(cx_ref, c, ch)
            dyb_c = dyb_ref[rows, :].astype(F32)
            q_taps = [_shift_dn(q_ext, 2 - k, ch) for k in range(3)]
            conv_q = w3_ref[2:3, :] * q_taps[2] + w3_ref[1:2, :] * q_taps[1] + w3_ref[0:1, :] * q_taps[0]
            dz_ref[2, rows, :] = (dyb_c * conv_q).astype(BF16)
            dyq = dyb_c * cb_ref[rows, :].astype(F32)
            q_scr[rows, :] = dyq
            for k in range(3):
                dw3_ref[k:k + 1, :] += jnp.sum(dyq * q_taps[k], axis=0, keepdims=True)
            return 0

        lax.fori_loop(0, s // ch1, first, 0)
        _scan_rev(a_scr, g_scr, dh_scr, s, cw)

        def second(c, _, ch=ch2):
            rows = pl.ds(pl.multiple_of(c * ch, HALO), ch)
            xc, ra, ia, a = xc_scr[rows, :], ra_scr[rows, :], ia_scr[rows, :], a_scr[rows, :]
            m2 = _one_minus_sq((-LRU_C) * ra * sp, a)
            inv_mult = lax.rsqrt(m2)
            mult = m2 * inv_mult
            dh = dh_scr[rows, :]
            h_prev = _shift_dn(_ext_before(h_ref, c, ch), 1, ch)
            dlog_a = dh * h_prev * a - (dh * ia * xc) * (a * a) * inv_mult
            dlam_ref[...] += jnp.sum(dlog_a * ra, axis=0, keepdims=True)
            dpre_a = (dlog_a * ((-LRU_C) * sp)) * (ra * (1.0 - ra))
            dpre_x = (dh * mult * xc) * (ia * (1.0 - ia))
            dba_ref[...] += jnp.sum(dpre_a, axis=0, keepdims=True)
            dbx_ref[...] += jnp.sum(dpre_x, axis=0, keepdims=True)
            xcb = xc.astype(BF16)
            dpa_b = dpre_a.astype(BF16)
            dpx_b = dpre_x.astype(BF16)
            dwa_ref[0] += _dot_tn(xcb, dpa_b)
            dwx_ref[0] += _dot_tn(xcb, dpx_b)
            a_scr[rows, :] = dh * mult * ia + _dot_nt(dpa_b, wa) + _dot_nt(dpx_b, wx)
            return 0

        lax.fori_loop(0, s // ch2, second, 0)
        dlam_ref[...] = dlam_ref[...] * (LRU_C * _sigmoid(-lam_ref[...]))

        def third(c, _, ch=ch3, n_chunks=s // ch3):
            rows = pl.ds(pl.multiple_of(c * ch, HALO), ch)
            dxc_ext = _ext_after(a_scr, c, ch, n_chunks)
            dxc = _shift_up(dxc_ext, 0, ch)
            xr_ext = _ext_before(xr_ref, c, ch)
            db4_ref[...] += jnp.sum(dxc, axis=0, keepdims=True)
            dxr = w4_ref[3:4, :] * dxc
            dw4_ref[3:4, :] += jnp.sum(dxc * _shift_dn(xr_ext, 0, ch), axis=0, keepdims=True)
            for k in range(3):
                dxr = dxr + w4_ref[k:k + 1, :] * _shift_up(dxc_ext, 3 - k, ch)
                dw4_ref[k:k + 1, :] += jnp.sum(dxc * _shift_dn(xr_ext, 3 - k, ch), axis=0, keepdims=True)
            dz_ref[0, rows, :] = dxr.astype(BF16)
            dyq_ext = _ext_after(q_scr, c, ch, n_chunks)
            dq = w3_ref[2:3, :] * _shift_up(dyq_ext, 0, ch)
            for k in range(2):
                dq = dq + w3_ref[k:k + 1, :] * _shift_up(dyq_ext, 2 - k, ch)
            dz_ref[3, rows, :] = (dq * cx_ref[rows, :].astype(F32)).astype(BF16)
            dz_ref[4, rows, :] = (dq * cc_ref[rows, :].astype(F32)).astype(BF16)
            return 0

        lax.fori_loop(0, s // ch3, third, 0)

    col = lambda j: (0, j)
    vec = pl.BlockSpec((1, cw), col)
    sq = pl.BlockSpec((1, cw, cw), lambda j: (j, 0, 0))
    act = pl.BlockSpec((s, cw), col)
    w4s, w3s = pl.BlockSpec((4, cw), col), pl.BlockSpec((3, cw), col)
    vec_shape = _sds((1, D_MODEL), F32)
    sq_shape = _sds((D_MODEL // cw, cw, cw), F32)
    return _host_call(
        body, name="mixers_bwd", grid=(D_MODEL // cw,),
        in_specs=_z_block_specs(s, cw, range(5)) + [act, act, act, w4s, vec, sq, vec, sq, vec, vec, w3s],
        out_specs=[pl.BlockSpec((5, s, cw), lambda j: (0, 0, j)), w4s, vec, sq, vec, sq, vec, vec, w3s],
        out_shape=[_sds((5, s, D_MODEL), BF16), _sds((4, D_MODEL), F32), vec_shape, sq_shape, vec_shape, sq_shape,
                   vec_shape, vec_shape, _sds((3, D_MODEL), F32)],
        scratch_shapes=[pltpu.VMEM((s, cw), F32)] * 7,
        semantics=("parallel",), args=(z, z, z, z, z, h, dya, dyb, w4, b4, w_a, b_a, w_x, b_x, lam, w3), comm=comm)


def _in_proj_bwd(dz5, dzg, w_in, x, dx2, g_mix, tm, comm):
    s = x.shape[0]

    def body(d5_ref, dg_ref, w_ref, x_ref, dx2_ref, g_ref, dx_ref, sums_ref):
        @pl.when(pl.program_id(0) == 0)
        def _():
            sums_ref[...] = jnp.zeros_like(sums_ref)

        dh1 = None
        for k in range(N_SEG):
            d = d5_ref[k] if k < 5 else dg_ref[k - 5]
            term = _dot_nt(d, w_ref[:, k * D_MODEL:(k + 1) * D_MODEL])
            dh1 = term if dh1 is None else dh1 + term
        xn, r1 = _rms(x_ref[...])
        sums_ref[0:1, :] += jnp.sum(dh1 * xn, axis=0, keepdims=True)
        dx_ref[...] = dx2_ref[...] + _rms_bwd(dh1 * g_ref[...], xn, r1)

    tile = pl.BlockSpec((tm, D_MODEL), lambda i: (i, 0))
    return _host_call(
        body, name="in_proj_bwd", grid=(s // tm,),
        in_specs=[pl.BlockSpec((5, tm, D_MODEL), lambda i: (0, i, 0)), pl.BlockSpec((2, tm, D_MODEL), lambda i: (0, i, 0)),
                  _resident(w_in.shape), tile, tile, pl.BlockSpec((1, D_MODEL), lambda i: (0, 0))],
        out_specs=[tile, pl.BlockSpec((8, D_MODEL), lambda i: (0, 0))],
        out_shape=[_sds((s, D_MODEL), F32), _sds((8, D_MODEL), F32)],
        semantics=("arbitrary",), args=(dz5, dzg, w_in, x, dx2, g_mix), comm=comm)


def _weight_grad(a, b, b_spec, out_spec, out_shape, *, n_blocks, chunks, width, tm, tk, name):
    s, m = a.shape
    nk = s // tk
    b_chunked = len([d for d in b_spec.block_shape if d is not None]) == 3
    blocks_out = len([d for d in out_spec.block_shape if d is not None]) == 3

    def body(a_ref, b_ref, o_ref, acc):
        k = pl.program_id(2)

        @pl.when(k == 0)
        def _():
            acc[...] = jnp.zeros_like(acc)

        at = a_ref[...].astype(BF16).T
        for j in range(chunks):
            cols = slice(j * width, (j + 1) * width)
            acc[:, cols] += _dot(at, (b_ref[j] if b_chunked else b_ref[:, cols]).astype(BF16))

        @pl.when(k == nk - 1)
        def _():
            if blocks_out:
                for j in range(chunks):
                    o_ref[j] = acc[:, j * width:(j + 1) * width].astype(o_ref.dtype)
            else:
                o_ref[...] = acc[...].astype(o_ref.dtype)

    return pl.pallas_call(
        body, name=name, grid=(m // tm, n_blocks, nk),
        in_specs=[pl.BlockSpec((tk, tm), lambda i, j, k: (k, i)), b_spec], out_specs=out_spec, out_shape=out_shape,
        scratch_shapes=[pltpu.VMEM((tm, chunks * width), F32)],
        compiler_params=_params("parallel", "parallel", "arbitrary"),
    )(a, b)


def _wgrad_2d(a, b, name, tk):
    m, n = a.shape[1], b.shape[1]
    tm = m
    return _weight_grad(a, b, pl.BlockSpec((tk, n), lambda i, j, k: (k, 0)), pl.BlockSpec((tm, n), lambda i, j, k: (i, 0)),
                        _sds((m, n), BF16), n_blocks=1, chunks=1, width=n, tm=tm, tk=tk, name=name)


def _wgrad_multi(pairs, name, tk):
    n = len(pairs)
    s, m = pairs[0][0].shape
    nn = pairs[0][1].shape[1]
    nk = s // tk

    def body(*refs):
        a_refs, b_refs, o_refs, accs = refs[:n], refs[n:2 * n], refs[2 * n:3 * n], refs[3 * n:]
        k = pl.program_id(0)
        for a_ref, b_ref, o_ref, acc in zip(a_refs, b_refs, o_refs, accs):
            @pl.when(k == 0)
            def _():
                acc[...] = jnp.zeros_like(acc)

            acc[...] += _dot(a_ref[...].astype(BF16).T, b_ref[...].astype(BF16))

            @pl.when(k == nk - 1)
            def _():
                o_ref[...] = acc[...].astype(BF16)

    whole = pl.BlockSpec((m, nn), lambda k: (0, 0))
    return pl.pallas_call(
        body, name=name, grid=(nk,),
        in_specs=[pl.BlockSpec((tk, m), lambda k: (k, 0))] * n + [pl.BlockSpec((tk, nn), lambda k: (k, 0))] * n,
        out_specs=[whole] * n, out_shape=[_sds((m, nn), BF16)] * n,
        scratch_shapes=[pltpu.VMEM((m, nn), F32)] * n,
        compiler_params=_params("arbitrary"),
    )(*[a for a, _ in pairs], *[b for _, b in pairs])


def _wgrad_in(h1, dz5, dzg, tm, tk):
    s, m = h1.shape
    nk = s // tk
    g5, gg = dz5.shape[0], dzg.shape[0]

    def body(a_ref, b5_ref, bg_ref, o_ref, acc):
        k = pl.program_id(1)

        @pl.when(k == 0)
        def _():
            acc[...] = jnp.zeros_like(acc)

        at = a_ref[...].T
        for j in range(g5 + gg):
            b = b5_ref[j] if j < g5 else bg_ref[j - g5]
            acc[:, j * D_MODEL:(j + 1) * D_MODEL] += _dot(at, b)

        @pl.when(k == nk - 1)
        def _():
            o_ref[...] = acc[...].astype(BF16)

    return pl.pallas_call(
        body, name="wgrad_in", grid=(m // tm, nk),
        in_specs=[pl.BlockSpec((tk, tm), lambda i, k: (k, i)), pl.BlockSpec((g5, tk, D_MODEL), lambda i, k: (0, k, 0)),
                  pl.BlockSpec((gg, tk, D_MODEL), lambda i, k: (0, k, 0))],
        out_specs=pl.BlockSpec((tm, (g5 + gg) * D_MODEL), lambda i, k: (i, 0)),
        out_shape=_sds((m, (g5 + gg) * D_MODEL), BF16),
        scratch_shapes=[pltpu.VMEM((tm, (g5 + gg) * D_MODEL), F32)],
        compiler_params=_params("parallel", "arbitrary"),
    )(h1, dz5, dzg)


def _wgrad_up(h2, du0, bw, tk):
    per_half = N_DEV // 2
    return _weight_grad(h2, du0, pl.BlockSpec((None, tk, D_FF), lambda i, j, k: (j, k, 0)),
                        pl.BlockSpec((per_half, D_MODEL, bw), lambda i, j, k: (j, 0, 0)), _sds((N_DEV, D_MODEL, bw), BF16),
                        n_blocks=2, chunks=per_half, width=bw, tm=D_MODEL, tk=tk, name="wgrad_up")


def _adam_math(w, g, m, v):
    m = ADAM_B1 * m + (1.0 - ADAM_B1) * g
    v = ADAM_B2 * v + (1.0 - ADAM_B2) * jnp.square(g)
    m_hat = m / (1.0 - ADAM_B1 ** ADAM_STEP)
    v_hat = v / (1.0 - ADAM_B2 ** ADAM_STEP)
    delta = -ADAM_LR * (m_hat / (jnp.sqrt(v_hat) + ADAM_EPS) + ADAM_WD * w)
    return delta, m, v


def _adam_shard(parts, w, m, v, name):
    r, c = w.shape
    tr = r if r <= 512 else 256
    n_arrays = len(parts)

    def body(*refs):
        p_refs = refs[:n_arrays]
        w_ref, m_ref, v_ref, g_ref, d_ref, nm_ref, nv_ref = refs[n_arrays:]
        g = None
        for p_ref, (arr, first) in zip(p_refs, parts):
            for k in range(first, arr.shape[0]):
                term = p_ref[k].astype(F32)
                g = term if g is None else g + term
        g_ref[...] = g
        d_ref[...], nm_ref[...], nv_ref[...] = _adam_math(w_ref[...], g, m_ref[...], v_ref[...])

    tile = pl.BlockSpec((tr, c), lambda i: (i, 0))
    return pl.pallas_call(
        body, name=name, grid=(r // tr,),
        in_specs=[pl.BlockSpec((arr.shape[0], tr, c), lambda i: (0, i, 0)) for arr, _ in parts] + [tile, tile, tile],
        out_specs=[tile] * 4, out_shape=[_sds((r, c), F32)] * 4,
        compiler_params=_params("parallel"),
    )(*[arr for arr, _ in parts], w, m, v)


def _adam_small(grads, ws, ms, vs, name):
    n = len(ws)
    g_arrays, g_slot = [], []
    for g in grads:
        arr = g[0] if isinstance(g, tuple) else g
        if not any(arr is a for a in g_arrays):
            g_arrays.append(arr)
        g_slot.append([arr is a for a in g_arrays].index(True))
    n_g = len(g_arrays)

    def body(*refs):
        g_refs, refs = refs[:n_g], refs[n_g:]
        w_refs, m_refs, v_refs, outs = refs[:n], refs[n:2 * n], refs[2 * n:3 * n], refs[3 * n:]
        for i in range(n):
            og, od, om, ov = outs[4 * i:4 * i + 4]
            g_ref = g_refs[g_slot[i]]
            if isinstance(grads[i], tuple):
                row = grads[i][1]
                for j in range(ws[i].shape[1] // D_MODEL):
                    cols = slice(j * D_MODEL, (j + 1) * D_MODEL)
                    g = g_ref[row + j:row + j + 1, :]
                    og[:, cols] = g
                    od[:, cols], om[:, cols], ov[:, cols] = _adam_math(w_refs[i][:, cols], g, m_refs[i][:, cols],
                                                                       v_refs[i][:, cols])
            else:
                g = g_ref[...]
                og[...] = g
                od[...], om[...], ov[...] = _adam_math(w_refs[i][...], g, m_refs[i][...], v_refs[i][...])

    vmem = pl.BlockSpec(memory_space=pltpu.VMEM)
    res = pl.pallas_call(
        body, name=name, in_specs=[vmem] * (n_g + 3 * n), out_specs=[vmem] * (4 * n),
        out_shape=[_sds(a.shape, F32) for a in ws for _ in range(4)],
    )(*g_arrays, *ws, *ms, *vs)
    return [res[4 * i:4 * i + 4] for i in range(n)]


def _pack_conv_shard(rnn, sc, ffn):
    top = jnp.concatenate([rnn[:3], sc, ffn], axis=1)
    row3 = jnp.concatenate([rnn[3:4], jnp.zeros((1, D_MODEL - RNN_BW), F32)], axis=1)
    return jnp.concatenate([top, row3, jnp.zeros((4, D_MODEL), F32)], axis=0)


_VECTORS = ("g_mix", "rnn_conv_b", "b_rg_a", "b_rg_x", "lru_lambda", "g_ffn", "g_ple", "g_final", "ffn_conv_b")
VEC_ROWS = 64
ROW_LOSS, ROW_RNN_CONV, ROW_SC_CONV, ROW_FFN_CONV = 14, 16, 20, 24


_SHARDED_BIG = ("w_in", "w_proj_a", "w_proj_b", "w_out", "w_up", "w_down", "w_ple_gate", "w_ple_proj")
_NAMES = ("g_mix", "w_in", "rnn_conv_w", "rnn_conv_b", "w_rg_a", "b_rg_a", "w_rg_x", "b_rg_x", "lru_lambda", "sc_conv_w",
          "w_proj_a", "w_proj_b", "w_out", "g_ffn", "w_up", "ffn_conv_w", "ffn_conv_b", "w_down", "w_ple_gate",
          "w_ple_proj", "g_ple", "g_final")


def _step(x, p, target, w, m, v):
    s = x.shape[0]
    tm = min(s, 256)
    tm_wide = min(s, 512)
    ch_mix = min(s, 512)
    ch_bwd = (min(s, 512), min(s, 1024), min(s, 256))
    my_index = 4 * lax.axis_index("x") + 2 * lax.axis_index("y") + lax.axis_index("c")

    big = {n: w[n][0] for n in _SHARDED_BIG}
    conv_shard = _pack_conv_shard(w["rnn_conv_w"][0], w["sc_conv_w"][0], w["ffn_conv_w"][0])
    shards = {n: big[n].astype(BF16) for n in _SHARDED_BIG}
    w_in, conv_all = _comm_only(_gather_comm([shards["w_in"], conv_shard], by_columns=(0,)), "in_proj_weight_gather")
    w4 =jnp.transpose(jnp.concatenate([conv_all[:, :3, :RNN_BW], conv_all[:, 3:4, :RNN_BW]], axis=1),
                       (1, 0, 2)).reshape(4, D_MODEL)
    w3 = jnp.transpose(conv_all[:, :3, RNN_BW:2 * RNN_BW], (1, 0, 2)).reshape(3, D_MODEL)
    wc = jnp.transpose(conv_all[:, :3, 2 * RNN_BW:], (1, 0, 2)).reshape(3, 2, D_FF).transpose(1, 0, 2)
    bc = w["ffn_conv_b"].reshape(2, 1, D_FF)
    b4, b_a, b_x, lam = w["rnn_conv_b"], w["b_rg_a"], w["b_rg_x"], w["lru_lambda"]
    w_a, w_x = w["w_rg_a"][0], w["w_rg_x"][0]
    g_final = w["g_final"].reshape(1, D_MODEL)

    tk = min(s, 512)
    received = {}

    comm = _gather_comm([shards[n] for n in ("w_proj_a", "w_proj_b", "w_out")], early_pass_on=True)
    (z, h1), (w_pa, w_pb, w_out) = _norm_in_proj(x, w["g_mix"], w_in, tm_wide, comm)
    w_pa, w_pb, w_out = (a.reshape(D_MODEL, D_MODEL) for a in (w_pa, w_pb, w_out))
    comm = _gather_comm([shards["w_up"]], early_pass_on=True)
    (ya, yb, h), (w_up_g,) = _mixers_fwd(z, w4, b4, w_a, b_a, w_x, b_x, lam, w3, ch_mix, comm)
    pa, pb, mm, x2, h2 = _merge_out(ya, yb, z, x, w_pa, w_pb, w_out, w["g_ffn"], tm)
    comm = _gather_comm([shards[n] for n in ("w_down", "w_ple_gate", "w_ple_proj")], by_columns=(2,), early_pass_on=True)
    (u0, f), (w_down, w_gate, w_pp) = _up_proj_act(h2, w_up_g, wc, bc, tm, comm)
    w_down = w_down.reshape(D_FF, D_MODEL)
    w_gate = w_gate.reshape(D_MODEL, D_MODEL)
    x3, dgt, de0, dx3, du, head_sums, dwc, dbc = _head(x2, f, u0, wc, bc, p, target, w_down, w_gate, w_pp, w["g_ple"],
                                                       g_final, tm)
    parts = [_wgrad_2d(x3, dgt, "wgrad_ple_gate", tk).reshape(N_DEV, RNN_BW, D_MODEL),
             jnp.transpose(_wgrad_2d(p, de0, "wgrad_ple_proj", tk).reshape(D_PLE, N_DEV, RNN_BW), (1, 0, 2)),
             _wgrad_2d(f, dx3, "wgrad_down", tk).reshape(N_DEV, D_FF // N_DEV, D_MODEL)]
    (dx2, dzg, dpa, dpb, dya, dyb, du0, ffn_sums), got = _up_bwd_merge_bwd(
        du, wc, w_up_g, x2, dx3, z, pa, pb, w_out, w_pa, w_pb, w["g_ffn"], tm, _exchange_comm(parts))
    received.update({n: [(g, 0)] for n, g in zip(("w_ple_gate", "w_ple_proj", "w_down"), got)})
    up_sum = _pair_reduce(_wgrad_up(h2, du0, w_up_g.shape[2], tk), "up_grad_pair_reduce")
    parts = [g.reshape(N_DEV, RNN_BW, D_MODEL)
             for g in _wgrad_multi([(mm, dx2), (ya, dpa), (yb, dpb)], "wgrad_merge", tk)]
    comm = _join_comms(_exchange_comm(parts), _chip_exchange_comm([up_sum]))
    (dz5, dw4, db4, dwa, dba, dwx, dbx, dlam, dw3), got = _mixers_bwd(z, h, dya, dyb, w4, b4, w_a, b_a, w_x, b_x, lam, w3,
                                                                       ch_bwd, comm)
    received.update({n: [(g, 0)] for n, g in zip(("w_out", "w_proj_a", "w_proj_b", "w_up"), got)})
    chip_sum = _pair_reduce(_wgrad_in(h1, dz5, dzg, 512, tk), "in_proj_grad_pair_reduce")
    (grad_x, mix_sums), got = _in_proj_bwd(dz5, dzg, w_in, x, dx2, w["g_mix"], tm, _chip_exchange_comm([chip_sum]))
    received["w_in"] = [(got[0], 0)]

    zero_row = jnp.zeros((1, D_MODEL), F32)
    vec_pack = jnp.concatenate(
        [mix_sums[0:1], db4, dba, dbx, dlam, ffn_sums[0:1], head_sums[2:3], head_sums[1:2], dbc.reshape(6, D_MODEL),
         head_sums[0:1], zero_row, dw4, dw3, zero_row, jnp.transpose(dwc, (1, 0, 2)).reshape(18, D_MODEL),
         jnp.zeros((VEC_ROWS - ROW_FFN_CONV - 18, D_MODEL), F32)], axis=0)
    gate_pack = jnp.concatenate([dwa.reshape(D_MODEL, RNN_BW), dwx.reshape(D_MODEL, RNN_BW)], axis=0)
    vec_total, gate_total = _small_all_reduce([vec_pack, gate_pack])
    loss = jnp.sum(vec_total[ROW_LOSS]) * (0.5 / D_MODEL)

    out = {}
    for n in _SHARDED_BIG:
        res = _adam_shard(received[n], big[n], m[n][0], v[n][0], "adam_" + n)
        out[n] = [r[None] for r in res]

    def flat(t, n):
        return t[n].reshape(1, -1)

    res = _adam_small([(vec_total, i) for i in range(len(_VECTORS))],
                      [flat(w, n) for n in _VECTORS], [flat(m, n) for n in _VECTORS], [flat(v, n) for n in _VECTORS],
                      "adam_vectors")
    for n, r in zip(_VECTORS, res):
        out[n] = [a.reshape(w[n].shape) for a in r]
    gates = ("w_rg_a", "w_rg_x")
    res = _adam_small([gate_total[:D_MODEL], gate_total[D_MODEL:]], *[[t[n].reshape(D_MODEL, RNN_BW) for n in gates]
                                                                      for t in (w, m, v)], "adam_gate_maps")
    for n, r in zip(gates, res):
        out[n] = [a.reshape(w[n].shape) for a in r]
    convs = ("rnn_conv_w", "sc_conv_w", "ffn_conv_w")
    bw_ffn = 2 * D_FF // N_DEV
    g_conv = [lax.dynamic_slice(vec_total[ROW_RNN_CONV:ROW_RNN_CONV + 4], (0, my_index * RNN_BW), (4, RNN_BW)),
              lax.dynamic_slice(vec_total[ROW_SC_CONV:ROW_SC_CONV + 3], (0, my_index * RNN_BW), (3, RNN_BW)),
              lax.dynamic_slice(vec_total[ROW_FFN_CONV:ROW_FFN_CONV + 18].reshape(3, 2 * D_FF), (0, my_index * bw_ffn),
                                (3, bw_ffn))]
    res = _adam_small(g_conv, *[[t[n][0] for n in convs] for t in (w, m, v)], "adam_conv")
    for n, r in zip(convs, res):
        out[n] = [a[None] for a in r]

    return (loss, grad_x[None]) + tuple(out[n][k] for k in range(4) for n in _NAMES)


def kernel(x, p, g_mix, w_in, rnn_conv_w, rnn_conv_b, w_rg_a, b_rg_a, w_rg_x, b_rg_x, lru_lambda, sc_conv_w, w_proj_a, w_proj_b, w_out, g_ffn, w_up, ffn_conv_w, ffn_conv_b, w_down, w_ple_gate, w_ple_proj, g_ple, g_final, loss_target, m_g_mix, m_w_in, m_rnn_conv_w, m_rnn_conv_b, m_w_rg_a, m_b_rg_a, m_w_rg_x, m_b_rg_x, m_lru_lambda, m_sc_conv_w, m_w_proj_a, m_w_proj_b, m_w_out, m_g_ffn, m_w_up, m_ffn_conv_w, m_ffn_conv_b, m_w_down, m_w_ple_gate, m_w_ple_proj, m_g_ple, m_g_final, v_g_mix, v_w_in, v_rnn_conv_w, v_rnn_conv_b, v_w_rg_a, v_b_rg_a, v_w_rg_x, v_b_rg_x, v_lru_lambda, v_sc_conv_w, v_w_proj_a, v_w_proj_b, v_w_out, v_g_ffn, v_w_up, v_ffn_conv_w, v_ffn_conv_b, v_w_down, v_w_ple_gate, v_w_ple_proj, v_g_ple, v_g_final):
    given = dict(locals())
    w = {n: given[n] for n in _NAMES}
    m = {n: given["m_" + n] for n in _NAMES}
    v = {n: given["v_" + n] for n in _NAMES}
    return _step(x[0], p[0, 0], loss_target[0], w, m, v)
```
